```python
import math
import jax, jax.numpy as jnp
from jax import lax
import numpy as np

D_MODEL = 2048
BATCH = 4
SEQ = 4096
DEPTH = 1

CHUNK = 64
Q_BLOCK = 128
RET_HEADS = 8
RET_QK_DIM = 128
RET_V_DIM = 256
RET_THETA = 10000.0
DIFF_HEADS = 8
DIFF_QK_DIM = 128
DIFF_V_DIM = 256
ROPE_THETA = 500000.0
ROPE_DIM = DIFF_QK_DIM // 4
N_GROUPS = 4
EXPERTS_PER_GROUP = 8
N_EXPERTS = N_GROUPS * EXPERTS_PER_GROUP
TOP_K = 2
EXPERT_FF = 1024
MOE_BLOCK = 256
LN_EPS = 1e-5
DEEPNORM_ALPHA = (2 * DEPTH) ** 0.25
DEEPNORM_BETA = (8 * DEPTH) ** -0.25
RET_Q_W = RET_HEADS * RET_QK_DIM
RET_V_W = RET_HEADS * RET_V_DIM
DIFF_QK_W = 2 * DIFF_HEADS * DIFF_QK_DIM
DIFF_V_W = DIFF_HEADS * DIFF_V_DIM
IN_SIZES = (RET_Q_W, RET_Q_W, RET_V_W, RET_V_W, DIFF_QK_W, DIFF_QK_W, DIFF_V_W, 2 * D_MODEL)
N_IN = sum(IN_SIZES)

kernel_name = "hybrid_retention_diffattn_hmoe_deepnorm"


def layer_norm(x, gain, bias=None, eps=LN_EPS):
    xf = x.astype(jnp.float32)
    mu = jnp.mean(xf, axis=-1, keepdims=True)
    var = jnp.mean(jnp.square(xf - mu), axis=-1, keepdims=True)
    y = (xf - mu) * lax.rsqrt(var + eps) * gain
    if bias is not None:
        y = y + bias
    return y.astype(x.dtype)


def rms_norm(x, gain, eps=LN_EPS):
    xf = x.astype(jnp.float32)
    y = xf * lax.rsqrt(jnp.mean(jnp.square(xf), axis=-1, keepdims=True) + eps) * gain
    return y.astype(x.dtype)


def rope_angles(positions, dim, theta):
    inv_freq = 1.0 / (theta ** (jnp.arange(0, dim, 2, dtype=jnp.float32) / dim))
    ang = positions.astype(jnp.float32)[..., None] * inv_freq
    return jnp.cos(ang), jnp.sin(ang)


def rotate(x, cos, sin):
    half = x.shape[-1] // 2
    x1, x2 = x[..., :half], x[..., half:]
    return jnp.concatenate([x1 * cos - x2 * sin, x2 * cos + x1 * sin], axis=-1)


def retention(q, k, v, positions):
    B, S, H, dk = q.shape
    dv = v.shape[-1]
    nc = S // CHUNK
    cos, sin = rope_angles(positions, dk, RET_THETA)
    cos, sin = cos[:, :, None, :], sin[:, :, None, :]
    q = rotate(q.astype(jnp.float32), cos, sin)
    k = rotate(k.astype(jnp.float32), cos, sin) * (dk ** -0.5)
    v = v.astype(jnp.float32)
    log_gamma = jnp.log(1.0 - 2.0 ** (-5.0 - jnp.arange(H, dtype=jnp.float32)))
    idx = jnp.arange(CHUNK, dtype=jnp.float32)
    intra_decay = jnp.exp(log_gamma[:, None, None] * jnp.abs(idx[:, None] - idx[None, :]))
    q_decay = jnp.exp(log_gamma[:, None] * (idx + 1.0))
    k_decay = jnp.exp(log_gamma[:, None] * (CHUNK - 1.0 - idx))
    chunk_decay = jnp.exp(log_gamma * CHUNK)

    def to_chunks(t):
        return t.reshape(B, nc, CHUNK, H, t.shape[-1]).transpose(1, 0, 3, 2, 4)

    qc, kc, vc = to_chunks(q), to_chunks(k), to_chunks(v)
    scores = jnp.einsum('nbhcd,nbhkd->nbhck', qc, kc) * intra_decay[None, None]
    intra = jnp.einsum('nbhck,nbhke->nbhce', scores, vc)

    def step(state, xs):
        qi, ki, vi = xs
        cross = jnp.einsum('bhcd,bhde->bhce', qi * q_decay[None, :, :, None], state)
        state = state * chunk_decay[None, :, None, None] + jnp.einsum(
            'bhcd,bhce->bhde', ki * k_decay[None, :, :, None], vi)
        return state, cross

    state0 = jnp.zeros((B, H, dk, dv), jnp.float32)
    _, cross = lax.scan(step, state0, (qc, kc, vc))
    out = intra + cross
    return out.transpose(1, 0, 3, 2, 4).reshape(B, S, H, dv)


def diff_attention(q, k, v, positions, lam, lambda_init, subln_gain):
    B, S, H, _, d = q.shape
    cos, sin = rope_angles(positions, ROPE_DIM, ROPE_THETA)
    cos, sin = cos[:, :, None, None, :], sin[:, :, None, None, :]

    def partial_rope(t):
        tr = rotate(t[..., :ROPE_DIM].astype(jnp.float32), cos, sin).astype(t.dtype)
        return jnp.concatenate([tr, t[..., ROPE_DIM:]], axis=-1)

    qh = partial_rope(q).transpose(0, 2, 3, 1, 4)
    kh = partial_rope(k).transpose(0, 2, 3, 1, 4)
    vh = v.transpose(0, 2, 1, 3)
    scale = d ** -0.5
    chunk_id = np.arange(S) // CHUNK
    outs = []
    for start in range(0, S, Q_BLOCK):
        end = start + Q_BLOCK
        s = jnp.einsum('bhiqd,bhikd->bhiqk', qh[:, :, :, start:end], kh[:, :, :, :end],
                       preferred_element_type=jnp.float32) * scale
        mask = chunk_id[start:end, None] >= chunk_id[None, :end]
        p = jax.nn.softmax(jnp.where(mask, s, -jnp.inf), axis=-1)
        a = p[:, :, 0] - lam * p[:, :, 1]
        outs.append(jnp.einsum('bhqk,bhke->bhqe', a.astype(vh.dtype), vh[:, :, :end]))
    o = jnp.concatenate(outs, axis=2)
    o = rms_norm(o, subln_gain) * (1.0 - lambda_init)
    return o.transpose(0, 2, 1, 3)


def mixer_sublayer(x, positions, w_in, ret_gn_gain, lq1, lk1, lq2, lk2, subln_gain,
                   w_ret_proj, w_diff_proj, w_out, lambda_init):
    B, S, _ = x.shape
    u = x @ w_in
    cuts = list(np.cumsum(IN_SIZES)[:-1])
    rq, rk, rv, rg, dq, dk, dvv, gates = jnp.split(u, cuts, axis=-1)
    ret = retention(rq.reshape(B, S, RET_HEADS, RET_QK_DIM), rk.reshape(B, S, RET_HEADS, RET_QK_DIM),
                    rv.reshape(B, S, RET_HEADS, RET_V_DIM), positions)
    ret = layer_norm(ret, ret_gn_gain).astype(x.dtype).reshape(B, S, RET_V_W) * jax.nn.silu(rg)
    lam = (jnp.exp(jnp.sum(lq1.astype(jnp.float32) * lk1.astype(jnp.float32)))
           - jnp.exp(jnp.sum(lq2.astype(jnp.float32) * lk2.astype(jnp.float32))) + lambda_init)
    diff = diff_attention(dq.reshape(B, S, DIFF_HEADS, 2, DIFF_QK_DIM),
                          dk.reshape(B, S, DIFF_HEADS, 2, DIFF_QK_DIM),
                          dvv.reshape(B, S, DIFF_HEADS, DIFF_V_DIM), positions, lam, lambda_init, subln_gain)
    diff = diff.reshape(B, S, DIFF_V_W)
    g_ret, g_diff = jnp.split(gates, 2, axis=-1)
    merged = jax.nn.sigmoid(g_ret) * (ret @ w_ret_proj) + jax.nn.sigmoid(g_diff) * (diff @ w_diff_proj)
    return merged @ w_out


def hierarchical_moe(h, w_group, b_group, w_router, b_router, w_gate, w_up, w_down):
    B, S, D = h.shape
    xt = h.reshape(-1, D)
    n_tok = xt.shape[0]
    group_prob = jax.nn.softmax((xt @ w_group + b_group).astype(jnp.float32), axis=-1)
    g = jnp.argmax(group_prob, axis=-1)
    p_g = jnp.take_along_axis(group_prob, g[:, None], axis=1)[:, 0]
    exp_logits = (xt @ w_router + b_router).astype(jnp.float32).reshape(n_tok, N_GROUPS, EXPERTS_PER_GROUP)
    within = jnp.take_along_axis(exp_logits, g[:, None, None], axis=1)[:, 0]
    top_vals, top_idx = lax.top_k(within, TOP_K)
    weights = (jax.nn.softmax(top_vals, axis=-1) * p_g[:, None]).reshape(-1)
    expert_ids = (g[:, None] * EXPERTS_PER_GROUP + top_idx).reshape(-1).astype(jnp.int32)
    token_ids = jnp.repeat(jnp.arange(n_tok, dtype=jnp.int32), TOP_K)
    n_assign = n_tok * TOP_K
    order = jnp.argsort(expert_ids)
    sorted_e = expert_ids[order]
    counts = jax.ops.segment_sum(jnp.ones((n_assign,), jnp.int32), expert_ids, num_segments=N_EXPERTS)
    padded = (counts + MOE_BLOCK - 1) // MOE_BLOCK * MOE_BLOCK
    pad_end = jnp.cumsum(padded)
    pad_start = pad_end - padded
    start = jnp.cumsum(counts) - counts
    dest = pad_start[sorted_e] + jnp.arange(n_assign, dtype=jnp.int32) - start[sorted_e]
    n_slots = -(-n_assign // MOE_BLOCK) * MOE_BLOCK + N_EXPERTS * MOE_BLOCK
    n_blocks = n_slots // MOE_BLOCK
    slot_token = jnp.zeros((n_slots,), jnp.int32).at[dest].set(token_ids[order])
    slot_weight = jnp.zeros((n_slots,), jnp.float32).at[dest].set(weights[order])
    block_expert = jnp.minimum(
        jnp.searchsorted(pad_end, jnp.arange(n_blocks, dtype=jnp.int32) * MOE_BLOCK, side='right'),
        N_EXPERTS - 1)
    xb = xt[slot_token].reshape(n_blocks, MOE_BLOCK, D)

    def expert_block(args):
        xblk, e = args
        return (jax.nn.silu(xblk @ w_gate[e]) * (xblk @ w_up[e])) @ w_down[e]

    yb = lax.map(expert_block, (xb, block_expert)).reshape(n_slots, D)
    y = jnp.zeros((n_tok, D), jnp.float32).at[slot_token].add(yb.astype(jnp.float32) * slot_weight[:, None])
    return y.astype(h.dtype).reshape(B, S, D)


def setup_inputs(seed: int = 0) -> dict:
    key = jax.random.key(seed)
    ks = jax.random.split(key, 24)
    f32 = jnp.float32
    nrm = lambda k, shape, s: jax.random.normal(k, shape, f32) * s
    start = jax.random.randint(ks[1], (BATCH, 1), 0, 8192, dtype=jnp.int32)
    return {
        "x": nrm(ks[0], (BATCH, SEQ, D_MODEL), 1.0),
        "positions": start + jnp.arange(SEQ, dtype=jnp.int32)[None, :],
        "w_in": nrm(ks[2], (DEPTH, D_MODEL, N_IN), D_MODEL ** -0.5),
        "ret_gn_gain": 1.0 + nrm(ks[3], (DEPTH, RET_HEADS, RET_V_DIM), 0.02),
        "diff_lambda_q1": nrm(ks[4], (DEPTH, DIFF_QK_DIM), 0.1),
        "diff_lambda_k1": nrm(ks[5], (DEPTH, DIFF_QK_DIM), 0.1),
        "diff_lambda_q2": nrm(ks[6], (DEPTH, DIFF_QK_DIM), 0.1),
        "diff_lambda_k2": nrm(ks[7], (DEPTH, DIFF_QK_DIM), 0.1),
        "diff_subln_gain": 1.0 + nrm(ks[8], (DEPTH, DIFF_V_DIM), 0.02),
        "w_ret_proj": nrm(ks[9], (DEPTH, RET_V_W, D_MODEL), RET_V_W ** -0.5 * DEEPNORM_BETA),
        "w_diff_proj": nrm(ks[10], (DEPTH, DIFF_V_W, D_MODEL), DIFF_V_W ** -0.5 * DEEPNORM_BETA),
        "w_out": nrm(ks[11], (DEPTH, D_MODEL, D_MODEL), D_MODEL ** -0.5 * DEEPNORM_BETA),
        "ln1_gain": 1.0 + nrm(ks[12], (DEPTH, D_MODEL), 0.02),
        "ln1_bias": nrm(ks[13], (DEPTH, D_MODEL), 0.02),
        "w_group": nrm(ks[14], (DEPTH, D_MODEL, N_GROUPS), D_MODEL ** -0.5),
        "b_group": nrm(ks[15], (DEPTH, N_GROUPS), 0.01),
        "w_router": nrm(ks[16], (DEPTH, D_MODEL, N_EXPERTS), D_MODEL ** -0.5),
        "b_router": nrm(ks[17], (DEPTH, N_EXPERTS), 0.01),
        "w_expert_gate": nrm(ks[18], (DEPTH, N_EXPERTS, D_MODEL, EXPERT_FF), D_MODEL ** -0.5),
        "w_expert_up": nrm(ks[19], (DEPTH, N_EXPERTS, D_MODEL, EXPERT_FF), D_MODEL ** -0.5),
        "w_expert_down": nrm(ks[20], (DEPTH, N_EXPERTS, EXPERT_FF, D_MODEL), EXPERT_FF ** -0.5 * DEEPNORM_BETA),
        "ln2_gain": 1.0 + nrm(ks[21], (DEPTH, D_MODEL), 0.02),
        "ln2_bias": nrm(ks[22], (DEPTH, D_MODEL), 0.02),
    }


def reference(x, positions, w_in, ret_gn_gain, diff_lambda_q1, diff_lambda_k1, diff_lambda_q2,
              diff_lambda_k2, diff_subln_gain, w_ret_proj, w_diff_proj, w_out, ln1_gain, ln1_bias,
              w_group, b_group, w_router, b_router, w_expert_gate, w_expert_up, w_expert_down,
              ln2_gain, ln2_bias):
    for l in range(DEPTH):
        lambda_init = 0.8 - 0.6 * math.exp(-0.3 * l)
        mix = mixer_sublayer(x, positions, w_in[l], ret_gn_gain[l], diff_lambda_q1[l], diff_lambda_k1[l],
                             diff_lambda_q2[l], diff_lambda_k2[l], diff_subln_gain[l], w_ret_proj[l],
                             w_diff_proj[l], w_out[l], lambda_init)
        h = layer_norm(DEEPNORM_ALPHA * x + mix, ln1_gain[l], ln1_bias[l])
        ffn = hierarchical_moe(h, w_group[l], b_group[l], w_router[l], b_router[l],
                               w_expert_gate[l], w_expert_up[l], w_expert_down[l])
        x = layer_norm(DEEPNORM_ALPHA * h + ffn, ln2_gain[l], ln2_bias[l])
    return x
```

```python
import functools
import math

import jax
import jax.numpy as jnp
import numpy as np
from jax import lax
from jax.experimental import pallas as pl
from jax.experimental.pallas import tpu as pltpu

F32 = jnp.float32
BF16 = jnp.bfloat16

CHUNK = 64
RET_HEADS = 8
RET_QK_DIM = 128
RET_V_DIM = 256
RET_THETA = 10000.0
DIFF_HEADS = 8
DIFF_QK_DIM = 128
DIFF_V_DIM = 256
ROPE_THETA = 500000.0
ROPE_DIM = DIFF_QK_DIM // 4
N_GROUPS = 4
EXPERTS_PER_GROUP = 8
N_EXPERTS = N_GROUPS * EXPERTS_PER_GROUP
TOP_K = 2
LN_EPS = 1e-5

LANES = 128
SUBLANES = 8
NEG_BIG = -1e30
TOKEN_BITS = 14

VMEM_LIMIT_BYTES = 56 * 1024 * 1024
TABLE_ROWS = 512
PROJ_TM = 1024
PROJ_TN = 1024
RET_BLOCK = 256
ATT_BLOCK = 256
MERGE_TM = 256
ROUTE_TM = 256
MOE_TM = 256
FINAL_TM = 256


def _cparams(n_axes):
    return pltpu.CompilerParams(
        dimension_semantics=("arbitrary",) * n_axes, vmem_limit_bytes=VMEM_LIMIT_BYTES)


def _sigmoid(x):
    return 1.0 / (1.0 + jnp.exp(-x))


def _chunk_of(idx):
    return jnp.right_shift(idx, CHUNK.bit_length() - 1)


def _dot(a, b):
    return jnp.dot(a, b, preferred_element_type=F32)


def _dot_nt(a, b):
    return lax.dot_general(a, b, (((1,), (1,)), ((), ())), preferred_element_type=F32)


def _table_kernel(pos_ref, freq_ref, sa_ref, sb_ref, cos_ref, sina_ref, sinb_ref):
    ang = pos_ref[...].astype(F32) * freq_ref[...]
    s = jnp.sin(ang)
    cos_ref[...] = jnp.cos(ang)
    sina_ref[...] = s * sa_ref[...]
    sinb_ref[...] = s * sb_ref[...]


def _rope_tables(pos_col, freq_row, sign_a, sign_b):
    n = pos_col.shape[0]
    tr = min(TABLE_ROWS, n)
    row = pl.BlockSpec((1, LANES), lambda i: (0, 0))
    out = pl.BlockSpec((tr, LANES), lambda i: (i, 0))
    shp = jax.ShapeDtypeStruct((n, LANES), F32)
    return pl.pallas_call(
        _table_kernel, grid=(n // tr,),
        in_specs=[pl.BlockSpec((tr, 1), lambda i: (i, 0)), row, row, row],
        out_specs=[out, out, out], out_shape=[shp, shp, shp],
        compiler_params=_cparams(1), name="rope_tables",
    )(pos_col, freq_row, sign_a, sign_b)


def _matmul_kernel(x_ref, w_ref, o_ref):
    o_ref[...] = _dot(x_ref[...], w_ref[...]).astype(o_ref.dtype)


def _matmul(x, w, out_dtype, name):
    m, k = x.shape
    n = w.shape[1]
    tm, tn = min(PROJ_TM, m), min(PROJ_TN, n)
    return pl.pallas_call(
        _matmul_kernel, grid=(n // tn, m // tm),
        in_specs=[pl.BlockSpec((tm, k), lambda j, i: (i, 0)),
                  pl.BlockSpec((k, tn), lambda j, i: (0, j))],
        out_specs=pl.BlockSpec((tm, tn), lambda j, i: (i, j)),
        out_shape=jax.ShapeDtypeStruct((m, n), out_dtype),
        compiler_params=_cparams(2), name=name,
    )(x, w)


def _retention_kernel(lg_ref, bd_ref, q_ref, k_ref, v_ref, g_ref, cos_ref, sin_ref, gain_ref,
                      o_ref, state_ref):
    h = pl.program_id(1)
    t = q_ref.shape[0]

    @pl.when(pl.program_id(2) == 0)
    def _():
        state_ref[...] = jnp.zeros_like(state_ref)

    lg = lg_ref[h]
    cos, sin = cos_ref[...], sin_ref[...]

    def rot(x):
        return x * cos + pltpu.roll(x, RET_QK_DIM // 2, 1) * sin

    q = rot(q_ref[...].astype(F32))
    k = rot(k_ref[...].astype(F32)) * (RET_QK_DIM ** -0.5)
    v = v_ref[...]

    ri = lax.broadcasted_iota(jnp.int32, (t, t), 0)
    ci = lax.broadcasted_iota(jnp.int32, (t, t), 1)
    decay = jnp.exp(lg * jnp.abs(ri - ci).astype(F32))
    visible = _chunk_of(ri) >= _chunk_of(ci)
    s = _dot_nt(q.astype(BF16), k.astype(BF16)) * jnp.where(visible, decay, 0.0)
    intra = _dot(s.astype(BF16), v)

    row = lax.broadcasted_iota(jnp.int32, (t, RET_QK_DIM), 0).astype(F32)
    state = state_ref[...]
    cross = _dot((q * jnp.exp(lg * (row + 1.0))).astype(BF16), state.astype(BF16))
    kd = k * jnp.exp(lg * (t - 1.0 - row))
    state_ref[...] = state * bd_ref[h] + _dot(kd.T.astype(BF16), v)

    out = intra + cross
    mu = jnp.mean(out, axis=-1, keepdims=True)
    var = jnp.mean(jnp.square(out - mu), axis=-1, keepdims=True)
    y = (out - mu) * lax.rsqrt(var + LN_EPS) * gain_ref[...]
    gate = g_ref[...].astype(F32)
    o_ref[...] = (y * (gate * _sigmoid(gate))).astype(o_ref.dtype)


def _retention(u, cos, sin, log_gamma, block_decay, gain, batch, seq):
    t = min(RET_BLOCK, seq)
    nb = seq // t
    rows = lambda b, h, i: b * nb + i
    qk_off = (RET_HEADS * RET_QK_DIM) // RET_QK_DIM
    v_off = (2 * RET_HEADS * RET_QK_DIM) // RET_V_DIM
    g_off = v_off + RET_HEADS
    smem = pl.BlockSpec(memory_space=pltpu.SMEM)
    return pl.pallas_call(
        _retention_kernel, grid=(batch, RET_HEADS, nb),
        in_specs=[
            smem, smem,
            pl.BlockSpec((t, RET_QK_DIM), lambda b, h, i: (rows(b, h, i), h)),
            pl.BlockSpec((t, RET_QK_DIM), lambda b, h, i: (rows(b, h, i), qk_off + h)),
            pl.BlockSpec((t, RET_V_DIM), lambda b, h, i: (rows(b, h, i), v_off + h)),
            pl.BlockSpec((t, RET_V_DIM), lambda b, h, i: (rows(b, h, i), g_off + h)),
            pl.BlockSpec((t, LANES), lambda b, h, i: (rows(b, h, i), 0)),
            pl.BlockSpec((t, LANES), lambda b, h, i: (rows(b, h, i), 0)),
            pl.BlockSpec((None, 1, RET_V_DIM), lambda b, h, i: (h, 0, 0)),
        ],
        out_specs=pl.BlockSpec((t, RET_V_DIM), lambda b, h, i: (rows(b, h, i), h)),
        out_shape=jax.ShapeDtypeStruct((batch * seq, RET_HEADS * RET_V_DIM), BF16),
        scratch_shapes=[pltpu.VMEM((RET_QK_DIM, RET_V_DIM), F32)],
        compiler_params=_cparams(3), name="retention",
    )(log_gamma, block_decay, u, u, u, u, cos, sin, gain)


def _partial_rope(x, cos, sin_lo, sin_hi):
    half = ROPE_DIM // 2
    return (x * cos + pltpu.roll(x, DIFF_QK_DIM - half, 1) * sin_lo
            + pltpu.roll(x, half, 1) * sin_hi)


def _diff_attn_kernel(lam_ref, q_ref, k_ref, v_ref, qc_ref, qa_ref, qb_ref, kc_ref, ka_ref, kb_ref,
                      gain_ref, o_ref, krot_ref, m_ref, l_ref, acc_ref, *, lambda_init):
    qblk = pl.program_id(2)
    tq = q_ref.shape[0]
    d = DIFF_QK_DIM

    @pl.when(qblk == 0)
    def _():
        def rope_rows(c, carry):
            rows = pl.ds(pl.multiple_of(c * tq, tq), tq)
            for i in range(2):
                x = k_ref[rows, i * d:(i + 1) * d].astype(F32)
                krot_ref[rows, i * d:(i + 1) * d] = _partial_rope(
                    x, kc_ref[rows, :], ka_ref[rows, :], kb_ref[rows, :]).astype(BF16)
            return carry
        lax.fori_loop(0, k_ref.shape[0] // tq, rope_rows, 0)

    scale = d ** -0.5
    qs = []
    for i in range(2):
        x = q_ref[:, i * d:(i + 1) * d].astype(F32)
        qs.append((_partial_rope(x, qc_ref[...], qa_ref[...], qb_ref[...]) * scale).astype(BF16))

    m_ref[...] = jnp.full_like(m_ref, NEG_BIG)
    l_ref[...] = jnp.zeros_like(l_ref)
    acc_ref[...] = jnp.zeros_like(acc_ref)

    def kv_step(j, diagonal):
        rows = pl.ds(pl.multiple_of(j * tq, tq), tq)
        vj = v_ref[rows, :]
        for i in range(2):
            s = _dot_nt(qs[i], krot_ref[rows, i * d:(i + 1) * d])
            if diagonal:
                ri = lax.broadcasted_iota(jnp.int32, s.shape, 0)
                ci = lax.broadcasted_iota(jnp.int32, s.shape, 1)
                s = jnp.where(_chunk_of(ri) >= _chunk_of(ci), s, NEG_BIG)
            m_old = m_ref[i]
            m_new = jnp.maximum(m_old, jnp.max(s, axis=-1, keepdims=True))
            alpha = jnp.exp(m_old - m_new)
            p = jnp.exp(s - m_new)
            l_ref[i] = alpha * l_ref[i] + jnp.sum(p, axis=-1, keepdims=True)
            acc_ref[i] = alpha * acc_ref[i] + _dot(p.astype(BF16), vj)
            m_ref[i] = m_new

    def body(j, carry):
        kv_step(j, False)
        return carry
    lax.fori_loop(0, qblk, body, 0)
    kv_step(qblk, True)

    lp = lam_ref[...]
    lam = (jnp.exp(jnp.sum(lp[0:1] * lp[1:2], axis=-1, keepdims=True))
           - jnp.exp(jnp.sum(lp[2:3] * lp[3:4], axis=-1, keepdims=True)) + lambda_init)
    o = acc_ref[0] / l_ref[0] - lam * (acc_ref[1] / l_ref[1])
    o = o * lax.rsqrt(jnp.mean(jnp.square(o), axis=-1, keepdims=True) + LN_EPS) * gain_ref[...]
    o_ref[...] = (o * (1.0 - lambda_init)).astype(o_ref.dtype)


def _diff_attention(u, cos, sin_lo, sin_hi, lam_params, gain, batch, seq, lambda_init):
    tq = min(ATT_BLOCK, seq)
    nb = seq // tq
    width = 2 * DIFF_QK_DIM
    ret_cols = 2 * RET_HEADS * RET_QK_DIM + 2 * RET_HEADS * RET_V_DIM
    q_off = ret_cols // width
    k_off = q_off + DIFF_HEADS
    v_off = k_off + DIFF_HEADS
    qrow = lambda b, h, i: b * nb + i
    qtab = pl.BlockSpec((tq, LANES), lambda b, h, i: (qrow(b, h, i), 0))
    ktab = pl.BlockSpec((seq, LANES), lambda b, h, i: (b, 0))
    return pl.pallas_call(
        functools.partial(_diff_attn_kernel, lambda_init=lambda_init),
        grid=(batch, DIFF_HEADS, nb),
        in_specs=[
            pl.BlockSpec((4, DIFF_QK_DIM), lambda b, h, i: (0, 0)),
            pl.BlockSpec((tq, width), lambda b, h, i: (qrow(b, h, i), q_off + h)),
            pl.BlockSpec((seq, width), lambda b, h, i: (b, k_off + h)),
            pl.BlockSpec((seq, DIFF_V_DIM), lambda b, h, i: (b, v_off + h)),
            qtab, qtab, qtab, ktab, ktab, ktab,
            pl.BlockSpec((1, DIFF_V_DIM), lambda b, h, i: (0, 0)),
        ],
        out_specs=pl.BlockSpec((tq, DIFF_V_DIM), lambda b, h, i: (qrow(b, h, i), h)),
        out_shape=jax.ShapeDtypeStruct((batch * seq, DIFF_HEADS * DIFF_V_DIM), BF16),
        scratch_shapes=[pltpu.VMEM((seq, width), BF16),
                        pltpu.VMEM((2, tq, 1), F32), pltpu.VMEM((2, tq, 1), F32),
                        pltpu.VMEM((2, tq, DIFF_V_DIM), F32)],
        compiler_params=_cparams(3), name="diff_attention",
    )(lam_params, u, u, u, cos, sin_lo, sin_hi, cos, sin_lo, sin_hi, gain)


def _merge_kernel(ret_ref, diff_ref, gr_ref, gd_ref, wr_ref, wd_ref, o_ref):
    a = _dot(ret_ref[...], wr_ref[...])
    b = _dot(diff_ref[...], wd_ref[...])
    merged = _sigmoid(gr_ref[...].astype(F32)) * a + _sigmoid(gd_ref[...].astype(F32)) * b
    o_ref[...] = merged.astype(o_ref.dtype)


def _merge(ret, diff, u, w_ret, w_diff):
    n, dm = ret.shape[0], w_ret.shape[1]
    tm = min(MERGE_TM, n)
    gate_off = (u.shape[1] - 2 * dm) // dm
    tile = lambda c: pl.BlockSpec((tm, dm), lambda i: (i, c))
    weight = lambda w: pl.BlockSpec(w.shape, lambda i: (0, 0))
    return pl.pallas_call(
        _merge_kernel, grid=(n // tm,),
        in_specs=[pl.BlockSpec((tm, ret.shape[1]), lambda i: (i, 0)),
                  pl.BlockSpec((tm, diff.shape[1]), lambda i: (i, 0)),
                  tile(gate_off), tile(gate_off + 1), weight(w_ret), weight(w_diff)],
        out_specs=tile(0), out_shape=jax.ShapeDtypeStruct((n, dm), BF16),
        compiler_params=_cparams(1), name="gated_merge",
    )(ret, diff, u, u, w_ret, w_diff)


def _layer_norm(z, gain, bias):
    mu = jnp.mean(z, axis=-1, keepdims=True)
    var = jnp.mean(jnp.square(z - mu), axis=-1, keepdims=True)
    return (z - mu) * lax.rsqrt(var + LN_EPS) * gain + bias


def _split_bf16(a):
    hi = a.astype(BF16)
    return hi, (a - hi.astype(F32)).astype(BF16)


def _route_kernel(m_ref, x_ref, wo_ref, gain_ref, bias_ref, wr_ref, br_ref, h_ref, r_ref, *, alpha):
    h = _layer_norm(alpha * x_ref[...] + _dot(m_ref[...], wo_ref[...]), gain_ref[...], bias_ref[...])
    h_ref[...] = h

    h_hi, h_lo = _split_bf16(h)
    w_hi, w_lo = _split_bf16(wr_ref[...])
    logits = _dot(h_hi, w_hi) + (_dot(h_hi, w_lo) + _dot(h_lo, w_hi)) + br_ref[...]

    lane = lax.broadcasted_iota(jnp.int32, logits.shape, 1).astype(F32)

    def first_argmax(vals):
        top = jnp.max(vals, axis=-1, keepdims=True)
        idx = jnp.min(jnp.where(vals == top, lane, float(LANES)), axis=-1, keepdims=True)
        return top, idx

    group_logits = jnp.where(lane < N_GROUPS, logits, NEG_BIG)
    g_top, g_idx = first_argmax(group_logits)
    p_group = 1.0 / jnp.sum(jnp.exp(group_logits - g_top), axis=-1, keepdims=True)

    first = N_GROUPS + g_idx * EXPERTS_PER_GROUP
    within = jnp.where((lane >= first) & (lane < first + EXPERTS_PER_GROUP), logits, NEG_BIG)
    v1, i1 = first_argmax(within)
    v2, i2 = first_argmax(jnp.where(lane == i1, NEG_BIG, within))
    e2 = jnp.exp(v2 - v1)
    w1 = p_group / (1.0 + e2)
    w2 = p_group * e2 / (1.0 + e2)
    r_ref[...] = jnp.where(lane == 0, i1 - N_GROUPS, jnp.where(lane == 1, i2 - N_GROUPS,
                           jnp.where(lane == 2, w1, jnp.where(lane == 3, w2, 0.0))))


def _route(merged, x, w_out, gain, bias, w_router, b_router, alpha):
    n, dm = x.shape
    tm = min(ROUTE_TM, n)
    tile = pl.BlockSpec((tm, dm), lambda i: (i, 0))
    full = lambda a: pl.BlockSpec(a.shape, lambda i: (0, 0))
    return pl.pallas_call(
        functools.partial(_route_kernel, alpha=alpha), grid=(n // tm,),
        in_specs=[tile, tile, full(w_out), full(gain), full(bias), full(w_router), full(b_router)],
        out_specs=[tile, pl.BlockSpec((tm, LANES), lambda i: (i, 0))],
        out_shape=[jax.ShapeDtypeStruct((n, dm), F32), jax.ShapeDtypeStruct((n, LANES), F32)],
        compiler_params=_cparams(1), name="outproj_ln_router",
    )(merged, x, w_out, gain, bias, w_router, b_router)


def _expert_kernel(tile_expert_ref, tile_rows_ref, packed_ref, n_valid_ref, h_hbm, wg_ref, wu_ref,
                   wd_ref, y_hbm, xbuf, ybuf, gather_sem, scatter_sem):
    del tile_expert_ref
    i = pl.program_id(0)
    n_valid = n_valid_ref[0]
    tm = xbuf.shape[1]
    slot = i & 1

    def start_gather(tile, buf):
        def issue(r, carry):
            tok = packed_ref[tile * tm + r] & ((1 << TOKEN_BITS) - 1)
            pltpu.make_async_copy(h_hbm.at[pl.ds(tok, 1)], xbuf.at[buf, pl.ds(r, 1)],
                                  gather_sem.at[buf]).start()
            return carry
        lax.fori_loop(0, tm, issue, 0, unroll=8)

    def wait_gather(buf):
        pltpu.make_async_copy(h_hbm.at[pl.ds(0, tm)], xbuf.at[buf], gather_sem.at[buf]).wait()

    def start_scatter(tile, buf):
        def issue(r, carry):
            dst = packed_ref[tile * tm + r] >> TOKEN_BITS
            pltpu.make_async_copy(ybuf.at[buf, pl.ds(r, 1)], y_hbm.at[pl.ds(dst, 1)],
                                  scatter_sem.at[buf]).start()
            return carry
        lax.fori_loop(0, tile_rows_ref[tile], issue, 0)

    def wait_scatter(tile):
        buf = tile & 1
        n_rows = tile_rows_ref[tile]
        n_aligned = pl.multiple_of(n_rows - (n_rows & (SUBLANES - 1)), SUBLANES)

        @pl.when(n_aligned > 0)
        def _():
            rows = pl.ds(0, n_aligned)
            pltpu.make_async_copy(ybuf.at[buf, rows], y_hbm.at[rows], scatter_sem.at[buf]).wait()

        def wait_row(r, carry):
            pltpu.make_async_copy(ybuf.at[buf, pl.ds(r, 1)], y_hbm.at[pl.ds(r, 1)],
                                  scatter_sem.at[buf]).wait()
            return carry
        lax.fori_loop(n_aligned, n_rows, wait_row, 0)

    @pl.when(i == 0)
    def _():
        start_gather(0, 0)

    @pl.when(i + 1 < n_valid)
    def _():
        start_gather(i + 1, 1 - slot)

    @pl.when(i < n_valid)
    def _():
        wait_gather(slot)

        @pl.when(i >= 2)
        def _():
            wait_scatter(i - 2)

        x = xbuf[slot].astype(BF16)
        g = _dot(x, wg_ref[...])
        act = (g * _sigmoid(g) * _dot(x, wu_ref[...])).astype(BF16)
        ybuf[slot] = _dot(act, wd_ref[...])
        start_scatter(i, slot)

    @pl.when(i == pl.num_programs(0) - 1)
    def _():
        wait_scatter(n_valid - 1)

        @pl.when(n_valid >= 2)
        def _():
            wait_scatter(n_valid - 2)


def _experts(h, tile_expert, tile_rows, packed, n_valid, w_gate, w_up, w_down, n_rows_out):
    n, dm = h.shape
    n_tiles = tile_expert.shape[0]
    tm = packed.shape[0] // n_tiles
    ff = w_gate.shape[2]
    wspec = lambda a, b: pl.BlockSpec((None, a, b), lambda i, te, tr, pk, nv: (te[i], 0, 0))
    grid_spec = pltpu.PrefetchScalarGridSpec(
        num_scalar_prefetch=4, grid=(n_tiles,),
        in_specs=[pl.BlockSpec(memory_space=pl.ANY), wspec(dm, ff), wspec(dm, ff), wspec(ff, dm)],
        out_specs=pl.BlockSpec(memory_space=pl.ANY),
        scratch_shapes=[pltpu.VMEM((2, tm, dm), F32), pltpu.VMEM((2, tm, dm), F32),
                        pltpu.SemaphoreType.DMA((2,)), pltpu.SemaphoreType.DMA((2,))])
    return pl.pallas_call(
        _expert_kernel, grid_spec=grid_spec,
        out_shape=jax.ShapeDtypeStruct((n_rows_out, dm), F32),
        compiler_params=_cparams(1), name="experts",
    )(tile_expert, tile_rows, packed, n_valid, h, w_gate, w_up, w_down)


def _dispatch_tables(expert_ids, tm):
    n_assign = expert_ids.shape[0]
    n_tiles = n_assign // tm + N_EXPERTS
    n_slots = n_tiles * tm
    onehot = (expert_ids[:, None] == jnp.arange(N_EXPERTS, dtype=jnp.int32)[None, :]).astype(jnp.int32)
    running = jnp.cumsum(onehot, axis=0)
    rank = jnp.take_along_axis(running, expert_ids[:, None], axis=1)[:, 0] - 1
    counts = running[-1]
    padded = (counts + tm - 1) // tm * tm
    pad_end = jnp.cumsum(padded)
    slot_of = (pad_end - padded)[expert_ids] + rank
    tile_expert = jnp.minimum(
        jnp.searchsorted(pad_end, jnp.arange(n_tiles, dtype=jnp.int32) * tm, side="right"),
        N_EXPERTS - 1).astype(jnp.int32)
    assign = jnp.full((n_slots,), -1, jnp.int32).at[slot_of].set(jnp.arange(n_assign, dtype=jnp.int32))
    is_pad = assign < 0
    tile_rows = jnp.sum((~is_pad).reshape(n_tiles, tm).astype(jnp.int32), axis=1)
    real = jnp.maximum(assign, 0)
    packed = (real << TOKEN_BITS) | (real // TOP_K)
    n_valid = (pad_end[-1] // tm).astype(jnp.int32).reshape(1)
    return tile_expert, tile_rows, packed, n_valid


def _final_kernel(h_ref, y_ref, r_ref, gain_ref, bias_ref, o_ref, *, alpha):
    dm = h_ref.shape[1]
    r = r_ref[...]
    ffn = y_ref[:, :dm] * r[:, 2:3] + y_ref[:, dm:] * r[:, 3:4]
    o_ref[...] = _layer_norm(alpha * h_ref[...] + ffn, gain_ref[...], bias_ref[...])


def _final(h, y_pairs, routing, gain, bias, alpha):
    n, dm = h.shape
    tm = min(FINAL_TM, n)
    tile = pl.BlockSpec((tm, dm), lambda i: (i, 0))
    full = lambda a: pl.BlockSpec(a.shape, lambda i: (0, 0))
    return pl.pallas_call(
        functools.partial(_final_kernel, alpha=alpha), grid=(n // tm,),
        in_specs=[tile, pl.BlockSpec((tm, TOP_K * dm), lambda i: (i, 0)),
                  pl.BlockSpec((tm, LANES), lambda i: (i, 0)), full(gain), full(bias)],
        out_specs=tile, out_shape=jax.ShapeDtypeStruct((n, dm), F32),
        compiler_params=_cparams(1), name="combine_ln",
    )(h, y_pairs, routing, gain, bias)


def _lane_row(values):
    row = np.zeros((1, LANES), np.float32)
    row[0, :len(values)] = values
    return jnp.asarray(row)


def _rope_constants():
    ret_half = RET_QK_DIM // 2
    ret_freq = 1.0 / (RET_THETA ** (jnp.arange(0, RET_QK_DIM, 2, dtype=F32) / RET_QK_DIM))
    ret_freq = jnp.concatenate([ret_freq, ret_freq]).reshape(1, LANES)
    ret_sign = _lane_row([-1.0] * ret_half + [1.0] * ret_half)
    half = ROPE_DIM // 2
    diff_freq = 1.0 / (ROPE_THETA ** (jnp.arange(0, ROPE_DIM, 2, dtype=F32) / ROPE_DIM))
    diff_freq = jnp.concatenate([diff_freq, diff_freq, jnp.zeros((LANES - ROPE_DIM,), F32)]).reshape(1, LANES)
    diff_lo = _lane_row([-1.0] * half)
    diff_hi = _lane_row([0.0] * half + [1.0] * half)
    return ret_freq, ret_sign, diff_freq, diff_lo, diff_hi


def kernel(x, positions, w_in, ret_gn_gain, diff_lambda_q1, diff_lambda_k1, diff_lambda_q2, diff_lambda_k2, diff_subln_gain, w_ret_proj, w_diff_proj, w_out, ln1_gain, ln1_bias, w_group, b_group, w_router, b_router, w_expert_gate, w_expert_up, w_expert_down, ln2_gain, ln2_bias):
    batch, seq, dm = x.shape
    n = batch * seq
    depth = w_in.shape[0]
    assert n <= (1 << TOKEN_BITS) and seq % CHUNK == 0
    alpha = (2 * depth) ** 0.25

    ret_freq, ret_sign, diff_freq, diff_lo, diff_hi = _rope_constants()
    pos_col = positions.reshape(n, 1)
    ret_cos, ret_sin, _ = _rope_tables(pos_col, ret_freq, ret_sign, ret_sign)
    diff_cos, diff_sin_lo, diff_sin_hi = _rope_tables(pos_col, diff_freq, diff_lo, diff_hi)

    log_gamma = jnp.log(1.0 - 2.0 ** (-5.0 - jnp.arange(RET_HEADS, dtype=F32)))
    block_decay = jnp.exp(log_gamma * min(RET_BLOCK, seq))

    xt = x.reshape(n, dm)
    for l in range(depth):
        lambda_init = 0.8 - 0.6 * math.exp(-0.3 * l)
        u = _matmul(xt.astype(BF16), w_in[l].astype(BF16), BF16, "in_proj")
        ret = _retention(u, ret_cos, ret_sin, log_gamma, block_decay,
                         ret_gn_gain[l].reshape(RET_HEADS, 1, RET_V_DIM), batch, seq)
        lam_params = jnp.stack([diff_lambda_q1[l], diff_lambda_k1[l], diff_lambda_q2[l], diff_lambda_k2[l]])
        diff = _diff_attention(u, diff_cos, diff_sin_lo, diff_sin_hi, lam_params.astype(F32),
                               diff_subln_gain[l].reshape(1, DIFF_V_DIM), batch, seq, lambda_init)
        merged = _merge(ret, diff, u, w_ret_proj[l].astype(BF16), w_diff_proj[l].astype(BF16))

        w_route = jnp.concatenate(
            [w_group[l], w_router[l], jnp.zeros((dm, LANES - N_GROUPS - N_EXPERTS), F32)], axis=1)
        b_route = jnp.concatenate(
            [b_group[l], b_router[l], jnp.zeros((LANES - N_GROUPS - N_EXPERTS,), F32)]).reshape(1, LANES)
        h, routing = _route(merged, xt, w_out[l].astype(BF16), ln1_gain[l].reshape(1, dm),
                            ln1_bias[l].reshape(1, dm), w_route, b_route, alpha)

        expert_ids = routing[:, :TOP_K].astype(jnp.int32).reshape(-1)
        tile_expert, tile_rows, packed, n_valid = _dispatch_tables(expert_ids, min(MOE_TM, n))
        y_rows = _experts(h, tile_expert, tile_rows, packed, n_valid, w_expert_gate[l].astype(BF16),
                          w_expert_up[l].astype(BF16), w_expert_down[l].astype(BF16), TOP_K * n)
        xt = _final(h, y_rows.reshape(n, TOP_K * dm), routing,
                    ln2_gain[l].reshape(1, dm), ln2_bias[l].reshape(1, dm), alpha)
    return xt.reshape(batch, seq, dm)
```

```python
import functools
import math

import jax
import jax.numpy as jnp
import numpy as np
from jax import lax
from jax.experimental import pallas as pl
from jax.experimental.pallas import tpu as pltpu

F32 = jnp.float32
BF16 = jnp.bfloat16

CHUNK = 64
RET_HEADS = 8
RET_QK_DIM = 128
RET_V_DIM = 256
RET_THETA = 10000.0
DIFF_HEADS = 8
DIFF_QK_DIM = 128
DIFF_V_DIM = 256
ROPE_THETA = 500000.0
ROPE_DIM = DIFF_QK_DIM // 4
N_GROUPS = 4
EXPERTS_PER_GROUP = 8
N_EXPERTS = N_GROUPS * EXPERTS_PER_GROUP
TOP_K = 2
LN_EPS = 1e-5

LANES = 128
SUBLANES = 8
NEG_BIG = -1e30
TOKEN_BITS = 14

VMEM_LIMIT_BYTES = 56 * 1024 * 1024
TABLE_ROWS = 512
PROJ_TM = 1024
PROJ_TN = 1024
RET_BLOCK = 256
ATT_BLOCK = 512
MERGE_TM = 256
ROUTE_TM = 256
MOE_TM = 256
FINAL_TM = 256


def _cparams(n_axes):
    return pltpu.CompilerParams(
        dimension_semantics=("arbitrary",) * n_axes, vmem_limit_bytes=VMEM_LIMIT_BYTES)


def _sigmoid(x):
    return 1.0 / (1.0 + jnp.exp(-x))


def _chunk_of(idx):
    return jnp.right_shift(idx, CHUNK.bit_length() - 1)


def _dot(a, b):
    return jnp.dot(a, b, preferred_element_type=F32)


def _dot_nt(a, b):
    return lax.dot_general(a, b, (((1,), (1,)), ((), ())), preferred_element_type=F32)


def _table_kernel(pos_ref, freq_ref, sa_ref, sb_ref, cos_ref, sina_ref, sinb_ref):
    ang = pos_ref[...].astype(F32) * freq_ref[...]
    s = jnp.sin(ang)
    cos_ref[...] = jnp.cos(ang)
    sina_ref[...] = s * sa_ref[...]
    sinb_ref[...] = s * sb_ref[...]


def _rope_tables(pos_col, freq_row, sign_a, sign_b):
    n = pos_col.shape[0]
    tr = min(TABLE_ROWS, n)
    row = pl.BlockSpec((1, LANES), lambda i: (0, 0))
    out = pl.BlockSpec((tr, LANES), lambda i: (i, 0))
    shp = jax.ShapeDtypeStruct((n, LANES), F32)
    return pl.pallas_call(
        _table_kernel, grid=(n // tr,),
        in_specs=[pl.BlockSpec((tr, 1), lambda i: (i, 0)), row, row, row],
        out_specs=[out, out, out], out_shape=[shp, shp, shp],
        compiler_params=_cparams(1), name="rope_tables",
    )(pos_col, freq_row, sign_a, sign_b)


def _matmul_kernel(x_ref, w_ref, o_ref):
    o_ref[...] = _dot(x_ref[...], w_ref[...]).astype(o_ref.dtype)


def _matmul(x, w, out_dtype, name):
    m, k = x.shape
    n = w.shape[1]
    tm, tn = min(PROJ_TM, m), min(PROJ_TN, n)
    return pl.pallas_call(
        _matmul_kernel, grid=(n // tn, m // tm),
        in_specs=[pl.BlockSpec((tm, k), lambda j, i: (i, 0)),
                  pl.BlockSpec((k, tn), lambda j, i: (0, j))],
        out_specs=pl.BlockSpec((tm, tn), lambda j, i: (i, j)),
        out_shape=jax.ShapeDtypeStruct((m, n), out_dtype),
        compiler_params=_cparams(2), name=name,
    )(x, w)


def _retention_kernel(lg_ref, bd_ref, q_ref, k_ref, v_ref, g_ref, cos_ref, sin_ref, gain_ref,
                      o_ref, state_ref):
    h = pl.program_id(1)
    t = q_ref.shape[0]

    @pl.when(pl.program_id(2) == 0)
    def _():
        state_ref[...] = jnp.zeros_like(state_ref)

    lg = lg_ref[h]
    cos, sin = cos_ref[...], sin_ref[...]

    def rot(x):
        return x * cos + pltpu.roll(x, RET_QK_DIM // 2, 1) * sin

    q = rot(q_ref[...].astype(F32))
    k = rot(k_ref[...].astype(F32)) * (RET_QK_DIM ** -0.5)
    v = v_ref[...]

    ri = lax.broadcasted_iota(jnp.int32, (t, t), 0)
    ci = lax.broadcasted_iota(jnp.int32, (t, t), 1)
    decay = jnp.exp(lg * jnp.abs(ri - ci).astype(F32))
    visible = _chunk_of(ri) >= _chunk_of(ci)
    s = _dot_nt(q.astype(BF16), k.astype(BF16)) * jnp.where(visible, decay, 0.0)
    intra = _dot(s.astype(BF16), v)

    row = lax.broadcasted_iota(jnp.int32, (t, RET_QK_DIM), 0).astype(F32)
    state = state_ref[...]
    cross = _dot((q * jnp.exp(lg * (row + 1.0))).astype(BF16), state.astype(BF16))
    kd = k * jnp.exp(lg * (t - 1.0 - row))
    state_ref[...] = state * bd_ref[h] + _dot(kd.T.astype(BF16), v)

    out = intra + cross
    mu = jnp.mean(out, axis=-1, keepdims=True)
    var = jnp.mean(jnp.square(out - mu), axis=-1, keepdims=True)
    y = (out - mu) * lax.rsqrt(var + LN_EPS) * gain_ref[...]
    gate = g_ref[...].astype(F32)
    o_ref[...] = (y * (gate * _sigmoid(gate))).astype(o_ref.dtype)


def _retention(u, cos, sin, log_gamma, block_decay, gain, batch, seq):
    t = min(RET_BLOCK, seq)
    nb = seq // t
    rows = lambda b, h, i: b * nb + i
    qk_off = (RET_HEADS * RET_QK_DIM) // RET_QK_DIM
    v_off = (2 * RET_HEADS * RET_QK_DIM) // RET_V_DIM
    g_off = v_off + RET_HEADS
    smem = pl.BlockSpec(memory_space=pltpu.SMEM)
    return pl.pallas_call(
        _retention_kernel, grid=(batch, RET_HEADS, nb),
        in_specs=[
            smem, smem,
            pl.BlockSpec((t, RET_QK_DIM), lambda b, h, i: (rows(b, h, i), h)),
            pl.BlockSpec((t, RET_QK_DIM), lambda b, h, i: (rows(b, h, i), qk_off + h)),
            pl.BlockSpec((t, RET_V_DIM), lambda b, h, i: (rows(b, h, i), v_off + h)),
            pl.BlockSpec((t, RET_V_DIM), lambda b, h, i: (rows(b, h, i), g_off + h)),
            pl.BlockSpec((t, LANES), lambda b, h, i: (rows(b, h, i), 0)),
            pl.BlockSpec((t, LANES), lambda b, h, i: (rows(b, h, i), 0)),
            pl.BlockSpec((None, 1, RET_V_DIM), lambda b, h, i: (h, 0, 0)),
        ],
        out_specs=pl.BlockSpec((t, RET_V_DIM), lambda b, h, i: (rows(b, h, i), h)),
        out_shape=jax.ShapeDtypeStruct((batch * seq, RET_HEADS * RET_V_DIM), BF16),
        scratch_shapes=[pltpu.VMEM((RET_QK_DIM, RET_V_DIM), F32)],
        compiler_params=_cparams(3), name="retention",
    )(log_gamma, block_decay, u, u, u, u, cos, sin, gain)


def _partial_rope(x, cos, sin_lo, sin_hi):
    half = ROPE_DIM // 2
    return (x * cos + pltpu.roll(x, DIFF_QK_DIM - half, 1) * sin_lo
            + pltpu.roll(x, half, 1) * sin_hi)


def _diff_attn_kernel(lam_ref, q_ref, k_ref, v_ref, qc_ref, qa_ref, qb_ref, kc_ref, ka_ref, kb_ref,
                      gain_ref, o_ref, krot_ref, vt_ref, m_ref, l_ref, acc_ref, *, lambda_init):
    qblk = pl.program_id(2)
    tq = q_ref.shape[0]
    d = DIFF_QK_DIM

    @pl.when(qblk == 0)
    def _():
        def prep_rows(c, carry):
            rows = pl.ds(pl.multiple_of(c * tq, tq), tq)
            for i in range(2):
                x = k_ref[rows, i * d:(i + 1) * d].astype(F32)
                krot_ref[rows, i * d:(i + 1) * d] = _partial_rope(
                    x, kc_ref[rows, :], ka_ref[rows, :], kb_ref[rows, :]).astype(BF16)
            vt_ref[c] = v_ref[rows, :].astype(F32).T.astype(BF16)
            return carry
        lax.fori_loop(0, k_ref.shape[0] // tq, prep_rows, 0)

    scale = d ** -0.5
    qs = []
    for i in range(2):
        x = q_ref[:, i * d:(i + 1) * d].astype(F32)
        qs.append((_partial_rope(x, qc_ref[...], qa_ref[...], qb_ref[...]) * scale).astype(BF16))

    m_ref[...] = jnp.full_like(m_ref, NEG_BIG)
    l_ref[...] = jnp.zeros_like(l_ref)
    acc_ref[...] = jnp.zeros_like(acc_ref)

    def kv_step(j, diagonal):
        rows = pl.ds(pl.multiple_of(j * tq, tq), tq)
        vt = vt_ref[j]
        for i in range(2):
            s = _dot_nt(krot_ref[rows, i * d:(i + 1) * d], qs[i])
            if diagonal:
                key = lax.broadcasted_iota(jnp.int32, s.shape, 0)
                qry = lax.broadcasted_iota(jnp.int32, s.shape, 1)
                s = jnp.where(_chunk_of(qry) >= _chunk_of(key), s, NEG_BIG)
            m_old = m_ref[i]
            m_new = jnp.maximum(m_old, jnp.max(s, axis=0, keepdims=True))
            alpha = jnp.exp(m_old - m_new)
            p = jnp.exp(s - m_new)
            l_ref[i] = alpha * l_ref[i] + jnp.sum(p, axis=0, keepdims=True)
            acc_ref[i] = alpha * acc_ref[i] + _dot(vt, p.astype(BF16))
            m_ref[i] = m_new

    def body(j, carry):
        kv_step(j, False)
        return carry
    lax.fori_loop(0, qblk, body, 0)
    kv_step(qblk, True)

    lp = lam_ref[...]
    lam = (jnp.exp(jnp.sum(lp[0:1] * lp[1:2], axis=-1, keepdims=True))
           - jnp.exp(jnp.sum(lp[2:3] * lp[3:4], axis=-1, keepdims=True)) + lambda_init)
    o = (acc_ref[0] / l_ref[0] - lam * (acc_ref[1] / l_ref[1])).T
    o = o * lax.rsqrt(jnp.mean(jnp.square(o), axis=-1, keepdims=True) + LN_EPS) * gain_ref[...]
    o_ref[...] = (o * (1.0 - lambda_init)).astype(o_ref.dtype)


def _diff_attention(u, cos, sin_lo, sin_hi, lam_params, gain, batch, seq, lambda_init):
    tq = min(ATT_BLOCK, seq)
    nb = seq // tq
    width = 2 * DIFF_QK_DIM
    ret_cols = 2 * RET_HEADS * RET_QK_DIM + 2 * RET_HEADS * RET_V_DIM
    q_off = ret_cols // width
    k_off = q_off + DIFF_HEADS
    v_off = k_off + DIFF_HEADS
    qrow = lambda b, h, i: b * nb + i
    qtab = pl.BlockSpec((tq, LANES), lambda b, h, i: (qrow(b, h, i), 0))
    ktab = pl.BlockSpec((seq, LANES), lambda b, h, i: (b, 0))
    return pl.pallas_call(
        functools.partial(_diff_attn_kernel, lambda_init=lambda_init),
        grid=(batch, DIFF_HEADS, nb),
        in_specs=[
            pl.BlockSpec((4, DIFF_QK_DIM), lambda b, h, i: (0, 0)),
            pl.BlockSpec((tq, width), lambda b, h, i: (qrow(b, h, i), q_off + h)),
            pl.BlockSpec((seq, width), lambda b, h, i: (b, k_off + h)),
            pl.BlockSpec((seq, DIFF_V_DIM), lambda b, h, i: (b, v_off + h)),
            qtab, qtab, qtab, ktab, ktab, ktab,
            pl.BlockSpec((1, DIFF_V_DIM), lambda b, h, i: (0, 0)),
        ],
        out_specs=pl.BlockSpec((tq, DIFF_V_DIM), lambda b, h, i: (qrow(b, h, i), h)),
        out_shape=jax.ShapeDtypeStruct((batch * seq, DIFF_HEADS * DIFF_V_DIM), BF16),
        scratch_shapes=[pltpu.VMEM((seq, width), BF16), pltpu.VMEM((nb, DIFF_V_DIM, tq), BF16),
                        pltpu.VMEM((2, 1, tq), F32), pltpu.VMEM((2, 1, tq), F32),
                        pltpu.VMEM((2, DIFF_V_DIM, tq), F32)],
        compiler_params=_cparams(3), name="diff_attention",
    )(lam_params, u, u, u, cos, sin_lo, sin_hi, cos, sin_lo, sin_hi, gain)


def _merge_kernel(ret_ref, diff_ref, gr_ref, gd_ref, wr_ref, wd_ref, o_ref):
    a = _dot(ret_ref[...], wr_ref[...])
    b = _dot(diff_ref[...], wd_ref[...])
    merged = _sigmoid(gr_ref[...].astype(F32)) * a + _sigmoid(gd_ref[...].astype(F32)) * b
    o_ref[...] = merged.astype(o_ref.dtype)


def _merge(ret, diff, u, w_ret, w_diff):
    n, dm = ret.shape[0], w_ret.shape[1]
    tm = min(MERGE_TM, n)
    gate_off = (u.shape[1] - 2 * dm) // dm
    tile = lambda c: pl.BlockSpec((tm, dm), lambda i: (i, c))
    weight = lambda w: pl.BlockSpec(w.shape, lambda i: (0, 0))
    return pl.pallas_call(
        _merge_kernel, grid=(n // tm,),
        in_specs=[pl.BlockSpec((tm, ret.shape[1]), lambda i: (i, 0)),
                  pl.BlockSpec((tm, diff.shape[1]), lambda i: (i, 0)),
                  tile(gate_off), tile(gate_off + 1), weight(w_ret), weight(w_diff)],
        out_specs=tile(0), out_shape=jax.ShapeDtypeStruct((n, dm), BF16),
        compiler_params=_cparams(1), name="gated_merge",
    )(ret, diff, u, u, w_ret, w_diff)


def _layer_norm(z, gain, bias):
    mu = jnp.mean(z, axis=-1, keepdims=True)
    var = jnp.mean(jnp.square(z - mu), axis=-1, keepdims=True)
    return (z - mu) * lax.rsqrt(var + LN_EPS) * gain + bias


def _split_bf16(a):
    hi = a.astype(BF16)
    return hi, (a - hi.astype(F32)).astype(BF16)


def _route_kernel(m_ref, x_ref, wo_ref, gain_ref, bias_ref, wr_ref, br_ref, h_ref, r_ref, *, alpha):
    h = _layer_norm(alpha * x_ref[...] + _dot(m_ref[...], wo_ref[...]), gain_ref[...], bias_ref[...])
    h_ref[...] = h

    h_hi, h_lo = _split_bf16(h)
    w_hi, w_lo = _split_bf16(wr_ref[...])
    logits = _dot(h_hi, w_hi) + (_dot(h_hi, w_lo) + _dot(h_lo, w_hi)) + br_ref[...]

    lane = lax.broadcasted_iota(jnp.int32, logits.shape, 1).astype(F32)

    def first_argmax(vals):
        top = jnp.max(vals, axis=-1, keepdims=True)
        idx = jnp.min(jnp.where(vals == top, lane, float(LANES)), axis=-1, keepdims=True)
        return top, idx

    group_logits = jnp.where(lane < N_GROUPS, logits, NEG_BIG)
    g_top, g_idx = first_argmax(group_logits)
    p_group = 1.0 / jnp.sum(jnp.exp(group_logits - g_top), axis=-1, keepdims=True)

    first = N_GROUPS + g_idx * EXPERTS_PER_GROUP
    within = jnp.where((lane >= first) & (lane < first + EXPERTS_PER_GROUP), logits, NEG_BIG)
    v1, i1 = first_argmax(within)
    v2, i2 = first_argmax(jnp.where(lane == i1, NEG_BIG, within))
    e2 = jnp.exp(v2 - v1)
    w1 = p_group / (1.0 + e2)
    w2 = p_group * e2 / (1.0 + e2)
    r_ref[...] = jnp.where(lane == 0, i1 - N_GROUPS, jnp.where(lane == 1, i2 - N_GROUPS,
                           jnp.where(lane == 2, w1, jnp.where(lane == 3, w2, 0.0))))


def _route(merged, x, w_out, gain, bias, w_router, b_router, alpha):
    n, dm = x.shape
    tm = min(ROUTE_TM, n)
    tile = pl.BlockSpec((tm, dm), lambda i: (i, 0))
    full = lambda a: pl.BlockSpec(a.shape, lambda i: (0, 0))
    return pl.pallas_call(
        functools.partial(_route_kernel, alpha=alpha), grid=(n // tm,),
        in_specs=[tile, tile, full(w_out), full(gain), full(bias), full(w_router), full(b_router)],
        out_specs=[tile, pl.BlockSpec((tm, LANES), lambda i: (i, 0))],
        out_shape=[jax.ShapeDtypeStruct((n, dm), F32), jax.ShapeDtypeStruct((n, LANES), F32)],
        compiler_params=_cparams(1), name="outproj_ln_router",
    )(merged, x, w_out, gain, bias, w_router, b_router)


def _expert_kernel(tile_expert_ref, tile_rows_ref, packed_ref, n_valid_ref, h_hbm, wg_ref, wu_ref,
                   wd_ref, y_hbm, xbuf, ybuf, gather_sem, scatter_sem):
    del tile_expert_ref
    i = pl.program_id(0)
    n_valid = n_valid_ref[0]
    tm = xbuf.shape[1]
    slot = i & 1

    def start_gather(tile, buf):
        def issue(r, carry):
            tok = packed_ref[tile * tm + r] & ((1 << TOKEN_BITS) - 1)
            pltpu.make_async_copy(h_hbm.at[pl.ds(tok, 1)], xbuf.at[buf, pl.ds(r, 1)],
                                  gather_sem.at[buf]).start()
            return carry
        lax.fori_loop(0, tm, issue, 0, unroll=8)

    def wait_gather(buf):
        pltpu.make_async_copy(h_hbm.at[pl.ds(0, tm)], xbuf.at[buf], gather_sem.at[buf]).wait()

    def start_scatter(tile, buf):
        def issue(r, carry):
            dst = packed_ref[tile * tm + r] >> TOKEN_BITS
            pltpu.make_async_copy(ybuf.at[buf, pl.ds(r, 1)], y_hbm.at[pl.ds(dst, 1)],
                                  scatter_sem.at[buf]).start()
            return carry
        lax.fori_loop(0, tile_rows_ref[tile], issue, 0)

    def wait_scatter(tile):
        buf = tile & 1
        n_rows = tile_rows_ref[tile]
        n_aligned = pl.multiple_of(n_rows - (n_rows & (SUBLANES - 1)), SUBLANES)

        @pl.when(n_aligned > 0)
        def _():
            rows = pl.ds(0, n_aligned)
            pltpu.make_async_copy(ybuf.at[buf, rows], y_hbm.at[rows], scatter_sem.at[buf]).wait()

        def wait_row(r, carry):
            pltpu.make_async_copy(ybuf.at[buf, pl.ds(r, 1)], y_hbm.at[pl.ds(r, 1)],
                                  scatter_sem.at[buf]).wait()
            return carry
        lax.fori_loop(n_aligned, n_rows, wait_row, 0)

    @pl.when(i == 0)
    def _():
        start_gather(0, 0)

    @pl.when(i + 1 < n_valid)
    def _():
        start_gather(i + 1, 1 - slot)

    @pl.when(i < n_valid)
    def _():
        wait_gather(slot)

        @pl.when(i >= 2)
        def _():
            wait_scatter(i - 2)

        x = xbuf[slot].astype(BF16)
        g = _dot(x, wg_ref[...])
        act = (g * _sigmoid(g) * _dot(x, wu_ref[...])).astype(BF16)
        ybuf[slot] = _dot(act, wd_ref[...])
        start_scatter(i, slot)

    @pl.when(i == pl.num_programs(0) - 1)
    def _():
        wait_scatter(n_valid - 1)

        @pl.when(n_valid >= 2)
        def _():
            wait_scatter(n_valid - 2)


def _experts(h, tile_expert, tile_rows, packed, n_valid, w_gate, w_up, w_down, n_rows_out):
    n, dm = h.shape
    n_tiles = tile_expert.shape[0]
    tm = packed.shape[0] // n_tiles
    ff = w_gate.shape[2]
    wspec = lambda a, b: pl.BlockSpec((None, a, b), lambda i, te, tr, pk, nv: (te[i], 0, 0))
    grid_spec = pltpu.PrefetchScalarGridSpec(
        num_scalar_prefetch=4, grid=(n_tiles,),
        in_specs=[pl.BlockSpec(memory_space=pl.ANY), wspec(dm, ff), wspec(dm, ff), wspec(ff, dm)],
        out_specs=pl.BlockSpec(memory_space=pl.ANY),
        scratch_shapes=[pltpu.VMEM((2, tm, dm), F32), pltpu.VMEM((2, tm, dm), F32),
                        pltpu.SemaphoreType.DMA((2,)), pltpu.SemaphoreType.DMA((2,))])
    return pl.pallas_call(
        _expert_kernel, grid_spec=grid_spec,
        out_shape=jax.ShapeDtypeStruct((n_rows_out, dm), F32),
        compiler_params=_cparams(1), name="experts",
    )(tile_expert, tile_rows, packed, n_valid, h, w_gate, w_up, w_down)


def _dispatch_tables(expert_ids, tm):
    n_assign = expert_ids.shape[0]
    n_tiles = n_assign // tm + N_EXPERTS
    n_slots = n_tiles * tm
    onehot = (expert_ids[:, None] == jnp.arange(N_EXPERTS, dtype=jnp.int32)[None, :]).astype(jnp.int32)
    running = jnp.cumsum(onehot, axis=0)
    rank = jnp.take_along_axis(running, expert_ids[:, None], axis=1)[:, 0] - 1
    counts = running[-1]
    padded = (counts + tm - 1) // tm * tm
    pad_end = jnp.cumsum(padded)
    slot_of = (pad_end - padded)[expert_ids] + rank
    tile_expert = jnp.minimum(
        jnp.searchsorted(pad_end, jnp.arange(n_tiles, dtype=jnp.int32) * tm, side="right"),
        N_EXPERTS - 1).astype(jnp.int32)
    assign = jnp.full((n_slots,), -1, jnp.int32).at[slot_of].set(jnp.arange(n_assign, dtype=jnp.int32))
    is_pad = assign < 0
    tile_rows = jnp.sum((~is_pad).reshape(n_tiles, tm).astype(jnp.int32), axis=1)
    real = jnp.maximum(assign, 0)
    token, choice = real // TOP_K, real % TOP_K
    packed = ((choice * (n_assign // TOP_K) + token) << TOKEN_BITS) | token
    n_valid = (pad_end[-1] // tm).astype(jnp.int32).reshape(1)
    return tile_expert, tile_rows, packed, n_valid


def _final_kernel(h_ref, y1_ref, y2_ref, r_ref, gain_ref, bias_ref, o_ref, *, alpha):
    r = r_ref[...]
    ffn = y1_ref[...] * r[:, 2:3] + y2_ref[...] * r[:, 3:4]
    o_ref[...] = _layer_norm(alpha * h_ref[...] + ffn, gain_ref[...], bias_ref[...])


def _final(h, y_rows, routing, gain, bias, alpha):
    n, dm = h.shape
    tm = min(FINAL_TM, n)
    tile = pl.BlockSpec((tm, dm), lambda i: (i, 0))
    second = pl.BlockSpec((tm, dm), lambda i: (n // tm + i, 0))
    full = lambda a: pl.BlockSpec(a.shape, lambda i: (0, 0))
    return pl.pallas_call(
        functools.partial(_final_kernel, alpha=alpha), grid=(n // tm,),
        in_specs=[tile, tile, second, pl.BlockSpec((tm, LANES), lambda i: (i, 0)), full(gain), full(bias)],
        out_specs=tile, out_shape=jax.ShapeDtypeStruct((n, dm), F32),
        compiler_params=_cparams(1), name="combine_ln",
    )(h, y_rows, y_rows, routing, gain, bias)


def _lane_row(values):
    row = np.zeros((1, LANES), np.float32)
    row[0, :len(values)] = values
    return jnp.asarray(row)


def _rope_constants():
    ret_half = RET_QK_DIM // 2
    ret_freq = 1.0 / (RET_THETA ** (jnp.arange(0, RET_QK_DIM, 2, dtype=F32) / RET_QK_DIM))
    ret_freq = jnp.concatenate([ret_freq, ret_freq]).reshape(1, LANES)
    ret_sign = _lane_row([-1.0] * ret_half + [1.0] * ret_half)
    half = ROPE_DIM // 2
    diff_freq = 1.0 / (ROPE_THETA ** (jnp.arange(0, ROPE_DIM, 2, dtype=F32) / ROPE_DIM))
    diff_freq = jnp.concatenate([diff_freq, diff_freq, jnp.zeros((LANES - ROPE_DIM,), F32)]).reshape(1, LANES)
    diff_lo = _lane_row([-1.0] * half)
    diff_hi = _lane_row([0.0] * half + [1.0] * half)
    return ret_freq, ret_sign, diff_freq, diff_lo, diff_hi


def kernel(x, positions, w_in, ret_gn_gain, diff_lambda_q1, diff_lambda_k1, diff_lambda_q2, diff_lambda_k2, diff_subln_gain, w_ret_proj, w_diff_proj, w_out, ln1_gain, ln1_bias, w_group, b_group, w_router, b_router, w_expert_gate, w_expert_up, w_expert_down, ln2_gain, ln2_bias):
    batch, seq, dm = x.shape
    n = batch * seq
    depth = w_in.shape[0]
    assert n <= (1 << TOKEN_BITS) and seq % CHUNK == 0
    alpha = (2 * depth) ** 0.25

    ret_freq, ret_sign, diff_freq, diff_lo, diff_hi = _rope_constants()
    pos_col = positions.reshape(n, 1)
    ret_cos, ret_sin, _ = _rope_tables(pos_col, ret_freq, ret_sign, ret_sign)
    diff_cos, diff_sin_lo, diff_sin_hi = _rope_tables(pos_col, diff_freq, diff_lo, diff_hi)

    log_gamma = jnp.log(1.0 - 2.0 ** (-5.0 - jnp.arange(RET_HEADS, dtype=F32)))
    block_decay = jnp.exp(log_gamma * min(RET_BLOCK, seq))

    xt = x.reshape(n, dm)
    for l in range(depth):
        lambda_init = 0.8 - 0.6 * math.exp(-0.3 * l)
        u = _matmul(xt.astype(BF16), w_in[l].astype(BF16), BF16, "in_proj")
        ret = _retention(u, ret_cos, ret_sin, log_gamma, block_decay,
                         ret_gn_gain[l].reshape(RET_HEADS, 1, RET_V_DIM), batch, seq)
        lam_params = jnp.stack([diff_lambda_q1[l], diff_lambda_k1[l], diff_lambda_q2[l], diff_lambda_k2[l]])
        diff = _diff_attention(u, diff_cos, diff_sin_lo, diff_sin_hi, lam_params.astype(F32),
                               diff_subln_gain[l].reshape(1, DIFF_V_DIM), batch, seq, lambda_init)
        merged = _merge(ret, diff, u, w_ret_proj[l].astype(BF16), w_diff_proj[l].astype(BF16))

        w_route = jnp.concatenate(
            [w_group[l], w_router[l], jnp.zeros((dm, LANES - N_GROUPS - N_EXPERTS), F32)], axis=1)
        b_route = jnp.concatenate(
            [b_group[l], b_router[l], jnp.zeros((LANES - N_GROUPS - N_EXPERTS,), F32)]).reshape(1, LANES)
        h, routing = _route(merged, xt, w_out[l].astype(BF16), ln1_gain[l].reshape(1, dm),
                            ln1_bias[l].reshape(1, dm), w_route, b_route, alpha)

        expert_ids = routing[:, :TOP_K].astype(jnp.int32).reshape(-1)
        tile_expert, tile_rows, packed, n_valid = _dispatch_tables(expert_ids, min(MOE_TM, n))
        y_rows = _experts(h, tile_expert, tile_rows, packed, n_valid, w_expert_gate[l].astype(BF16),
                          w_expert_up[l].astype(BF16), w_expert_down[l].astype(BF16), TOP_K * n)
        xt = _final(h, y_rows, routing,
                    ln2_gain[l].reshape(1, dm), ln2_bias[l].reshape(1, dm), alpha)
    return xt.reshape(batch, seq, dm)
```

```python
import functools
import math

import jax
import jax.numpy as jnp
import numpy as np
from jax import lax
from jax.experimental import pallas as pl
from jax.experimental.pallas import tpu as pltpu

F32 = jnp.float32
BF16 = jnp.bfloat16

CHUNK = 64
RET_HEADS = 8
RET_QK_DIM = 128
RET_V_DIM = 256
RET_THETA = 10000.0
DIFF_HEADS = 8
DIFF_QK_DIM = 128
DIFF_V_DIM = 256
ROPE_THETA = 500000.0
ROPE_DIM = DIFF_QK_DIM // 4
N_GROUPS = 4
EXPERTS_PER_GROUP = 8
N_EXPERTS = N_GROUPS * EXPERTS_PER_GROUP
TOP_K = 2
LN_EPS = 1e-5

LANES = 128
SUBLANES = 8
NEG_BIG = -1e30
TOKEN_BITS = 14

VMEM_LIMIT_BYTES = 56 * 1024 * 1024
TABLE_ROWS = 512
PROJ_TM = 1024
PROJ_TN = 1024
RET_BLOCK = 256
ATT_BLOCK = 512
MERGE_TM = 256
ROUTE_TM = 256
MOE_TM = 256
DMA_GROUP = 8
FINAL_TM = 256


def _cparams(n_axes):
    return pltpu.CompilerParams(
        dimension_semantics=("arbitrary",) * n_axes, vmem_limit_bytes=VMEM_LIMIT_BYTES)


def _sigmoid(x):
    return 1.0 / (1.0 + jnp.exp(-x))


def _chunk_of(idx):
    return jnp.right_shift(idx, CHUNK.bit_length() - 1)


def _dot(a, b):
    return jnp.dot(a, b, preferred_element_type=F32)


def _dot_nt(a, b):
    return lax.dot_general(a, b, (((1,), (1,)), ((), ())), preferred_element_type=F32)


def _table_kernel(pos_ref, freq_ref, sa_ref, sb_ref, cos_ref, sina_ref, sinb_ref):
    ang = pos_ref[...].astype(F32) * freq_ref[...]
    s = jnp.sin(ang)
    cos_ref[...] = jnp.cos(ang)
    sina_ref[...] = s * sa_ref[...]
    sinb_ref[...] = s * sb_ref[...]


def _rope_tables(pos_col, freq_row, sign_a, sign_b):
    n = pos_col.shape[0]
    tr = min(TABLE_ROWS, n)
    row = pl.BlockSpec((1, LANES), lambda i: (0, 0))
    out = pl.BlockSpec((tr, LANES), lambda i: (i, 0))
    shp = jax.ShapeDtypeStruct((n, LANES), F32)
    return pl.pallas_call(
        _table_kernel, grid=(n // tr,),
        in_specs=[pl.BlockSpec((tr, 1), lambda i: (i, 0)), row, row, row],
        out_specs=[out, out, out], out_shape=[shp, shp, shp],
        compiler_params=_cparams(1), name="rope_tables",
    )(pos_col, freq_row, sign_a, sign_b)


def _matmul_kernel(x_ref, w_ref, o_ref, wb_ref):
    @pl.when(pl.program_id(1) == 0)
    def _():
        wb_ref[...] = w_ref[...].astype(BF16)

    o_ref[...] = _dot(x_ref[...], wb_ref[...]).astype(o_ref.dtype)


def _matmul(x, w, out_dtype, name):
    m, k = x.shape
    n = w.shape[1]
    tm, tn = min(PROJ_TM, m), min(PROJ_TN, n)
    return pl.pallas_call(
        _matmul_kernel, grid=(n // tn, m // tm),
        in_specs=[pl.BlockSpec((tm, k), lambda j, i: (i, 0)),
                  pl.BlockSpec((k, tn), lambda j, i: (0, j))],
        out_specs=pl.BlockSpec((tm, tn), lambda j, i: (i, j)),
        out_shape=jax.ShapeDtypeStruct((m, n), out_dtype),
        scratch_shapes=[pltpu.VMEM((k, tn), BF16)],
        compiler_params=_cparams(2), name=name,
    )(x, w)


def _retention_kernel(lg_ref, bd_ref, q_ref, k_ref, v_ref, g_ref, cos_ref, sin_ref, gain_ref,
                      o_ref, state_ref, decay_ref, qdecay_ref, kdecay_ref):
    h = pl.program_id(1)
    t = q_ref.shape[0]

    lg = lg_ref[h]

    @pl.when(pl.program_id(2) == 0)
    def _():
        state_ref[...] = jnp.zeros_like(state_ref)
        ri = lax.broadcasted_iota(jnp.int32, (t, t), 0)
        ci = lax.broadcasted_iota(jnp.int32, (t, t), 1)
        decay = jnp.exp(lg * jnp.abs(ri - ci).astype(F32))
        decay_ref[...] = jnp.where(_chunk_of(ri) >= _chunk_of(ci), decay, 0.0)
        row = lax.broadcasted_iota(jnp.int32, (t, RET_QK_DIM), 0).astype(F32)
        qdecay_ref[...] = jnp.exp(lg * (row + 1.0))
        kdecay_ref[...] = jnp.exp(lg * (t - 1.0 - row))

    cos, sin = cos_ref[...], sin_ref[...]

    def rot(x):
        return x * cos + pltpu.roll(x, RET_QK_DIM // 2, 1) * sin

    q = rot(q_ref[...].astype(F32))
    k = rot(k_ref[...].astype(F32)) * (RET_QK_DIM ** -0.5)
    v = v_ref[...]

    s = _dot_nt(q.astype(BF16), k.astype(BF16)) * decay_ref[...]
    intra = _dot(s.astype(BF16), v)

    state = state_ref[...]
    cross = _dot((q * qdecay_ref[...]).astype(BF16), state.astype(BF16))
    kd = k * kdecay_ref[...]
    state_ref[...] = state * bd_ref[h] + _dot(kd.T.astype(BF16), v)

    out = intra + cross
    mu = jnp.mean(out, axis=-1, keepdims=True)
    var = jnp.mean(jnp.square(out - mu), axis=-1, keepdims=True)
    y = (out - mu) * lax.rsqrt(var + LN_EPS) * gain_ref[...]
    gate = g_ref[...].astype(F32)
    o_ref[...] = (y * (gate * _sigmoid(gate))).astype(o_ref.dtype)


def _retention(u, cos, sin, log_gamma, block_decay, gain, batch, seq):
    t = min(RET_BLOCK, seq)
    nb = seq // t
    rows = lambda b, h, i: b * nb + i
    qk_off = (RET_HEADS * RET_QK_DIM) // RET_QK_DIM
    v_off = (2 * RET_HEADS * RET_QK_DIM) // RET_V_DIM
    g_off = v_off + RET_HEADS
    smem = pl.BlockSpec(memory_space=pltpu.SMEM)
    return pl.pallas_call(
        _retention_kernel, grid=(batch, RET_HEADS, nb),
        in_specs=[
            smem, smem,
            pl.BlockSpec((t, RET_QK_DIM), lambda b, h, i: (rows(b, h, i), h)),
            pl.BlockSpec((t, RET_QK_DIM), lambda b, h, i: (rows(b, h, i), qk_off + h)),
            pl.BlockSpec((t, RET_V_DIM), lambda b, h, i: (rows(b, h, i), v_off + h)),
            pl.BlockSpec((t, RET_V_DIM), lambda b, h, i: (rows(b, h, i), g_off + h)),
            pl.BlockSpec((t, LANES), lambda b, h, i: (rows(b, h, i), 0)),
            pl.BlockSpec((t, LANES), lambda b, h, i: (rows(b, h, i), 0)),
            pl.BlockSpec((None, 1, RET_V_DIM), lambda b, h, i: (h, 0, 0)),
        ],
        out_specs=pl.BlockSpec((t, RET_V_DIM), lambda b, h, i: (rows(b, h, i), h)),
        out_shape=jax.ShapeDtypeStruct((batch * seq, RET_HEADS * RET_V_DIM), BF16),
        scratch_shapes=[pltpu.VMEM((RET_QK_DIM, RET_V_DIM), F32), pltpu.VMEM((t, t), F32),
                        pltpu.VMEM((t, RET_QK_DIM), F32), pltpu.VMEM((t, RET_QK_DIM), F32)],
        compiler_params=_cparams(3), name="retention",
    )(log_gamma, block_decay, u, u, u, u, cos, sin, gain)


def _partial_rope(x, cos, sin_lo, sin_hi):
    half = ROPE_DIM // 2
    return (x * cos + pltpu.roll(x, DIFF_QK_DIM - half, 1) * sin_lo
            + pltpu.roll(x, half, 1) * sin_hi)


def _diff_attn_kernel(lam_ref, q_ref, k_ref, v_ref, qc_ref, qa_ref, qb_ref, kc_ref, ka_ref, kb_ref,
                      gain_ref, o_ref, krot_ref, vt_ref, m_ref, l_ref, acc_ref, *, lambda_init):
    qblk = pl.program_id(2)
    tq = q_ref.shape[0]
    d = DIFF_QK_DIM

    @pl.when(qblk == 0)
    def _():
        def prep_rows(c, carry):
            rows = pl.ds(pl.multiple_of(c * tq, tq), tq)
            for i in range(2):
                x = k_ref[rows, i * d:(i + 1) * d].astype(F32)
                krot_ref[rows, i * d:(i + 1) * d] = _partial_rope(
                    x, kc_ref[rows, :], ka_ref[rows, :], kb_ref[rows, :]).astype(BF16)
            vt_ref[c] = v_ref[rows, :].astype(F32).T.astype(BF16)
            return carry
        lax.fori_loop(0, k_ref.shape[0] // tq, prep_rows, 0)

    scale = d ** -0.5
    qs = []
    for i in range(2):
        x = q_ref[:, i * d:(i + 1) * d].astype(F32)
        qs.append((_partial_rope(x, qc_ref[...], qa_ref[...], qb_ref[...]) * scale).astype(BF16))

    m_ref[...] = jnp.full_like(m_ref, NEG_BIG)
    l_ref[...] = jnp.zeros_like(l_ref)
    acc_ref[...] = jnp.zeros_like(acc_ref)

    def kv_step(j, diagonal):
        rows = pl.ds(pl.multiple_of(j * tq, tq), tq)
        vt = vt_ref[j]
        for i in range(2):
            s = _dot_nt(krot_ref[rows, i * d:(i + 1) * d], qs[i])
            if diagonal:
                key = lax.broadcasted_iota(jnp.int32, s.shape, 0)
                qry = lax.broadcasted_iota(jnp.int32, s.shape, 1)
                s = jnp.where(_chunk_of(qry) >= _chunk_of(key), s, NEG_BIG)
            m_old = m_ref[i]
            m_new = jnp.maximum(m_old, jnp.max(s, axis=0, keepdims=True))
            alpha = jnp.exp(m_old - m_new)
            p = jnp.exp(s - m_new)
            l_ref[i] = alpha * l_ref[i] + jnp.sum(p, axis=0, keepdims=True)
            acc_ref[i] = alpha * acc_ref[i] + _dot(vt, p.astype(BF16))
            m_ref[i] = m_new

    def body(j, carry):
        kv_step(j, False)
        return carry
    lax.fori_loop(0, qblk, body, 0)
    kv_step(qblk, True)

    lp = lam_ref[...]
    lam = (jnp.exp(jnp.sum(lp[0:1] * lp[1:2], axis=-1, keepdims=True))
           - jnp.exp(jnp.sum(lp[2:3] * lp[3:4], axis=-1, keepdims=True)) + lambda_init)
    o = (acc_ref[0] / l_ref[0] - lam * (acc_ref[1] / l_ref[1])).T
    o = o * lax.rsqrt(jnp.mean(jnp.square(o), axis=-1, keepdims=True) + LN_EPS) * gain_ref[...]
    o_ref[...] = (o * (1.0 - lambda_init)).astype(o_ref.dtype)


def _diff_attention(u, cos, sin_lo, sin_hi, lam_params, gain, batch, seq, lambda_init):
    tq = min(ATT_BLOCK, seq)
    nb = seq // tq
    width = 2 * DIFF_QK_DIM
    ret_cols = 2 * RET_HEADS * RET_QK_DIM + 2 * RET_HEADS * RET_V_DIM
    q_off = ret_cols // width
    k_off = q_off + DIFF_HEADS
    v_off = k_off + DIFF_HEADS
    qrow = lambda b, h, i: b * nb + i
    qtab = pl.BlockSpec((tq, LANES), lambda b, h, i: (qrow(b, h, i), 0))
    ktab = pl.BlockSpec((seq, LANES), lambda b, h, i: (b, 0))
    return pl.pallas_call(
        functools.partial(_diff_attn_kernel, lambda_init=lambda_init),
        grid=(batch, DIFF_HEADS, nb),
        in_specs=[
            pl.BlockSpec((4, DIFF_QK_DIM), lambda b, h, i: (0, 0)),
            pl.BlockSpec((tq, width), lambda b, h, i: (qrow(b, h, i), q_off + h)),
            pl.BlockSpec((seq, width), lambda b, h, i: (b, k_off + h)),
            pl.BlockSpec((seq, DIFF_V_DIM), lambda b, h, i: (b, v_off + h)),
            qtab, qtab, qtab, ktab, ktab, ktab,
            pl.BlockSpec((1, DIFF_V_DIM), lambda b, h, i: (0, 0)),
        ],
        out_specs=pl.BlockSpec((tq, DIFF_V_DIM), lambda b, h, i: (qrow(b, h, i), h)),
        out_shape=jax.ShapeDtypeStruct((batch * seq, DIFF_HEADS * DIFF_V_DIM), BF16),
        scratch_shapes=[pltpu.VMEM((seq, width), BF16), pltpu.VMEM((nb, DIFF_V_DIM, tq), BF16),
                        pltpu.VMEM((2, 1, tq), F32), pltpu.VMEM((2, 1, tq), F32),
                        pltpu.VMEM((2, DIFF_V_DIM, tq), F32)],
        compiler_params=_cparams(3), name="diff_attention",
    )(lam_params, u, u, u, cos, sin_lo, sin_hi, cos, sin_lo, sin_hi, gain)


def _merge_kernel(ret_ref, diff_ref, gr_ref, gd_ref, wr_ref, wd_ref, o_ref):
    a = _dot(ret_ref[...], wr_ref[...])
    b = _dot(diff_ref[...], wd_ref[...])
    merged = _sigmoid(gr_ref[...].astype(F32)) * a + _sigmoid(gd_ref[...].astype(F32)) * b
    o_ref[...] = merged.astype(o_ref.dtype)


def _merge(ret, diff, u, w_ret, w_diff):
    n, dm = ret.shape[0], w_ret.shape[1]
    tm = min(MERGE_TM, n)
    gate_off = (u.shape[1] - 2 * dm) // dm
    tile = lambda c: pl.BlockSpec((tm, dm), lambda i: (i, c))
    weight = lambda w: pl.BlockSpec(w.shape, lambda i: (0, 0))
    return pl.pallas_call(
        _merge_kernel, grid=(n // tm,),
        in_specs=[pl.BlockSpec((tm, ret.shape[1]), lambda i: (i, 0)),
                  pl.BlockSpec((tm, diff.shape[1]), lambda i: (i, 0)),
                  tile(gate_off), tile(gate_off + 1), weight(w_ret), weight(w_diff)],
        out_specs=tile(0), out_shape=jax.ShapeDtypeStruct((n, dm), BF16),
        compiler_params=_cparams(1), name="gated_merge",
    )(ret, diff, u, u, w_ret, w_diff)


def _layer_norm(z, gain, bias):
    mu = jnp.mean(z, axis=-1, keepdims=True)
    var = jnp.mean(jnp.square(z - mu), axis=-1, keepdims=True)
    return (z - mu) * lax.rsqrt(var + LN_EPS) * gain + bias


def _split_bf16(a):
    hi = a.astype(BF16)
    return hi, (a - hi.astype(F32)).astype(BF16)


def _route_kernel(m_ref, x_ref, wo_ref, gain_ref, bias_ref, wr_ref, br_ref, h_ref, r_ref, *, alpha):
    h = _layer_norm(alpha * x_ref[...] + _dot(m_ref[...], wo_ref[...]), gain_ref[...], bias_ref[...])
    h_ref[...] = h

    h_hi, h_lo = _split_bf16(h)
    w_hi, w_lo = _split_bf16(wr_ref[...])
    logits = _dot(h_hi, w_hi) + (_dot(h_hi, w_lo) + _dot(h_lo, w_hi)) + br_ref[...]

    lane = lax.broadcasted_iota(jnp.int32, logits.shape, 1).astype(F32)

    def first_argmax(vals):
        top = jnp.max(vals, axis=-1, keepdims=True)
        idx = jnp.min(jnp.where(vals == top, lane, float(LANES)), axis=-1, keepdims=True)
        return top, idx

    group_logits = jnp.where(lane < N_GROUPS, logits, NEG_BIG)
    g_top, g_idx = first_argmax(group_logits)
    p_group = 1.0 / jnp.sum(jnp.exp(group_logits - g_top), axis=-1, keepdims=True)

    first = N_GROUPS + g_idx * EXPERTS_PER_GROUP
    within = jnp.where((lane >= first) & (lane < first + EXPERTS_PER_GROUP), logits, NEG_BIG)
    v1, i1 = first_argmax(within)
    v2, i2 = first_argmax(jnp.where(lane == i1, NEG_BIG, within))
    e2 = jnp.exp(v2 - v1)
    w1 = p_group / (1.0 + e2)
    w2 = p_group * e2 / (1.0 + e2)
    r_ref[...] = jnp.where(lane == 0, i1 - N_GROUPS, jnp.where(lane == 1, i2 - N_GROUPS,
                           jnp.where(lane == 2, w1, jnp.where(lane == 3, w2, 0.0))))


def _route(merged, x, w_out, gain, bias, w_router, b_router, alpha):
    n, dm = x.shape
    tm = min(ROUTE_TM, n)
    tile = pl.BlockSpec((tm, dm), lambda i: (i, 0))
    full = lambda a: pl.BlockSpec(a.shape, lambda i: (0, 0))
    return pl.pallas_call(
        functools.partial(_route_kernel, alpha=alpha), grid=(n // tm,),
        in_specs=[tile, tile, full(w_out), full(gain), full(bias), full(w_router), full(b_router)],
        out_specs=[tile, pl.BlockSpec((tm, LANES), lambda i: (i, 0))],
        out_shape=[jax.ShapeDtypeStruct((n, dm), F32), jax.ShapeDtypeStruct((n, LANES), F32)],
        compiler_params=_cparams(1), name="outproj_ln_router",
    )(merged, x, w_out, gain, bias, w_router, b_router)


def _expert_kernel(tile_expert_ref, tile_rows_ref, packed_ref, n_valid_ref, h_hbm, wg_ref, wu_ref,
                   wd_ref, y_hbm, xbuf, ybuf, gather_sem, scatter_sem):
    del tile_expert_ref
    i = pl.program_id(0)
    n_valid = n_valid_ref[0]
    tm = xbuf.shape[1]
    slot = i & 1

    def slot_words(tile, g):
        return [packed_ref[tile * tm + g * DMA_GROUP + k] for k in range(DMA_GROUP)]

    def for_buffer(dynamic_buf, fn):
        for buf in range(2):
            pl.when(dynamic_buf == buf)(functools.partial(fn, buf))

    def start_gather(tile, dynamic_buf):
        def issue(buf):
            for g in range(tm // DMA_GROUP):
                for k, word in enumerate(slot_words(tile, g)):
                    tok = word & ((1 << TOKEN_BITS) - 1)
                    pltpu.make_async_copy(h_hbm.at[pl.ds(tok, 1)],
                                          xbuf.at[buf, pl.ds(g * DMA_GROUP + k, 1)],
                                          gather_sem.at[buf]).start()
        for_buffer(dynamic_buf, issue)

    def wait_gather(buf):
        pltpu.make_async_copy(h_hbm.at[pl.ds(0, tm)], xbuf.at[buf], gather_sem.at[buf]).wait()

    def start_scatter(tile, dynamic_buf):
        n_rows = tile_rows_ref[tile]

        def issue(buf):
            for g in range(tm // DMA_GROUP):
                for k, word in enumerate(slot_words(tile, g)):
                    r = g * DMA_GROUP + k

                    @pl.when(r < n_rows)
                    def _():
                        pltpu.make_async_copy(ybuf.at[buf, pl.ds(r, 1)],
                                              y_hbm.at[pl.ds(word >> TOKEN_BITS, 1)],
                                              scatter_sem.at[buf]).start()
        for_buffer(dynamic_buf, issue)

    def wait_scatter(tile):
        buf = tile & 1
        n_rows = tile_rows_ref[tile]
        n_aligned = pl.multiple_of(n_rows - (n_rows & (SUBLANES - 1)), SUBLANES)

        @pl.when(n_aligned > 0)
        def _():
            rows = pl.ds(0, n_aligned)
            pltpu.make_async_copy(ybuf.at[buf, rows], y_hbm.at[rows], scatter_sem.at[buf]).wait()

        def wait_row(r, carry):
            pltpu.make_async_copy(ybuf.at[buf, pl.ds(r, 1)], y_hbm.at[pl.ds(r, 1)],
                                  scatter_sem.at[buf]).wait()
            return carry
        lax.fori_loop(n_aligned, n_rows, wait_row, 0)

    @pl.when(i == 0)
    def _():
        start_gather(0, 0)

    @pl.when(i + 1 < n_valid)
    def _():
        start_gather(i + 1, 1 - slot)

    @pl.when(i < n_valid)
    def _():
        wait_gather(slot)

        @pl.when(i >= 2)
        def _():
            wait_scatter(i - 2)

        x = xbuf[slot].astype(BF16)
        g = _dot(x, wg_ref[...])
        act = (g * _sigmoid(g) * _dot(x, wu_ref[...])).astype(BF16)
        ybuf[slot] = _dot(act, wd_ref[...])
        start_scatter(i, slot)

    @pl.when(i == pl.num_programs(0) - 1)
    def _():
        wait_scatter(n_valid - 1)

        @pl.when(n_valid >= 2)
        def _():
            wait_scatter(n_valid - 2)


def _experts(h, tile_expert, tile_rows, packed, n_valid, w_gate, w_up, w_down, n_rows_out):
    n, dm = h.shape
    n_tiles = tile_expert.shape[0]
    tm = packed.shape[0] // n_tiles
    ff = w_gate.shape[2]
    wspec = lambda a, b: pl.BlockSpec((None, a, b), lambda i, te, tr, pk, nv: (te[i], 0, 0))
    grid_spec = pltpu.PrefetchScalarGridSpec(
        num_scalar_prefetch=4, grid=(n_tiles,),
        in_specs=[pl.BlockSpec(memory_space=pl.ANY), wspec(dm, ff), wspec(dm, ff), wspec(ff, dm)],
        out_specs=pl.BlockSpec(memory_space=pl.ANY),
        scratch_shapes=[pltpu.VMEM((2, tm, dm), F32), pltpu.VMEM((2, tm, dm), F32),
                        pltpu.SemaphoreType.DMA((2,)), pltpu.SemaphoreType.DMA((2,))])
    return pl.pallas_call(
        _expert_kernel, grid_spec=grid_spec,
        out_shape=jax.ShapeDtypeStruct((n_rows_out, dm), F32),
        compiler_params=_cparams(1), name="experts",
    )(tile_expert, tile_rows, packed, n_valid, h, w_gate, w_up, w_down)


def _dispatch_tables(expert_ids, tm):
    n_assign = expert_ids.shape[0]
    n_tiles = n_assign // tm + N_EXPERTS
    n_slots = n_tiles * tm
    onehot = (expert_ids[:, None] == jnp.arange(N_EXPERTS, dtype=jnp.int32)[None, :]).astype(jnp.int32)
    running = jnp.cumsum(onehot, axis=0)
    rank = jnp.take_along_axis(running, expert_ids[:, None], axis=1)[:, 0] - 1
    counts = running[-1]
    padded = (counts + tm - 1) // tm * tm
    pad_end = jnp.cumsum(padded)
    slot_of = (pad_end - padded)[expert_ids] + rank
    tile_expert = jnp.minimum(
        jnp.searchsorted(pad_end, jnp.arange(n_tiles, dtype=jnp.int32) * tm, side="right"),
        N_EXPERTS - 1).astype(jnp.int32)
    assign = jnp.full((n_slots,), -1, jnp.int32).at[slot_of].set(jnp.arange(n_assign, dtype=jnp.int32))
    is_pad = assign < 0
    tile_rows = jnp.sum((~is_pad).reshape(n_tiles, tm).astype(jnp.int32), axis=1)
    real = jnp.maximum(assign, 0)
    token, choice = real // TOP_K, real % TOP_K
    packed = ((choice * (n_assign // TOP_K) + token) << TOKEN_BITS) | token
    n_valid = (pad_end[-1] // tm).astype(jnp.int32).reshape(1)
    return tile_expert, tile_rows, packed, n_valid


def _final_kernel(h_ref, y1_ref, y2_ref, r_ref, gain_ref, bias_ref, o_ref, *, alpha):
    r = r_ref[...]
    ffn = y1_ref[...] * r[:, 2:3] + y2_ref[...] * r[:, 3:4]
    o_ref[...] = _layer_norm(alpha * h_ref[...] + ffn, gain_ref[...], bias_ref[...])


def _final(h, y_rows, routing, gain, bias, alpha):
    n, dm = h.shape
    tm = min(FINAL_TM, n)
    tile = pl.BlockSpec((tm, dm), lambda i: (i, 0))
    second = pl.BlockSpec((tm, dm), lambda i: (n // tm + i, 0))
    full = lambda a: pl.BlockSpec(a.shape, lambda i: (0, 0))
    return pl.pallas_call(
        functools.partial(_final_kernel, alpha=alpha), grid=(n // tm,),
        in_specs=[tile, tile, second, pl.BlockSpec((tm, LANES), lambda i: (i, 0)), full(gain), full(bias)],
        out_specs=tile, out_shape=jax.ShapeDtypeStruct((n, dm), F32),
        compiler_params=_cparams(1), name="combine_ln",
    )(h, y_rows, y_rows, routing, gain, bias)


def _lane_row(values):
    row = np.zeros((1, LANES), np.float32)
    row[0, :len(values)] = values
    return jnp.asarray(row)


def _rope_constants():
    ret_half = RET_QK_DIM // 2
    ret_freq = 1.0 / (RET_THETA ** (jnp.arange(0, RET_QK_DIM, 2, dtype=F32) / RET_QK_DIM))
    ret_freq = jnp.concatenate([ret_freq, ret_freq]).reshape(1, LANES)
    ret_sign = _lane_row([-1.0] * ret_half + [1.0] * ret_half)
    half = ROPE_DIM // 2
    diff_freq = 1.0 / (ROPE_THETA ** (jnp.arange(0, ROPE_DIM, 2, dtype=F32) / ROPE_DIM))
    diff_freq = jnp.concatenate([diff_freq, diff_freq, jnp.zeros((LANES - ROPE_DIM,), F32)]).reshape(1, LANES)
    diff_lo = _lane_row([-1.0] * half)
    diff_hi = _lane_row([0.0] * half + [1.0] * half)
    return ret_freq, ret_sign, diff_freq, diff_lo, diff_hi


def kernel(x, positions, w_in, ret_gn_gain, diff_lambda_q1, diff_lambda_k1, diff_lambda_q2, diff_lambda_k2, diff_subln_gain, w_ret_proj, w_diff_proj, w_out, ln1_gain, ln1_bias, w_group, b_group, w_router, b_router, w_expert_gate, w_expert_up, w_expert_down, ln2_gain, ln2_bias):
    batch, seq, dm = x.shape
    n = batch * seq
    depth = w_in.shape[0]
    assert n <= (1 << TOKEN_BITS) and seq % CHUNK == 0
    alpha = (2 * depth) ** 0.25

    ret_freq, ret_sign, diff_freq, diff_lo, diff_hi = _rope_constants()
    pos_col = positions.reshape(n, 1)
    ret_cos, ret_sin, _ = _rope_tables(pos_col, ret_freq, ret_sign, ret_sign)
    diff_cos, diff_sin_lo, diff_sin_hi = _rope_tables(pos_col, diff_freq, diff_lo, diff_hi)

    log_gamma = jnp.log(1.0 - 2.0 ** (-5.0 - jnp.arange(RET_HEADS, dtype=F32)))
    block_decay = jnp.exp(log_gamma * min(RET_BLOCK, seq))

    xt = x.reshape(n, dm)
    for l in range(depth):
        lambda_init = 0.8 - 0.6 * math.exp(-0.3 * l)
        u = _matmul(xt.astype(BF16), w_in[l], BF16, "in_proj")
        ret = _retention(u, ret_cos, ret_sin, log_gamma, block_decay,
                         ret_gn_gain[l].reshape(RET_HEADS, 1, RET_V_DIM), batch, seq)
        lam_params = jnp.stack([diff_lambda_q1[l], diff_lambda_k1[l], diff_lambda_q2[l], diff_lambda_k2[l]])
        diff = _diff_attention(u, diff_cos, diff_sin_lo, diff_sin_hi, lam_params.astype(F32),
                               diff_subln_gain[l].reshape(1, DIFF_V_DIM), batch, seq, lambda_init)
        merged = _merge(ret, diff, u, w_ret_proj[l].astype(BF16), w_diff_proj[l].astype(BF16))

        w_route = jnp.concatenate(
            [w_group[l], w_router[l], jnp.zeros((dm, LANES - N_GROUPS - N_EXPERTS), F32)], axis=1)
        b_route = jnp.concatenate(
            [b_group[l], b_router[l], jnp.zeros((LANES - N_GROUPS - N_EXPERTS,), F32)]).reshape(1, LANES)
        h, routing = _route(merged, xt, w_out[l].astype(BF16), ln1_gain[l].reshape(1, dm),
                            ln1_bias[l].reshape(1, dm), w_route, b_route, alpha)

        expert_ids = routing[:, :TOP_K].astype(jnp.int32).reshape(-1)
        tile_expert, tile_rows, packed, n_valid = _dispatch_tables(expert_ids, min(MOE_TM, n))
        y_rows = _experts(h, tile_expert, tile_rows, packed, n_valid, w_expert_gate[l].astype(BF16),
                          w_expert_up[l].astype(BF16), w_expert_down[l].astype(BF16), TOP_K * n)
        xt = _final(h, y_rows, routing,
                    ln2_gain[l].reshape(1, dm), ln2_bias[l].reshape(1, dm), alpha)
    return xt.reshape(batch, seq, dm)
```

```python
import functools
import math

import jax
import jax.numpy as jnp
import numpy as np
from jax import lax
from jax.experimental import pallas as pl
from jax.experimental.pallas import tpu as pltpu

F32 = jnp.float32
BF16 = jnp.bfloat16

CHUNK = 64
RET_HEADS = 8
RET_QK_DIM = 128
RET_V_DIM = 256
RET_THETA = 10000.0
DIFF_HEADS = 8
DIFF_QK_DIM = 128
DIFF_V_DIM = 256
ROPE_THETA = 500000.0
ROPE_DIM = DIFF_QK_DIM // 4
N_GROUPS = 4
EXPERTS_PER_GROUP = 8
N_EXPERTS = N_GROUPS * EXPERTS_PER_GROUP
TOP_K = 2
LN_EPS = 1e-5

LANES = 128
SUBLANES = 8
BF16_ROWS = 16
NEG_BIG = -1e30
TOKEN_BITS = 14

VMEM_LIMIT_BYTES = 56 * 1024 * 1024
TABLE_ROWS = 512
PROJ_TM = 1024
PROJ_TN = 1024
RET_BLOCK = 256
RET_HEAD_GROUP = 8
ATT_BLOCK = 512
ATT_QUERY_SPLITS = 2
MERGE_TM = 256
ROUTE_TM = 256
MOE_TM = 256
DMA_GROUP = 8
FINAL_TM = 256


def _cparams(n_axes):
    return pltpu.CompilerParams(
        dimension_semantics=("arbitrary",) * n_axes, vmem_limit_bytes=VMEM_LIMIT_BYTES)


def _sigmoid(x):
    return 1.0 / (1.0 + jnp.exp(-x))


def _chunk_of(idx):
    return jnp.right_shift(idx, CHUNK.bit_length() - 1)


def _dot(a, b):
    return jnp.dot(a, b, preferred_element_type=F32)


def _dot_nt(a, b):
    return lax.dot_general(a, b, (((1,), (1,)), ((), ())), preferred_element_type=F32)


def _table_kernel(pos_ref, freq_ref, sa_ref, sb_ref, cos_ref, sina_ref, sinb_ref):
    ang = pos_ref[...].astype(F32) * freq_ref[...]
    s = jnp.sin(ang)
    cos_ref[...] = jnp.cos(ang)
    sina_ref[...] = s * sa_ref[...]
    sinb_ref[...] = s * sb_ref[...]


def _rope_tables(pos_col, freq_row, sign_a, sign_b):
    n = pos_col.shape[0]
    tr = min(TABLE_ROWS, n)
    row = pl.BlockSpec((1, LANES), lambda i: (0, 0))
    out = pl.BlockSpec((tr, LANES), lambda i: (i, 0))
    shp = jax.ShapeDtypeStruct((n, LANES), F32)
    return pl.pallas_call(
        _table_kernel, grid=(n // tr,),
        in_specs=[pl.BlockSpec((tr, 1), lambda i: (i, 0)), row, row, row],
        out_specs=[out, out, out], out_shape=[shp, shp, shp],
        compiler_params=_cparams(1), name="rope_tables",
    )(pos_col, freq_row, sign_a, sign_b)


def _matmul_kernel(x_ref, w_ref, o_ref, wb_ref):
    @pl.when(pl.program_id(1) == 0)
    def _():
        wb_ref[...] = w_ref[...].astype(BF16)

    o_ref[...] = _dot(x_ref[...], wb_ref[...]).astype(o_ref.dtype)


def _matmul(x, w, out_dtype, name):
    m, k = x.shape
    n = w.shape[1]
    tm, tn = min(PROJ_TM, m), min(PROJ_TN, n)
    return pl.pallas_call(
        _matmul_kernel, grid=(n // tn, m // tm),
        in_specs=[pl.BlockSpec((tm, k), lambda j, i: (i, 0)),
                  pl.BlockSpec((k, tn), lambda j, i: (0, j))],
        out_specs=pl.BlockSpec((tm, tn), lambda j, i: (i, j)),
        out_shape=jax.ShapeDtypeStruct((m, n), out_dtype),
        scratch_shapes=[pltpu.VMEM((k, tn), BF16)],
        compiler_params=_cparams(2), name=name,
    )(x, w)


def _retention_kernel(lg_ref, bd_ref, q_ref, k_ref, v_ref, g_ref, cos_ref, sin_ref, gain_ref,
                      o_ref, state_ref, decay_ref, qdecay_ref, kdecay_ref):
    group = state_ref.shape[0]
    first_head = pl.program_id(1) * group
    t = q_ref.shape[0]
    dk, dv = RET_QK_DIM, RET_V_DIM

    @pl.when(pl.program_id(2) == 0)
    def _():
        state_ref[...] = jnp.zeros_like(state_ref)
        ri = lax.broadcasted_iota(jnp.int32, (t, t), 0)
        ci = lax.broadcasted_iota(jnp.int32, (t, t), 1)
        dist = jnp.abs(ri - ci).astype(F32)
        visible = _chunk_of(ri) >= _chunk_of(ci)
        row = lax.broadcasted_iota(jnp.int32, (t, dk), 0).astype(F32)
        for j in range(group):
            lg = lg_ref[first_head + j]
            decay_ref[j] = jnp.where(visible, jnp.exp(lg * dist), 0.0)
            qdecay_ref[j] = jnp.exp(lg * (row + 1.0))
            kdecay_ref[j] = jnp.exp(lg * (t - 1.0 - row))

    cos, sin = cos_ref[...], sin_ref[...]

    def rot(x):
        return x * cos + pltpu.roll(x, dk // 2, 1) * sin

    for j in range(group):
        q = rot(q_ref[:, j * dk:(j + 1) * dk].astype(F32))
        k = rot(k_ref[:, j * dk:(j + 1) * dk].astype(F32)) * (dk ** -0.5)
        v = v_ref[:, j * dv:(j + 1) * dv]

        s = _dot_nt(q.astype(BF16), k.astype(BF16)) * decay_ref[j]
        intra = _dot(s.astype(BF16), v)

        state = state_ref[j]
        cross = _dot((q * qdecay_ref[j]).astype(BF16), state.astype(BF16))
        kd = k * kdecay_ref[j]
        state_ref[j] = state * bd_ref[first_head + j] + _dot(kd.T.astype(BF16), v)

        out = intra + cross
        mu = jnp.mean(out, axis=-1, keepdims=True)
        var = jnp.mean(jnp.square(out - mu), axis=-1, keepdims=True)
        y = (out - mu) * lax.rsqrt(var + LN_EPS) * gain_ref[j]
        gate = g_ref[:, j * dv:(j + 1) * dv].astype(F32)
        o_ref[:, j * dv:(j + 1) * dv] = (y * (gate * _sigmoid(gate))).astype(o_ref.dtype)


def _retention(u, cos, sin, log_gamma, block_decay, gain, batch, seq):
    t = min(RET_BLOCK, seq)
    nb = seq // t
    grp = RET_HEAD_GROUP
    n_groups = RET_HEADS // grp
    rows = lambda b, h, i: b * nb + i
    qk_w, v_w = grp * RET_QK_DIM, grp * RET_V_DIM
    k_off, v_off, g_off = n_groups, n_groups, 2 * n_groups
    smem = pl.BlockSpec(memory_space=pltpu.SMEM)
    return pl.pallas_call(
        _retention_kernel, grid=(batch, n_groups, nb),
        in_specs=[
            smem, smem,
            pl.BlockSpec((t, qk_w), lambda b, h, i: (rows(b, h, i), h)),
            pl.BlockSpec((t, qk_w), lambda b, h, i: (rows(b, h, i), k_off + h)),
            pl.BlockSpec((t, v_w), lambda b, h, i: (rows(b, h, i), v_off + h)),
            pl.BlockSpec((t, v_w), lambda b, h, i: (rows(b, h, i), g_off + h)),
            pl.BlockSpec((t, LANES), lambda b, h, i: (rows(b, h, i), 0)),
            pl.BlockSpec((t, LANES), lambda b, h, i: (rows(b, h, i), 0)),
            pl.BlockSpec((grp, 1, RET_V_DIM), lambda b, h, i: (h, 0, 0)),
        ],
        out_specs=pl.BlockSpec((t, v_w), lambda b, h, i: (rows(b, h, i), h)),
        out_shape=jax.ShapeDtypeStruct((batch * seq, RET_HEADS * RET_V_DIM), BF16),
        scratch_shapes=[pltpu.VMEM((grp, RET_QK_DIM, RET_V_DIM), F32), pltpu.VMEM((grp, t, t), F32),
                        pltpu.VMEM((grp, t, RET_QK_DIM), F32), pltpu.VMEM((grp, t, RET_QK_DIM), F32)],
        compiler_params=_cparams(3), name="retention",
    )(log_gamma, block_decay, u, u, u, u, cos, sin, gain)


def _partial_rope(x, cos, sin_lo, sin_hi):
    half = ROPE_DIM // 2
    return (x * cos + pltpu.roll(x, DIFF_QK_DIM - half, 1) * sin_lo
            + pltpu.roll(x, half, 1) * sin_hi)


def _diff_attn_kernel(lam_ref, q_ref, k_ref, v_ref, qc_ref, qa_ref, qb_ref, kc_ref, ka_ref, kb_ref,
                      gain_ref, o_ref, krot_ref, vt_ref, m_ref, acc_ref, *, lambda_init):
    qblk = pl.program_id(2)
    tq = q_ref.shape[0]
    d = DIFF_QK_DIM

    @pl.when(qblk == 0)
    def _():
        def prep_rows(c, carry):
            rows = pl.ds(pl.multiple_of(c * tq, tq), tq)
            for i in range(2):
                x = k_ref[rows, i * d:(i + 1) * d].astype(F32)
                krot_ref[rows, i * d:(i + 1) * d] = _partial_rope(
                    x, kc_ref[rows, :], ka_ref[rows, :], kb_ref[rows, :]).astype(BF16)
            vt_ref[c, :DIFF_V_DIM] = v_ref[rows, :].astype(F32).T.astype(BF16)
            vt_ref[c, DIFF_V_DIM:] = jnp.ones((BF16_ROWS, tq), BF16)
            return carry
        lax.fori_loop(0, k_ref.shape[0] // tq, prep_rows, 0)

    scale = d ** -0.5
    qs = []
    for i in range(2):
        x = q_ref[:, i * d:(i + 1) * d].astype(F32)
        qs.append((_partial_rope(x, qc_ref[...], qa_ref[...], qb_ref[...]) * scale).astype(BF16))

    m_ref[...] = jnp.full_like(m_ref, NEG_BIG)
    acc_ref[...] = jnp.zeros_like(acc_ref)

    def kv_step(j, diagonal):
        rows = pl.ds(pl.multiple_of(j * tq, tq), tq)
        vt = vt_ref[j]
        hw = tq // ATT_QUERY_SPLITS
        for i in range(2):
            kj = krot_ref[rows, i * d:(i + 1) * d]
            for c in range(ATT_QUERY_SPLITS):
                cols = slice(c * hw, (c + 1) * hw)
                s = _dot_nt(kj, qs[i][cols])
                if diagonal:
                    key = lax.broadcasted_iota(jnp.int32, s.shape, 0)
                    qry = lax.broadcasted_iota(jnp.int32, s.shape, 1) + c * hw
                    s = jnp.where(_chunk_of(qry) >= _chunk_of(key), s, NEG_BIG)
                m_old = m_ref[i, :, cols]
                m_new = jnp.maximum(m_old, jnp.max(s, axis=0, keepdims=True))
                alpha = jnp.exp(m_old - m_new)
                p = jnp.exp(s - m_new)
                acc_ref[i, :, cols] = alpha * acc_ref[i, :, cols] + _dot(vt, p.astype(BF16))
                m_ref[i, :, cols] = m_new

    def body(j, carry):
        kv_step(j, False)
        return carry
    lax.fori_loop(0, qblk, body, 0)
    kv_step(qblk, True)

    lp = lam_ref[...]
    lam = (jnp.exp(jnp.sum(lp[0:1] * lp[1:2], axis=-1, keepdims=True))
           - jnp.exp(jnp.sum(lp[2:3] * lp[3:4], axis=-1, keepdims=True)) + lambda_init)
    dv = DIFF_V_DIM
    o = (acc_ref[0, :dv] / acc_ref[0, dv:dv + 1] - lam * (acc_ref[1, :dv] / acc_ref[1, dv:dv + 1])).T
    o = o * lax.rsqrt(jnp.mean(jnp.square(o), axis=-1, keepdims=True) + LN_EPS) * gain_ref[...]
    o_ref[...] = (o * (1.0 - lambda_init)).astype(o_ref.dtype)


def _diff_attention(u, cos, sin_lo, sin_hi, lam_params, gain, batch, seq, lambda_init):
    tq = min(ATT_BLOCK, seq)
    nb = seq // tq
    width = 2 * DIFF_QK_DIM
    ret_cols = 2 * RET_HEADS * RET_QK_DIM + 2 * RET_HEADS * RET_V_DIM
    q_off = ret_cols // width
    k_off = q_off + DIFF_HEADS
    v_off = k_off + DIFF_HEADS
    qrow = lambda b, h, i: b * nb + i
    qtab = pl.BlockSpec((tq, LANES), lambda b, h, i: (qrow(b, h, i), 0))
    ktab = pl.BlockSpec((seq, LANES), lambda b, h, i: (b, 0))
    return pl.pallas_call(
        functools.partial(_diff_attn_kernel, lambda_init=lambda_init),
        grid=(batch, DIFF_HEADS, nb),
        in_specs=[
            pl.BlockSpec((4, DIFF_QK_DIM), lambda b, h, i: (0, 0)),
            pl.BlockSpec((tq, width), lambda b, h, i: (qrow(b, h, i), q_off + h)),
            pl.BlockSpec((seq, width), lambda b, h, i: (b, k_off + h)),
            pl.BlockSpec((seq, DIFF_V_DIM), lambda b, h, i: (b, v_off + h)),
            qtab, qtab, qtab, ktab, ktab, ktab,
            pl.BlockSpec((1, DIFF_V_DIM), lambda b, h, i: (0, 0)),
        ],
        out_specs=pl.BlockSpec((tq, DIFF_V_DIM), lambda b, h, i: (qrow(b, h, i), h)),
        out_shape=jax.ShapeDtypeStruct((batch * seq, DIFF_HEADS * DIFF_V_DIM), BF16),
        scratch_shapes=[pltpu.VMEM((seq, width), BF16),
                        pltpu.VMEM((nb, DIFF_V_DIM + BF16_ROWS, tq), BF16),
                        pltpu.VMEM((2, 1, tq), F32),
                        pltpu.VMEM((2, DIFF_V_DIM + BF16_ROWS, tq), F32)],
        compiler_params=_cparams(3), name="diff_attention",
    )(lam_params, u, u, u, cos, sin_lo, sin_hi, cos, sin_lo, sin_hi, gain)


def _merge_kernel(ret_ref, diff_ref, gr_ref, gd_ref, wr_ref, wd_ref, o_ref):
    a = _dot(ret_ref[...], wr_ref[...])
    b = _dot(diff_ref[...], wd_ref[...])
    merged = _sigmoid(gr_ref[...].astype(F32)) * a + _sigmoid(gd_ref[...].astype(F32)) * b
    o_ref[...] = merged.astype(o_ref.dtype)


def _merge(ret, diff, u, w_ret, w_diff):
    n, dm = ret.shape[0], w_ret.shape[1]
    tm = min(MERGE_TM, n)
    gate_off = (u.shape[1] - 2 * dm) // dm
    tile = lambda c: pl.BlockSpec((tm, dm), lambda i: (i, c))
    weight = lambda w: pl.BlockSpec(w.shape, lambda i: (0, 0))
    return pl.pallas_call(
        _merge_kernel, grid=(n // tm,),
        in_specs=[pl.BlockSpec((tm, ret.shape[1]), lambda i: (i, 0)),
                  pl.BlockSpec((tm, diff.shape[1]), lambda i: (i, 0)),
                  tile(gate_off), tile(gate_off + 1), weight(w_ret), weight(w_diff)],
        out_specs=tile(0), out_shape=jax.ShapeDtypeStruct((n, dm), BF16),
        compiler_params=_cparams(1), name="gated_merge",
    )(ret, diff, u, u, w_ret, w_diff)


def _layer_norm(z, gain, bias):
    mu = jnp.mean(z, axis=-1, keepdims=True)
    var = jnp.mean(jnp.square(z - mu), axis=-1, keepdims=True)
    return (z - mu) * lax.rsqrt(var + LN_EPS) * gain + bias


def _split_bf16(a):
    hi = a.astype(BF16)
    return hi, (a - hi.astype(F32)).astype(BF16)


def _route_kernel(m_ref, x_ref, wo_ref, gain_ref, bias_ref, wr_ref, br_ref, h_ref, r_ref, count_ref,
                  *, alpha):
    @pl.when(pl.program_id(0) == 0)
    def _():
        count_ref[...] = jnp.zeros_like(count_ref)

    h = _layer_norm(alpha * x_ref[...] + _dot(m_ref[...], wo_ref[...]), gain_ref[...], bias_ref[...])
    h_ref[...] = h

    h_hi, h_lo = _split_bf16(h)
    w_hi, w_lo = _split_bf16(wr_ref[...])
    logits = _dot(h_hi, w_hi) + (_dot(h_hi, w_lo) + _dot(h_lo, w_hi)) + br_ref[...]

    lane = lax.broadcasted_iota(jnp.int32, logits.shape, 1).astype(F32)

    def first_argmax(vals):
        top = jnp.max(vals, axis=-1, keepdims=True)
        idx = jnp.min(jnp.where(vals == top, lane, float(LANES)), axis=-1, keepdims=True)
        return top, idx

    group_logits = jnp.where(lane < N_GROUPS, logits, NEG_BIG)
    g_top, g_idx = first_argmax(group_logits)
    p_group = 1.0 / jnp.sum(jnp.exp(group_logits - g_top), axis=-1, keepdims=True)

    first = N_GROUPS + g_idx * EXPERTS_PER_GROUP
    within = jnp.where((lane >= first) & (lane < first + EXPERTS_PER_GROUP), logits, NEG_BIG)
    v1, i1 = first_argmax(within)
    v2, i2 = first_argmax(jnp.where(lane == i1, NEG_BIG, within))
    e2 = jnp.exp(v2 - v1)
    w1 = p_group / (1.0 + e2)
    w2 = p_group * e2 / (1.0 + e2)
    x1, x2 = i1 - N_GROUPS, i2 - N_GROUPS

    hit1 = jnp.where(lane == x1, 1.0, 0.0)
    hit2 = jnp.where(lane == x2, 1.0, 0.0)
    hits = hit1 + hit2
    tm = hits.shape[0]
    earlier = (lax.broadcasted_iota(jnp.int32, (tm, tm), 0) > lax.broadcasted_iota(jnp.int32, (tm, tm), 1))
    before = count_ref[...] + _dot(jnp.where(earlier, 1.0, 0.0).astype(BF16), hits.astype(BF16))
    rank1 = jnp.sum(before * hit1, axis=-1, keepdims=True)
    rank2 = jnp.sum(before * hit2, axis=-1, keepdims=True)
    count_ref[...] += jnp.sum(hits, axis=0, keepdims=True)

    fields = (x1, x2, w1, w2, rank1, rank2)
    row = jnp.zeros_like(logits)
    for pos, value in enumerate(fields):
        row = jnp.where(lane == pos, value, row)
    r_ref[...] = row


def _route(merged, x, w_out, gain, bias, w_router, b_router, alpha):
    n, dm = x.shape
    tm = min(ROUTE_TM, n)
    tile = pl.BlockSpec((tm, dm), lambda i: (i, 0))
    full = lambda a: pl.BlockSpec(a.shape, lambda i: (0, 0))
    return pl.pallas_call(
        functools.partial(_route_kernel, alpha=alpha), grid=(n // tm,),
        in_specs=[tile, tile, full(w_out), full(gain), full(bias), full(w_router), full(b_router)],
        out_specs=[tile, pl.BlockSpec((tm, LANES), lambda i: (i, 0)),
                   pl.BlockSpec((1, LANES), lambda i: (0, 0))],
        out_shape=[jax.ShapeDtypeStruct((n, dm), F32), jax.ShapeDtypeStruct((n, LANES), F32),
                   jax.ShapeDtypeStruct((1, LANES), F32)],
        compiler_params=_cparams(1), name="outproj_ln_router",
    )(merged, x, w_out, gain, bias, w_router, b_router)


def _expert_kernel(tile_expert_ref, tile_rows_ref, packed_ref, n_valid_ref, h_hbm, wg_ref, wu_ref,
                   wd_ref, y_hbm, xbuf, ybuf, gather_sem, scatter_sem):
    del tile_expert_ref
    i = pl.program_id(0)
    n_valid = n_valid_ref[0]
    tm = xbuf.shape[1]
    slot = i & 1

    def slot_words(tile, g):
        return [packed_ref[tile * tm + g * DMA_GROUP + k] for k in range(DMA_GROUP)]

    def for_buffer(dynamic_buf, fn):
        for buf in range(2):
            pl.when(dynamic_buf == buf)(functools.partial(fn, buf))

    def start_gather(tile, dynamic_buf):
        def issue(buf):
            for g in range(tm // DMA_GROUP):
                for k, word in enumerate(slot_words(tile, g)):
                    tok = word & ((1 << TOKEN_BITS) - 1)
                    pltpu.make_async_copy(h_hbm.at[pl.ds(tok, 1)],
                                          xbuf.at[buf, pl.ds(g * DMA_GROUP + k, 1)],
                                          gather_sem.at[buf]).start()
        for_buffer(dynamic_buf, issue)

    def wait_gather(buf):
        pltpu.make_async_copy(h_hbm.at[pl.ds(0, tm)], xbuf.at[buf], gather_sem.at[buf]).wait()

    def start_scatter(tile, dynamic_buf):
        n_rows = tile_rows_ref[tile]

        def issue(buf):
            for g in range(tm // DMA_GROUP):
                for k, word in enumerate(slot_words(tile, g)):
                    r = g * DMA_GROUP + k

                    @pl.when(r < n_rows)
                    def _():
                        pltpu.make_async_copy(ybuf.at[buf, pl.ds(r, 1)],
                                              y_hbm.at[pl.ds(word >> TOKEN_BITS, 1)],
                                              scatter_sem.at[buf]).start()
        for_buffer(dynamic_buf, issue)

    def wait_scatter(tile):
        buf = tile & 1
        n_rows = tile_rows_ref[tile]
        n_aligned = pl.multiple_of(n_rows - (n_rows & (SUBLANES - 1)), SUBLANES)

        @pl.when(n_aligned > 0)
        def _():
            rows = pl.ds(0, n_aligned)
            pltpu.make_async_copy(ybuf.at[buf, rows], y_hbm.at[rows], scatter_sem.at[buf]).wait()

        def wait_row(r, carry):
            pltpu.make_async_copy(ybuf.at[buf, pl.ds(r, 1)], y_hbm.at[pl.ds(r, 1)],
                                  scatter_sem.at[buf]).wait()
            return carry
        lax.fori_loop(n_aligned, n_rows, wait_row, 0)

    @pl.when(i == 0)
    def _():
        start_gather(0, 0)

    @pl.when(i + 1 < n_valid)
    def _():
        start_gather(i + 1, 1 - slot)

    @pl.when(i < n_valid)
    def _():
        wait_gather(slot)

        @pl.when(i >= 2)
        def _():
            wait_scatter(i - 2)

        x = xbuf[slot].astype(BF16)
        g = _dot(x, wg_ref[...])
        act = (g * _sigmoid(g) * _dot(x, wu_ref[...])).astype(BF16)
        ybuf[slot] = _dot(act, wd_ref[...])
        start_scatter(i, slot)

    @pl.when(i == pl.num_programs(0) - 1)
    def _():
        wait_scatter(n_valid - 1)

        @pl.when(n_valid >= 2)
        def _():
            wait_scatter(n_valid - 2)


def _experts(h, tile_expert, tile_rows, packed, n_valid, w_gate, w_up, w_down, n_rows_out):
    n, dm = h.shape
    n_tiles = tile_expert.shape[0]
    tm = packed.shape[0] // n_tiles
    ff = w_gate.shape[2]
    wspec = lambda a, b: pl.BlockSpec((None, a, b), lambda i, te, tr, pk, nv: (te[i], 0, 0))
    grid_spec = pltpu.PrefetchScalarGridSpec(
        num_scalar_prefetch=4, grid=(n_tiles,),
        in_specs=[pl.BlockSpec(memory_space=pl.ANY), wspec(dm, ff), wspec(dm, ff), wspec(ff, dm)],
        out_specs=pl.BlockSpec(memory_space=pl.ANY),
        scratch_shapes=[pltpu.VMEM((2, tm, dm), F32), pltpu.VMEM((2, tm, dm), F32),
                        pltpu.SemaphoreType.DMA((2,)), pltpu.SemaphoreType.DMA((2,))])
    return pl.pallas_call(
        _expert_kernel, grid_spec=grid_spec,
        out_shape=jax.ShapeDtypeStruct((n_rows_out, dm), F32),
        compiler_params=_cparams(1), name="experts",
    )(tile_expert, tile_rows, packed, n_valid, h, w_gate, w_up, w_down)


def _dispatch_tables(expert_ids, ranks, counts, tm):
    n_assign = expert_ids.shape[0]
    n_tiles = n_assign // tm + N_EXPERTS
    n_slots = n_tiles * tm
    experts = jnp.arange(N_EXPERTS, dtype=jnp.int32)
    padded = (counts + tm - 1) // tm * tm
    pad_end = jnp.cumsum(padded)
    pad_start = pad_end - padded
    slot_of = jnp.sum(jnp.where(expert_ids[:, None] == experts[None, :], pad_start[None, :], 0), axis=1) + ranks
    tile_start = jnp.arange(n_tiles, dtype=jnp.int32) * tm
    tile_expert = jnp.minimum(
        jnp.sum((pad_end[None, :] <= tile_start[:, None]).astype(jnp.int32), axis=1), N_EXPERTS - 1)
    tile_rows = jnp.clip((pad_start + counts)[tile_expert] - tile_start, 0, tm)
    assign = jnp.full((n_slots,), -1, jnp.int32).at[slot_of].set(jnp.arange(n_assign, dtype=jnp.int32))
    real = jnp.maximum(assign, 0)
    token, choice = real // TOP_K, real % TOP_K
    packed = ((choice * (n_assign // TOP_K) + token) << TOKEN_BITS) | token
    n_valid = (pad_end[-1] // tm).astype(jnp.int32).reshape(1)
    return tile_expert, tile_rows, packed, n_valid


def _final_kernel(h_ref, y1_ref, y2_ref, r_ref, gain_ref, bias_ref, o_ref, *, alpha):
    r = r_ref[...]
    ffn = y1_ref[...] * r[:, 2:3] + y2_ref[...] * r[:, 3:4]
    o_ref[...] = _layer_norm(alpha * h_ref[...] + ffn, gain_ref[...], bias_ref[...])


def _final(h, y_rows, routing, gain, bias, alpha):
    n, dm = h.shape
    tm = min(FINAL_TM, n)
    tile = pl.BlockSpec((tm, dm), lambda i: (i, 0))
    second = pl.BlockSpec((tm, dm), lambda i: (n // tm + i, 0))
    full = lambda a: pl.BlockSpec(a.shape, lambda i: (0, 0))
    return pl.pallas_call(
        functools.partial(_final_kernel, alpha=alpha), grid=(n // tm,),
        in_specs=[tile, tile, second, pl.BlockSpec((tm, LANES), lambda i: (i, 0)), full(gain), full(bias)],
        out_specs=tile, out_shape=jax.ShapeDtypeStruct((n, dm), F32),
        compiler_params=_cparams(1), name="combine_ln",
    )(h, y_rows, y_rows, routing, gain, bias)


def _lane_row(values):
    row = np.zeros((1, LANES), np.float32)
    row[0, :len(values)] = values
    return jnp.asarray(row)


def _rope_constants():
    ret_half = RET_QK_DIM // 2
    ret_freq = 1.0 / (RET_THETA ** (jnp.arange(0, RET_QK_DIM, 2, dtype=F32) / RET_QK_DIM))
    ret_freq = jnp.concatenate([ret_freq, ret_freq]).reshape(1, LANES)
    ret_sign = _lane_row([-1.0] * ret_half + [1.0] * ret_half)
    half = ROPE_DIM // 2
    diff_freq = 1.0 / (ROPE_THETA ** (jnp.arange(0, ROPE_DIM, 2, dtype=F32) / ROPE_DIM))
    diff_freq = jnp.concatenate([diff_freq, diff_freq, jnp.zeros((LANES - ROPE_DIM,), F32)]).reshape(1, LANES)
    diff_lo = _lane_row([-1.0] * half)
    diff_hi = _lane_row([0.0] * half + [1.0] * half)
    return ret_freq, ret_sign, diff_freq, diff_lo, diff_hi


def kernel(x, positions, w_in, ret_gn_gain, diff_lambda_q1, diff_lambda_k1, diff_lambda_q2, diff_lambda_k2, diff_subln_gain, w_ret_proj, w_diff_proj, w_out, ln1_gain, ln1_bias, w_group, b_group, w_router, b_router, w_expert_gate, w_expert_up, w_expert_down, ln2_gain, ln2_bias):
    batch, seq, dm = x.shape
    n = batch * seq
    depth = w_in.shape[0]
    assert n <= (1 << TOKEN_BITS) and seq % CHUNK == 0
    alpha = (2 * depth) ** 0.25

    ret_freq, ret_sign, diff_freq, diff_lo, diff_hi = _rope_constants()
    pos_col = positions.reshape(n, 1)
    ret_cos, ret_sin, _ = _rope_tables(pos_col, ret_freq, ret_sign, ret_sign)
    diff_cos, diff_sin_lo, diff_sin_hi = _rope_tables(pos_col, diff_freq, diff_lo, diff_hi)

    log_gamma = jnp.log(1.0 - 2.0 ** (-5.0 - jnp.arange(RET_HEADS, dtype=F32)))
    block_decay = jnp.exp(log_gamma * min(RET_BLOCK, seq))

    xt = x.reshape(n, dm)
    for l in range(depth):
        lambda_init = 0.8 - 0.6 * math.exp(-0.3 * l)
        u = _matmul(xt.astype(BF16), w_in[l], BF16, "in_proj")
        ret = _retention(u, ret_cos, ret_sin, log_gamma, block_decay,
                         ret_gn_gain[l].reshape(RET_HEADS, 1, RET_V_DIM), batch, seq)
        lam_params = jnp.stack([diff_lambda_q1[l], diff_lambda_k1[l], diff_lambda_q2[l], diff_lambda_k2[l]])
        diff = _diff_attention(u, diff_cos, diff_sin_lo, diff_sin_hi, lam_params.astype(F32),
                               diff_subln_gain[l].reshape(1, DIFF_V_DIM), batch, seq, lambda_init)
        merged = _merge(ret, diff, u, w_ret_proj[l].astype(BF16), w_diff_proj[l].astype(BF16))

        w_route = jnp.concatenate(
            [w_group[l], w_router[l], jnp.zeros((dm, LANES - N_GROUPS - N_EXPERTS), F32)], axis=1)
        b_route = jnp.concatenate(
            [b_group[l], b_router[l], jnp.zeros((LANES - N_GROUPS - N_EXPERTS,), F32)]).reshape(1, LANES)
        h, routing, counts = _route(merged, xt, w_out[l].astype(BF16), ln1_gain[l].reshape(1, dm),
                                    ln1_bias[l].reshape(1, dm), w_route, b_route, alpha)

        expert_ids = routing[:, :TOP_K].astype(jnp.int32).reshape(-1)
        ranks = routing[:, 2 * TOP_K:3 * TOP_K].astype(jnp.int32).reshape(-1)
        tile_expert, tile_rows, packed, n_valid = _dispatch_tables(
            expert_ids, ranks, counts[0, :N_EXPERTS].astype(jnp.int32), min(MOE_TM, n))
        y_rows = _experts(h, tile_expert, tile_rows, packed, n_valid, w_expert_gate[l].astype(BF16),
                          w_expert_up[l].astype(BF16), w_expert_down[l].astype(BF16), TOP_K * n)
        xt = _final(h, y_rows, routing,
                    ln2_gain[l].reshape(1, dm), ln2_bias[l].reshape(1, dm), alpha)
    return xt.reshape(batch, seq, dm)
```

```python
import functools
import math

import jax
import jax.numpy as jnp
import numpy as np
from jax import lax
from jax.experimental import pallas as pl
from jax.experimental.pallas import tpu as pltpu

F32 = jnp.float32
BF16 = jnp.bfloat16

CHUNK = 64
RET_HEADS = 8
RET_QK_DIM = 128
RET_V_DIM = 256
RET_THETA = 10000.0
DIFF_HEADS = 8
DIFF_QK_DIM = 128
DIFF_V_DIM = 256
ROPE_THETA = 500000.0
ROPE_DIM = DIFF_QK_DIM // 4
N_GROUPS = 4
EXPERTS_PER_GROUP = 8
N_EXPERTS = N_GROUPS * EXPERTS_PER_GROUP
TOP_K = 2
LN_EPS = 1e-5

LANES = 128
SUBLANES = 8
BF16_ROWS = 16
NEG_BIG = -1e30
TOKEN_BITS = 14

VMEM_LIMIT_BYTES = 56 * 1024 * 1024
TABLE_ROWS = 512
PROJ_TM = 1024
PROJ_TN = 1024
RET_BLOCK = 256
RET_HEAD_GROUP = 8
ATT_BLOCK = 512
ATT_QUERY_SPLITS = 1
MERGE_TM = 256
ROUTE_TM = 256
MOE_TM = 256
DMA_GROUP = 8
FINAL_TM = 256


def _cparams(n_axes):
    return pltpu.CompilerParams(
        dimension_semantics=("arbitrary",) * n_axes, vmem_limit_bytes=VMEM_LIMIT_BYTES)


def _sigmoid(x):
    return 1.0 / (1.0 + jnp.exp(-x))


def _chunk_of(idx):
    return jnp.right_shift(idx, CHUNK.bit_length() - 1)


def _dot(a, b):
    return jnp.dot(a, b, preferred_element_type=F32)


def _dot_nt(a, b):
    return lax.dot_general(a, b, (((1,), (1,)), ((), ())), preferred_element_type=F32)


def _table_kernel(pos_ref, freq_ref, sa_ref, sb_ref, cos_ref, sina_ref, sinb_ref):
    ang = pos_ref[...].astype(F32) * freq_ref[...]
    s = jnp.sin(ang)
    cos_ref[...] = jnp.cos(ang)
    sina_ref[...] = s * sa_ref[...]
    sinb_ref[...] = s * sb_ref[...]


def _rope_tables(pos_col, freq_row, sign_a, sign_b):
    n = pos_col.shape[0]
    tr = min(TABLE_ROWS, n)
    row = pl.BlockSpec((1, LANES), lambda i: (0, 0))
    out = pl.BlockSpec((tr, LANES), lambda i: (i, 0))
    shp = jax.ShapeDtypeStruct((n, LANES), F32)
    return pl.pallas_call(
        _table_kernel, grid=(n // tr,),
        in_specs=[pl.BlockSpec((tr, 1), lambda i: (i, 0)), row, row, row],
        out_specs=[out, out, out], out_shape=[shp, shp, shp],
        compiler_params=_cparams(1), name="rope_tables",
    )(pos_col, freq_row, sign_a, sign_b)


def _matmul_kernel(x_ref, w_ref, o_ref, wb_ref):
    @pl.when(pl.program_id(1) == 0)
    def _():
        wb_ref[...] = w_ref[...].astype(BF16)

    o_ref[...] = _dot(x_ref[...], wb_ref[...]).astype(o_ref.dtype)


def _matmul(x, w, out_dtype, name):
    m, k = x.shape
    n = w.shape[1]
    tm, tn = min(PROJ_TM, m), min(PROJ_TN, n)
    return pl.pallas_call(
        _matmul_kernel, grid=(n // tn, m // tm),
        in_specs=[pl.BlockSpec((tm, k), lambda j, i: (i, 0)),
                  pl.BlockSpec((k, tn), lambda j, i: (0, j))],
        out_specs=pl.BlockSpec((tm, tn), lambda j, i: (i, j)),
        out_shape=jax.ShapeDtypeStruct((m, n), out_dtype),
        scratch_shapes=[pltpu.VMEM((k, tn), BF16)],
        compiler_params=_cparams(2), name=name,
    )(x, w)


def _retention_kernel(lg_ref, bd_ref, q_ref, k_ref, v_ref, g_ref, cos_ref, sin_ref, gain_ref,
                      o_ref, state_ref, decay_ref, qdecay_ref, kdecay_ref):
    group = state_ref.shape[0]
    first_head = pl.program_id(1) * group
    t = q_ref.shape[0]
    dk, dv = RET_QK_DIM, RET_V_DIM

    @pl.when(pl.program_id(2) == 0)
    def _():
        state_ref[...] = jnp.zeros_like(state_ref)
        ri = lax.broadcasted_iota(jnp.int32, (t, t), 0)
        ci = lax.broadcasted_iota(jnp.int32, (t, t), 1)
        dist = jnp.abs(ri - ci).astype(F32)
        visible = _chunk_of(ri) >= _chunk_of(ci)
        row = lax.broadcasted_iota(jnp.int32, (t, dk), 0).astype(F32)
        for j in range(group):
            lg = lg_ref[first_head + j]
            decay_ref[j] = jnp.where(visible, jnp.exp(lg * dist), 0.0)
            qdecay_ref[j] = jnp.exp(lg * (row + 1.0))
            kdecay_ref[j] = jnp.exp(lg * (t - 1.0 - row))

    cos, sin = cos_ref[...], sin_ref[...]

    def rot(x):
        return x * cos + pltpu.roll(x, dk // 2, 1) * sin

    for j in range(group):
        q = rot(q_ref[:, j * dk:(j + 1) * dk].astype(F32))
        k = rot(k_ref[:, j * dk:(j + 1) * dk].astype(F32)) * (dk ** -0.5)
        v = v_ref[:, j * dv:(j + 1) * dv]

        s = _dot_nt(q.astype(BF16), k.astype(BF16)) * decay_ref[j]
        intra = _dot(s.astype(BF16), v)

        state = state_ref[j]
        cross = _dot((q * qdecay_ref[j]).astype(BF16), state.astype(BF16))
        kd = k * kdecay_ref[j]
        state_ref[j] = state * bd_ref[first_head + j] + _dot(kd.T.astype(BF16), v)

        out = intra + cross
        mu = jnp.mean(out, axis=-1, keepdims=True)
        var = jnp.mean(jnp.square(out - mu), axis=-1, keepdims=True)
        y = (out - mu) * lax.rsqrt(var + LN_EPS) * gain_ref[j]
        gate = g_ref[:, j * dv:(j + 1) * dv].astype(F32)
        o_ref[:, j * dv:(j + 1) * dv] = (y * (gate * _sigmoid(gate))).astype(o_ref.dtype)


def _retention(u, cos, sin, log_gamma, block_decay, gain, batch, seq):
    t = min(RET_BLOCK, seq)
    nb = seq // t
    grp = RET_HEAD_GROUP
    n_groups = RET_HEADS // grp
    rows = lambda b, h, i: b * nb + i
    qk_w, v_w = grp * RET_QK_DIM, grp * RET_V_DIM
    k_off, v_off, g_off = n_groups, n_groups, 2 * n_groups
    smem = pl.BlockSpec(memory_space=pltpu.SMEM)
    return pl.pallas_call(
        _retention_kernel, grid=(batch, n_groups, nb),
        in_specs=[
            smem, smem,
            pl.BlockSpec((t, qk_w), lambda b, h, i: (rows(b, h, i), h)),
            pl.BlockSpec((t, qk_w), lambda b, h, i: (rows(b, h, i), k_off + h)),
            pl.BlockSpec((t, v_w), lambda b, h, i: (rows(b, h, i), v_off + h)),
            pl.BlockSpec((t, v_w), lambda b, h, i: (rows(b, h, i), g_off + h)),
            pl.BlockSpec((t, LANES), lambda b, h, i: (rows(b, h, i), 0)),
            pl.BlockSpec((t, LANES), lambda b, h, i: (rows(b, h, i), 0)),
            pl.BlockSpec((grp, 1, RET_V_DIM), lambda b, h, i: (h, 0, 0)),
        ],
        out_specs=pl.BlockSpec((t, v_w), lambda b, h, i: (rows(b, h, i), h)),
        out_shape=jax.ShapeDtypeStruct((batch * seq, RET_HEADS * RET_V_DIM), BF16),
        scratch_shapes=[pltpu.VMEM((grp, RET_QK_DIM, RET_V_DIM), F32), pltpu.VMEM((grp, t, t), F32),
                        pltpu.VMEM((grp, t, RET_QK_DIM), F32), pltpu.VMEM((grp, t, RET_QK_DIM), F32)],
        compiler_params=_cparams(3), name="retention",
    )(log_gamma, block_decay, u, u, u, u, cos, sin, gain)


def _partial_rope(x, cos, sin_lo, sin_hi):
    half = ROPE_DIM // 2
    return (x * cos + pltpu.roll(x, DIFF_QK_DIM - half, 1) * sin_lo
            + pltpu.roll(x, half, 1) * sin_hi)


def _diff_attn_kernel(lam_ref, q_ref, k_ref, v_ref, qc_ref, qa_ref, qb_ref, kc_ref, ka_ref, kb_ref,
                      gain_ref, o_ref, krot_ref, vt_ref, m_ref, acc_ref, sa_ref, sb_ref, *,
                      lambda_init):
    qblk = pl.program_id(2)
    tq = q_ref.shape[0]
    d = DIFF_QK_DIM

    @pl.when(qblk == 0)
    def _():
        def prep_rows(c, carry):
            rows = pl.ds(pl.multiple_of(c * tq, tq), tq)
            for i in range(2):
                x = k_ref[rows, i * d:(i + 1) * d].astype(F32)
                krot_ref[rows, i * d:(i + 1) * d] = _partial_rope(
                    x, kc_ref[rows, :], ka_ref[rows, :], kb_ref[rows, :]).astype(BF16)
            vt_ref[c, :DIFF_V_DIM] = v_ref[rows, :].astype(F32).T.astype(BF16)
            vt_ref[c, DIFF_V_DIM:] = jnp.ones((BF16_ROWS, tq), BF16)
            return carry
        lax.fori_loop(0, k_ref.shape[0] // tq, prep_rows, 0)

    scale = d ** -0.5
    qs = []
    for i in range(2):
        x = q_ref[:, i * d:(i + 1) * d].astype(F32)
        qs.append((_partial_rope(x, qc_ref[...], qa_ref[...], qb_ref[...]) * scale).astype(BF16))

    m_ref[...] = jnp.full_like(m_ref, NEG_BIG)
    acc_ref[...] = jnp.zeros_like(acc_ref)

    hw = tq // ATT_QUERY_SPLITS
    s_bufs = (sa_ref, sb_ref)

    def scores(j, buf, diagonal):
        rows = pl.ds(pl.multiple_of(j * tq, tq), tq)
        for i in range(2):
            kj = krot_ref[rows, i * d:(i + 1) * d]
            for c in range(ATT_QUERY_SPLITS):
                cols = slice(c * hw, (c + 1) * hw)
                s = _dot_nt(kj, qs[i][cols])
                if diagonal:
                    key = lax.broadcasted_iota(jnp.int32, s.shape, 0)
                    qry = lax.broadcasted_iota(jnp.int32, s.shape, 1) + c * hw
                    s = jnp.where(_chunk_of(qry) >= _chunk_of(key), s, NEG_BIG)
                s_bufs[buf][i, :, cols] = s

    def absorb(j, buf):
        vt = vt_ref[j]
        for i in range(2):
            for c in range(ATT_QUERY_SPLITS):
                cols = slice(c * hw, (c + 1) * hw)
                s = s_bufs[buf][i, :, cols]
                m_old = m_ref[i, :, cols]
                m_new = jnp.maximum(m_old, jnp.max(s, axis=0, keepdims=True))
                alpha = jnp.exp(m_old - m_new)
                p = jnp.exp(s - m_new)
                acc_ref[i, :, cols] = alpha * acc_ref[i, :, cols] + _dot(vt, p.astype(BF16))
                m_ref[i, :, cols] = m_new

    scores(qblk, 0, True)

    def pair(p, carry):
        pending = jnp.where(p == 0, qblk, 2 * p - 1)
        scores(2 * p, 1, False)
        absorb(pending, 0)
        scores(2 * p + 1, 0, False)
        absorb(2 * p, 1)
        return carry
    n_pairs = qblk // 2
    lax.fori_loop(0, n_pairs, pair, 0)
    pending = jnp.where(n_pairs == 0, qblk, 2 * n_pairs - 1)

    @pl.when(qblk % 2 == 1)
    def _():
        scores(qblk - 1, 1, False)
        absorb(pending, 0)
        absorb(qblk - 1, 1)

    @pl.when(qblk % 2 == 0)
    def _():
        absorb(pending, 0)

    lp = lam_ref[...]
    lam = (jnp.exp(jnp.sum(lp[0:1] * lp[1:2], axis=-1, keepdims=True))
           - jnp.exp(jnp.sum(lp[2:3] * lp[3:4], axis=-1, keepdims=True)) + lambda_init)
    dv = DIFF_V_DIM
    o = (acc_ref[0, :dv] / acc_ref[0, dv:dv + 1] - lam * (acc_ref[1, :dv] / acc_ref[1, dv:dv + 1])).T
    o = o * lax.rsqrt(jnp.mean(jnp.square(o), axis=-1, keepdims=True) + LN_EPS) * gain_ref[...]
    o_ref[...] = (o * (1.0 - lambda_init)).astype(o_ref.dtype)


def _diff_attention(u, cos, sin_lo, sin_hi, lam_params, gain, batch, seq, lambda_init):
    tq = min(ATT_BLOCK, seq)
    nb = seq // tq
    width = 2 * DIFF_QK_DIM
    ret_cols = 2 * RET_HEADS * RET_QK_DIM + 2 * RET_HEADS * RET_V_DIM
    q_off = ret_cols // width
    k_off = q_off + DIFF_HEADS
    v_off = k_off + DIFF_HEADS
    qrow = lambda b, h, i: b * nb + i
    qtab = pl.BlockSpec((tq, LANES), lambda b, h, i: (qrow(b, h, i), 0))
    ktab = pl.BlockSpec((seq, LANES), lambda b, h, i: (b, 0))
    return pl.pallas_call(
        functools.partial(_diff_attn_kernel, lambda_init=lambda_init),
        grid=(batch, DIFF_HEADS, nb),
        in_specs=[
            pl.BlockSpec((4, DIFF_QK_DIM), lambda b, h, i: (0, 0)),
            pl.BlockSpec((tq, width), lambda b, h, i: (qrow(b, h, i), q_off + h)),
            pl.BlockSpec((seq, width), lambda b, h, i: (b, k_off + h)),
            pl.BlockSpec((seq, DIFF_V_DIM), lambda b, h, i: (b, v_off + h)),
            qtab, qtab, qtab, ktab, ktab, ktab,
            pl.BlockSpec((1, DIFF_V_DIM), lambda b, h, i: (0, 0)),
        ],
        out_specs=pl.BlockSpec((tq, DIFF_V_DIM), lambda b, h, i: (qrow(b, h, i), h)),
        out_shape=jax.ShapeDtypeStruct((batch * seq, DIFF_HEADS * DIFF_V_DIM), BF16),
        scratch_shapes=[pltpu.VMEM((seq, width), BF16),
                        pltpu.VMEM((nb, DIFF_V_DIM + BF16_ROWS, tq), BF16),
                        pltpu.VMEM((2, 1, tq), F32),
                        pltpu.VMEM((2, DIFF_V_DIM + BF16_ROWS, tq), F32),
                        pltpu.VMEM((2, tq, tq), F32), pltpu.VMEM((2, tq, tq), F32)],
        compiler_params=_cparams(3), name="diff_attention",
    )(lam_params, u, u, u, cos, sin_lo, sin_hi, cos, sin_lo, sin_hi, gain)


def _merge_kernel(ret_ref, diff_ref, gr_ref, gd_ref, wr_ref, wd_ref, o_ref):
    a = _dot(ret_ref[...], wr_ref[...])
    b = _dot(diff_ref[...], wd_ref[...])
    merged = _sigmoid(gr_ref[...].astype(F32)) * a + _sigmoid(gd_ref[...].astype(F32)) * b
    o_ref[...] = merged.astype(o_ref.dtype)


def _merge(ret, diff, u, w_ret, w_diff):
    n, dm = ret.shape[0], w_ret.shape[1]
    tm = min(MERGE_TM, n)
    gate_off = (u.shape[1] - 2 * dm) // dm
    tile = lambda c: pl.BlockSpec((tm, dm), lambda i: (i, c))
    weight = lambda w: pl.BlockSpec(w.shape, lambda i: (0, 0))
    return pl.pallas_call(
        _merge_kernel, grid=(n // tm,),
        in_specs=[pl.BlockSpec((tm, ret.shape[1]), lambda i: (i, 0)),
                  pl.BlockSpec((tm, diff.shape[1]), lambda i: (i, 0)),
                  tile(gate_off), tile(gate_off + 1), weight(w_ret), weight(w_diff)],
        out_specs=tile(0), out_shape=jax.ShapeDtypeStruct((n, dm), BF16),
        compiler_params=_cparams(1), name="gated_merge",
    )(ret, diff, u, u, w_ret, w_diff)


def _layer_norm(z, gain, bias):
    mu = jnp.mean(z, axis=-1, keepdims=True)
    var = jnp.mean(jnp.square(z - mu), axis=-1, keepdims=True)
    return (z - mu) * lax.rsqrt(var + LN_EPS) * gain + bias


def _split_bf16(a):
    hi = a.astype(BF16)
    return hi, (a - hi.astype(F32)).astype(BF16)


def _route_kernel(m_ref, x_ref, wo_ref, gain_ref, bias_ref, wr_ref, br_ref, h_ref, r_ref, count_ref,
                  *, alpha):
    @pl.when(pl.program_id(0) == 0)
    def _():
        count_ref[...] = jnp.zeros_like(count_ref)

    h = _layer_norm(alpha * x_ref[...] + _dot(m_ref[...], wo_ref[...]), gain_ref[...], bias_ref[...])
    h_ref[...] = h

    h_hi, h_lo = _split_bf16(h)
    w_hi, w_lo = _split_bf16(wr_ref[...])
    logits = _dot(h_hi, w_hi) + (_dot(h_hi, w_lo) + _dot(h_lo, w_hi)) + br_ref[...]

    lane = lax.broadcasted_iota(jnp.int32, logits.shape, 1).astype(F32)

    def first_argmax(vals):
        top = jnp.max(vals, axis=-1, keepdims=True)
        idx = jnp.min(jnp.where(vals == top, lane, float(LANES)), axis=-1, keepdims=True)
        return top, idx

    group_logits = jnp.where(lane < N_GROUPS, logits, NEG_BIG)
    g_top, g_idx = first_argmax(group_logits)
    p_group = 1.0 / jnp.sum(jnp.exp(group_logits - g_top), axis=-1, keepdims=True)

    first = N_GROUPS + g_idx * EXPERTS_PER_GROUP
    within = jnp.where((lane >= first) & (lane < first + EXPERTS_PER_GROUP), logits, NEG_BIG)
    v1, i1 = first_argmax(within)
    v2, i2 = first_argmax(jnp.where(lane == i1, NEG_BIG, within))
    e2 = jnp.exp(v2 - v1)
    w1 = p_group / (1.0 + e2)
    w2 = p_group * e2 / (1.0 + e2)
    x1, x2 = i1 - N_GROUPS, i2 - N_GROUPS

    hit1 = jnp.where(lane == x1, 1.0, 0.0)
    hit2 = jnp.where(lane == x2, 1.0, 0.0)
    hits = hit1 + hit2
    tm = hits.shape[0]
    earlier = (lax.broadcasted_iota(jnp.int32, (tm, tm), 0) > lax.broadcasted_iota(jnp.int32, (tm, tm), 1))
    before = count_ref[...] + _dot(jnp.where(earlier, 1.0, 0.0).astype(BF16), hits.astype(BF16))
    rank1 = jnp.sum(before * hit1, axis=-1, keepdims=True)
    rank2 = jnp.sum(before * hit2, axis=-1, keepdims=True)
    count_ref[...] += jnp.sum(hits, axis=0, keepdims=True)

    fields = (x1, x2, w1, w2, rank1, rank2)
    row = jnp.zeros_like(logits)
    for pos, value in enumerate(fields):
        row = jnp.where(lane == pos, value, row)
    r_ref[...] = row


def _route(merged, x, w_out, gain, bias, w_router, b_router, alpha):
    n, dm = x.shape
    tm = min(ROUTE_TM, n)
    tile = pl.BlockSpec((tm, dm), lambda i: (i, 0))
    full = lambda a: pl.BlockSpec(a.shape, lambda i: (0, 0))
    return pl.pallas_call(
        functools.partial(_route_kernel, alpha=alpha), grid=(n // tm,),
        in_specs=[tile, tile, full(w_out), full(gain), full(bias), full(w_router), full(b_router)],
        out_specs=[tile, pl.BlockSpec((tm, LANES), lambda i: (i, 0)),
                   pl.BlockSpec((1, LANES), lambda i: (0, 0))],
        out_shape=[jax.ShapeDtypeStruct((n, dm), F32), jax.ShapeDtypeStruct((n, LANES), F32),
                   jax.ShapeDtypeStruct((1, LANES), F32)],
        compiler_params=_cparams(1), name="outproj_ln_router",
    )(merged, x, w_out, gain, bias, w_router, b_router)


def _expert_kernel(tile_expert_ref, tile_rows_ref, packed_ref, n_valid_ref, h_hbm, wg_ref, wu_ref,
                   wd_ref, y_hbm, xbuf, ybuf, gather_sem, scatter_sem):
    del tile_expert_ref
    i = pl.program_id(0)
    n_valid = n_valid_ref[0]
    tm = xbuf.shape[1]
    slot = i & 1

    def slot_words(tile, g):
        return [packed_ref[tile * tm + g * DMA_GROUP + k] for k in range(DMA_GROUP)]

    def for_buffer(dynamic_buf, fn):
        for buf in range(2):
            pl.when(dynamic_buf == buf)(functools.partial(fn, buf))

    def start_gather(tile, dynamic_buf):
        def issue(buf):
            for g in range(tm // DMA_GROUP):
                for k, word in enumerate(slot_words(tile, g)):
                    tok = word & ((1 << TOKEN_BITS) - 1)
                    pltpu.make_async_copy(h_hbm.at[pl.ds(tok, 1)],
                                          xbuf.at[buf, pl.ds(g * DMA_GROUP + k, 1)],
                                          gather_sem.at[buf]).start()
        for_buffer(dynamic_buf, issue)

    def wait_gather(buf):
        pltpu.make_async_copy(h_hbm.at[pl.ds(0, tm)], xbuf.at[buf], gather_sem.at[buf]).wait()

    def start_scatter(tile, dynamic_buf):
        n_rows = tile_rows_ref[tile]

        def issue(buf):
            for g in range(tm // DMA_GROUP):
                for k, word in enumerate(slot_words(tile, g)):
                    r = g * DMA_GROUP + k

                    @pl.when(r < n_rows)
                    def _():
                        pltpu.make_async_copy(ybuf.at[buf, pl.ds(r, 1)],
                                              y_hbm.at[pl.ds(word >> TOKEN_BITS, 1)],
                                              scatter_sem.at[buf]).start()
        for_buffer(dynamic_buf, issue)

    def wait_scatter(tile):
        buf = tile & 1
        n_rows = tile_rows_ref[tile]
        n_aligned = pl.multiple_of(n_rows - (n_rows & (SUBLANES - 1)), SUBLANES)

        @pl.when(n_aligned > 0)
        def _():
            rows = pl.ds(0, n_aligned)
            pltpu.make_async_copy(ybuf.at[buf, rows], y_hbm.at[rows], scatter_sem.at[buf]).wait()

        def wait_row(r, carry):
            pltpu.make_async_copy(ybuf.at[buf, pl.ds(r, 1)], y_hbm.at[pl.ds(r, 1)],
                                  scatter_sem.at[buf]).wait()
            return carry
        lax.fori_loop(n_aligned, n_rows, wait_row, 0)

    @pl.when(i == 0)
    def _():
        start_gather(0, 0)

    @pl.when(i + 1 < n_valid)
    def _():
        start_gather(i + 1, 1 - slot)

    @pl.when(i < n_valid)
    def _():
        wait_gather(slot)

        @pl.when(i >= 2)
        def _():
            wait_scatter(i - 2)

        x = xbuf[slot].astype(BF16)
        g = _dot(x, wg_ref[...])
        act = (g * _sigmoid(g) * _dot(x, wu_ref[...])).astype(BF16)
        ybuf[slot] = _dot(act, wd_ref[...])
        start_scatter(i, slot)

    @pl.when(i == pl.num_programs(0) - 1)
    def _():
        wait_scatter(n_valid - 1)

        @pl.when(n_valid >= 2)
        def _():
            wait_scatter(n_valid - 2)


def _experts(h, tile_expert, tile_rows, packed, n_valid, w_gate, w_up, w_down, n_rows_out):
    n, dm = h.shape
    n_tiles = tile_expert.shape[0]
    tm = packed.shape[0] // n_tiles
    ff = w_gate.shape[2]
    wspec = lambda a, b: pl.BlockSpec((None, a, b), lambda i, te, tr, pk, nv: (te[i], 0, 0))
    grid_spec = pltpu.PrefetchScalarGridSpec(
        num_scalar_prefetch=4, grid=(n_tiles,),
        in_specs=[pl.BlockSpec(memory_space=pl.ANY), wspec(dm, ff), wspec(dm, ff), wspec(ff, dm)],
        out_specs=pl.BlockSpec(memory_space=pl.ANY),
        scratch_shapes=[pltpu.VMEM((2, tm, dm), F32), pltpu.VMEM((2, tm, dm), F32),
                        pltpu.SemaphoreType.DMA((2,)), pltpu.SemaphoreType.DMA((2,))])
    return pl.pallas_call(
        _expert_kernel, grid_spec=grid_spec,
        out_shape=jax.ShapeDtypeStruct((n_rows_out, dm), F32),
        compiler_params=_cparams(1), name="experts",
    )(tile_expert, tile_rows, packed, n_valid, h, w_gate, w_up, w_down)


def _dispatch_tables(expert_ids, ranks, counts, tm):
    n_assign = expert_ids.shape[0]
    n_tiles = n_assign // tm + N_EXPERTS
    n_slots = n_tiles * tm
    experts = jnp.arange(N_EXPERTS, dtype=jnp.int32)
    padded = (counts + tm - 1) // tm * tm
    pad_end = jnp.cumsum(padded)
    pad_start = pad_end - padded
    slot_of = jnp.sum(jnp.where(expert_ids[:, None] == experts[None, :], pad_start[None, :], 0), axis=1) + ranks
    tile_start = jnp.arange(n_tiles, dtype=jnp.int32) * tm
    tile_expert = jnp.minimum(
        jnp.sum((pad_end[None, :] <= tile_start[:, None]).astype(jnp.int32), axis=1), N_EXPERTS - 1)
    tile_rows = jnp.clip((pad_start + counts)[tile_expert] - tile_start, 0, tm)
    assign = jnp.full((n_slots,), -1, jnp.int32).at[slot_of].set(jnp.arange(n_assign, dtype=jnp.int32))
    real = jnp.maximum(assign, 0)
    token, choice = real // TOP_K, real % TOP_K
    packed = ((choice * (n_assign // TOP_K) + token) << TOKEN_BITS) | token
    n_valid = (pad_end[-1] // tm).astype(jnp.int32).reshape(1)
    return tile_expert, tile_rows, packed, n_valid


def _final_kernel(h_ref, y1_ref, y2_ref, r_ref, gain_ref, bias_ref, o_ref, *, alpha):
    r = r_ref[...]
    ffn = y1_ref[...] * r[:, 2:3] + y2_ref[...] * r[:, 3:4]
    o_ref[...] = _layer_norm(alpha * h_ref[...] + ffn, gain_ref[...], bias_ref[...])


def _final(h, y_rows, routing, gain, bias, alpha):
    n, dm = h.shape
    tm = min(FINAL_TM, n)
    tile = pl.BlockSpec((tm, dm), lambda i: (i, 0))
    second = pl.BlockSpec((tm, dm), lambda i: (n // tm + i, 0))
    full = lambda a: pl.BlockSpec(a.shape, lambda i: (0, 0))
    return pl.pallas_call(
        functools.partial(_final_kernel, alpha=alpha), grid=(n // tm,),
        in_specs=[tile, tile, second, pl.BlockSpec((tm, LANES), lambda i: (i, 0)), full(gain), full(bias)],
        out_specs=tile, out_shape=jax.ShapeDtypeStruct((n, dm), F32),
        compiler_params=_cparams(1), name="combine_ln",
    )(h, y_rows, y_rows, routing, gain, bias)


def _lane_row(values):
    row = np.zeros((1, LANES), np.float32)
    row[0, :len(values)] = values
    return jnp.asarray(row)


def _rope_constants():
    ret_half = RET_QK_DIM // 2
    ret_freq = 1.0 / (RET_THETA ** (jnp.arange(0, RET_QK_DIM, 2, dtype=F32) / RET_QK_DIM))
    ret_freq = jnp.concatenate([ret_freq, ret_freq]).reshape(1, LANES)
    ret_sign = _lane_row([-1.0] * ret_half + [1.0] * ret_half)
    half = ROPE_DIM // 2
    diff_freq = 1.0 / (ROPE_THETA ** (jnp.arange(0, ROPE_DIM, 2, dtype=F32) / ROPE_DIM))
    diff_freq = jnp.concatenate([diff_freq, diff_freq, jnp.zeros((LANES - ROPE_DIM,), F32)]).reshape(1, LANES)
    diff_lo = _lane_row([-1.0] * half)
    diff_hi = _lane_row([0.0] * half + [1.0] * half)
    return ret_freq, ret_sign, diff_freq, diff_lo, diff_hi


def kernel(x, positions, w_in, ret_gn_gain, diff_lambda_q1, diff_lambda_k1, diff_lambda_q2, diff_lambda_k2, diff_subln_gain, w_ret_proj, w_diff_proj, w_out, ln1_gain, ln1_bias, w_group, b_group, w_router, b_router, w_expert_gate, w_expert_up, w_expert_down, ln2_gain, ln2_bias):
    batch, seq, dm = x.shape
    n = batch * seq
    depth = w_in.shape[0]
    assert n <= (1 << TOKEN_BITS) and seq % CHUNK == 0
    alpha = (2 * depth) ** 0.25

    ret_freq, ret_sign, diff_freq, diff_lo, diff_hi = _rope_constants()
    pos_col = positions.reshape(n, 1)
    ret_cos, ret_sin, _ = _rope_tables(pos_col, ret_freq, ret_sign, ret_sign)
    diff_cos, diff_sin_lo, diff_sin_hi = _rope_tables(pos_col, diff_freq, diff_lo, diff_hi)

    log_gamma = jnp.log(1.0 - 2.0 ** (-5.0 - jnp.arange(RET_HEADS, dtype=F32)))
    block_decay = jnp.exp(log_gamma * min(RET_BLOCK, seq))

    xt = x.reshape(n, dm)
    for l in range(depth):
        lambda_init = 0.8 - 0.6 * math.exp(-0.3 * l)
        u = _matmul(xt.astype(BF16), w_in[l], BF16, "in_proj")
        ret = _retention(u, ret_cos, ret_sin, log_gamma, block_decay,
                         ret_gn_gain[l].reshape(RET_HEADS, 1, RET_V_DIM), batch, seq)
        lam_params = jnp.stack([diff_lambda_q1[l], diff_lambda_k1[l], diff_lambda_q2[l], diff_lambda_k2[l]])
        diff = _diff_attention(u, diff_cos, diff_sin_lo, diff_sin_hi, lam_params.astype(F32),
                               diff_subln_gain[l].reshape(1, DIFF_V_DIM), batch, seq, lambda_init)
        merged = _merge(ret, diff, u, w_ret_proj[l].astype(BF16), w_diff_proj[l].astype(BF16))

        w_route = jnp.concatenate(
            [w_group[l], w_router[l], jnp.zeros((dm, LANES - N_GROUPS - N_EXPERTS), F32)], axis=1)
        b_route = jnp.concatenate(
            [b_group[l], b_router[l], jnp.zeros((LANES - N_GROUPS - N_EXPERTS,), F32)]).reshape(1, LANES)
        h, routing, counts = _route(merged, xt, w_out[l].astype(BF16), ln1_gain[l].reshape(1, dm),
                                    ln1_bias[l].reshape(1, dm), w_route, b_route, alpha)

        expert_ids = routing[:, :TOP_K].astype(jnp.int32).reshape(-1)
        ranks = routing[:, 2 * TOP_K:3 * TOP_K].astype(jnp.int32).reshape(-1)
        tile_expert, tile_rows, packed, n_valid = _dispatch_tables(
            expert_ids, ranks, counts[0, :N_EXPERTS].astype(jnp.int32), min(MOE_TM, n))
        y_rows = _experts(h, tile_expert, tile_rows, packed, n_valid, w_expert_gate[l].astype(BF16),
                          w_expert_up[l].astype(BF16), w_expert_down[l].astype(BF16), TOP_K * n)
        xt = _final(h, y_rows, routing,
                    ln2_gain[l].reshape(1, dm), ln2_bias[l].reshape(1, dm), alpha)
    return xt.reshape(batch, seq, dm)
```

```python
import functools
import math

import jax
import jax.numpy as jnp
import numpy as np
from jax import lax
from jax.experimental import pallas as pl
from jax.experimental.pallas import tpu as pltpu

F32 = jnp.float32
BF16 = jnp.bfloat16

CHUNK = 64
RET_HEADS = 8
RET_QK_DIM = 128
RET_V_DIM = 256
RET_THETA = 10000.0
DIFF_HEADS = 8
DIFF_QK_DIM = 128
DIFF_V_DIM = 256
ROPE_THETA = 500000.0
ROPE_DIM = DIFF_QK_DIM // 4
N_GROUPS = 4
EXPERTS_PER_GROUP = 8
N_EXPERTS = N_GROUPS * EXPERTS_PER_GROUP
TOP_K = 2
LN_EPS = 1e-5

LANES = 128
SUBLANES = 8
BF16_ROWS = 16
NEG_BIG = -1e30
TOKEN_BITS = 14

VMEM_LIMIT_BYTES = 56 * 1024 * 1024
TABLE_ROWS = 512
PROJ_TM = 1024
PROJ_TN = 1024
RET_BLOCK = 256
RET_HEAD_GROUP = 8
ATT_BLOCK = 1024
ATT_QUERY_SPLITS = 1
MERGE_TM = 256
ROUTE_TM = 256
MOE_TM = 256
DMA_GROUP = 8
WEIGHT_CAST_ROWS = 256
NOT_RUN_START = -2
NO_NEXT_EXPERT = -1
FINAL_TM = 256


def _cparams(n_axes):
    return pltpu.CompilerParams(
        dimension_semantics=("arbitrary",) * n_axes, vmem_limit_bytes=VMEM_LIMIT_BYTES)


def _sigmoid(x):
    return 1.0 / (1.0 + jnp.exp(-x))


def _chunk_of(idx):
    return jnp.right_shift(idx, CHUNK.bit_length() - 1)


def _dot(a, b):
    return jnp.dot(a, b, preferred_element_type=F32)


def _dot_nt(a, b):
    return lax.dot_general(a, b, (((1,), (1,)), ((), ())), preferred_element_type=F32)


def _table_kernel(pos_ref, freq_ref, sa_ref, sb_ref, cos_ref, sina_ref, sinb_ref):
    ang = pos_ref[...].astype(F32) * freq_ref[...]
    s = jnp.sin(ang)
    cos_ref[...] = jnp.cos(ang)
    sina_ref[...] = s * sa_ref[...]
    sinb_ref[...] = s * sb_ref[...]


def _rope_tables(pos_col, freq_row, sign_a, sign_b):
    n = pos_col.shape[0]
    tr = min(TABLE_ROWS, n)
    row = pl.BlockSpec((1, LANES), lambda i: (0, 0))
    out = pl.BlockSpec((tr, LANES), lambda i: (i, 0))
    shp = jax.ShapeDtypeStruct((n, LANES), F32)
    return pl.pallas_call(
        _table_kernel, grid=(n // tr,),
        in_specs=[pl.BlockSpec((tr, 1), lambda i: (i, 0)), row, row, row],
        out_specs=[out, out, out], out_shape=[shp, shp, shp],
        compiler_params=_cparams(1), name="rope_tables",
    )(pos_col, freq_row, sign_a, sign_b)


def _matmul_kernel(x_ref, w_ref, o_ref, wb_ref):
    @pl.when(pl.program_id(1) == 0)
    def _():
        wb_ref[...] = w_ref[...].astype(BF16)

    o_ref[...] = _dot(x_ref[...], wb_ref[...]).astype(o_ref.dtype)


def _matmul(x, w, out_dtype, name):
    m, k = x.shape
    n = w.shape[1]
    tm, tn = min(PROJ_TM, m), min(PROJ_TN, n)
    return pl.pallas_call(
        _matmul_kernel, grid=(n // tn, m // tm),
        in_specs=[pl.BlockSpec((tm, k), lambda j, i: (i, 0)),
                  pl.BlockSpec((k, tn), lambda j, i: (0, j))],
        out_specs=pl.BlockSpec((tm, tn), lambda j, i: (i, j)),
        out_shape=jax.ShapeDtypeStruct((m, n), out_dtype),
        scratch_shapes=[pltpu.VMEM((k, tn), BF16)],
        compiler_params=_cparams(2), name=name,
    )(x, w)


def _retention_kernel(lg_ref, bd_ref, q_ref, k_ref, v_ref, g_ref, cos_ref, sin_ref, gain_ref,
                      o_ref, state_ref, decay_ref, qdecay_ref, kdecay_ref):
    group = state_ref.shape[0]
    first_head = pl.program_id(1) * group
    t = q_ref.shape[0]
    dk, dv = RET_QK_DIM, RET_V_DIM

    @pl.when(pl.program_id(2) == 0)
    def _():
        state_ref[...] = jnp.zeros_like(state_ref)
        ri = lax.broadcasted_iota(jnp.int32, (t, t), 0)
        ci = lax.broadcasted_iota(jnp.int32, (t, t), 1)
        dist = jnp.abs(ri - ci).astype(F32)
        visible = _chunk_of(ri) >= _chunk_of(ci)
        row = lax.broadcasted_iota(jnp.int32, (t, dk), 0).astype(F32)
        for j in range(group):
            lg = lg_ref[first_head + j]
            decay_ref[j] = jnp.where(visible, jnp.exp(lg * dist), 0.0)
            qdecay_ref[j] = jnp.exp(lg * (row + 1.0))
            kdecay_ref[j] = jnp.exp(lg * (t - 1.0 - row))

    cos, sin = cos_ref[...], sin_ref[...]

    def rot(x):
        return x * cos + pltpu.roll(x, dk // 2, 1) * sin

    for j in range(group):
        q = rot(q_ref[:, j * dk:(j + 1) * dk].astype(F32))
        k = rot(k_ref[:, j * dk:(j + 1) * dk].astype(F32)) * (dk ** -0.5)
        v = v_ref[:, j * dv:(j + 1) * dv]

        s = _dot_nt(q.astype(BF16), k.astype(BF16)) * decay_ref[j]
        intra = _dot(s.astype(BF16), v)

        state = state_ref[j]
        cross = _dot((q * qdecay_ref[j]).astype(BF16), state.astype(BF16))
        kd = k * kdecay_ref[j]
        state_ref[j] = state * bd_ref[first_head + j] + _dot(kd.T.astype(BF16), v)

        out = intra + cross
        mu = jnp.mean(out, axis=-1, keepdims=True)
        var = jnp.mean(jnp.square(out - mu), axis=-1, keepdims=True)
        y = (out - mu) * lax.rsqrt(var + LN_EPS) * gain_ref[j]
        gate = g_ref[:, j * dv:(j + 1) * dv].astype(F32)
        o_ref[:, j * dv:(j + 1) * dv] = (y * (gate * _sigmoid(gate))).astype(o_ref.dtype)


def _retention(u, cos, sin, log_gamma, block_decay, gain, batch, seq):
    t = min(RET_BLOCK, seq)
    nb = seq // t
    grp = RET_HEAD_GROUP
    n_groups = RET_HEADS // grp
    rows = lambda b, h, i: b * nb + i
    qk_w, v_w = grp * RET_QK_DIM, grp * RET_V_DIM
    k_off, v_off, g_off = n_groups, n_groups, 2 * n_groups
    smem = pl.BlockSpec(memory_space=pltpu.SMEM)
    return pl.pallas_call(
        _retention_kernel, grid=(batch, n_groups, nb),
        in_specs=[
            smem, smem,
            pl.BlockSpec((t, qk_w), lambda b, h, i: (rows(b, h, i), h)),
            pl.BlockSpec((t, qk_w), lambda b, h, i: (rows(b, h, i), k_off + h)),
            pl.BlockSpec((t, v_w), lambda b, h, i: (rows(b, h, i), v_off + h)),
            pl.BlockSpec((t, v_w), lambda b, h, i: (rows(b, h, i), g_off + h)),
            pl.BlockSpec((t, LANES), lambda b, h, i: (rows(b, h, i), 0)),
            pl.BlockSpec((t, LANES), lambda b, h, i: (rows(b, h, i), 0)),
            pl.BlockSpec((grp, 1, RET_V_DIM), lambda b, h, i: (h, 0, 0)),
        ],
        out_specs=pl.BlockSpec((t, v_w), lambda b, h, i: (rows(b, h, i), h)),
        out_shape=jax.ShapeDtypeStruct((batch * seq, RET_HEADS * RET_V_DIM), BF16),
        scratch_shapes=[pltpu.VMEM((grp, RET_QK_DIM, RET_V_DIM), F32), pltpu.VMEM((grp, t, t), F32),
                        pltpu.VMEM((grp, t, RET_QK_DIM), F32), pltpu.VMEM((grp, t, RET_QK_DIM), F32)],
        compiler_params=_cparams(3), name="retention",
    )(log_gamma, block_decay, u, u, u, u, cos, sin, gain)


def _partial_rope(x, cos, sin_lo, sin_hi):
    half = ROPE_DIM // 2
    return (x * cos + pltpu.roll(x, DIFF_QK_DIM - half, 1) * sin_lo
            + pltpu.roll(x, half, 1) * sin_hi)


def _diff_attn_kernel(lam_ref, q_ref, k_ref, v_ref, qc_ref, qa_ref, qb_ref, kc_ref, ka_ref, kb_ref,
                      gain_ref, o_ref, krot_ref, vt_ref, m_ref, acc_ref, sa_ref, sb_ref, *,
                      lambda_init):
    qblk = pl.program_id(2)
    tq = q_ref.shape[0]
    d = DIFF_QK_DIM

    @pl.when(qblk == 0)
    def _():
        def prep_rows(c, carry):
            rows = pl.ds(pl.multiple_of(c * tq, tq), tq)
            for i in range(2):
                x = k_ref[rows, i * d:(i + 1) * d].astype(F32)
                krot_ref[rows, i * d:(i + 1) * d] = _partial_rope(
                    x, kc_ref[rows, :], ka_ref[rows, :], kb_ref[rows, :]).astype(BF16)
            vt_ref[c, :DIFF_V_DIM] = v_ref[rows, :].astype(F32).T.astype(BF16)
            vt_ref[c, DIFF_V_DIM:] = jnp.ones((BF16_ROWS, tq), BF16)
            return carry
        lax.fori_loop(0, k_ref.shape[0] // tq, prep_rows, 0)

    scale = d ** -0.5
    qs = []
    for i in range(2):
        x = q_ref[:, i * d:(i + 1) * d].astype(F32)
        qs.append((_partial_rope(x, qc_ref[...], qa_ref[...], qb_ref[...]) * scale).astype(BF16))

    m_ref[...] = jnp.full_like(m_ref, NEG_BIG)
    acc_ref[...] = jnp.zeros_like(acc_ref)

    hw = tq // ATT_QUERY_SPLITS
    s_bufs = (sa_ref, sb_ref)

    def scores(j, buf, diagonal):
        rows = pl.ds(pl.multiple_of(j * tq, tq), tq)
        for i in range(2):
            kj = krot_ref[rows, i * d:(i + 1) * d]
            for c in range(ATT_QUERY_SPLITS):
                cols = slice(c * hw, (c + 1) * hw)
                s = _dot_nt(kj, qs[i][cols])
                if diagonal:
                    key = lax.broadcasted_iota(jnp.int32, s.shape, 0)
                    qry = lax.broadcasted_iota(jnp.int32, s.shape, 1) + c * hw
                    s = jnp.where(_chunk_of(qry) >= _chunk_of(key), s, NEG_BIG)
                s_bufs[buf][i, :, cols] = s

    def absorb(j, buf):
        vt = vt_ref[j]
        for i in range(2):
            for c in range(ATT_QUERY_SPLITS):
                cols = slice(c * hw, (c + 1) * hw)
                s = s_bufs[buf][i, :, cols]
                m_old = m_ref[i, :, cols]
                m_new = jnp.maximum(m_old, jnp.max(s, axis=0, keepdims=True))
                alpha = jnp.exp(m_old - m_new)
                p = jnp.exp(s - m_new)
                acc_ref[i, :, cols] = alpha * acc_ref[i, :, cols] + _dot(vt, p.astype(BF16))
                m_ref[i, :, cols] = m_new

    scores(qblk, 0, True)

    def pair(p, carry):
        pending = jnp.where(p == 0, qblk, 2 * p - 1)
        scores(2 * p, 1, False)
        absorb(pending, 0)
        scores(2 * p + 1, 0, False)
        absorb(2 * p, 1)
        return carry
    n_pairs = qblk // 2
    lax.fori_loop(0, n_pairs, pair, 0)
    pending = jnp.where(n_pairs == 0, qblk, 2 * n_pairs - 1)

    @pl.when(qblk % 2 == 1)
    def _():
        scores(qblk - 1, 1, False)
        absorb(pending, 0)
        absorb(qblk - 1, 1)

    @pl.when(qblk % 2 == 0)
    def _():
        absorb(pending, 0)

    lp = lam_ref[...]
    lam = (jnp.exp(jnp.sum(lp[0:1] * lp[1:2], axis=-1, keepdims=True))
           - jnp.exp(jnp.sum(lp[2:3] * lp[3:4], axis=-1, keepdims=True)) + lambda_init)
    dv = DIFF_V_DIM
    o = (acc_ref[0, :dv] / acc_ref[0, dv:dv + 1] - lam * (acc_ref[1, :dv] / acc_ref[1, dv:dv + 1])).T
    o = o * lax.rsqrt(jnp.mean(jnp.square(o), axis=-1, keepdims=True) + LN_EPS) * gain_ref[...]
    o_ref[...] = (o * (1.0 - lambda_init)).astype(o_ref.dtype)


def _diff_attention(u, cos, sin_lo, sin_hi, lam_params, gain, batch, seq, lambda_init):
    tq = min(ATT_BLOCK, seq)
    nb = seq // tq
    width = 2 * DIFF_QK_DIM
    ret_cols = 2 * RET_HEADS * RET_QK_DIM + 2 * RET_HEADS * RET_V_DIM
    q_off = ret_cols // width
    k_off = q_off + DIFF_HEADS
    v_off = k_off + DIFF_HEADS
    qrow = lambda b, h, i: b * nb + i
    qtab = pl.BlockSpec((tq, LANES), lambda b, h, i: (qrow(b, h, i), 0))
    ktab = pl.BlockSpec((seq, LANES), lambda b, h, i: (b, 0))
    return pl.pallas_call(
        functools.partial(_diff_attn_kernel, lambda_init=lambda_init),
        grid=(batch, DIFF_HEADS, nb),
        in_specs=[
            pl.BlockSpec((4, DIFF_QK_DIM), lambda b, h, i: (0, 0)),
            pl.BlockSpec((tq, width), lambda b, h, i: (qrow(b, h, i), q_off + h)),
            pl.BlockSpec((seq, width), lambda b, h, i: (b, k_off + h)),
            pl.BlockSpec((seq, DIFF_V_DIM), lambda b, h, i: (b, v_off + h)),
            qtab, qtab, qtab, ktab, ktab, ktab,
            pl.BlockSpec((1, DIFF_V_DIM), lambda b, h, i: (0, 0)),
        ],
        out_specs=pl.BlockSpec((tq, DIFF_V_DIM), lambda b, h, i: (qrow(b, h, i), h)),
        out_shape=jax.ShapeDtypeStruct((batch * seq, DIFF_HEADS * DIFF_V_DIM), BF16),
        scratch_shapes=[pltpu.VMEM((seq, width), BF16),
                        pltpu.VMEM((nb, DIFF_V_DIM + BF16_ROWS, tq), BF16),
                        pltpu.VMEM((2, 1, tq), F32),
                        pltpu.VMEM((2, DIFF_V_DIM + BF16_ROWS, tq), F32),
                        pltpu.VMEM((2, tq, tq), F32), pltpu.VMEM((2, tq, tq), F32)],
        compiler_params=_cparams(3), name="diff_attention",
    )(lam_params, u, u, u, cos, sin_lo, sin_hi, cos, sin_lo, sin_hi, gain)


def _merge_kernel(ret_ref, diff_ref, gr_ref, gd_ref, wr_ref, wd_ref, o_ref):
    a = _dot(ret_ref[...], wr_ref[...])
    b = _dot(diff_ref[...], wd_ref[...])
    merged = _sigmoid(gr_ref[...].astype(F32)) * a + _sigmoid(gd_ref[...].astype(F32)) * b
    o_ref[...] = merged.astype(o_ref.dtype)


def _merge(ret, diff, u, w_ret, w_diff):
    n, dm = ret.shape[0], w_ret.shape[1]
    tm = min(MERGE_TM, n)
    gate_off = (u.shape[1] - 2 * dm) // dm
    tile = lambda c: pl.BlockSpec((tm, dm), lambda i: (i, c))
    weight = lambda w: pl.BlockSpec(w.shape, lambda i: (0, 0))
    return pl.pallas_call(
        _merge_kernel, grid=(n // tm,),
        in_specs=[pl.BlockSpec((tm, ret.shape[1]), lambda i: (i, 0)),
                  pl.BlockSpec((tm, diff.shape[1]), lambda i: (i, 0)),
                  tile(gate_off), tile(gate_off + 1), weight(w_ret), weight(w_diff)],
        out_specs=tile(0), out_shape=jax.ShapeDtypeStruct((n, dm), BF16),
        compiler_params=_cparams(1), name="gated_merge",
    )(ret, diff, u, u, w_ret, w_diff)


def _layer_norm(z, gain, bias):
    mu = jnp.mean(z, axis=-1, keepdims=True)
    var = jnp.mean(jnp.square(z - mu), axis=-1, keepdims=True)
    return (z - mu) * lax.rsqrt(var + LN_EPS) * gain + bias


def _split_bf16(a):
    hi = a.astype(BF16)
    return hi, (a - hi.astype(F32)).astype(BF16)


def _route_kernel(m_ref, x_ref, wo_ref, gain_ref, bias_ref, wr_ref, br_ref, h_ref, r_ref, count_ref,
                  *, alpha):
    @pl.when(pl.program_id(0) == 0)
    def _():
        count_ref[...] = jnp.zeros_like(count_ref)

    h = _layer_norm(alpha * x_ref[...] + _dot(m_ref[...], wo_ref[...]), gain_ref[...], bias_ref[...])
    h_ref[...] = h

    h_hi, h_lo = _split_bf16(h)
    w_hi, w_lo = _split_bf16(wr_ref[...])
    logits = _dot(h_hi, w_hi) + (_dot(h_hi, w_lo) + _dot(h_lo, w_hi)) + br_ref[...]

    lane = lax.broadcasted_iota(jnp.int32, logits.shape, 1).astype(F32)

    def first_argmax(vals):
        top = jnp.max(vals, axis=-1, keepdims=True)
        idx = jnp.min(jnp.where(vals == top, lane, float(LANES)), axis=-1, keepdims=True)
        return top, idx

    group_logits = jnp.where(lane < N_GROUPS, logits, NEG_BIG)
    g_top, g_idx = first_argmax(group_logits)
    p_group = 1.0 / jnp.sum(jnp.exp(group_logits - g_top), axis=-1, keepdims=True)

    first = N_GROUPS + g_idx * EXPERTS_PER_GROUP
    within = jnp.where((lane >= first) & (lane < first + EXPERTS_PER_GROUP), logits, NEG_BIG)
    v1, i1 = first_argmax(within)
    v2, i2 = first_argmax(jnp.where(lane == i1, NEG_BIG, within))
    e2 = jnp.exp(v2 - v1)
    w1 = p_group / (1.0 + e2)
    w2 = p_group * e2 / (1.0 + e2)
    x1, x2 = i1 - N_GROUPS, i2 - N_GROUPS

    hit1 = jnp.where(lane == x1, 1.0, 0.0)
    hit2 = jnp.where(lane == x2, 1.0, 0.0)
    hits = hit1 + hit2
    tm = hits.shape[0]
    earlier = (lax.broadcasted_iota(jnp.int32, (tm, tm), 0) > lax.broadcasted_iota(jnp.int32, (tm, tm), 1))
    before = count_ref[...] + _dot(jnp.where(earlier, 1.0, 0.0).astype(BF16), hits.astype(BF16))
    rank1 = jnp.sum(before * hit1, axis=-1, keepdims=True)
    rank2 = jnp.sum(before * hit2, axis=-1, keepdims=True)
    count_ref[...] += jnp.sum(hits, axis=0, keepdims=True)

    fields = (x1, x2, w1, w2, rank1, rank2)
    row = jnp.zeros_like(logits)
    for pos, value in enumerate(fields):
        row = jnp.where(lane == pos, value, row)
    r_ref[...] = row


def _route(merged, x, w_out, gain, bias, w_router, b_router, alpha):
    n, dm = x.shape
    tm = min(ROUTE_TM, n)
    tile = pl.BlockSpec((tm, dm), lambda i: (i, 0))
    full = lambda a: pl.BlockSpec(a.shape, lambda i: (0, 0))
    return pl.pallas_call(
        functools.partial(_route_kernel, alpha=alpha), grid=(n // tm,),
        in_specs=[tile, tile, full(w_out), full(gain), full(bias), full(w_router), full(b_router)],
        out_specs=[tile, pl.BlockSpec((tm, LANES), lambda i: (i, 0)),
                   pl.BlockSpec((1, LANES), lambda i: (0, 0))],
        out_shape=[jax.ShapeDtypeStruct((n, dm), F32), jax.ShapeDtypeStruct((n, LANES), F32),
                   jax.ShapeDtypeStruct((1, LANES), F32)],
        compiler_params=_cparams(1), name="outproj_ln_router",
    )(merged, x, w_out, gain, bias, w_router, b_router)


def _expert_kernel(tile_expert_ref, run_next_ref, tile_rows_ref, packed_ref, n_valid_ref, h_hbm,
                   wg_hbm, wu_hbm, wd_hbm, y_hbm, xbuf, ybuf, stage_g, stage_u, stage_d,
                   wg_ref, wu_ref, wd_ref, gather_sem, scatter_sem, weight_sem):
    i = pl.program_id(0)
    weights = ((wg_hbm, stage_g, wg_ref), (wu_hbm, stage_u, wu_ref), (wd_hbm, stage_d, wd_ref))

    def start_weights(expert):
        for k, (hbm, stage, _) in enumerate(weights):
            pltpu.make_async_copy(hbm.at[expert], stage, weight_sem.at[k]).start()

    def take_weights():
        for k, (hbm, stage, dst) in enumerate(weights):
            pltpu.make_async_copy(hbm.at[0], stage, weight_sem.at[k]).wait()

            def round_rows(c, carry, stage=stage, dst=dst):
                rows = pl.ds(pl.multiple_of(c * WEIGHT_CAST_ROWS, WEIGHT_CAST_ROWS), WEIGHT_CAST_ROWS)
                dst[rows, :] = stage[rows, :].astype(BF16)
                return carry
            lax.fori_loop(0, stage.shape[0] // WEIGHT_CAST_ROWS, round_rows, 0)
    n_valid = n_valid_ref[0]
    tm = xbuf.shape[1]
    slot = i & 1

    def slot_words(tile, g):
        return [packed_ref[tile * tm + g * DMA_GROUP + k] for k in range(DMA_GROUP)]

    def for_buffer(dynamic_buf, fn):
        for buf in range(2):
            pl.when(dynamic_buf == buf)(functools.partial(fn, buf))

    def start_gather(tile, dynamic_buf):
        def issue(buf):
            for g in range(tm // DMA_GROUP):
                for k, word in enumerate(slot_words(tile, g)):
                    tok = word & ((1 << TOKEN_BITS) - 1)
                    pltpu.make_async_copy(h_hbm.at[pl.ds(tok, 1)],
                                          xbuf.at[buf, pl.ds(g * DMA_GROUP + k, 1)],
                                          gather_sem.at[buf]).start()
        for_buffer(dynamic_buf, issue)

    def wait_gather(buf):
        pltpu.make_async_copy(h_hbm.at[pl.ds(0, tm)], xbuf.at[buf], gather_sem.at[buf]).wait()

    def start_scatter(tile, dynamic_buf):
        n_rows = tile_rows_ref[tile]

        def issue(buf):
            for g in range(tm // DMA_GROUP):
                for k, word in enumerate(slot_words(tile, g)):
                    r = g * DMA_GROUP + k

                    @pl.when(r < n_rows)
                    def _():
                        pltpu.make_async_copy(ybuf.at[buf, pl.ds(r, 1)],
                                              y_hbm.at[pl.ds(word >> TOKEN_BITS, 1)],
                                              scatter_sem.at[buf]).start()
        for_buffer(dynamic_buf, issue)

    def wait_scatter(tile):
        buf = tile & 1
        n_rows = tile_rows_ref[tile]
        n_aligned = pl.multiple_of(n_rows - (n_rows & (SUBLANES - 1)), SUBLANES)

        @pl.when(n_aligned > 0)
        def _():
            rows = pl.ds(0, n_aligned)
            pltpu.make_async_copy(ybuf.at[buf, rows], y_hbm.at[rows], scatter_sem.at[buf]).wait()

        def wait_row(r, carry):
            pltpu.make_async_copy(ybuf.at[buf, pl.ds(r, 1)], y_hbm.at[pl.ds(r, 1)],
                                  scatter_sem.at[buf]).wait()
            return carry
        lax.fori_loop(n_aligned, n_rows, wait_row, 0)

    @pl.when(i == 0)
    def _():
        start_gather(0, 0)
        start_weights(tile_expert_ref[0])

    @pl.when(i + 1 < n_valid)
    def _():
        start_gather(i + 1, 1 - slot)

    @pl.when(i < n_valid)
    def _():
        run_next = run_next_ref[i]

        @pl.when(run_next != NOT_RUN_START)
        def _():
            take_weights()

            @pl.when(run_next != NO_NEXT_EXPERT)
            def _():
                start_weights(run_next)

        wait_gather(slot)

        @pl.when(i >= 2)
        def _():
            wait_scatter(i - 2)

        x = xbuf[slot].astype(BF16)
        g = _dot(x, wg_ref[...])
        act = (g * _sigmoid(g) * _dot(x, wu_ref[...])).astype(BF16)
        ybuf[slot] = _dot(act, wd_ref[...])
        start_scatter(i, slot)

    @pl.when(i == pl.num_programs(0) - 1)
    def _():
        wait_scatter(n_valid - 1)

        @pl.when(n_valid >= 2)
        def _():
            wait_scatter(n_valid - 2)


def _experts(h, tile_expert, run_next, tile_rows, packed, n_valid, w_gate, w_up, w_down, n_rows_out):
    n, dm = h.shape
    n_tiles = tile_expert.shape[0]
    tm = packed.shape[0] // n_tiles
    ff = w_gate.shape[2]
    hbm = pl.BlockSpec(memory_space=pl.ANY)
    grid_spec = pltpu.PrefetchScalarGridSpec(
        num_scalar_prefetch=5, grid=(n_tiles,),
        in_specs=[hbm, hbm, hbm, hbm], out_specs=hbm,
        scratch_shapes=[pltpu.VMEM((2, tm, dm), F32), pltpu.VMEM((2, tm, dm), F32),
                        pltpu.VMEM((dm, ff), F32), pltpu.VMEM((dm, ff), F32), pltpu.VMEM((ff, dm), F32),
                        pltpu.VMEM((dm, ff), BF16), pltpu.VMEM((dm, ff), BF16), pltpu.VMEM((ff, dm), BF16),
                        pltpu.SemaphoreType.DMA((2,)), pltpu.SemaphoreType.DMA((2,)),
                        pltpu.SemaphoreType.DMA((3,))])
    return pl.pallas_call(
        _expert_kernel, grid_spec=grid_spec,
        out_shape=jax.ShapeDtypeStruct((n_rows_out, dm), F32),
        compiler_params=_cparams(1), name="experts",
    )(tile_expert, run_next, tile_rows, packed, n_valid, h, w_gate, w_up, w_down)


def _dispatch_tables(expert_ids, ranks, counts, tm):
    n_assign = expert_ids.shape[0]
    n_tiles = n_assign // tm + N_EXPERTS
    n_slots = n_tiles * tm
    experts = jnp.arange(N_EXPERTS, dtype=jnp.int32)
    padded = (counts + tm - 1) // tm * tm
    pad_end = jnp.cumsum(padded)
    pad_start = pad_end - padded
    slot_of = jnp.sum(jnp.where(expert_ids[:, None] == experts[None, :], pad_start[None, :], 0), axis=1) + ranks
    tile_start = jnp.arange(n_tiles, dtype=jnp.int32) * tm
    tile_expert = jnp.minimum(
        jnp.sum((pad_end[None, :] <= tile_start[:, None]).astype(jnp.int32), axis=1), N_EXPERTS - 1)
    tile_rows = jnp.clip((pad_start + counts)[tile_expert] - tile_start, 0, tm)
    later = lax.cummin(jnp.where(counts > 0, experts, N_EXPERTS)[::-1])[::-1]
    following = jnp.concatenate([later[1:], jnp.full((1,), N_EXPERTS, jnp.int32)])[tile_expert]
    run_start = jnp.concatenate([jnp.ones((1,), bool), tile_expert[1:] != tile_expert[:-1]])
    run_next = jnp.where(run_start, jnp.where(following < N_EXPERTS, following, NO_NEXT_EXPERT),
                         NOT_RUN_START).astype(jnp.int32)
    assign =jnp.full((n_slots,), -1, jnp.int32).at[slot_of].set(jnp.arange(n_assign, dtype=jnp.int32))
    real = jnp.maximum(assign, 0)
    token, choice = real // TOP_K, real % TOP_K
    packed = ((choice * (n_assign // TOP_K) + token) << TOKEN_BITS) | token
    n_valid = (pad_end[-1] // tm).astype(jnp.int32).reshape(1)
    return tile_expert, run_next, tile_rows, packed, n_valid


def _final_kernel(h_ref, y1_ref, y2_ref, r_ref, gain_ref, bias_ref, o_ref, *, alpha):
    r = r_ref[...]
    ffn = y1_ref[...] * r[:, 2:3] + y2_ref[...] * r[:, 3:4]
    o_ref[...] = _layer_norm(alpha * h_ref[...] + ffn, gain_ref[...], bias_ref[...])


def _final(h, y_rows, routing, gain, bias, alpha):
    n, dm = h.shape
    tm = min(FINAL_TM, n)
    tile = pl.BlockSpec((tm, dm), lambda i: (i, 0))
    second = pl.BlockSpec((tm, dm), lambda i: (n // tm + i, 0))
    full = lambda a: pl.BlockSpec(a.shape, lambda i: (0, 0))
    return pl.pallas_call(
        functools.partial(_final_kernel, alpha=alpha), grid=(n // tm,),
        in_specs=[tile, tile, second, pl.BlockSpec((tm, LANES), lambda i: (i, 0)), full(gain), full(bias)],
        out_specs=tile, out_shape=jax.ShapeDtypeStruct((n, dm), F32),
        compiler_params=_cparams(1), name="combine_ln",
    )(h, y_rows, y_rows, routing, gain, bias)


def _lane_row(values):
    row = np.zeros((1, LANES), np.float32)
    row[0, :len(values)] = values
    return jnp.asarray(row)


def _rope_constants():
    ret_half = RET_QK_DIM // 2
    ret_freq = 1.0 / (RET_THETA ** (jnp.arange(0, RET_QK_DIM, 2, dtype=F32) / RET_QK_DIM))
    ret_freq = jnp.concatenate([ret_freq, ret_freq]).reshape(1, LANES)
    ret_sign = _lane_row([-1.0] * ret_half + [1.0] * ret_half)
    half = ROPE_DIM // 2
    diff_freq = 1.0 / (ROPE_THETA ** (jnp.arange(0, ROPE_DIM, 2, dtype=F32) / ROPE_DIM))
    diff_freq = jnp.concatenate([diff_freq, diff_freq, jnp.zeros((LANES - ROPE_DIM,), F32)]).reshape(1, LANES)
    diff_lo = _lane_row([-1.0] * half)
    diff_hi = _lane_row([0.0] * half + [1.0] * half)
    return ret_freq, ret_sign, diff_freq, diff_lo, diff_hi


def kernel(x, positions, w_in, ret_gn_gain, diff_lambda_q1, diff_lambda_k1, diff_lambda_q2, diff_lambda_k2, diff_subln_gain, w_ret_proj, w_diff_proj, w_out, ln1_gain, ln1_bias, w_group, b_group, w_router, b_router, w_expert_gate, w_expert_up, w_expert_down, ln2_gain, ln2_bias):
    batch, seq, dm = x.shape
    n = batch * seq
    depth = w_in.shape[0]
    assert n <= (1 << TOKEN_BITS) and seq % CHUNK == 0
    alpha = (2 * depth) ** 0.25

    ret_freq, ret_sign, diff_freq, diff_lo, diff_hi = _rope_constants()
    pos_col = positions.reshape(n, 1)
    ret_cos, ret_sin, _ = _rope_tables(pos_col, ret_freq, ret_sign, ret_sign)
    diff_cos, diff_sin_lo, diff_sin_hi = _rope_tables(pos_col, diff_freq, diff_lo, diff_hi)

    log_gamma = jnp.log(1.0 - 2.0 ** (-5.0 - jnp.arange(RET_HEADS, dtype=F32)))
    block_decay = jnp.exp(log_gamma * min(RET_BLOCK, seq))

    xt = x.reshape(n, dm)
    for l in range(depth):
        lambda_init = 0.8 - 0.6 * math.exp(-0.3 * l)
        u = _matmul(xt.astype(BF16), w_in[l], BF16, "in_proj")
        ret = _retention(u, ret_cos, ret_sin, log_gamma, block_decay,
                         ret_gn_gain[l].reshape(RET_HEADS, 1, RET_V_DIM), batch, seq)
        lam_params = jnp.stack([diff_lambda_q1[l], diff_lambda_k1[l], diff_lambda_q2[l], diff_lambda_k2[l]])
        diff = _diff_attention(u, diff_cos, diff_sin_lo, diff_sin_hi, lam_params.astype(F32),
                               diff_subln_gain[l].reshape(1, DIFF_V_DIM), batch, seq, lambda_init)
        merged = _merge(ret, diff, u, w_ret_proj[l].astype(BF16), w_diff_proj[l].astype(BF16))

        w_route = jnp.concatenate(
            [w_group[l], w_router[l], jnp.zeros((dm, LANES - N_GROUPS - N_EXPERTS), F32)], axis=1)
        b_route = jnp.concatenate(
            [b_group[l], b_router[l], jnp.zeros((LANES - N_GROUPS - N_EXPERTS,), F32)]).reshape(1, LANES)
        h, routing, counts = _route(merged, xt, w_out[l].astype(BF16), ln1_gain[l].reshape(1, dm),
                                    ln1_bias[l].reshape(1, dm), w_route, b_route, alpha)

        expert_ids = routing[:, :TOP_K].astype(jnp.int32).reshape(-1)
        ranks = routing[:, 2 * TOP_K:3 * TOP_K].astype(jnp.int32).reshape(-1)
        tile_expert, run_next, tile_rows, packed, n_valid = _dispatch_tables(
            expert_ids, ranks, counts[0, :N_EXPERTS].astype(jnp.int32), min(MOE_TM, n))
        y_rows = _experts(h, tile_expert, run_next, tile_rows, packed, n_valid, w_expert_gate[l],
                          w_expert_up[l], w_expert_down[l], TOP_K * n)
        xt = _final(h, y_rows, routing,
                    ln2_gain[l].reshape(1, dm), ln2_bias[l].reshape(1, dm), alpha)
    return xt.reshape(batch, seq, dm)
```

```python
import functools
import math

import jax
import jax.numpy as jnp
from jax import lax
from jax.experimental import pallas as pl
from jax.experimental.pallas import tpu as pltpu

F32 = jnp.float32
BF16 = jnp.bfloat16

CHUNK = 64
RET_HEADS = 8
RET_QK_DIM = 128
RET_V_DIM = 256
RET_THETA = 10000.0
DIFF_HEADS = 8
DIFF_QK_DIM = 128
DIFF_V_DIM = 256
ROPE_THETA = 500000.0
ROPE_DIM = DIFF_QK_DIM // 4
N_GROUPS = 4
EXPERTS_PER_GROUP = 8
N_EXPERTS = N_GROUPS * EXPERTS_PER_GROUP
TOP_K = 2
LN_EPS = 1e-5

LANES = 128
SUBLANES = 8
BF16_ROWS = 16
NEG_BIG = -1e30
TOKEN_BITS = 14

VMEM_LIMIT_BYTES = 56 * 1024 * 1024
TABLE_ROWS = 512
PROJ_TM = 1024
PROJ_TN = 1024
RET_BLOCK = 256
RET_HEAD_GROUP = 8
ATT_BLOCK = 1024
ATT_QUERY_SPLITS = 1
MERGE_TM = 256
ROUTE_TM = 512
ROUTE_SPLITS = 2
MOE_TM = 256
DMA_GROUP = 8
WEIGHT_CAST_ROWS = 256
NOT_RUN_START = -2
NO_NEXT_EXPERT = -1
FINAL_TM = 256


def _cparams(n_axes):
    return pltpu.CompilerParams(
        dimension_semantics=("arbitrary",) * n_axes, vmem_limit_bytes=VMEM_LIMIT_BYTES)


def _sigmoid(x):
    return 1.0 / (1.0 + jnp.exp(-x))


def _chunk_of(idx):
    return jnp.right_shift(idx, CHUNK.bit_length() - 1)


def _dot(a, b):
    return jnp.dot(a, b, preferred_element_type=F32)


def _dot_nt(a, b):
    return lax.dot_general(a, b, (((1,), (1,)), ((), ())), preferred_element_type=F32)


def _table_kernel(pos_ref, freq_ref, rc_ref, rs_ref, dc_ref, dlo_ref, dhi_ref):
    rh, dh = RET_QK_DIM // 2, ROPE_DIM // 2
    ang = pos_ref[...].astype(F32) * freq_ref[...]
    c, s = jnp.cos(ang), jnp.sin(ang)
    lane = lax.broadcasted_iota(jnp.int32, ang.shape, 1)
    rc_ref[...] = jnp.where(lane < rh, c, pltpu.roll(c, rh, 1))
    rs_ref[...] = jnp.where(lane < rh, -s, pltpu.roll(s, rh, 1))
    c_lo, s_lo = pltpu.roll(c, LANES - rh, 1), pltpu.roll(s, LANES - rh, 1)
    c_hi, s_hi = pltpu.roll(c, LANES - rh + dh, 1), pltpu.roll(s, LANES - rh + dh, 1)
    dc_ref[...] = jnp.where(lane < dh, c_lo, jnp.where(lane < 2 * dh, c_hi, 1.0))
    dlo_ref[...] = jnp.where(lane < dh, -s_lo, 0.0)
    dhi_ref[...] = jnp.where((lane >= dh) & (lane < 2 * dh), s_hi, 0.0)


def _rope_tables(pos_col, freq_row):
    n = pos_col.shape[0]
    tr = min(TABLE_ROWS, n)
    out = pl.BlockSpec((tr, LANES), lambda i: (i, 0))
    shp = jax.ShapeDtypeStruct((n, LANES), F32)
    return pl.pallas_call(
        _table_kernel, grid=(n // tr,),
        in_specs=[pl.BlockSpec((tr, 1), lambda i: (i, 0)), pl.BlockSpec((1, LANES), lambda i: (0, 0))],
        out_specs=[out] * 5, out_shape=[shp] * 5,
        compiler_params=_cparams(1), name="rope_tables",
    )(pos_col, freq_row)


def _matmul_kernel(x_ref, w_ref, o_ref, wb_ref):
    @pl.when(pl.program_id(1) == 0)
    def _():
        wb_ref[...] = w_ref[...].astype(BF16)

    o_ref[...] = _dot(x_ref[...].astype(BF16), wb_ref[...]).astype(o_ref.dtype)


def _matmul(x, w, out_dtype, name):
    m, k = x.shape
    n = w.shape[1]
    tm, tn = min(PROJ_TM, m), min(PROJ_TN, n)
    return pl.pallas_call(
        _matmul_kernel, grid=(n // tn, m // tm),
        in_specs=[pl.BlockSpec((tm, k), lambda j, i: (i, 0)),
                  pl.BlockSpec((k, tn), lambda j, i: (0, j))],
        out_specs=pl.BlockSpec((tm, tn), lambda j, i: (i, j)),
        out_shape=jax.ShapeDtypeStruct((m, n), out_dtype),
        scratch_shapes=[pltpu.VMEM((k, tn), BF16)],
        compiler_params=_cparams(2), name=name,
    )(x, w)


def _retention_kernel(lg_ref, bd_ref, q_ref, k_ref, v_ref, g_ref, cos_ref, sin_ref, gain_ref,
                      o_ref, state_ref, decay_ref, qdecay_ref, kdecay_ref):
    group = state_ref.shape[0]
    first_head = pl.program_id(1) * group
    t = q_ref.shape[0]
    dk, dv = RET_QK_DIM, RET_V_DIM

    @pl.when(pl.program_id(2) == 0)
    def _():
        state_ref[...] = jnp.zeros_like(state_ref)
        ri = lax.broadcasted_iota(jnp.int32, (t, t), 0)
        ci = lax.broadcasted_iota(jnp.int32, (t, t), 1)
        dist = jnp.abs(ri - ci).astype(F32)
        visible = _chunk_of(ri) >= _chunk_of(ci)
        row = lax.broadcasted_iota(jnp.int32, (t, dk), 0).astype(F32)
        for j in range(group):
            lg = lg_ref[first_head + j]
            decay_ref[j] = jnp.where(visible, jnp.exp(lg * dist), 0.0)
            qdecay_ref[j] = jnp.exp(lg * (row + 1.0))
            kdecay_ref[j] = jnp.exp(lg * (t - 1.0 - row))

    cos, sin = cos_ref[...], sin_ref[...]

    def rot(x):
        return x * cos + pltpu.roll(x, dk // 2, 1) * sin

    for j in range(group):
        q = rot(q_ref[:, j * dk:(j + 1) * dk].astype(F32))
        k = rot(k_ref[:, j * dk:(j + 1) * dk].astype(F32)) * (dk ** -0.5)
        v = v_ref[:, j * dv:(j + 1) * dv]

        s = _dot_nt(q.astype(BF16), k.astype(BF16)) * decay_ref[j]
        intra = _dot(s.astype(BF16), v)

        state = state_ref[j]
        cross = _dot((q * qdecay_ref[j]).astype(BF16), state.astype(BF16))
        kd = k * kdecay_ref[j]
        state_ref[j] = state * bd_ref[first_head + j] + _dot(kd.T.astype(BF16), v)

        out = intra + cross
        mu = jnp.mean(out, axis=-1, keepdims=True)
        var = jnp.mean(jnp.square(out - mu), axis=-1, keepdims=True)
        y = (out - mu) * lax.rsqrt(var + LN_EPS) * gain_ref[j]
        gate = g_ref[:, j * dv:(j + 1) * dv].astype(F32)
        o_ref[:, j * dv:(j + 1) * dv] = (y * (gate * _sigmoid(gate))).astype(o_ref.dtype)


def _retention(u, cos, sin, log_gamma, block_decay, gain, batch, seq):
    t = min(RET_BLOCK, seq)
    nb = seq // t
    grp = RET_HEAD_GROUP
    n_groups = RET_HEADS // grp
    rows = lambda b, h, i: b * nb + i
    qk_w, v_w = grp * RET_QK_DIM, grp * RET_V_DIM
    k_off, v_off, g_off = n_groups, n_groups, 2 * n_groups
    smem = pl.BlockSpec(memory_space=pltpu.SMEM)
    return pl.pallas_call(
        _retention_kernel, grid=(batch, n_groups, nb),
        in_specs=[
            smem, smem,
            pl.BlockSpec((t, qk_w), lambda b, h, i: (rows(b, h, i), h)),
            pl.BlockSpec((t, qk_w), lambda b, h, i: (rows(b, h, i), k_off + h)),
            pl.BlockSpec((t, v_w), lambda b, h, i: (rows(b, h, i), v_off + h)),
            pl.BlockSpec((t, v_w), lambda b, h, i: (rows(b, h, i), g_off + h)),
            pl.BlockSpec((t, LANES), lambda b, h, i: (rows(b, h, i), 0)),
            pl.BlockSpec((t, LANES), lambda b, h, i: (rows(b, h, i), 0)),
            pl.BlockSpec((grp, 1, RET_V_DIM), lambda b, h, i: (h, 0, 0)),
        ],
        out_specs=pl.BlockSpec((t, v_w), lambda b, h, i: (rows(b, h, i), h)),
        out_shape=jax.ShapeDtypeStruct((batch * seq, RET_HEADS * RET_V_DIM), BF16),
        scratch_shapes=[pltpu.VMEM((grp, RET_QK_DIM, RET_V_DIM), F32), pltpu.VMEM((grp, t, t), F32),
                        pltpu.VMEM((grp, t, RET_QK_DIM), F32), pltpu.VMEM((grp, t, RET_QK_DIM), F32)],
        compiler_params=_cparams(3), name="retention",
    )(log_gamma, block_decay, u, u, u, u, cos, sin, gain)


def _partial_rope(x, cos, sin_lo, sin_hi):
    half = ROPE_DIM // 2
    return (x * cos + pltpu.roll(x, DIFF_QK_DIM - half, 1) * sin_lo
            + pltpu.roll(x, half, 1) * sin_hi)


def _diff_attn_kernel(lam_ref, q_ref, k_ref, v_ref, qc_ref, qa_ref, qb_ref, kc_ref, ka_ref, kb_ref,
                      gain_ref, o_ref, krot_ref, vt_ref, m_ref, acc_ref, sa_ref, sb_ref, *,
                      lambda_init):
    qblk = pl.program_id(2)
    tq = q_ref.shape[0]
    d = DIFF_QK_DIM

    @pl.when(qblk == 0)
    def _():
        def prep_rows(c, carry):
            rows = pl.ds(pl.multiple_of(c * tq, tq), tq)
            for i in range(2):
                x = k_ref[rows, i * d:(i + 1) * d].astype(F32)
                krot_ref[rows, i * d:(i + 1) * d] = _partial_rope(
                    x, kc_ref[rows, :], ka_ref[rows, :], kb_ref[rows, :]).astype(BF16)
            vt_ref[c, :DIFF_V_DIM] = v_ref[rows, :].astype(F32).T.astype(BF16)
            vt_ref[c, DIFF_V_DIM:] = jnp.ones((BF16_ROWS, tq), BF16)
            return carry
        lax.fori_loop(0, k_ref.shape[0] // tq, prep_rows, 0)

    scale = d ** -0.5
    qs = []
    for i in range(2):
        x = q_ref[:, i * d:(i + 1) * d].astype(F32)
        qs.append((_partial_rope(x, qc_ref[...], qa_ref[...], qb_ref[...]) * scale).astype(BF16))

    m_ref[...] = jnp.full_like(m_ref, NEG_BIG)
    acc_ref[...] = jnp.zeros_like(acc_ref)

    hw = tq // ATT_QUERY_SPLITS
    s_bufs = (sa_ref, sb_ref)

    def scores(j, buf, diagonal):
        rows = pl.ds(pl.multiple_of(j * tq, tq), tq)
        for i in range(2):
            kj = krot_ref[rows, i * d:(i + 1) * d]
            for c in range(ATT_QUERY_SPLITS):
                cols = slice(c * hw, (c + 1) * hw)
                s = _dot_nt(kj, qs[i][cols])
                if diagonal:
                    key = lax.broadcasted_iota(jnp.int32, s.shape, 0)
                    qry = lax.broadcasted_iota(jnp.int32, s.shape, 1) + c * hw
                    s = jnp.where(_chunk_of(qry) >= _chunk_of(key), s, NEG_BIG)
                s_bufs[buf][i, :, cols] = s

    def absorb(j, buf):
        vt = vt_ref[j]
        for i in range(2):
            for c in range(ATT_QUERY_SPLITS):
                cols = slice(c * hw, (c + 1) * hw)
                s = s_bufs[buf][i, :, cols]
                m_old = m_ref[i, :, cols]
                m_new = jnp.maximum(m_old, jnp.max(s, axis=0, keepdims=True))
                alpha = jnp.exp(m_old - m_new)
                p = jnp.exp(s - m_new)
                acc_ref[i, :, cols] = alpha * acc_ref[i, :, cols] + _dot(vt, p.astype(BF16))
                m_ref[i, :, cols] = m_new

    scores(qblk, 0, True)

    def pair(p, carry):
        pending = jnp.where(p == 0, qblk, 2 * p - 1)
        scores(2 * p, 1, False)
        absorb(pending, 0)
        scores(2 * p + 1, 0, False)
        absorb(2 * p, 1)
        return carry
    n_pairs = qblk // 2
    lax.fori_loop(0, n_pairs, pair, 0)
    pending = jnp.where(n_pairs == 0, qblk, 2 * n_pairs - 1)

    @pl.when(qblk % 2 == 1)
    def _():
        scores(qblk - 1, 1, False)
        absorb(pending, 0)
        absorb(qblk - 1, 1)

    @pl.when(qblk % 2 == 0)
    def _():
        absorb(pending, 0)

    lp = lam_ref[...]
    lam = (jnp.exp(jnp.sum(lp[0:1] * lp[1:2], axis=-1, keepdims=True))
           - jnp.exp(jnp.sum(lp[2:3] * lp[3:4], axis=-1, keepdims=True)) + lambda_init)
    dv = DIFF_V_DIM
    o = (acc_ref[0, :dv] / acc_ref[0, dv:dv + 1] - lam * (acc_ref[1, :dv] / acc_ref[1, dv:dv + 1])).T
    o = o * lax.rsqrt(jnp.mean(jnp.square(o), axis=-1, keepdims=True) + LN_EPS) * gain_ref[...]
    o_ref[...] = (o * (1.0 - lambda_init)).astype(o_ref.dtype)


def _diff_attention(u, cos, sin_lo, sin_hi, lam_params, gain, batch, seq, lambda_init):
    tq = min(ATT_BLOCK, seq)
    nb = seq // tq
    width = 2 * DIFF_QK_DIM
    ret_cols = 2 * RET_HEADS * RET_QK_DIM + 2 * RET_HEADS * RET_V_DIM
    q_off = ret_cols // width
    k_off = q_off + DIFF_HEADS
    v_off = k_off + DIFF_HEADS
    qrow = lambda b, h, i: b * nb + i
    qtab = pl.BlockSpec((tq, LANES), lambda b, h, i: (qrow(b, h, i), 0))
    ktab = pl.BlockSpec((seq, LANES), lambda b, h, i: (b, 0))
    return pl.pallas_call(
        functools.partial(_diff_attn_kernel, lambda_init=lambda_init),
        grid=(batch, DIFF_HEADS, nb),
        in_specs=[
            pl.BlockSpec((4, DIFF_QK_DIM), lambda b, h, i: (0, 0)),
            pl.BlockSpec((tq, width), lambda b, h, i: (qrow(b, h, i), q_off + h)),
            pl.BlockSpec((seq, width), lambda b, h, i: (b, k_off + h)),
            pl.BlockSpec((seq, DIFF_V_DIM), lambda b, h, i: (b, v_off + h)),
            qtab, qtab, qtab, ktab, ktab, ktab,
            pl.BlockSpec((1, DIFF_V_DIM), lambda b, h, i: (0, 0)),
        ],
        out_specs=pl.BlockSpec((tq, DIFF_V_DIM), lambda b, h, i: (qrow(b, h, i), h)),
        out_shape=jax.ShapeDtypeStruct((batch * seq, DIFF_HEADS * DIFF_V_DIM), BF16),
        scratch_shapes=[pltpu.VMEM((seq, width), BF16),
                        pltpu.VMEM((nb, DIFF_V_DIM + BF16_ROWS, tq), BF16),
                        pltpu.VMEM((2, 1, tq), F32),
                        pltpu.VMEM((2, DIFF_V_DIM + BF16_ROWS, tq), F32),
                        pltpu.VMEM((2, tq, tq), F32), pltpu.VMEM((2, tq, tq), F32)],
        compiler_params=_cparams(3), name="diff_attention",
    )(lam_params, u, u, u, cos, sin_lo, sin_hi, cos, sin_lo, sin_hi, gain)


def _merge_kernel(ret_ref, diff_ref, gr_ref, gd_ref, wr_ref, wd_ref, o_ref):
    a = _dot(ret_ref[...], wr_ref[...])
    b = _dot(diff_ref[...], wd_ref[...])
    merged = _sigmoid(gr_ref[...].astype(F32)) * a + _sigmoid(gd_ref[...].astype(F32)) * b
    o_ref[...] = merged.astype(o_ref.dtype)


def _merge(ret, diff, u, w_ret, w_diff):
    n, dm = ret.shape[0], w_ret.shape[1]
    tm = min(MERGE_TM, n)
    gate_off = (u.shape[1] - 2 * dm) // dm
    tile = lambda c: pl.BlockSpec((tm, dm), lambda i: (i, c))
    weight = lambda w: pl.BlockSpec(w.shape, lambda i: (0, 0))
    return pl.pallas_call(
        _merge_kernel, grid=(n // tm,),
        in_specs=[pl.BlockSpec((tm, ret.shape[1]), lambda i: (i, 0)),
                  pl.BlockSpec((tm, diff.shape[1]), lambda i: (i, 0)),
                  tile(gate_off), tile(gate_off + 1), weight(w_ret), weight(w_diff)],
        out_specs=tile(0), out_shape=jax.ShapeDtypeStruct((n, dm), BF16),
        compiler_params=_cparams(1), name="gated_merge",
    )(ret, diff, u, u, w_ret, w_diff)


def _layer_norm(z, gain, bias):
    mu = jnp.mean(z, axis=-1, keepdims=True)
    var = jnp.mean(jnp.square(z - mu), axis=-1, keepdims=True)
    return (z - mu) * lax.rsqrt(var + LN_EPS) * gain + bias


def _split_bf16(a):
    hi = a.astype(BF16)
    return hi, (a - hi.astype(F32)).astype(BF16)


def _route_kernel(m_ref, x_ref, wo_ref, gain_ref, bias_ref, wr_ref, br_ref, h_ref, r_ref, count_ref,
                  *, alpha):
    @pl.when(pl.program_id(0) == 0)
    def _():
        count_ref[...] = jnp.zeros_like(count_ref)

    w_hi, w_lo = _split_bf16(wr_ref[...])
    part = h_ref.shape[0] // ROUTE_SPLITS
    for g in range(ROUTE_SPLITS):
        rows = slice(g * part, (g + 1) * part)
        h = _layer_norm(alpha * x_ref[rows, :] + _dot(m_ref[rows, :], wo_ref[...]),
                        gain_ref[...], bias_ref[...])
        h_ref[rows, :] = h
        h_hi, h_lo = _split_bf16(h)
        logits = _dot(h_hi, w_hi) + (_dot(h_hi, w_lo) + _dot(h_lo, w_hi)) + br_ref[...]
        r_ref[rows, :] = _routing_rows(logits, count_ref)


def _routing_rows(logits, count_ref):
    lane = lax.broadcasted_iota(jnp.int32, logits.shape, 1).astype(F32)

    def first_argmax(vals):
        top = jnp.max(vals, axis=-1, keepdims=True)
        idx = jnp.min(jnp.where(vals == top, lane, float(LANES)), axis=-1, keepdims=True)
        return top, idx

    group_logits = jnp.where(lane < N_GROUPS, logits, NEG_BIG)
    g_top, g_idx = first_argmax(group_logits)
    p_group = 1.0 / jnp.sum(jnp.exp(group_logits - g_top), axis=-1, keepdims=True)

    first = N_GROUPS + g_idx * EXPERTS_PER_GROUP
    within = jnp.where((lane >= first) & (lane < first + EXPERTS_PER_GROUP), logits, NEG_BIG)
    v1, i1 = first_argmax(within)
    v2, i2 = first_argmax(jnp.where(lane == i1, NEG_BIG, within))
    e2 = jnp.exp(v2 - v1)
    w1 = p_group / (1.0 + e2)
    w2 = p_group * e2 / (1.0 + e2)
    x1, x2 = i1 - N_GROUPS, i2 - N_GROUPS

    hit1 = jnp.where(lane == x1, 1.0, 0.0)
    hit2 = jnp.where(lane == x2, 1.0, 0.0)
    hits = hit1 + hit2
    tm = hits.shape[0]
    earlier = (lax.broadcasted_iota(jnp.int32, (tm, tm), 0) > lax.broadcasted_iota(jnp.int32, (tm, tm), 1))
    before = count_ref[...] + _dot(jnp.where(earlier, 1.0, 0.0).astype(BF16), hits.astype(BF16))
    rank1 = jnp.sum(before * hit1, axis=-1, keepdims=True)
    rank2 = jnp.sum(before * hit2, axis=-1, keepdims=True)
    count_ref[...] += jnp.sum(hits, axis=0, keepdims=True)

    fields = (x1, x2, w1, w2, rank1, rank2)
    row = jnp.zeros_like(logits)
    for pos, value in enumerate(fields):
        row = jnp.where(lane == pos, value, row)
    return row


def _route(merged, x, w_out, gain, bias, w_router, b_router, alpha):
    n, dm = x.shape
    tm = min(ROUTE_TM, n)
    tile = pl.BlockSpec((tm, dm), lambda i: (i, 0))
    full = lambda a: pl.BlockSpec(a.shape, lambda i: (0, 0))
    return pl.pallas_call(
        functools.partial(_route_kernel, alpha=alpha), grid=(n // tm,),
        in_specs=[tile, tile, full(w_out), full(gain), full(bias), full(w_router), full(b_router)],
        out_specs=[tile, pl.BlockSpec((tm, LANES), lambda i: (i, 0)),
                   pl.BlockSpec((1, LANES), lambda i: (0, 0))],
        out_shape=[jax.ShapeDtypeStruct((n, dm), F32), jax.ShapeDtypeStruct((n, LANES), F32),
                   jax.ShapeDtypeStruct((1, LANES), F32)],
        compiler_params=_cparams(1), name="outproj_ln_router",
    )(merged, x, w_out, gain, bias, w_router, b_router)


def _expert_kernel(tile_expert_ref, run_next_ref, tile_rows_ref, packed_ref, n_valid_ref, h_hbm,
                   wg_hbm, wu_hbm, wd_hbm, y_hbm, xbuf, ybuf, stage_g, stage_u, stage_d,
                   wg_ref, wu_ref, wd_ref, gather_sem, scatter_sem, weight_sem):
    i = pl.program_id(0)
    weights = ((wg_hbm, stage_g, wg_ref), (wu_hbm, stage_u, wu_ref), (wd_hbm, stage_d, wd_ref))

    def start_weights(expert):
        for k, (hbm, stage, _) in enumerate(weights):
            pltpu.make_async_copy(hbm.at[expert], stage, weight_sem.at[k]).start()

    def take_weights():
        for k, (hbm, stage, dst) in enumerate(weights):
            pltpu.make_async_copy(hbm.at[0], stage, weight_sem.at[k]).wait()

            def round_rows(c, carry, stage=stage, dst=dst):
                rows = pl.ds(pl.multiple_of(c * WEIGHT_CAST_ROWS, WEIGHT_CAST_ROWS), WEIGHT_CAST_ROWS)
                dst[rows, :] = stage[rows, :].astype(BF16)
                return carry
            lax.fori_loop(0, stage.shape[0] // WEIGHT_CAST_ROWS, round_rows, 0)
    n_valid = n_valid_ref[0]
    tm = xbuf.shape[1]
    slot = i & 1

    def slot_words(tile, g):
        return [packed_ref[tile * tm + g * DMA_GROUP + k] for k in range(DMA_GROUP)]

    def for_buffer(dynamic_buf, fn):
        for buf in range(2):
            pl.when(dynamic_buf == buf)(functools.partial(fn, buf))

    def start_gather(tile, dynamic_buf):
        def issue(buf):
            for g in range(tm // DMA_GROUP):
                for k, word in enumerate(slot_words(tile, g)):
                    tok = word & ((1 << TOKEN_BITS) - 1)
                    pltpu.make_async_copy(h_hbm.at[pl.ds(tok, 1)],
                                          xbuf.at[buf, pl.ds(g * DMA_GROUP + k, 1)],
                                          gather_sem.at[buf]).start()
        for_buffer(dynamic_buf, issue)

    def wait_gather(buf):
        pltpu.make_async_copy(h_hbm.at[pl.ds(0, tm)], xbuf.at[buf], gather_sem.at[buf]).wait()

    def start_scatter(tile, dynamic_buf):
        n_rows = tile_rows_ref[tile]

        def issue(buf):
            for g in range(tm // DMA_GROUP):
                for k, word in enumerate(slot_words(tile, g)):
                    r = g * DMA_GROUP + k

                    @pl.when(r < n_rows)
                    def _():
                        pltpu.make_async_copy(ybuf.at[buf, pl.ds(r, 1)],
                                              y_hbm.at[pl.ds(word >> TOKEN_BITS, 1)],
                                              scatter_sem.at[buf]).start()
        for_buffer(dynamic_buf, issue)

    def wait_scatter(tile):
        buf = tile & 1
        n_rows = tile_rows_ref[tile]
        n_aligned = pl.multiple_of(n_rows - (n_rows & (SUBLANES - 1)), SUBLANES)

        @pl.when(n_aligned > 0)
        def _():
            rows = pl.ds(0, n_aligned)
            pltpu.make_async_copy(ybuf.at[buf, rows], y_hbm.at[rows], scatter_sem.at[buf]).wait()

        def wait_row(r, carry):
            pltpu.make_async_copy(ybuf.at[buf, pl.ds(r, 1)], y_hbm.at[pl.ds(r, 1)],
                                  scatter_sem.at[buf]).wait()
            return carry
        lax.fori_loop(n_aligned, n_rows, wait_row, 0)

    @pl.when(i == 0)
    def _():
        start_gather(0, 0)
        start_weights(tile_expert_ref[0])

    @pl.when(i + 1 < n_valid)
    def _():
        start_gather(i + 1, 1 - slot)

    @pl.when(i < n_valid)
    def _():
        run_next = run_next_ref[i]

        @pl.when(run_next != NOT_RUN_START)
        def _():
            take_weights()

            @pl.when(run_next != NO_NEXT_EXPERT)
            def _():
                start_weights(run_next)

        wait_gather(slot)

        @pl.when(i >= 2)
        def _():
            wait_scatter(i - 2)

        x = xbuf[slot].astype(BF16)
        g = _dot(x, wg_ref[...])
        act = (g * _sigmoid(g) * _dot(x, wu_ref[...])).astype(BF16)
        ybuf[slot] = _dot(act, wd_ref[...])
        start_scatter(i, slot)

    @pl.when(i == pl.num_programs(0) - 1)
    def _():
        wait_scatter(n_valid - 1)

        @pl.when(n_valid >= 2)
        def _():
            wait_scatter(n_valid - 2)


def _experts(h, tile_expert, run_next, tile_rows, packed, n_valid, w_gate, w_up, w_down, n_rows_out):
    n, dm = h.shape
    n_tiles = tile_expert.shape[0]
    tm = packed.shape[0] // n_tiles
    ff = w_gate.shape[2]
    hbm = pl.BlockSpec(memory_space=pl.ANY)
    grid_spec = pltpu.PrefetchScalarGridSpec(
        num_scalar_prefetch=5, grid=(n_tiles,),
        in_specs=[hbm, hbm, hbm, hbm], out_specs=hbm,
        scratch_shapes=[pltpu.VMEM((2, tm, dm), F32), pltpu.VMEM((2, tm, dm), F32),
                        pltpu.VMEM((dm, ff), F32), pltpu.VMEM((dm, ff), F32), pltpu.VMEM((ff, dm), F32),
                        pltpu.VMEM((dm, ff), BF16), pltpu.VMEM((dm, ff), BF16), pltpu.VMEM((ff, dm), BF16),
                        pltpu.SemaphoreType.DMA((2,)), pltpu.SemaphoreType.DMA((2,)),
                        pltpu.SemaphoreType.DMA((3,))])
    return pl.pallas_call(
        _expert_kernel, grid_spec=grid_spec,
        out_shape=jax.ShapeDtypeStruct((n_rows_out, dm), F32),
        compiler_params=_cparams(1), name="experts",
    )(tile_expert, run_next, tile_rows, packed, n_valid, h, w_gate, w_up, w_down)


def _dispatch_tables(expert_ids, ranks, counts, tm):
    n_assign = expert_ids.shape[0]
    n_tiles = n_assign // tm + N_EXPERTS
    n_slots = n_tiles * tm
    experts = jnp.arange(N_EXPERTS, dtype=jnp.int32)
    padded = (counts + tm - 1) // tm * tm
    pad_end = jnp.cumsum(padded)
    pad_start = pad_end - padded
    slot_of = jnp.sum(jnp.where(expert_ids[:, None] == experts[None, :], pad_start[None, :], 0), axis=1) + ranks
    tile_start = jnp.arange(n_tiles, dtype=jnp.int32) * tm
    tile_expert = jnp.minimum(
        jnp.sum((pad_end[None, :] <= tile_start[:, None]).astype(jnp.int32), axis=1), N_EXPERTS - 1)
    tile_rows = jnp.clip((pad_start + counts)[tile_expert] - tile_start, 0, tm)
    later = lax.cummin(jnp.where(counts > 0, experts, N_EXPERTS)[::-1])[::-1]
    following = jnp.concatenate([later[1:], jnp.full((1,), N_EXPERTS, jnp.int32)])[tile_expert]
    run_start = jnp.concatenate([jnp.ones((1,), bool), tile_expert[1:] != tile_expert[:-1]])
    run_next = jnp.where(run_start, jnp.where(following < N_EXPERTS, following, NO_NEXT_EXPERT),
                         NOT_RUN_START).astype(jnp.int32)
    assign =jnp.full((n_slots,), -1, jnp.int32).at[slot_of].set(jnp.arange(n_assign, dtype=jnp.int32))
    real = jnp.maximum(assign, 0)
    token, choice = real // TOP_K, real % TOP_K
    packed = ((choice * (n_assign // TOP_K) + token) << TOKEN_BITS) | token
    n_valid = (pad_end[-1] // tm).astype(jnp.int32).reshape(1)
    return tile_expert, run_next, tile_rows, packed, n_valid


def _final_kernel(h_ref, y1_ref, y2_ref, r_ref, gain_ref, bias_ref, o_ref, *, alpha):
    r = r_ref[...]
    ffn = y1_ref[...] * r[:, 2:3] + y2_ref[...] * r[:, 3:4]
    o_ref[...] = _layer_norm(alpha * h_ref[...] + ffn, gain_ref[...], bias_ref[...])


def _final(h, y_rows, routing, gain, bias, alpha):
    n, dm = h.shape
    tm = min(FINAL_TM, n)
    tile = pl.BlockSpec((tm, dm), lambda i: (i, 0))
    second = pl.BlockSpec((tm, dm), lambda i: (n // tm + i, 0))
    full = lambda a: pl.BlockSpec(a.shape, lambda i: (0, 0))
    return pl.pallas_call(
        functools.partial(_final_kernel, alpha=alpha), grid=(n // tm,),
        in_specs=[tile, tile, second, pl.BlockSpec((tm, LANES), lambda i: (i, 0)), full(gain), full(bias)],
        out_specs=tile, out_shape=jax.ShapeDtypeStruct((n, dm), F32),
        compiler_params=_cparams(1), name="combine_ln",
    )(h, y_rows, y_rows, routing, gain, bias)


def _rope_frequencies():
    ret_freq = 1.0 / (RET_THETA ** (jnp.arange(0, RET_QK_DIM, 2, dtype=F32) / RET_QK_DIM))
    diff_freq = 1.0 / (ROPE_THETA ** (jnp.arange(0, ROPE_DIM, 2, dtype=F32) / ROPE_DIM))
    pad = jnp.zeros((LANES - ret_freq.shape[0] - diff_freq.shape[0],), F32)
    return jnp.concatenate([ret_freq, diff_freq, pad]).reshape(1, LANES)


def kernel(x, positions, w_in, ret_gn_gain, diff_lambda_q1, diff_lambda_k1, diff_lambda_q2, diff_lambda_k2, diff_subln_gain, w_ret_proj, w_diff_proj, w_out, ln1_gain, ln1_bias, w_group, b_group, w_router, b_router, w_expert_gate, w_expert_up, w_expert_down, ln2_gain, ln2_bias):
    batch, seq, dm = x.shape
    n = batch * seq
    depth = w_in.shape[0]
    assert n <= (1 << TOKEN_BITS) and seq % CHUNK == 0
    alpha = (2 * depth) ** 0.25

    ret_cos, ret_sin, diff_cos, diff_sin_lo, diff_sin_hi = _rope_tables(
        positions.reshape(n, 1), _rope_frequencies())

    log_gamma = jnp.log(1.0 - 2.0 ** (-5.0 - jnp.arange(RET_HEADS, dtype=F32)))
    block_decay = jnp.exp(log_gamma * min(RET_BLOCK, seq))

    xt = x.reshape(n, dm)
    for l in range(depth):
        lambda_init = 0.8 - 0.6 * math.exp(-0.3 * l)
        u = _matmul(xt, w_in[l], BF16, "in_proj")
        ret = _retention(u, ret_cos, ret_sin, log_gamma, block_decay,
                         ret_gn_gain[l].reshape(RET_HEADS, 1, RET_V_DIM), batch, seq)
        lam_params = jnp.stack([diff_lambda_q1[l], diff_lambda_k1[l], diff_lambda_q2[l], diff_lambda_k2[l]])
        diff = _diff_attention(u, diff_cos, diff_sin_lo, diff_sin_hi, lam_params.astype(F32),
                               diff_subln_gain[l].reshape(1, DIFF_V_DIM), batch, seq, lambda_init)
        merged = _merge(ret, diff, u, w_ret_proj[l].astype(BF16), w_diff_proj[l].astype(BF16))

        w_route = jnp.concatenate(
            [w_group[l], w_router[l], jnp.zeros((dm, LANES - N_GROUPS - N_EXPERTS), F32)], axis=1)
        b_route = jnp.concatenate(
            [b_group[l], b_router[l], jnp.zeros((LANES - N_GROUPS - N_EXPERTS,), F32)]).reshape(1, LANES)
        h, routing, counts = _route(merged, xt, w_out[l].astype(BF16), ln1_gain[l].reshape(1, dm),
                                    ln1_bias[l].reshape(1, dm), w_route, b_route, alpha)

        expert_ids = routing[:, :TOP_K].astype(jnp.int32).reshape(-1)
        ranks = routing[:, 2 * TOP_K:3 * TOP_K].astype(jnp.int32).reshape(-1)
        tile_expert, run_next, tile_rows, packed, n_valid = _dispatch_tables(
            expert_ids, ranks, counts[0, :N_EXPERTS].astype(jnp.int32), min(MOE_TM, n))
        y_rows = _experts(h, tile_expert, run_next, tile_rows, packed, n_valid, w_expert_gate[l],
                          w_expert_up[l], w_expert_down[l], TOP_K * n)
        xt = _final(h, y_rows, routing,
                    ln2_gain[l].reshape(1, dm), ln2_bias[l].reshape(1, dm), alpha)
    return xt.reshape(batch, seq, dm)
```

```python
import functools
import math

import jax
import jax.numpy as jnp
from jax import lax
from jax.experimental import pallas as pl
from jax.experimental.pallas import tpu as pltpu

F32 = jnp.float32
BF16 = jnp.bfloat16

CHUNK = 64
RET_HEADS = 8
RET_QK_DIM = 128
RET_V_DIM = 256
RET_THETA = 10000.0
DIFF_HEADS = 8
DIFF_QK_DIM = 128
DIFF_V_DIM = 256
ROPE_THETA = 500000.0
ROPE_DIM = DIFF_QK_DIM // 4
N_GROUPS = 4
EXPERTS_PER_GROUP = 8
N_EXPERTS = N_GROUPS * EXPERTS_PER_GROUP
TOP_K = 2
LN_EPS = 1e-5

LANES = 128
SUBLANES = 8
BF16_ROWS = 16
NEG_BIG = -1e30
TOKEN_BITS = 14

VMEM_LIMIT_BYTES = 56 * 1024 * 1024
ATT_VMEM_LIMIT_BYTES = 60 * 1024 * 1024
TABLE_ROWS = 512
PROJ_TM = 1024
PROJ_TN = 1024
RET_BLOCK = 256
RET_HEAD_GROUP = 8
ATT_BLOCK = 1024
MERGE_TM = 256
ROUTE_TM = 512
ROUTE_SPLITS = 2
MOE_TM = 256
DMA_GROUP = 8
WEIGHT_CAST_ROWS = 256
NOT_RUN_START = -2
NO_NEXT_EXPERT = -1
FINAL_TM = 256


def _cparams(n_axes, vmem_limit_bytes=VMEM_LIMIT_BYTES):
    return pltpu.CompilerParams(
        dimension_semantics=("arbitrary",) * n_axes, vmem_limit_bytes=vmem_limit_bytes)


def _sigmoid(x):
    return 1.0 / (1.0 + jnp.exp(-x))


def _chunk_of(idx):
    return jnp.right_shift(idx, CHUNK.bit_length() - 1)


def _dot(a, b):
    return jnp.dot(a, b, preferred_element_type=F32)


def _dot_nt(a, b):
    return lax.dot_general(a, b, (((1,), (1,)), ((), ())), preferred_element_type=F32)


def _table_kernel(pos_ref, freq_ref, rc_ref, rs_ref, dc_ref, dlo_ref, dhi_ref):
    rh, dh = RET_QK_DIM // 2, ROPE_DIM // 2
    ang = pos_ref[...].astype(F32) * freq_ref[...]
    c, s = jnp.cos(ang), jnp.sin(ang)
    lane = lax.broadcasted_iota(jnp.int32, ang.shape, 1)
    rc_ref[...] = jnp.where(lane < rh, c, pltpu.roll(c, rh, 1))
    rs_ref[...] = jnp.where(lane < rh, -s, pltpu.roll(s, rh, 1))
    c_lo, s_lo = pltpu.roll(c, LANES - rh, 1), pltpu.roll(s, LANES - rh, 1)
    c_hi, s_hi = pltpu.roll(c, LANES - rh + dh, 1), pltpu.roll(s, LANES - rh + dh, 1)
    dc_ref[...] = jnp.where(lane < dh, c_lo, jnp.where(lane < 2 * dh, c_hi, 1.0))
    dlo_ref[...] = jnp.where(lane < dh, -s_lo, 0.0)
    dhi_ref[...] = jnp.where((lane >= dh) & (lane < 2 * dh), s_hi, 0.0)


def _rope_tables(pos_col, freq_row):
    n = pos_col.shape[0]
    tr = min(TABLE_ROWS, n)
    out = pl.BlockSpec((tr, LANES), lambda i: (i, 0))
    shp = jax.ShapeDtypeStruct((n, LANES), F32)
    return pl.pallas_call(
        _table_kernel, grid=(n // tr,),
        in_specs=[pl.BlockSpec((tr, 1), lambda i: (i, 0)), pl.BlockSpec((1, LANES), lambda i: (0, 0))],
        out_specs=[out] * 5, out_shape=[shp] * 5,
        compiler_params=_cparams(1), name="rope_tables",
    )(pos_col, freq_row)


def _matmul_kernel(x_ref, w_ref, o_ref, wb_ref):
    @pl.when(pl.program_id(1) == 0)
    def _():
        wb_ref[...] = w_ref[...].astype(BF16)

    o_ref[...] = _dot(x_ref[...].astype(BF16), wb_ref[...]).astype(o_ref.dtype)


def _matmul(x, w, out_dtype, name):
    m, k = x.shape
    n = w.shape[1]
    tm, tn = min(PROJ_TM, m), min(PROJ_TN, n)
    return pl.pallas_call(
        _matmul_kernel, grid=(n // tn, m // tm),
        in_specs=[pl.BlockSpec((tm, k), lambda j, i: (i, 0)),
                  pl.BlockSpec((k, tn), lambda j, i: (0, j))],
        out_specs=pl.BlockSpec((tm, tn), lambda j, i: (i, j)),
        out_shape=jax.ShapeDtypeStruct((m, n), out_dtype),
        scratch_shapes=[pltpu.VMEM((k, tn), BF16)],
        compiler_params=_cparams(2), name=name,
    )(x, w)


def _retention_kernel(lg_ref, bd_ref, q_ref, k_ref, v_ref, g_ref, cos_ref, sin_ref, gain_ref,
                      o_ref, state_ref, decay_ref, qdecay_ref, kdecay_ref):
    group = state_ref.shape[0]
    first_head = pl.program_id(1) * group
    t = q_ref.shape[0]
    dk, dv = RET_QK_DIM, RET_V_DIM

    @pl.when(pl.program_id(2) == 0)
    def _():
        state_ref[...] = jnp.zeros_like(state_ref)
        ri = lax.broadcasted_iota(jnp.int32, (t, t), 0)
        ci = lax.broadcasted_iota(jnp.int32, (t, t), 1)
        dist = jnp.abs(ri - ci).astype(F32)
        visible = _chunk_of(ri) >= _chunk_of(ci)
        row = lax.broadcasted_iota(jnp.int32, (t, dk), 0).astype(F32)
        for j in range(group):
            lg = lg_ref[first_head + j]
            decay_ref[j] = jnp.where(visible, jnp.exp(lg * dist), 0.0)
            qdecay_ref[j] = jnp.exp(lg * (row + 1.0))
            kdecay_ref[j] = jnp.exp(lg * (t - 1.0 - row))

    cos, sin = cos_ref[...], sin_ref[...]

    def rot(x):
        return x * cos + pltpu.roll(x, dk // 2, 1) * sin

    for j in range(group):
        q = rot(q_ref[:, j * dk:(j + 1) * dk].astype(F32))
        k = rot(k_ref[:, j * dk:(j + 1) * dk].astype(F32)) * (dk ** -0.5)
        v = v_ref[:, j * dv:(j + 1) * dv]

        s = _dot_nt(q.astype(BF16), k.astype(BF16)) * decay_ref[j]
        intra = _dot(s.astype(BF16), v)

        state = state_ref[j]
        cross = _dot((q * qdecay_ref[j]).astype(BF16), state.astype(BF16))
        kd = k * kdecay_ref[j]
        state_ref[j] = state * bd_ref[first_head + j] + _dot(kd.T.astype(BF16), v)

        out = intra + cross
        mu = jnp.mean(out, axis=-1, keepdims=True)
        var = jnp.mean(jnp.square(out - mu), axis=-1, keepdims=True)
        y = (out - mu) * lax.rsqrt(var + LN_EPS) * gain_ref[j]
        gate = g_ref[:, j * dv:(j + 1) * dv].astype(F32)
        o_ref[:, j * dv:(j + 1) * dv] = (y * (gate * _sigmoid(gate))).astype(o_ref.dtype)


def _retention(u, cos, sin, log_gamma, block_decay, gain, batch, seq):
    t = min(RET_BLOCK, seq)
    nb = seq // t
    grp = RET_HEAD_GROUP
    n_groups = RET_HEADS // grp
    rows = lambda b, h, i: b * nb + i
    qk_w, v_w = grp * RET_QK_DIM, grp * RET_V_DIM
    k_off, v_off, g_off = n_groups, n_groups, 2 * n_groups
    smem = pl.BlockSpec(memory_space=pltpu.SMEM)
    return pl.pallas_call(
        _retention_kernel, grid=(batch, n_groups, nb),
        in_specs=[
            smem, smem,
            pl.BlockSpec((t, qk_w), lambda b, h, i: (rows(b, h, i), h)),
            pl.BlockSpec((t, qk_w), lambda b, h, i: (rows(b, h, i), k_off + h)),
            pl.BlockSpec((t, v_w), lambda b, h, i: (rows(b, h, i), v_off + h)),
            pl.BlockSpec((t, v_w), lambda b, h, i: (rows(b, h, i), g_off + h)),
            pl.BlockSpec((t, LANES), lambda b, h, i: (rows(b, h, i), 0)),
            pl.BlockSpec((t, LANES), lambda b, h, i: (rows(b, h, i), 0)),
            pl.BlockSpec((grp, 1, RET_V_DIM), lambda b, h, i: (h, 0, 0)),
        ],
        out_specs=pl.BlockSpec((t, v_w), lambda b, h, i: (rows(b, h, i), h)),
        out_shape=jax.ShapeDtypeStruct((batch * seq, RET_HEADS * RET_V_DIM), BF16),
        scratch_shapes=[pltpu.VMEM((grp, RET_QK_DIM, RET_V_DIM), F32), pltpu.VMEM((grp, t, t), F32),
                        pltpu.VMEM((grp, t, RET_QK_DIM), F32), pltpu.VMEM((grp, t, RET_QK_DIM), F32)],
        compiler_params=_cparams(3), name="retention",
    )(log_gamma, block_decay, u, u, u, u, cos, sin, gain)


def _partial_rope(x, cos, sin_lo, sin_hi):
    half = ROPE_DIM // 2
    return (x * cos + pltpu.roll(x, DIFF_QK_DIM - half, 1) * sin_lo
            + pltpu.roll(x, half, 1) * sin_hi)


def _diff_attn_kernel(lam_ref, q_ref, k_ref, v_ref, qc_ref, qa_ref, qb_ref, kc_ref, ka_ref, kb_ref,
                      gain_ref, o_ref, krot_ref, vt_ref, m_ref, acc_ref, sa_ref, sb_ref, mask_ref, *,
                      lambda_init):
    qblk = pl.program_id(2)
    tq = q_ref.shape[0]
    d = DIFF_QK_DIM

    @pl.when((pl.program_id(0) == 0) & (pl.program_id(1) == 0) & (qblk == 0))
    def _():
        key = lax.broadcasted_iota(jnp.int32, (tq, tq), 0)
        qry = lax.broadcasted_iota(jnp.int32, (tq, tq), 1)
        mask_ref[...] = jnp.where(_chunk_of(qry) >= _chunk_of(key), 0.0, NEG_BIG)

    @pl.when(qblk == 0)
    def _():
        def prep_rows(c, carry):
            rows = pl.ds(pl.multiple_of(c * tq, tq), tq)
            for i in range(2):
                x = k_ref[rows, i * d:(i + 1) * d].astype(F32)
                krot_ref[rows, i * d:(i + 1) * d] = _partial_rope(
                    x, kc_ref[rows, :], ka_ref[rows, :], kb_ref[rows, :]).astype(BF16)
            vt_ref[c, :DIFF_V_DIM] = v_ref[rows, :].astype(F32).T.astype(BF16)
            vt_ref[c, DIFF_V_DIM:] = jnp.ones((BF16_ROWS, tq), BF16)
            return carry
        lax.fori_loop(0, k_ref.shape[0] // tq, prep_rows, 0)

    scale = d ** -0.5
    qs = []
    for i in range(2):
        x = q_ref[:, i * d:(i + 1) * d].astype(F32)
        qs.append((_partial_rope(x, qc_ref[...], qa_ref[...], qb_ref[...]) * scale).astype(BF16))

    m_ref[...] = jnp.full_like(m_ref, NEG_BIG)
    acc_ref[...] = jnp.zeros_like(acc_ref)

    s_bufs = (sa_ref, sb_ref)

    def scores(j, buf, diagonal):
        rows = pl.ds(pl.multiple_of(j * tq, tq), tq)
        for i in range(2):
            s = _dot_nt(krot_ref[rows, i * d:(i + 1) * d], qs[i])
            s_bufs[buf][i] = s + mask_ref[...] if diagonal else s

    def absorb(j, buf):
        vt = vt_ref[j]
        for i in range(2):
            s = s_bufs[buf][i]
            m_old = m_ref[i]
            m_new = jnp.maximum(m_old, jnp.max(s, axis=0, keepdims=True))
            alpha = jnp.exp(m_old - m_new)
            p = jnp.exp(s - m_new)
            acc_ref[i] = alpha * acc_ref[i] + _dot(vt, p.astype(BF16))
            m_ref[i] = m_new

    scores(qblk, 0, True)

    def pair(p, carry):
        pending = jnp.where(p == 0, qblk, 2 * p - 1)
        scores(2 * p, 1, False)
        absorb(pending, 0)
        scores(2 * p + 1, 0, False)
        absorb(2 * p, 1)
        return carry
    n_pairs = qblk // 2
    lax.fori_loop(0, n_pairs, pair, 0)
    pending = jnp.where(n_pairs == 0, qblk, 2 * n_pairs - 1)

    @pl.when(qblk % 2 == 1)
    def _():
        scores(qblk - 1, 1, False)
        absorb(pending, 0)
        absorb(qblk - 1, 1)

    @pl.when(qblk % 2 == 0)
    def _():
        absorb(pending, 0)

    lp = lam_ref[...]
    lam = (jnp.exp(jnp.sum(lp[0:1] * lp[1:2], axis=-1, keepdims=True))
           - jnp.exp(jnp.sum(lp[2:3] * lp[3:4], axis=-1, keepdims=True)) + lambda_init)
    dv = DIFF_V_DIM
    o = (acc_ref[0, :dv] / acc_ref[0, dv:dv + 1] - lam * (acc_ref[1, :dv] / acc_ref[1, dv:dv + 1])).T
    o = o * lax.rsqrt(jnp.mean(jnp.square(o), axis=-1, keepdims=True) + LN_EPS) * gain_ref[...]
    o_ref[...] = (o * (1.0 - lambda_init)).astype(o_ref.dtype)


def _diff_attention(u, cos, sin_lo, sin_hi, lam_params, gain, batch, seq, lambda_init):
    tq = min(ATT_BLOCK, seq)
    nb = seq // tq
    width = 2 * DIFF_QK_DIM
    ret_cols = 2 * RET_HEADS * RET_QK_DIM + 2 * RET_HEADS * RET_V_DIM
    q_off = ret_cols // width
    k_off = q_off + DIFF_HEADS
    v_off = k_off + DIFF_HEADS
    qrow = lambda b, h, i: b * nb + i
    qtab = pl.BlockSpec((tq, LANES), lambda b, h, i: (qrow(b, h, i), 0))
    ktab = pl.BlockSpec((seq, LANES), lambda b, h, i: (b, 0), pipeline_mode=pl.Buffered(1))
    return pl.pallas_call(
        functools.partial(_diff_attn_kernel, lambda_init=lambda_init),
        grid=(batch, DIFF_HEADS, nb),
        in_specs=[
            pl.BlockSpec((4, DIFF_QK_DIM), lambda b, h, i: (0, 0)),
            pl.BlockSpec((tq, width), lambda b, h, i: (qrow(b, h, i), q_off + h)),
            pl.BlockSpec((seq, width), lambda b, h, i: (b, k_off + h)),
            pl.BlockSpec((seq, DIFF_V_DIM), lambda b, h, i: (b, v_off + h)),
            qtab, qtab, qtab, ktab, ktab, ktab,
            pl.BlockSpec((1, DIFF_V_DIM), lambda b, h, i: (0, 0)),
        ],
        out_specs=pl.BlockSpec((tq, DIFF_V_DIM), lambda b, h, i: (qrow(b, h, i), h)),
        out_shape=jax.ShapeDtypeStruct((batch * seq, DIFF_HEADS * DIFF_V_DIM), BF16),
        scratch_shapes=[pltpu.VMEM((seq, width), BF16),
                        pltpu.VMEM((nb, DIFF_V_DIM + BF16_ROWS, tq), BF16),
                        pltpu.VMEM((2, 1, tq), F32),
                        pltpu.VMEM((2, DIFF_V_DIM + BF16_ROWS, tq), F32),
                        pltpu.VMEM((2, tq, tq), F32), pltpu.VMEM((2, tq, tq), F32),
                        pltpu.VMEM((tq, tq), F32)],
        compiler_params=_cparams(3, ATT_VMEM_LIMIT_BYTES), name="diff_attention",
    )(lam_params, u, u, u, cos, sin_lo, sin_hi, cos, sin_lo, sin_hi, gain)


def _merge_kernel(ret_ref, diff_ref, gr_ref, gd_ref, wr_ref, wd_ref, o_ref):
    a = _dot(ret_ref[...], wr_ref[...])
    b = _dot(diff_ref[...], wd_ref[...])
    merged = _sigmoid(gr_ref[...].astype(F32)) * a + _sigmoid(gd_ref[...].astype(F32)) * b
    o_ref[...] = merged.astype(o_ref.dtype)


def _merge(ret, diff, u, w_ret, w_diff):
    n, dm = ret.shape[0], w_ret.shape[1]
    tm = min(MERGE_TM, n)
    gate_off = (u.shape[1] - 2 * dm) // dm
    tile = lambda c: pl.BlockSpec((tm, dm), lambda i: (i, c))
    weight = lambda w: pl.BlockSpec(w.shape, lambda i: (0, 0))
    return pl.pallas_call(
        _merge_kernel, grid=(n // tm,),
        in_specs=[pl.BlockSpec((tm, ret.shape[1]), lambda i: (i, 0)),
                  pl.BlockSpec((tm, diff.shape[1]), lambda i: (i, 0)),
                  tile(gate_off), tile(gate_off + 1), weight(w_ret), weight(w_diff)],
        out_specs=tile(0), out_shape=jax.ShapeDtypeStruct((n, dm), BF16),
        compiler_params=_cparams(1), name="gated_merge",
    )(ret, diff, u, u, w_ret, w_diff)


def _layer_norm(z, gain, bias):
    mu = jnp.mean(z, axis=-1, keepdims=True)
    var = jnp.mean(jnp.square(z - mu), axis=-1, keepdims=True)
    return (z - mu) * lax.rsqrt(var + LN_EPS) * gain + bias


def _split_bf16(a):
    hi = a.astype(BF16)
    return hi, (a - hi.astype(F32)).astype(BF16)


def _route_kernel(m_ref, x_ref, wo_ref, gain_ref, bias_ref, wr_ref, br_ref, h_ref, r_ref, count_ref,
                  *, alpha):
    @pl.when(pl.program_id(0) == 0)
    def _():
        count_ref[...] = jnp.zeros_like(count_ref)

    w_hi, w_lo = _split_bf16(wr_ref[...])
    part = h_ref.shape[0] // ROUTE_SPLITS
    for g in range(ROUTE_SPLITS):
        rows = slice(g * part, (g + 1) * part)
        h = _layer_norm(alpha * x_ref[rows, :] + _dot(m_ref[rows, :], wo_ref[...]),
                        gain_ref[...], bias_ref[...])
        h_ref[rows, :] = h
        h_hi, h_lo = _split_bf16(h)
        logits = _dot(h_hi, w_hi) + (_dot(h_hi, w_lo) + _dot(h_lo, w_hi)) + br_ref[...]
        r_ref[rows, :] = _routing_rows(logits, count_ref)


def _routing_rows(logits, count_ref):
    lane = lax.broadcasted_iota(jnp.int32, logits.shape, 1).astype(F32)

    def first_argmax(vals):
        top = jnp.max(vals, axis=-1, keepdims=True)
        idx = jnp.min(jnp.where(vals == top, lane, float(LANES)), axis=-1, keepdims=True)
        return top, idx

    group_logits = jnp.where(lane < N_GROUPS, logits, NEG_BIG)
    g_top, g_idx = first_argmax(group_logits)
    p_group = 1.0 / jnp.sum(jnp.exp(group_logits - g_top), axis=-1, keepdims=True)

    first = N_GROUPS + g_idx * EXPERTS_PER_GROUP
    within = jnp.where((lane >= first) & (lane < first + EXPERTS_PER_GROUP), logits, NEG_BIG)
    v1, i1 = first_argmax(within)
    v2, i2 = first_argmax(jnp.where(lane == i1, NEG_BIG, within))
    e2 = jnp.exp(v2 - v1)
    w1 = p_group / (1.0 + e2)
    w2 = p_group * e2 / (1.0 + e2)
    x1, x2 = i1 - N_GROUPS, i2 - N_GROUPS

    hit1 = jnp.where(lane == x1, 1.0, 0.0)
    hit2 = jnp.where(lane == x2, 1.0, 0.0)
    hits = hit1 + hit2
    tm = hits.shape[0]
    earlier = (lax.broadcasted_iota(jnp.int32, (tm, tm), 0) > lax.broadcasted_iota(jnp.int32, (tm, tm), 1))
    before = count_ref[...] + _dot(jnp.where(earlier, 1.0, 0.0).astype(BF16), hits.astype(BF16))
    rank1 = jnp.sum(before * hit1, axis=-1, keepdims=True)
    rank2 = jnp.sum(before * hit2, axis=-1, keepdims=True)
    count_ref[...] += jnp.sum(hits, axis=0, keepdims=True)

    fields = (x1, x2, w1, w2, rank1, rank2)
    row = jnp.zeros_like(logits)
    for pos, value in enumerate(fields):
        row = jnp.where(lane == pos, value, row)
    return row


def _route(merged, x, w_out, gain, bias, w_router, b_router, alpha):
    n, dm = x.shape
    tm = min(ROUTE_TM, n)
    tile = pl.BlockSpec((tm, dm), lambda i: (i, 0))
    full = lambda a: pl.BlockSpec(a.shape, lambda i: (0, 0))
    return pl.pallas_call(
        functools.partial(_route_kernel, alpha=alpha), grid=(n // tm,),
        in_specs=[tile, tile, full(w_out), full(gain), full(bias), full(w_router), full(b_router)],
        out_specs=[tile, pl.BlockSpec((tm, LANES), lambda i: (i, 0)),
                   pl.BlockSpec((1, LANES), lambda i: (0, 0))],
        out_shape=[jax.ShapeDtypeStruct((n, dm), F32), jax.ShapeDtypeStruct((n, LANES), F32),
                   jax.ShapeDtypeStruct((1, LANES), F32)],
        compiler_params=_cparams(1), name="outproj_ln_router",
    )(merged, x, w_out, gain, bias, w_router, b_router)


def _expert_kernel(tile_expert_ref, run_next_ref, tile_rows_ref, src_ref, dst_ref, n_valid_ref, h_hbm,
                   wg_hbm, wu_hbm, wd_hbm, y_hbm, xbuf, ybuf, stage_g, stage_u, stage_d,
                   wg_ref, wu_ref, wd_ref, gather_sem, scatter_sem, weight_sem):
    i = pl.program_id(0)
    weights = ((wg_hbm, stage_g, wg_ref), (wu_hbm, stage_u, wu_ref), (wd_hbm, stage_d, wd_ref))

    def start_weights(expert):
        for k, (hbm, stage, _) in enumerate(weights):
            pltpu.make_async_copy(hbm.at[expert], stage, weight_sem.at[k]).start()

    def take_weights():
        for k, (hbm, stage, dst) in enumerate(weights):
            pltpu.make_async_copy(hbm.at[0], stage, weight_sem.at[k]).wait()

            def round_rows(c, carry, stage=stage, dst=dst):
                rows = pl.ds(pl.multiple_of(c * WEIGHT_CAST_ROWS, WEIGHT_CAST_ROWS), WEIGHT_CAST_ROWS)
                dst[rows, :] = stage[rows, :].astype(BF16)
                return carry
            lax.fori_loop(0, stage.shape[0] // WEIGHT_CAST_ROWS, round_rows, 0)

    n_valid = n_valid_ref[0]
    tm = xbuf.shape[1]
    slot = i & 1

    def slot_words(table_ref, tile, g):
        return [table_ref[tile * tm + g * DMA_GROUP + k] for k in range(DMA_GROUP)]

    def for_buffer(dynamic_buf, fn):
        for buf in range(2):
            pl.when(dynamic_buf == buf)(functools.partial(fn, buf))

    def start_gather(tile, dynamic_buf):
        def issue(buf):
            for g in range(tm // DMA_GROUP):
                for k, tok in enumerate(slot_words(src_ref, tile, g)):
                    pltpu.make_async_copy(h_hbm.at[pl.ds(tok, 1)],
                                          xbuf.at[buf, pl.ds(g * DMA_GROUP + k, 1)],
                                          gather_sem.at[buf]).start()
        for_buffer(dynamic_buf, issue)

    def wait_gather(buf):
        pltpu.make_async_copy(h_hbm.at[pl.ds(0, tm)], xbuf.at[buf], gather_sem.at[buf]).wait()

    def start_scatter(tile, dynamic_buf):
        n_rows = tile_rows_ref[tile]

        def issue(buf):
            for g in range(tm // DMA_GROUP):
                for k, row in enumerate(slot_words(dst_ref, tile, g)):
                    r = g * DMA_GROUP + k

                    @pl.when(r < n_rows)
                    def _():
                        pltpu.make_async_copy(ybuf.at[buf, pl.ds(r, 1)], y_hbm.at[pl.ds(row, 1)],
                                              scatter_sem.at[buf]).start()
        for_buffer(dynamic_buf, issue)

    def wait_scatter(tile):
        buf = tile & 1
        n_rows = tile_rows_ref[tile]
        n_aligned = pl.multiple_of(n_rows - (n_rows & (SUBLANES - 1)), SUBLANES)

        @pl.when(n_aligned > 0)
        def _():
            rows = pl.ds(0, n_aligned)
            pltpu.make_async_copy(ybuf.at[buf, rows], y_hbm.at[rows], scatter_sem.at[buf]).wait()

        def wait_row(r, carry):
            pltpu.make_async_copy(ybuf.at[buf, pl.ds(r, 1)], y_hbm.at[pl.ds(r, 1)],
                                  scatter_sem.at[buf]).wait()
            return carry
        lax.fori_loop(n_aligned, n_rows, wait_row, 0)

    @pl.when(i == 0)
    def _():
        start_gather(0, 0)
        start_weights(tile_expert_ref[0])

    @pl.when(i + 1 < n_valid)
    def _():
        start_gather(i + 1, 1 - slot)

    @pl.when(i < n_valid)
    def _():
        run_next = run_next_ref[i]

        @pl.when(run_next != NOT_RUN_START)
        def _():
            take_weights()

            @pl.when(run_next != NO_NEXT_EXPERT)
            def _():
                start_weights(run_next)

        wait_gather(slot)

        @pl.when(i >= 2)
        def _():
            wait_scatter(i - 2)

        x = xbuf[slot].astype(BF16)
        g = _dot(x, wg_ref[...])
        act = (g * _sigmoid(g) * _dot(x, wu_ref[...])).astype(BF16)
        ybuf[slot] = _dot(act, wd_ref[...])
        start_scatter(i, slot)

    @pl.when(i == pl.num_programs(0) - 1)
    def _():
        wait_scatter(n_valid - 1)

        @pl.when(n_valid >= 2)
        def _():
            wait_scatter(n_valid - 2)


def _experts(h, tile_expert, run_next, tile_rows, src, dst, n_valid, w_gate, w_up, w_down, n_rows_out):
    n, dm = h.shape
    n_tiles = tile_expert.shape[0]
    tm = src.shape[0] // n_tiles
    ff = w_gate.shape[2]
    hbm = pl.BlockSpec(memory_space=pl.ANY)
    grid_spec = pltpu.PrefetchScalarGridSpec(
        num_scalar_prefetch=6, grid=(n_tiles,),
        in_specs=[hbm, hbm, hbm, hbm], out_specs=hbm,
        scratch_shapes=[pltpu.VMEM((2, tm, dm), F32), pltpu.VMEM((2, tm, dm), F32),
                        pltpu.VMEM((dm, ff), F32), pltpu.VMEM((dm, ff), F32), pltpu.VMEM((ff, dm), F32),
                        pltpu.VMEM((dm, ff), BF16), pltpu.VMEM((dm, ff), BF16), pltpu.VMEM((ff, dm), BF16),
                        pltpu.SemaphoreType.DMA((2,)), pltpu.SemaphoreType.DMA((2,)),
                        pltpu.SemaphoreType.DMA((3,))])
    return pl.pallas_call(
        _expert_kernel, grid_spec=grid_spec,
        out_shape=jax.ShapeDtypeStruct((n_rows_out, dm), F32),
        compiler_params=_cparams(1), name="experts",
    )(tile_expert, run_next, tile_rows, src, dst, n_valid, h, w_gate, w_up, w_down)


def _dispatch_tables(expert_ids, ranks, counts, tm):
    n_assign = expert_ids.shape[0]
    n_tiles = n_assign // tm + N_EXPERTS
    n_slots = n_tiles * tm
    experts = jnp.arange(N_EXPERTS, dtype=jnp.int32)
    padded = (counts + tm - 1) // tm * tm
    pad_end = jnp.cumsum(padded)
    pad_start = pad_end - padded
    slot_of = jnp.sum(jnp.where(expert_ids[:, None] == experts[None, :], pad_start[None, :], 0), axis=1) + ranks
    tile_start = jnp.arange(n_tiles, dtype=jnp.int32) * tm
    tile_expert = jnp.minimum(
        jnp.sum((pad_end[None, :] <= tile_start[:, None]).astype(jnp.int32), axis=1), N_EXPERTS - 1)
    tile_rows = jnp.clip((pad_start + counts)[tile_expert] - tile_start, 0, tm)
    later = lax.cummin(jnp.where(counts > 0, experts, N_EXPERTS)[::-1])[::-1]
    following = jnp.concatenate([later[1:], jnp.full((1,), N_EXPERTS, jnp.int32)])[tile_expert]
    run_start = jnp.concatenate([jnp.ones((1,), bool), tile_expert[1:] != tile_expert[:-1]])
    run_next = jnp.where(run_start, jnp.where(following < N_EXPERTS, following, NO_NEXT_EXPERT),
                         NOT_RUN_START).astype(jnp.int32)
    assign = jnp.full((n_slots,), -1, jnp.int32).at[slot_of].set(jnp.arange(n_assign, dtype=jnp.int32))
    real = jnp.maximum(assign, 0)
    token, choice = real // TOP_K, real % TOP_K
    dst = choice * (n_assign // TOP_K) + token
    n_valid = (pad_end[-1] // tm).astype(jnp.int32).reshape(1)
    return tile_expert, run_next, tile_rows, token, dst, n_valid


def _final_kernel(h_ref, y1_ref, y2_ref, r_ref, gain_ref, bias_ref, o_ref, *, alpha):
    r = r_ref[...]
    ffn = y1_ref[...] * r[:, 2:3] + y2_ref[...] * r[:, 3:4]
    o_ref[...] = _layer_norm(alpha * h_ref[...] + ffn, gain_ref[...], bias_ref[...])


def _final(h, y_rows, routing, gain, bias, alpha):
    n, dm = h.shape
    tm = min(FINAL_TM, n)
    tile = pl.BlockSpec((tm, dm), lambda i: (i, 0))
    second = pl.BlockSpec((tm, dm), lambda i: (n // tm + i, 0))
    full = lambda a: pl.BlockSpec(a.shape, lambda i: (0, 0))
    return pl.pallas_call(
        functools.partial(_final_kernel, alpha=alpha), grid=(n // tm,),
        in_specs=[tile, tile, second, pl.BlockSpec((tm, LANES), lambda i: (i, 0)), full(gain), full(bias)],
        out_specs=tile, out_shape=jax.ShapeDtypeStruct((n, dm), F32),
        compiler_params=_cparams(1), name="combine_ln",
    )(h, y_rows, y_rows, routing, gain, bias)


def _rope_frequencies():
    ret_freq = 1.0 / (RET_THETA ** (jnp.arange(0, RET_QK_DIM, 2, dtype=F32) / RET_QK_DIM))
    diff_freq = 1.0 / (ROPE_THETA ** (jnp.arange(0, ROPE_DIM, 2, dtype=F32) / ROPE_DIM))
    pad = jnp.zeros((LANES - ret_freq.shape[0] - diff_freq.shape[0],), F32)
    return jnp.concatenate([ret_freq, diff_freq, pad]).reshape(1, LANES)


def kernel(x, positions, w_in, ret_gn_gain, diff_lambda_q1, diff_lambda_k1, diff_lambda_q2, diff_lambda_k2, diff_subln_gain, w_ret_proj, w_diff_proj, w_out, ln1_gain, ln1_bias, w_group, b_group, w_router, b_router, w_expert_gate, w_expert_up, w_expert_down, ln2_gain, ln2_bias):
    batch, seq, dm = x.shape
    n = batch * seq
    depth = w_in.shape[0]
    assert seq % CHUNK == 0
    alpha = (2 * depth) ** 0.25

    ret_cos, ret_sin, diff_cos, diff_sin_lo, diff_sin_hi = _rope_tables(
        positions.reshape(n, 1), _rope_frequencies())

    log_gamma = jnp.log(1.0 - 2.0 ** (-5.0 - jnp.arange(RET_HEADS, dtype=F32)))
    block_decay = jnp.exp(log_gamma * min(RET_BLOCK, seq))

    xt = x.reshape(n, dm)
    for l in range(depth):
        lambda_init = 0.8 - 0.6 * math.exp(-0.3 * l)
        u = _matmul(xt, w_in[l], BF16, "in_proj")
        ret = _retention(u, ret_cos, ret_sin, log_gamma, block_decay,
                         ret_gn_gain[l].reshape(RET_HEADS, 1, RET_V_DIM), batch, seq)
        lam_params = jnp.stack([diff_lambda_q1[l], diff_lambda_k1[l], diff_lambda_q2[l], diff_lambda_k2[l]])
        diff = _diff_attention(u, diff_cos, diff_sin_lo, diff_sin_hi, lam_params.astype(F32),
                               diff_subln_gain[l].reshape(1, DIFF_V_DIM), batch, seq, lambda_init)
        merged = _merge(ret, diff, u, w_ret_proj[l].astype(BF16), w_diff_proj[l].astype(BF16))

        w_route = jnp.concatenate(
            [w_group[l], w_router[l], jnp.zeros((dm, LANES - N_GROUPS - N_EXPERTS), F32)], axis=1)
        b_route = jnp.concatenate(
            [b_group[l], b_router[l], jnp.zeros((LANES - N_GROUPS - N_EXPERTS,), F32)]).reshape(1, LANES)
        h, routing, counts = _route(merged, xt, w_out[l].astype(BF16), ln1_gain[l].reshape(1, dm),
                                    ln1_bias[l].reshape(1, dm), w_route, b_route, alpha)

        expert_ids = routing[:, :TOP_K].astype(jnp.int32).reshape(-1)
        ranks = routing[:, 2 * TOP_K:3 * TOP_K].astype(jnp.int32).reshape(-1)
        tile_expert, run_next, tile_rows, src, dst, n_valid = _dispatch_tables(
            expert_ids, ranks, counts[0, :N_EXPERTS].astype(jnp.int32), min(MOE_TM, n))
        y_rows = _experts(h, tile_expert, run_next, tile_rows, src, dst, n_valid, w_expert_gate[l],
                          w_expert_up[l], w_expert_down[l], TOP_K * n)
        xt = _final(h, y_rows, routing,
                    ln2_gain[l].reshape(1, dm), ln2_bias[l].reshape(1, dm), alpha)
    return xt.reshape(batch, seq, dm)
```

```python
import functools
import math

import jax
import jax.numpy as jnp
from jax import lax
from jax.experimental import pallas as pl
from jax.experimental.pallas import tpu as pltpu

F32 = jnp.float32
BF16 = jnp.bfloat16

CHUNK = 64
RET_HEADS = 8
RET_QK_DIM = 128
RET_V_DIM = 256
RET_THETA = 10000.0
DIFF_HEADS = 8
DIFF_QK_DIM = 128
DIFF_V_DIM = 256
ROPE_THETA = 500000.0
ROPE_DIM = DIFF_QK_DIM // 4
N_GROUPS = 4
EXPERTS_PER_GROUP = 8
N_EXPERTS = N_GROUPS * EXPERTS_PER_GROUP
TOP_K = 2
LN_EPS = 1e-5

LANES = 128
BF16_ROWS = 16
NEG_BIG = -1e30

VMEM_LIMIT_BYTES = 56 * 1024 * 1024
TABLE_ROWS = 512
PROJ_TM = 1024
PROJ_TN = 1024
RET_BLOCK = 256
RET_HEAD_GROUP = 8
ATT_BLOCK = 1024
MERGE_TM = 256
ROUTE_TM = 512
ROUTE_SPLITS = 2
MOE_TM = 256
DMA_GROUP = 8
WEIGHT_CAST_ROWS = 256
NOT_RUN_START = -2
NO_NEXT_EXPERT = -1
FINAL_TM = 256


def _cparams(n_axes):
    return pltpu.CompilerParams(
        dimension_semantics=("arbitrary",) * n_axes, vmem_limit_bytes=VMEM_LIMIT_BYTES)


def _sigmoid(x):
    return 1.0 / (1.0 + jnp.exp(-x))


def _chunk_of(idx):
    return jnp.right_shift(idx, CHUNK.bit_length() - 1)


def _dot(a, b):
    return jnp.dot(a, b, preferred_element_type=F32)


def _dot_nt(a, b):
    return lax.dot_general(a, b, (((1,), (1,)), ((), ())), preferred_element_type=F32)


def _table_kernel(pos_ref, freq_ref, rc_ref, rs_ref, dc_ref, dlo_ref, dhi_ref):
    rh, dh = RET_QK_DIM // 2, ROPE_DIM // 2
    ang = pos_ref[...].astype(F32) * freq_ref[...]
    c, s = jnp.cos(ang), jnp.sin(ang)
    lane = lax.broadcasted_iota(jnp.int32, ang.shape, 1)
    rc_ref[...] = jnp.where(lane < rh, c, pltpu.roll(c, rh, 1))
    rs_ref[...] = jnp.where(lane < rh, -s, pltpu.roll(s, rh, 1))
    c_lo, s_lo = pltpu.roll(c, LANES - rh, 1), pltpu.roll(s, LANES - rh, 1)
    c_hi, s_hi = pltpu.roll(c, LANES - rh + dh, 1), pltpu.roll(s, LANES - rh + dh, 1)
    dc_ref[...] = jnp.where(lane < dh, c_lo, jnp.where(lane < 2 * dh, c_hi, 1.0))
    dlo_ref[...] = jnp.where(lane < dh, -s_lo, 0.0)
    dhi_ref[...] = jnp.where((lane >= dh) & (lane < 2 * dh), s_hi, 0.0)


def _rope_tables(pos_col, freq_row):
    n = pos_col.shape[0]
    tr = min(TABLE_ROWS, n)
    out = pl.BlockSpec((tr, LANES), lambda i: (i, 0))
    shp = jax.ShapeDtypeStruct((n, LANES), F32)
    return pl.pallas_call(
        _table_kernel, grid=(n // tr,),
        in_specs=[pl.BlockSpec((tr, 1), lambda i: (i, 0)), pl.BlockSpec((1, LANES), lambda i: (0, 0))],
        out_specs=[out] * 5, out_shape=[shp] * 5,
        compiler_params=_cparams(1), name="rope_tables",
    )(pos_col, freq_row)


def _matmul_kernel(x_ref, w_ref, o_ref, wb_ref):
    @pl.when(pl.program_id(1) == 0)
    def _():
        wb_ref[...] = w_ref[...].astype(BF16)

    o_ref[...] = _dot(x_ref[...].astype(BF16), wb_ref[...]).astype(o_ref.dtype)


def _matmul(x, w, out_dtype, name):
    m, k = x.shape
    n = w.shape[1]
    tm, tn = min(PROJ_TM, m), min(PROJ_TN, n)
    return pl.pallas_call(
        _matmul_kernel, grid=(n // tn, m // tm),
        in_specs=[pl.BlockSpec((tm, k), lambda j, i: (i, 0)),
                  pl.BlockSpec((k, tn), lambda j, i: (0, j))],
        out_specs=pl.BlockSpec((tm, tn), lambda j, i: (i, j)),
        out_shape=jax.ShapeDtypeStruct((m, n), out_dtype),
        scratch_shapes=[pltpu.VMEM((k, tn), BF16)],
        compiler_params=_cparams(2), name=name,
    )(x, w)


def _retention_kernel(lg_ref, bd_ref, q_ref, k_ref, v_ref, g_ref, cos_ref, sin_ref, gain_ref,
                      o_ref, state_ref, decay_ref, qdecay_ref, kdecay_ref):
    group = state_ref.shape[0]
    first_head = pl.program_id(1) * group
    t = q_ref.shape[0]
    dk, dv = RET_QK_DIM, RET_V_DIM

    @pl.when(pl.program_id(2) == 0)
    def _():
        state_ref[...] = jnp.zeros_like(state_ref)
        ri = lax.broadcasted_iota(jnp.int32, (t, t), 0)
        ci = lax.broadcasted_iota(jnp.int32, (t, t), 1)
        dist = jnp.abs(ri - ci).astype(F32)
        visible = _chunk_of(ri) >= _chunk_of(ci)
        row = lax.broadcasted_iota(jnp.int32, (t, dk), 0).astype(F32)
        for j in range(group):
            lg = lg_ref[first_head + j]
            decay_ref[j] = jnp.where(visible, jnp.exp(lg * dist), 0.0)
            qdecay_ref[j] = jnp.exp(lg * (row + 1.0))
            kdecay_ref[j] = jnp.exp(lg * (t - 1.0 - row))

    cos, sin = cos_ref[...], sin_ref[...]

    def rot(x):
        return x * cos + pltpu.roll(x, dk // 2, 1) * sin

    for j in range(group):
        q = rot(q_ref[:, j * dk:(j + 1) * dk].astype(F32))
        k = rot(k_ref[:, j * dk:(j + 1) * dk].astype(F32)) * (dk ** -0.5)
        v = v_ref[:, j * dv:(j + 1) * dv]

        s = _dot_nt(q.astype(BF16), k.astype(BF16)) * decay_ref[j]
        intra = _dot(s.astype(BF16), v)

        state = state_ref[j]
        cross = _dot((q * qdecay_ref[j]).astype(BF16), state.astype(BF16))
        kd = k * kdecay_ref[j]
        state_ref[j] = state * bd_ref[first_head + j] + _dot(kd.T.astype(BF16), v)

        out = intra + cross
        mu = jnp.mean(out, axis=-1, keepdims=True)
        var = jnp.mean(jnp.square(out - mu), axis=-1, keepdims=True)
        y = (out - mu) * lax.rsqrt(var + LN_EPS) * gain_ref[j]
        gate = g_ref[:, j * dv:(j + 1) * dv].astype(F32)
        o_ref[:, j * dv:(j + 1) * dv] = (y * (gate * _sigmoid(gate))).astype(o_ref.dtype)


def _retention(u, cos, sin, log_gamma, block_decay, gain, batch, seq):
    t = min(RET_BLOCK, seq)
    nb = seq // t
    grp = RET_HEAD_GROUP
    n_groups = RET_HEADS // grp
    rows = lambda b, h, i: b * nb + i
    qk_w, v_w = grp * RET_QK_DIM, grp * RET_V_DIM
    k_off, v_off, g_off = n_groups, n_groups, 2 * n_groups
    smem = pl.BlockSpec(memory_space=pltpu.SMEM)
    return pl.pallas_call(
        _retention_kernel, grid=(batch, n_groups, nb),
        in_specs=[
            smem, smem,
            pl.BlockSpec((t, qk_w), lambda b, h, i: (rows(b, h, i), h)),
            pl.BlockSpec((t, qk_w), lambda b, h, i: (rows(b, h, i), k_off + h)),
            pl.BlockSpec((t, v_w), lambda b, h, i: (rows(b, h, i), v_off + h)),
            pl.BlockSpec((t, v_w), lambda b, h, i: (rows(b, h, i), g_off + h)),
            pl.BlockSpec((t, LANES), lambda b, h, i: (rows(b, h, i), 0)),
            pl.BlockSpec((t, LANES), lambda b, h, i: (rows(b, h, i), 0)),
            pl.BlockSpec((grp, 1, RET_V_DIM), lambda b, h, i: (h, 0, 0)),
        ],
        out_specs=pl.BlockSpec((t, v_w), lambda b, h, i: (rows(b, h, i), h)),
        out_shape=jax.ShapeDtypeStruct((batch * seq, RET_HEADS * RET_V_DIM), BF16),
        scratch_shapes=[pltpu.VMEM((grp, RET_QK_DIM, RET_V_DIM), F32), pltpu.VMEM((grp, t, t), F32),
                        pltpu.VMEM((grp, t, RET_QK_DIM), F32), pltpu.VMEM((grp, t, RET_QK_DIM), F32)],
        compiler_params=_cparams(3), name="retention",
    )(log_gamma, block_decay, u, u, u, u, cos, sin, gain)


def _partial_rope(x, cos, sin_lo, sin_hi):
    half = ROPE_DIM // 2
    return (x * cos + pltpu.roll(x, DIFF_QK_DIM - half, 1) * sin_lo
            + pltpu.roll(x, half, 1) * sin_hi)


def _diff_attn_kernel(lam_ref, q_ref, k_ref, v_ref, qc_ref, qa_ref, qb_ref, kc_ref, ka_ref, kb_ref,
                      gain_ref, o_ref, krot_ref, vt_ref, m_ref, acc_ref, sa_ref, sb_ref, *,
                      lambda_init):
    qblk = pl.program_id(2)
    tq = q_ref.shape[0]
    d = DIFF_QK_DIM

    @pl.when(qblk == 0)
    def _():
        def prep_rows(c, carry):
            rows = pl.ds(pl.multiple_of(c * tq, tq), tq)
            for i in range(2):
                x = k_ref[rows, i * d:(i + 1) * d].astype(F32)
                krot_ref[rows, i * d:(i + 1) * d] = _partial_rope(
                    x, kc_ref[rows, :], ka_ref[rows, :], kb_ref[rows, :]).astype(BF16)
            vt_ref[c, :DIFF_V_DIM] = v_ref[rows, :].astype(F32).T.astype(BF16)
            vt_ref[c, DIFF_V_DIM:] = jnp.ones((BF16_ROWS, tq), BF16)
            return carry
        lax.fori_loop(0, k_ref.shape[0] // tq, prep_rows, 0)

    scale = d ** -0.5
    qs = []
    for i in range(2):
        x = q_ref[:, i * d:(i + 1) * d].astype(F32)
        qs.append((_partial_rope(x, qc_ref[...], qa_ref[...], qb_ref[...]) * scale).astype(BF16))

    m_ref[...] = jnp.full_like(m_ref, NEG_BIG)
    acc_ref[...] = jnp.zeros_like(acc_ref)

    s_bufs = (sa_ref, sb_ref)

    def scores(j, buf, diagonal):
        rows = pl.ds(pl.multiple_of(j * tq, tq), tq)
        for i in range(2):
            s = _dot_nt(krot_ref[rows, i * d:(i + 1) * d], qs[i])
            if diagonal:
                key = lax.broadcasted_iota(jnp.int32, s.shape, 0)
                qry = lax.broadcasted_iota(jnp.int32, s.shape, 1)
                s = jnp.where(_chunk_of(qry) >= _chunk_of(key), s, NEG_BIG)
            s_bufs[buf][i] = s

    def absorb(j, buf):
        vt = vt_ref[j]
        for i in range(2):
            s = s_bufs[buf][i]
            m_old = m_ref[i]
            m_new = jnp.maximum(m_old, jnp.max(s, axis=0, keepdims=True))
            alpha = jnp.exp(m_old - m_new)
            p = jnp.exp(s - m_new)
            acc_ref[i] = alpha * acc_ref[i] + _dot(vt, p.astype(BF16))
            m_ref[i] = m_new

    scores(qblk, 0, True)

    def pair(p, carry):
        pending = jnp.where(p == 0, qblk, 2 * p - 1)
        scores(2 * p, 1, False)
        absorb(pending, 0)
        scores(2 * p + 1, 0, False)
        absorb(2 * p, 1)
        return carry
    n_pairs = qblk // 2
    lax.fori_loop(0, n_pairs, pair, 0)
    pending = jnp.where(n_pairs == 0, qblk, 2 * n_pairs - 1)

    @pl.when(qblk % 2 == 1)
    def _():
        scores(qblk - 1, 1, False)
        absorb(pending, 0)
        absorb(qblk - 1, 1)

    @pl.when(qblk % 2 == 0)
    def _():
        absorb(pending, 0)

    lp = lam_ref[...]
    lam = (jnp.exp(jnp.sum(lp[0:1] * lp[1:2], axis=-1, keepdims=True))
           - jnp.exp(jnp.sum(lp[2:3] * lp[3:4], axis=-1, keepdims=True)) + lambda_init)
    dv = DIFF_V_DIM
    o = (acc_ref[0, :dv] / acc_ref[0, dv:dv + 1] - lam * (acc_ref[1, :dv] / acc_ref[1, dv:dv + 1])).T
    o = o * lax.rsqrt(jnp.mean(jnp.square(o), axis=-1, keepdims=True) + LN_EPS) * gain_ref[...]
    o_ref[...] = (o * (1.0 - lambda_init)).astype(o_ref.dtype)


def _diff_attention(u, cos, sin_lo, sin_hi, lam_params, gain, batch, seq, lambda_init):
    tq = min(ATT_BLOCK, seq)
    nb = seq // tq
    width = 2 * DIFF_QK_DIM
    ret_cols = 2 * RET_HEADS * RET_QK_DIM + 2 * RET_HEADS * RET_V_DIM
    q_off = ret_cols // width
    k_off = q_off + DIFF_HEADS
    v_off = k_off + DIFF_HEADS
    qrow = lambda b, h, i: b * nb + i
    qtab = pl.BlockSpec((tq, LANES), lambda b, h, i: (qrow(b, h, i), 0))
    ktab = pl.BlockSpec((seq, LANES), lambda b, h, i: (b, 0), pipeline_mode=pl.Buffered(1))
    return pl.pallas_call(
        functools.partial(_diff_attn_kernel, lambda_init=lambda_init),
        grid=(batch, DIFF_HEADS, nb),
        in_specs=[
            pl.BlockSpec((4, DIFF_QK_DIM), lambda b, h, i: (0, 0)),
            pl.BlockSpec((tq, width), lambda b, h, i: (qrow(b, h, i), q_off + h)),
            pl.BlockSpec((seq, width), lambda b, h, i: (b, k_off + h)),
            pl.BlockSpec((seq, DIFF_V_DIM), lambda b, h, i: (b, v_off + h)),
            qtab, qtab, qtab, ktab, ktab, ktab,
            pl.BlockSpec((1, DIFF_V_DIM), lambda b, h, i: (0, 0)),
        ],
        out_specs=pl.BlockSpec((tq, DIFF_V_DIM), lambda b, h, i: (qrow(b, h, i), h)),
        out_shape=jax.ShapeDtypeStruct((batch * seq, DIFF_HEADS * DIFF_V_DIM), BF16),
        scratch_shapes=[pltpu.VMEM((seq, width), BF16),
                        pltpu.VMEM((nb, DIFF_V_DIM + BF16_ROWS, tq), BF16),
                        pltpu.VMEM((2, 1, tq), F32),
                        pltpu.VMEM((2, DIFF_V_DIM + BF16_ROWS, tq), F32),
                        pltpu.VMEM((2, tq, tq), F32), pltpu.VMEM((2, tq, tq), F32)],
        compiler_params=_cparams(3), name="diff_attention",
    )(lam_params, u, u, u, cos, sin_lo, sin_hi, cos, sin_lo, sin_hi, gain)


def _merge_kernel(ret_ref, diff_ref, gr_ref, gd_ref, wr_ref, wd_ref, o_ref):
    a = _dot(ret_ref[...], wr_ref[...])
    b = _dot(diff_ref[...], wd_ref[...])
    merged = _sigmoid(gr_ref[...].astype(F32)) * a + _sigmoid(gd_ref[...].astype(F32)) * b
    o_ref[...] = merged.astype(o_ref.dtype)


def _merge(ret, diff, u, w_ret, w_diff):
    n, dm = ret.shape[0], w_ret.shape[1]
    tm = min(MERGE_TM, n)
    gate_off = (u.shape[1] - 2 * dm) // dm
    tile = lambda c: pl.BlockSpec((tm, dm), lambda i: (i, c))
    weight = lambda w: pl.BlockSpec(w.shape, lambda i: (0, 0))
    return pl.pallas_call(
        _merge_kernel, grid=(n // tm,),
        in_specs=[pl.BlockSpec((tm, ret.shape[1]), lambda i: (i, 0)),
                  pl.BlockSpec((tm, diff.shape[1]), lambda i: (i, 0)),
                  tile(gate_off), tile(gate_off + 1), weight(w_ret), weight(w_diff)],
        out_specs=tile(0), out_shape=jax.ShapeDtypeStruct((n, dm), BF16),
        compiler_params=_cparams(1), name="gated_merge",
    )(ret, diff, u, u, w_ret, w_diff)


def _layer_norm(z, gain, bias):
    mu = jnp.mean(z, axis=-1, keepdims=True)
    var = jnp.mean(jnp.square(z - mu), axis=-1, keepdims=True)
    return (z - mu) * lax.rsqrt(var + LN_EPS) * gain + bias


def _split_bf16(a):
    hi = a.astype(BF16)
    return hi, (a - hi.astype(F32)).astype(BF16)


def _route_kernel(m_ref, x_ref, wo_ref, gain_ref, bias_ref, wr_ref, br_ref, h_ref, r_ref, count_ref,
                  *, alpha):
    @pl.when(pl.program_id(0) == 0)
    def _():
        count_ref[...] = jnp.zeros_like(count_ref)

    w_hi, w_lo = _split_bf16(wr_ref[...])
    part = h_ref.shape[0] // ROUTE_SPLITS
    for g in range(ROUTE_SPLITS):
        rows = slice(g * part, (g + 1) * part)
        h = _layer_norm(alpha * x_ref[rows, :] + _dot(m_ref[rows, :], wo_ref[...]),
                        gain_ref[...], bias_ref[...])
        h_ref[rows, :] = h
        h_hi, h_lo = _split_bf16(h)
        logits = _dot(h_hi, w_hi) + (_dot(h_hi, w_lo) + _dot(h_lo, w_hi)) + br_ref[...]
        r_ref[rows, :] = _routing_rows(logits, count_ref)


def _routing_rows(logits, count_ref):
    lane = lax.broadcasted_iota(jnp.int32, logits.shape, 1).astype(F32)

    def first_argmax(vals):
        top = jnp.max(vals, axis=-1, keepdims=True)
        idx = jnp.min(jnp.where(vals == top, lane, float(LANES)), axis=-1, keepdims=True)
        return top, idx

    group_logits = jnp.where(lane < N_GROUPS, logits, NEG_BIG)
    g_top, g_idx = first_argmax(group_logits)
    p_group = 1.0 / jnp.sum(jnp.exp(group_logits - g_top), axis=-1, keepdims=True)

    first = N_GROUPS + g_idx * EXPERTS_PER_GROUP
    within = jnp.where((lane >= first) & (lane < first + EXPERTS_PER_GROUP), logits, NEG_BIG)
    v1, i1 = first_argmax(within)
    v2, i2 = first_argmax(jnp.where(lane == i1, NEG_BIG, within))
    e2 = jnp.exp(v2 - v1)
    w1 = p_group / (1.0 + e2)
    w2 = p_group * e2 / (1.0 + e2)
    x1, x2 = i1 - N_GROUPS, i2 - N_GROUPS

    hit1 = jnp.where(lane == x1, 1.0, 0.0)
    hit2 = jnp.where(lane == x2, 1.0, 0.0)
    hits = hit1 + hit2
    tm = hits.shape[0]
    earlier = (lax.broadcasted_iota(jnp.int32, (tm, tm), 0) > lax.broadcasted_iota(jnp.int32, (tm, tm), 1))
    before = count_ref[...] + _dot(jnp.where(earlier, 1.0, 0.0).astype(BF16), hits.astype(BF16))
    rank1 = jnp.sum(before * hit1, axis=-1, keepdims=True)
    rank2 = jnp.sum(before * hit2, axis=-1, keepdims=True)
    count_ref[...] += jnp.sum(hits, axis=0, keepdims=True)

    fields = (x1, x2, w1, w2, rank1, rank2)
    row = jnp.zeros_like(logits)
    for pos, value in enumerate(fields):
        row = jnp.where(lane == pos, value, row)
    return row


def _route(merged, x, w_out, gain, bias, w_router, b_router, alpha):
    n, dm = x.shape
    tm = min(ROUTE_TM, n)
    tile = pl.BlockSpec((tm, dm), lambda i: (i, 0))
    full = lambda a: pl.BlockSpec(a.shape, lambda i: (0, 0))
    return pl.pallas_call(
        functools.partial(_route_kernel, alpha=alpha), grid=(n // tm,),
        in_specs=[tile, tile, full(w_out), full(gain), full(bias), full(w_router), full(b_router)],
        out_specs=[tile, pl.BlockSpec((tm, LANES), lambda i: (i, 0)),
                   pl.BlockSpec((1, LANES), lambda i: (0, 0))],
        out_shape=[jax.ShapeDtypeStruct((n, dm), F32), jax.ShapeDtypeStruct((n, LANES), F32),
                   jax.ShapeDtypeStruct((1, LANES), F32)],
        compiler_params=_cparams(1), name="outproj_ln_router",
    )(merged, x, w_out, gain, bias, w_router, b_router)


def _row_copies(table_ref, first, n_rows, make_copy):
    for g in range(n_rows // DMA_GROUP):
        entries = [table_ref[first + g * DMA_GROUP + k] for k in range(DMA_GROUP)]
        for k, entry in enumerate(entries):
            make_copy(entry, g * DMA_GROUP + k).start()


def _for_buffer(dynamic_buf, fn):
    for buf in range(2):
        pl.when(dynamic_buf == buf)(functools.partial(fn, buf))


def _expert_kernel(tile_expert_ref, run_next_ref, src_ref, n_valid_ref, h_hbm, wg_hbm, wu_hbm, wd_hbm,
                   y_ref, xbuf, stage_g, stage_u, stage_d, wg_ref, wu_ref, wd_ref, gather_sem,
                   weight_sem):
    i = pl.program_id(0)
    n_valid = n_valid_ref[0]
    tm = xbuf.shape[1]
    slot = i & 1
    weights = ((wg_hbm, stage_g, wg_ref), (wu_hbm, stage_u, wu_ref), (wd_hbm, stage_d, wd_ref))

    def start_weights(expert):
        for k, (hbm, stage, _) in enumerate(weights):
            pltpu.make_async_copy(hbm.at[expert], stage, weight_sem.at[k]).start()

    def take_weights():
        for k, (hbm, stage, dst) in enumerate(weights):
            pltpu.make_async_copy(hbm.at[0], stage, weight_sem.at[k]).wait()

            def round_rows(c, carry, stage=stage, dst=dst):
                rows = pl.ds(pl.multiple_of(c * WEIGHT_CAST_ROWS, WEIGHT_CAST_ROWS), WEIGHT_CAST_ROWS)
                dst[rows, :] = stage[rows, :].astype(BF16)
                return carry
            lax.fori_loop(0, stage.shape[0] // WEIGHT_CAST_ROWS, round_rows, 0)

    def start_gather(tile, dynamic_buf):
        def issue(buf):
            _row_copies(src_ref, tile * tm, tm, lambda tok, r: pltpu.make_async_copy(
                h_hbm.at[pl.ds(tok, 1)], xbuf.at[buf, pl.ds(r, 1)], gather_sem.at[buf]))
        _for_buffer(dynamic_buf, issue)

    @pl.when(i == 0)
    def _():
        start_gather(0, 0)
        start_weights(tile_expert_ref[0])

    @pl.when(i + 1 < n_valid)
    def _():
        start_gather(i + 1, 1 - slot)

    @pl.when(i < n_valid)
    def _():
        run_next = run_next_ref[i]

        @pl.when(run_next != NOT_RUN_START)
        def _():
            take_weights()

            @pl.when(run_next != NO_NEXT_EXPERT)
            def _():
                start_weights(run_next)

        pltpu.make_async_copy(h_hbm.at[pl.ds(0, tm)], xbuf.at[slot], gather_sem.at[slot]).wait()
        x = xbuf[slot].astype(BF16)
        g = _dot(x, wg_ref[...])
        act = (g * _sigmoid(g) * _dot(x, wu_ref[...])).astype(BF16)
        y_ref[...] = _dot(act, wd_ref[...])

    @pl.when(i >= n_valid)
    def _():
        y_ref[...] = jnp.zeros_like(y_ref)


def _experts(h, tile_expert, run_next, src, n_valid, w_gate, w_up, w_down):
    n, dm = h.shape
    n_tiles = tile_expert.shape[0]
    tm = src.shape[0] // n_tiles
    ff = w_gate.shape[2]
    hbm = pl.BlockSpec(memory_space=pl.ANY)
    grid_spec = pltpu.PrefetchScalarGridSpec(
        num_scalar_prefetch=4, grid=(n_tiles,),
        in_specs=[hbm, hbm, hbm, hbm],
        out_specs=pl.BlockSpec((tm, dm), lambda i, *_: (i, 0)),
        scratch_shapes=[pltpu.VMEM((2, tm, dm), F32),
                        pltpu.VMEM((dm, ff), F32), pltpu.VMEM((dm, ff), F32), pltpu.VMEM((ff, dm), F32),
                        pltpu.VMEM((dm, ff), BF16), pltpu.VMEM((dm, ff), BF16), pltpu.VMEM((ff, dm), BF16),
                        pltpu.SemaphoreType.DMA((2,)), pltpu.SemaphoreType.DMA((3,))])
    return pl.pallas_call(
        _expert_kernel, grid_spec=grid_spec,
        out_shape=jax.ShapeDtypeStruct((src.shape[0], dm), F32),
        compiler_params=_cparams(1), name="experts",
    )(tile_expert, run_next, src, n_valid, h, w_gate, w_up, w_down)


def _dispatch_tables(expert_ids, ranks, counts, tm):
    n_assign = expert_ids.shape[0]
    n_tiles = n_assign // tm + N_EXPERTS
    n_slots = n_tiles * tm
    experts = jnp.arange(N_EXPERTS, dtype=jnp.int32)
    padded = (counts + tm - 1) // tm * tm
    pad_end = jnp.cumsum(padded)
    pad_start = pad_end - padded
    slot_of = jnp.sum(jnp.where(expert_ids[:, None] == experts[None, :], pad_start[None, :], 0), axis=1) + ranks
    tile_start = jnp.arange(n_tiles, dtype=jnp.int32) * tm
    tile_expert = jnp.minimum(
        jnp.sum((pad_end[None, :] <= tile_start[:, None]).astype(jnp.int32), axis=1), N_EXPERTS - 1)
    later = lax.cummin(jnp.where(counts > 0, experts, N_EXPERTS)[::-1])[::-1]
    following = jnp.concatenate([later[1:], jnp.full((1,), N_EXPERTS, jnp.int32)])[tile_expert]
    run_start = jnp.concatenate([jnp.ones((1,), bool), tile_expert[1:] != tile_expert[:-1]])
    run_next = jnp.where(run_start, jnp.where(following < N_EXPERTS, following, NO_NEXT_EXPERT),
                         NOT_RUN_START).astype(jnp.int32)
    token = jnp.arange(n_assign, dtype=jnp.int32) // TOP_K
    src = jnp.zeros((n_slots,), jnp.int32).at[slot_of].set(token)
    n_valid = (pad_end[-1] // tm).astype(jnp.int32).reshape(1)
    return slot_of, tile_expert, run_next, src, n_valid


def _final_kernel(slot_ref, h_ref, y_hbm, r_ref, gain_ref, bias_ref, o_ref, ybuf, sem, *, alpha):
    i = pl.program_id(0)
    tm = o_ref.shape[0]
    slot = i & 1

    def start_gather(step, dynamic_buf):
        def issue(buf):
            _row_copies(slot_ref, step * (TOP_K * tm), TOP_K * tm, lambda row, r: pltpu.make_async_copy(
                y_hbm.at[pl.ds(row, 1)], ybuf.at[buf, r % TOP_K, pl.ds(r // TOP_K, 1)], sem.at[buf]))
        _for_buffer(dynamic_buf, issue)

    @pl.when(i == 0)
    def _():
        start_gather(0, 0)

    @pl.when(i + 1 < pl.num_programs(0))
    def _():
        start_gather(i + 1, 1 - slot)

    for k in range(TOP_K):
        pltpu.make_async_copy(y_hbm.at[pl.ds(0, tm)], ybuf.at[slot, k], sem.at[slot]).wait()
    r = r_ref[...]
    ffn = ybuf[slot, 0] * r[:, 2:3] + ybuf[slot, 1] * r[:, 3:4]
    o_ref[...] = _layer_norm(alpha * h_ref[...] + ffn, gain_ref[...], bias_ref[...])


def _final(h, y_slots, slot_of, routing, gain, bias, alpha):
    n, dm = h.shape
    tm = min(FINAL_TM, n)
    tile = pl.BlockSpec((tm, dm), lambda i, *_: (i, 0))
    full = lambda a: pl.BlockSpec(a.shape, lambda i, *_: (0, 0))
    grid_spec = pltpu.PrefetchScalarGridSpec(
        num_scalar_prefetch=1, grid=(n // tm,),
        in_specs=[tile, pl.BlockSpec(memory_space=pl.ANY), pl.BlockSpec((tm, LANES), lambda i, *_: (i, 0)),
                  full(gain), full(bias)],
        out_specs=tile,
        scratch_shapes=[pltpu.VMEM((2, TOP_K, tm, dm), F32), pltpu.SemaphoreType.DMA((2,))])
    return pl.pallas_call(
        functools.partial(_final_kernel, alpha=alpha), grid_spec=grid_spec,
        out_shape=jax.ShapeDtypeStruct((n, dm), F32),
        compiler_params=_cparams(1), name="combine_ln",
    )(slot_of, h, y_slots, routing, gain, bias)


def _rope_frequencies():
    ret_freq = 1.0 / (RET_THETA ** (jnp.arange(0, RET_QK_DIM, 2, dtype=F32) / RET_QK_DIM))
    diff_freq = 1.0 / (ROPE_THETA ** (jnp.arange(0, ROPE_DIM, 2, dtype=F32) / ROPE_DIM))
    pad = jnp.zeros((LANES - ret_freq.shape[0] - diff_freq.shape[0],), F32)
    return jnp.concatenate([ret_freq, diff_freq, pad]).reshape(1, LANES)


def kernel(x, positions, w_in, ret_gn_gain, diff_lambda_q1, diff_lambda_k1, diff_lambda_q2, diff_lambda_k2, diff_subln_gain, w_ret_proj, w_diff_proj, w_out, ln1_gain, ln1_bias, w_group, b_group, w_router, b_router, w_expert_gate, w_expert_up, w_expert_down, ln2_gain, ln2_bias):
    batch, seq, dm = x.shape
    n = batch * seq
    depth = w_in.shape[0]
    assert seq % CHUNK == 0
    alpha = (2 * depth) ** 0.25

    ret_cos, ret_sin, diff_cos, diff_sin_lo, diff_sin_hi = _rope_tables(
        positions.reshape(n, 1), _rope_frequencies())

    log_gamma = jnp.log(1.0 - 2.0 ** (-5.0 - jnp.arange(RET_HEADS, dtype=F32)))
    block_decay = jnp.exp(log_gamma * min(RET_BLOCK, seq))

    xt = x.reshape(n, dm)
    for l in range(depth):
        lambda_init = 0.8 - 0.6 * math.exp(-0.3 * l)
        u = _matmul(xt, w_in[l], BF16, "in_proj")
        ret = _retention(u, ret_cos, ret_sin, log_gamma, block_decay,
                         ret_gn_gain[l].reshape(RET_HEADS, 1, RET_V_DIM), batch, seq)
        lam_params = jnp.stack([diff_lambda_q1[l], diff_lambda_k1[l], diff_lambda_q2[l], diff_lambda_k2[l]])
        diff = _diff_attention(u, diff_cos, diff_sin_lo, diff_sin_hi, lam_params.astype(F32),
                               diff_subln_gain[l].reshape(1, DIFF_V_DIM), batch, seq, lambda_init)
        merged = _merge(ret, diff, u, w_ret_proj[l].astype(BF16), w_diff_proj[l].astype(BF16))

        w_route = jnp.concatenate(
            [w_group[l], w_router[l], jnp.zeros((dm, LANES - N_GROUPS - N_EXPERTS), F32)], axis=1)
        b_route = jnp.concatenate(
            [b_group[l], b_router[l], jnp.zeros((LANES - N_GROUPS - N_EXPERTS,), F32)]).reshape(1, LANES)
        h, routing, counts = _route(merged, xt, w_out[l].astype(BF16), ln1_gain[l].reshape(1, dm),
                                    ln1_bias[l].reshape(1, dm), w_route, b_route, alpha)

        expert_ids = routing[:, :TOP_K].astype(jnp.int32).reshape(-1)
        ranks = routing[:, 2 * TOP_K:3 * TOP_K].astype(jnp.int32).reshape(-1)
        slot_of, tile_expert, run_next, src, n_valid = _dispatch_tables(
            expert_ids, ranks, counts[0, :N_EXPERTS].astype(jnp.int32), min(MOE_TM, n))
        y_slots = _experts(h, tile_expert, run_next, src, n_valid, w_expert_gate[l], w_expert_up[l],
                           w_expert_down[l])
        xt = _final(h, y_slots, slot_of, routing,
                    ln2_gain[l].reshape(1, dm), ln2_bias[l].reshape(1, dm), alpha)
    return xt.reshape(batch, seq, dm)
```

```python
import functools
import math

import jax
import jax.numpy as jnp
from jax import lax
from jax.experimental import pallas as pl
from jax.experimental.pallas import tpu as pltpu

F32 = jnp.float32
BF16 = jnp.bfloat16

CHUNK = 64
RET_HEADS = 8
RET_QK_DIM = 128
RET_V_DIM = 256
RET_THETA = 10000.0
DIFF_HEADS = 8
DIFF_QK_DIM = 128
DIFF_V_DIM = 256
ROPE_THETA = 500000.0
ROPE_DIM = DIFF_QK_DIM // 4
N_GROUPS = 4
EXPERTS_PER_GROUP = 8
N_EXPERTS = N_GROUPS * EXPERTS_PER_GROUP
TOP_K = 2
LN_EPS = 1e-5

LANES = 128
BF16_ROWS = 16
NEG_BIG = -1e30

VMEM_LIMIT_BYTES = 56 * 1024 * 1024
TABLE_ROWS = 512
PROJ_TM = 1024
PROJ_TN = 1024
RET_BLOCK = 256
RET_HEAD_GROUP = 8
ATT_BLOCK = 1024
MERGE_TM = 256
ROUTE_TM = 512
ROUTE_SPLITS = 2
MOE_TM = 256
DMA_GROUP = 8
WEIGHT_CAST_ROWS = 256
NOT_RUN_START = -2
NO_NEXT_EXPERT = -1
FINAL_TM = 256


def _cparams(n_axes):
    return pltpu.CompilerParams(
        dimension_semantics=("arbitrary",) * n_axes, vmem_limit_bytes=VMEM_LIMIT_BYTES)


def _sigmoid(x):
    return 1.0 / (1.0 + jnp.exp(-x))


def _chunk_of(idx):
    return jnp.right_shift(idx, CHUNK.bit_length() - 1)


def _dot(a, b):
    return jnp.dot(a, b, preferred_element_type=F32)


def _dot_nt(a, b):
    return lax.dot_general(a, b, (((1,), (1,)), ((), ())), preferred_element_type=F32)


def _table_kernel(pos_ref, freq_ref, rc_ref, rs_ref, dc_ref, dlo_ref, dhi_ref):
    rh, dh = RET_QK_DIM // 2, ROPE_DIM // 2
    ang = pos_ref[...].astype(F32) * freq_ref[...]
    c, s = jnp.cos(ang), jnp.sin(ang)
    lane = lax.broadcasted_iota(jnp.int32, ang.shape, 1)
    rc_ref[...] = jnp.where(lane < rh, c, pltpu.roll(c, rh, 1))
    rs_ref[...] = jnp.where(lane < rh, -s, pltpu.roll(s, rh, 1))
    c_lo, s_lo = pltpu.roll(c, LANES - rh, 1), pltpu.roll(s, LANES - rh, 1)
    c_hi, s_hi = pltpu.roll(c, LANES - rh + dh, 1), pltpu.roll(s, LANES - rh + dh, 1)
    dc_ref[...] = jnp.where(lane < dh, c_lo, jnp.where(lane < 2 * dh, c_hi, 1.0))
    dlo_ref[...] = jnp.where(lane < dh, -s_lo, 0.0)
    dhi_ref[...] = jnp.where((lane >= dh) & (lane < 2 * dh), s_hi, 0.0)


def _rope_tables(pos_col, freq_row):
    n = pos_col.shape[0]
    tr = min(TABLE_ROWS, n)
    out = pl.BlockSpec((tr, LANES), lambda i: (i, 0))
    shp = jax.ShapeDtypeStruct((n, LANES), F32)
    return pl.pallas_call(
        _table_kernel, grid=(n // tr,),
        in_specs=[pl.BlockSpec((tr, 1), lambda i: (i, 0)), pl.BlockSpec((1, LANES), lambda i: (0, 0))],
        out_specs=[out] * 5, out_shape=[shp] * 5,
        compiler_params=_cparams(1), name="rope_tables",
    )(pos_col, freq_row)


def _matmul_kernel(x_ref, w_ref, o_ref, wb_ref):
    @pl.when(pl.program_id(1) == 0)
    def _():
        wb_ref[...] = w_ref[...].astype(BF16)

    o_ref[...] = _dot(x_ref[...].astype(BF16), wb_ref[...]).astype(o_ref.dtype)


def _matmul(x, w, out_dtype, name):
    m, k = x.shape
    n = w.shape[1]
    tm, tn = min(PROJ_TM, m), min(PROJ_TN, n)
    return pl.pallas_call(
        _matmul_kernel, grid=(n // tn, m // tm),
        in_specs=[pl.BlockSpec((tm, k), lambda j, i: (i, 0)),
                  pl.BlockSpec((k, tn), lambda j, i: (0, j))],
        out_specs=pl.BlockSpec((tm, tn), lambda j, i: (i, j)),
        out_shape=jax.ShapeDtypeStruct((m, n), out_dtype),
        scratch_shapes=[pltpu.VMEM((k, tn), BF16)],
        compiler_params=_cparams(2), name=name,
    )(x, w)


def _retention_kernel(lg_ref, bd_ref, q_ref, k_ref, v_ref, g_ref, cos_ref, sin_ref, gain_ref,
                      o_ref, state_ref, decay_ref, qdecay_ref, kdecay_ref):
    group = state_ref.shape[0]
    first_head = pl.program_id(1) * group
    t = q_ref.shape[0]
    dk, dv = RET_QK_DIM, RET_V_DIM

    @pl.when(pl.program_id(2) == 0)
    def _():
        state_ref[...] = jnp.zeros_like(state_ref)
        ri = lax.broadcasted_iota(jnp.int32, (t, t), 0)
        ci = lax.broadcasted_iota(jnp.int32, (t, t), 1)
        dist = jnp.abs(ri - ci).astype(F32)
        visible = _chunk_of(ri) >= _chunk_of(ci)
        row = lax.broadcasted_iota(jnp.int32, (t, dk), 0).astype(F32)
        for j in range(group):
            lg = lg_ref[first_head + j]
            decay_ref[j] = jnp.where(visible, jnp.exp(lg * dist), 0.0)
            qdecay_ref[j] = jnp.exp(lg * (row + 1.0))
            kdecay_ref[j] = jnp.exp(lg * (t - 1.0 - row))

    cos, sin = cos_ref[...], sin_ref[...]

    def rot(x):
        return x * cos + pltpu.roll(x, dk // 2, 1) * sin

    for j in range(group):
        q = rot(q_ref[:, j * dk:(j + 1) * dk].astype(F32))
        k = rot(k_ref[:, j * dk:(j + 1) * dk].astype(F32)) * (dk ** -0.5)
        v = v_ref[:, j * dv:(j + 1) * dv]

        s = _dot_nt(q.astype(BF16), k.astype(BF16)) * decay_ref[j]
        intra = _dot(s.astype(BF16), v)

        state = state_ref[j]
        cross = _dot((q * qdecay_ref[j]).astype(BF16), state.astype(BF16))
        kd = k * kdecay_ref[j]
        state_ref[j] = state * bd_ref[first_head + j] + _dot(kd.T.astype(BF16), v)

        out = intra + cross
        mu = jnp.mean(out, axis=-1, keepdims=True)
        var = jnp.mean(jnp.square(out - mu), axis=-1, keepdims=True)
        y = (out - mu) * lax.rsqrt(var + LN_EPS) * gain_ref[j]
        gate = g_ref[:, j * dv:(j + 1) * dv].astype(F32)
        o_ref[:, j * dv:(j + 1) * dv] = (y * (gate * _sigmoid(gate))).astype(o_ref.dtype)


def _retention(u, cos, sin, log_gamma, block_decay, gain, batch, seq):
    t = min(RET_BLOCK, seq)
    nb = seq // t
    grp = RET_HEAD_GROUP
    n_groups = RET_HEADS // grp
    rows = lambda b, h, i: b * nb + i
    qk_w, v_w = grp * RET_QK_DIM, grp * RET_V_DIM
    k_off, v_off, g_off = n_groups, n_groups, 2 * n_groups
    smem = pl.BlockSpec(memory_space=pltpu.SMEM)
    return pl.pallas_call(
        _retention_kernel, grid=(batch, n_groups, nb),
        in_specs=[
            smem, smem,
            pl.BlockSpec((t, qk_w), lambda b, h, i: (rows(b, h, i), h)),
            pl.BlockSpec((t, qk_w), lambda b, h, i: (rows(b, h, i), k_off + h)),
            pl.BlockSpec((t, v_w), lambda b, h, i: (rows(b, h, i), v_off + h)),
            pl.BlockSpec((t, v_w), lambda b, h, i: (rows(b, h, i), g_off + h)),
            pl.BlockSpec((t, LANES), lambda b, h, i: (rows(b, h, i), 0)),
            pl.BlockSpec((t, LANES), lambda b, h, i: (rows(b, h, i), 0)),
            pl.BlockSpec((grp, 1, RET_V_DIM), lambda b, h, i: (h, 0, 0)),
        ],
        out_specs=pl.BlockSpec((t, v_w), lambda b, h, i: (rows(b, h, i), h)),
        out_shape=jax.ShapeDtypeStruct((batch * seq, RET_HEADS * RET_V_DIM), BF16),
        scratch_shapes=[pltpu.VMEM((grp, RET_QK_DIM, RET_V_DIM), F32), pltpu.VMEM((grp, t, t), F32),
                        pltpu.VMEM((grp, t, RET_QK_DIM), F32), pltpu.VMEM((grp, t, RET_QK_DIM), F32)],
        compiler_params=_cparams(3), name="retention",
    )(log_gamma, block_decay, u, u, u, u, cos, sin, gain)


def _partial_rope(x, cos, sin_lo, sin_hi):
    half = ROPE_DIM // 2
    return (x * cos + pltpu.roll(x, DIFF_QK_DIM - half, 1) * sin_lo
            + pltpu.roll(x, half, 1) * sin_hi)


def _diff_attn_kernel(lam_ref, q_ref, k_ref, v_ref, qc_ref, qa_ref, qb_ref, kc_ref, ka_ref, kb_ref,
                      gain_ref, o_ref, krot_ref, vt_ref, m_ref, acc_ref, sa_ref, sb_ref, *,
                      lambda_init):
    qblk = pl.program_id(2)
    tq = q_ref.shape[0]
    d = DIFF_QK_DIM

    @pl.when(qblk == 0)
    def _():
        def prep_rows(c, carry):
            rows = pl.ds(pl.multiple_of(c * tq, tq), tq)
            for i in range(2):
                x = k_ref[rows, i * d:(i + 1) * d].astype(F32)
                krot_ref[rows, i * d:(i + 1) * d] = _partial_rope(
                    x, kc_ref[rows, :], ka_ref[rows, :], kb_ref[rows, :]).astype(BF16)
            vt_ref[c, :DIFF_V_DIM] = v_ref[rows, :].astype(F32).T.astype(BF16)
            vt_ref[c, DIFF_V_DIM:] = jnp.ones((BF16_ROWS, tq), BF16)
            return carry
        lax.fori_loop(0, k_ref.shape[0] // tq, prep_rows, 0)

    scale = d ** -0.5
    qs = []
    for i in range(2):
        x = q_ref[:, i * d:(i + 1) * d].astype(F32)
        qs.append((_partial_rope(x, qc_ref[...], qa_ref[...], qb_ref[...]) * scale).astype(BF16))

    m_ref[...] = jnp.full_like(m_ref, NEG_BIG)
    acc_ref[...] = jnp.zeros_like(acc_ref)

    s_bufs = (sa_ref, sb_ref)

    def scores(j, buf, diagonal):
        rows = pl.ds(pl.multiple_of(j * tq, tq), tq)
        for i in range(2):
            s = _dot_nt(krot_ref[rows, i * d:(i + 1) * d], qs[i])
            if diagonal:
                key = lax.broadcasted_iota(jnp.int32, s.shape, 0)
                qry = lax.broadcasted_iota(jnp.int32, s.shape, 1)
                s = jnp.where(_chunk_of(qry) >= _chunk_of(key), s, NEG_BIG)
            s_bufs[buf][i] = s

    def absorb(j, buf):
        vt = vt_ref[j]
        for i in range(2):
            s = s_bufs[buf][i]
            m_old = m_ref[i]
            m_new = jnp.maximum(m_old, jnp.max(s, axis=0, keepdims=True))
            alpha = jnp.exp(m_old - m_new)
            p = jnp.exp(s - m_new)
            acc_ref[i] = alpha * acc_ref[i] + _dot(vt, p.astype(BF16))
            m_ref[i] = m_new

    scores(qblk, 0, True)

    def pair(p, carry):
        pending = jnp.where(p == 0, qblk, 2 * p - 1)
        scores(2 * p, 1, False)
        absorb(pending, 0)
        scores(2 * p + 1, 0, False)
        absorb(2 * p, 1)
        return carry
    n_pairs = qblk // 2
    lax.fori_loop(0, n_pairs, pair, 0)
    pending = jnp.where(n_pairs == 0, qblk, 2 * n_pairs - 1)

    @pl.when(qblk % 2 == 1)
    def _():
        scores(qblk - 1, 1, False)
        absorb(pending, 0)
        absorb(qblk - 1, 1)

    @pl.when(qblk % 2 == 0)
    def _():
        absorb(pending, 0)

    lp = lam_ref[...]
    lam = (jnp.exp(jnp.sum(lp[0:1] * lp[1:2], axis=-1, keepdims=True))
           - jnp.exp(jnp.sum(lp[2:3] * lp[3:4], axis=-1, keepdims=True)) + lambda_init)
    dv = DIFF_V_DIM
    o = (acc_ref[0, :dv] / acc_ref[0, dv:dv + 1] - lam * (acc_ref[1, :dv] / acc_ref[1, dv:dv + 1])).T
    o = o * lax.rsqrt(jnp.mean(jnp.square(o), axis=-1, keepdims=True) + LN_EPS) * gain_ref[...]
    o_ref[...] = (o * (1.0 - lambda_init)).astype(o_ref.dtype)


def _diff_attention(u, cos, sin_lo, sin_hi, lam_params, gain, batch, seq, lambda_init):
    tq = min(ATT_BLOCK, seq)
    nb = seq // tq
    width = 2 * DIFF_QK_DIM
    ret_cols = 2 * RET_HEADS * RET_QK_DIM + 2 * RET_HEADS * RET_V_DIM
    q_off = ret_cols // width
    k_off = q_off + DIFF_HEADS
    v_off = k_off + DIFF_HEADS
    qrow = lambda b, h, i: b * nb + i
    qtab = pl.BlockSpec((tq, LANES), lambda b, h, i: (qrow(b, h, i), 0))
    ktab = pl.BlockSpec((seq, LANES), lambda b, h, i: (b, 0), pipeline_mode=pl.Buffered(1))
    return pl.pallas_call(
        functools.partial(_diff_attn_kernel, lambda_init=lambda_init),
        grid=(batch, DIFF_HEADS, nb),
        in_specs=[
            pl.BlockSpec((4, DIFF_QK_DIM), lambda b, h, i: (0, 0)),
            pl.BlockSpec((tq, width), lambda b, h, i: (qrow(b, h, i), q_off + h)),
            pl.BlockSpec((seq, width), lambda b, h, i: (b, k_off + h)),
            pl.BlockSpec((seq, DIFF_V_DIM), lambda b, h, i: (b, v_off + h)),
            qtab, qtab, qtab, ktab, ktab, ktab,
            pl.BlockSpec((1, DIFF_V_DIM), lambda b, h, i: (0, 0)),
        ],
        out_specs=pl.BlockSpec((tq, DIFF_V_DIM), lambda b, h, i: (qrow(b, h, i), h)),
        out_shape=jax.ShapeDtypeStruct((batch * seq, DIFF_HEADS * DIFF_V_DIM), BF16),
        scratch_shapes=[pltpu.VMEM((seq, width), BF16),
                        pltpu.VMEM((nb, DIFF_V_DIM + BF16_ROWS, tq), BF16),
                        pltpu.VMEM((2, 1, tq), F32),
                        pltpu.VMEM((2, DIFF_V_DIM + BF16_ROWS, tq), F32),
                        pltpu.VMEM((2, tq, tq), F32), pltpu.VMEM((2, tq, tq), F32)],
        compiler_params=_cparams(3), name="diff_attention",
    )(lam_params, u, u, u, cos, sin_lo, sin_hi, cos, sin_lo, sin_hi, gain)


def _merge_kernel(ret_ref, diff_ref, gr_ref, gd_ref, wr_ref, wd_ref, o_ref):
    a = _dot(ret_ref[...], wr_ref[...])
    b = _dot(diff_ref[...], wd_ref[...])
    merged = _sigmoid(gr_ref[...].astype(F32)) * a + _sigmoid(gd_ref[...].astype(F32)) * b
    o_ref[...] = merged.astype(o_ref.dtype)


def _merge(ret, diff, u, w_ret, w_diff):
    n, dm = ret.shape[0], w_ret.shape[1]
    tm = min(MERGE_TM, n)
    gate_off = (u.shape[1] - 2 * dm) // dm
    tile = lambda c: pl.BlockSpec((tm, dm), lambda i: (i, c))
    weight = lambda w: pl.BlockSpec(w.shape, lambda i: (0, 0))
    return pl.pallas_call(
        _merge_kernel, grid=(n // tm,),
        in_specs=[pl.BlockSpec((tm, ret.shape[1]), lambda i: (i, 0)),
                  pl.BlockSpec((tm, diff.shape[1]), lambda i: (i, 0)),
                  tile(gate_off), tile(gate_off + 1), weight(w_ret), weight(w_diff)],
        out_specs=tile(0), out_shape=jax.ShapeDtypeStruct((n, dm), BF16),
        compiler_params=_cparams(1), name="gated_merge",
    )(ret, diff, u, u, w_ret, w_diff)


def _layer_norm(z, gain, bias):
    mu = jnp.mean(z, axis=-1, keepdims=True)
    var = jnp.mean(jnp.square(z - mu), axis=-1, keepdims=True)
    return (z - mu) * lax.rsqrt(var + LN_EPS) * gain + bias


def _split_bf16(a):
    hi = a.astype(BF16)
    return hi, (a - hi.astype(F32)).astype(BF16)


def _route_kernel(m_ref, x_ref, wo_ref, gain_ref, bias_ref, wr_ref, br_ref, h_ref, r_ref, count_ref,
                  *, alpha):
    @pl.when(pl.program_id(0) == 0)
    def _():
        count_ref[...] = jnp.zeros_like(count_ref)

    w_hi, w_lo = _split_bf16(wr_ref[...])
    part = h_ref.shape[0] // ROUTE_SPLITS
    for g in range(ROUTE_SPLITS):
        rows = slice(g * part, (g + 1) * part)
        h = _layer_norm(alpha * x_ref[rows, :] + _dot(m_ref[rows, :], wo_ref[...]),
                        gain_ref[...], bias_ref[...])
        h_ref[rows, :] = h
        h_hi, h_lo = _split_bf16(h)
        logits = _dot(h_hi, w_hi) + (_dot(h_hi, w_lo) + _dot(h_lo, w_hi)) + br_ref[...]
        r_ref[rows, :] = _routing_rows(logits, count_ref)


def _routing_rows(logits, count_ref):
    lane = lax.broadcasted_iota(jnp.int32, logits.shape, 1).astype(F32)

    def first_argmax(vals):
        top = jnp.max(vals, axis=-1, keepdims=True)
        idx = jnp.min(jnp.where(vals == top, lane, float(LANES)), axis=-1, keepdims=True)
        return top, idx

    group_logits = jnp.where(lane < N_GROUPS, logits, NEG_BIG)
    g_top, g_idx = first_argmax(group_logits)
    p_group = 1.0 / jnp.sum(jnp.exp(group_logits - g_top), axis=-1, keepdims=True)

    first = N_GROUPS + g_idx * EXPERTS_PER_GROUP
    within = jnp.where((lane >= first) & (lane < first + EXPERTS_PER_GROUP), logits, NEG_BIG)
    v1, i1 = first_argmax(within)
    v2, i2 = first_argmax(jnp.where(lane == i1, NEG_BIG, within))
    e2 = jnp.exp(v2 - v1)
    w1 = p_group / (1.0 + e2)
    w2 = p_group * e2 / (1.0 + e2)
    x1, x2 = i1 - N_GROUPS, i2 - N_GROUPS

    hit1 = jnp.where(lane == x1, 1.0, 0.0)
    hit2 = jnp.where(lane == x2, 1.0, 0.0)
    hits = hit1 + hit2
    tm = hits.shape[0]
    earlier = (lax.broadcasted_iota(jnp.int32, (tm, tm), 0) > lax.broadcasted_iota(jnp.int32, (tm, tm), 1))
    before = count_ref[...] + _dot(jnp.where(earlier, 1.0, 0.0).astype(BF16), hits.astype(BF16))
    rank1 = jnp.sum(before * hit1, axis=-1, keepdims=True)
    rank2 = jnp.sum(before * hit2, axis=-1, keepdims=True)
    count_ref[...] += jnp.sum(hits, axis=0, keepdims=True)

    fields = (x1, x2, w1, w2, rank1, rank2)
    row = jnp.zeros_like(logits)
    for pos, value in enumerate(fields):
        row = jnp.where(lane == pos, value, row)
    return row


def _route(merged, x, w_out, gain, bias, w_router, b_router, alpha):
    n, dm = x.shape
    tm = min(ROUTE_TM, n)
    tile = pl.BlockSpec((tm, dm), lambda i: (i, 0))
    full = lambda a: pl.BlockSpec(a.shape, lambda i: (0, 0))
    return pl.pallas_call(
        functools.partial(_route_kernel, alpha=alpha), grid=(n // tm,),
        in_specs=[tile, tile, full(w_out), full(gain), full(bias), full(w_router), full(b_router)],
        out_specs=[tile, pl.BlockSpec((tm, LANES), lambda i: (i, 0)),
                   pl.BlockSpec((1, LANES), lambda i: (0, 0))],
        out_shape=[jax.ShapeDtypeStruct((n, dm), F32), jax.ShapeDtypeStruct((n, LANES), F32),
                   jax.ShapeDtypeStruct((1, LANES), F32)],
        compiler_params=_cparams(1), name="outproj_ln_router",
    )(merged, x, w_out, gain, bias, w_router, b_router)


def _row_copies(table_ref, first, n_rows, make_copy, n_queues):
    for g in range(n_rows // DMA_GROUP):
        entries = [table_ref[first + g * DMA_GROUP + k] for k in range(DMA_GROUP)]
        for k, entry in enumerate(entries):
            r = g * DMA_GROUP + k
            make_copy(entry, r).start(priority=r % n_queues)


def _for_buffer(dynamic_buf, fn):
    for buf in range(2):
        pl.when(dynamic_buf == buf)(functools.partial(fn, buf))


def _expert_kernel(tile_expert_ref, run_next_ref, src_ref, n_valid_ref, h_hbm, wg_hbm, wu_hbm, wd_hbm,
                   y_ref, xbuf, stage_g, stage_u, stage_d, wg_ref, wu_ref, wd_ref, gather_sem,
                   weight_sem):
    i = pl.program_id(0)
    n_valid = n_valid_ref[0]
    tm = xbuf.shape[1]
    slot = i & 1
    weights = ((wg_hbm, stage_g, wg_ref), (wu_hbm, stage_u, wu_ref), (wd_hbm, stage_d, wd_ref))

    def start_weights(expert):
        for k, (hbm, stage, _) in enumerate(weights):
            pltpu.make_async_copy(hbm.at[expert], stage, weight_sem.at[k]).start(priority=1)

    def take_weights():
        for k, (hbm, stage, dst) in enumerate(weights):
            pltpu.make_async_copy(hbm.at[0], stage, weight_sem.at[k]).wait()

            def round_rows(c, carry, stage=stage, dst=dst):
                rows = pl.ds(pl.multiple_of(c * WEIGHT_CAST_ROWS, WEIGHT_CAST_ROWS), WEIGHT_CAST_ROWS)
                dst[rows, :] = stage[rows, :].astype(BF16)
                return carry
            lax.fori_loop(0, stage.shape[0] // WEIGHT_CAST_ROWS, round_rows, 0)

    def start_gather(tile, dynamic_buf):
        def issue(buf):
            _row_copies(src_ref, tile * tm, tm, lambda tok, r: pltpu.make_async_copy(
                h_hbm.at[pl.ds(tok, 1)], xbuf.at[buf, pl.ds(r, 1)], gather_sem.at[buf]), n_queues=1)
        _for_buffer(dynamic_buf, issue)

    @pl.when(i == 0)
    def _():
        start_gather(0, 0)
        start_weights(tile_expert_ref[0])

    @pl.when(i + 1 < n_valid)
    def _():
        start_gather(i + 1, 1 - slot)

    @pl.when(i < n_valid)
    def _():
        run_next = run_next_ref[i]

        @pl.when(run_next != NOT_RUN_START)
        def _():
            take_weights()

            @pl.when(run_next != NO_NEXT_EXPERT)
            def _():
                start_weights(run_next)

        pltpu.make_async_copy(h_hbm.at[pl.ds(0, tm)], xbuf.at[slot], gather_sem.at[slot]).wait()
        x = xbuf[slot].astype(BF16)
        g = _dot(x, wg_ref[...])
        act = (g * _sigmoid(g) * _dot(x, wu_ref[...])).astype(BF16)
        y_ref[...] = _dot(act, wd_ref[...])

    @pl.when(i >= n_valid)
    def _():
        y_ref[...] = jnp.zeros_like(y_ref)


def _experts(h, tile_expert, run_next, src, n_valid, w_gate, w_up, w_down):
    n, dm = h.shape
    n_tiles = tile_expert.shape[0]
    tm = src.shape[0] // n_tiles
    ff = w_gate.shape[2]
    hbm = pl.BlockSpec(memory_space=pl.ANY)
    grid_spec = pltpu.PrefetchScalarGridSpec(
        num_scalar_prefetch=4, grid=(n_tiles,),
        in_specs=[hbm, hbm, hbm, hbm],
        out_specs=pl.BlockSpec((tm, dm), lambda i, *_: (i, 0)),
        scratch_shapes=[pltpu.VMEM((2, tm, dm), F32),
                        pltpu.VMEM((dm, ff), F32), pltpu.VMEM((dm, ff), F32), pltpu.VMEM((ff, dm), F32),
                        pltpu.VMEM((dm, ff), BF16), pltpu.VMEM((dm, ff), BF16), pltpu.VMEM((ff, dm), BF16),
                        pltpu.SemaphoreType.DMA((2,)), pltpu.SemaphoreType.DMA((3,))])
    return pl.pallas_call(
        _expert_kernel, grid_spec=grid_spec,
        out_shape=jax.ShapeDtypeStruct((src.shape[0], dm), F32),
        compiler_params=_cparams(1), name="experts",
    )(tile_expert, run_next, src, n_valid, h, w_gate, w_up, w_down)


def _dispatch_tables(expert_ids, ranks, counts, tm):
    n_assign = expert_ids.shape[0]
    n_tiles = n_assign // tm + N_EXPERTS
    n_slots = n_tiles * tm
    experts = jnp.arange(N_EXPERTS, dtype=jnp.int32)
    padded = (counts + tm - 1) // tm * tm
    pad_end = jnp.cumsum(padded)
    pad_start = pad_end - padded
    slot_of = jnp.sum(jnp.where(expert_ids[:, None] == experts[None, :], pad_start[None, :], 0), axis=1) + ranks
    tile_start = jnp.arange(n_tiles, dtype=jnp.int32) * tm
    tile_expert = jnp.minimum(
        jnp.sum((pad_end[None, :] <= tile_start[:, None]).astype(jnp.int32), axis=1), N_EXPERTS - 1)
    later = lax.cummin(jnp.where(counts > 0, experts, N_EXPERTS)[::-1])[::-1]
    following = jnp.concatenate([later[1:], jnp.full((1,), N_EXPERTS, jnp.int32)])[tile_expert]
    run_start = jnp.concatenate([jnp.ones((1,), bool), tile_expert[1:] != tile_expert[:-1]])
    run_next = jnp.where(run_start, jnp.where(following < N_EXPERTS, following, NO_NEXT_EXPERT),
                         NOT_RUN_START).astype(jnp.int32)
    token = jnp.arange(n_assign, dtype=jnp.int32) // TOP_K
    src = jnp.zeros((n_slots,), jnp.int32).at[slot_of].set(token)
    n_valid = (pad_end[-1] // tm).astype(jnp.int32).reshape(1)
    return slot_of, tile_expert, run_next, src, n_valid


def _final_kernel(slot_ref, h_ref, y_hbm, r_ref, gain_ref, bias_ref, o_ref, ybuf, sem, *, alpha):
    i = pl.program_id(0)
    tm = o_ref.shape[0]
    slot = i & 1

    def start_gather(step, dynamic_buf):
        def issue(buf):
            _row_copies(slot_ref, step * (TOP_K * tm), TOP_K * tm, lambda row, r: pltpu.make_async_copy(
                y_hbm.at[pl.ds(row, 1)], ybuf.at[buf, r % TOP_K, pl.ds(r // TOP_K, 1)], sem.at[buf]),
                n_queues=2)
        _for_buffer(dynamic_buf, issue)

    @pl.when(i == 0)
    def _():
        start_gather(0, 0)

    @pl.when(i + 1 < pl.num_programs(0))
    def _():
        start_gather(i + 1, 1 - slot)

    for k in range(TOP_K):
        pltpu.make_async_copy(y_hbm.at[pl.ds(0, tm)], ybuf.at[slot, k], sem.at[slot]).wait()
    r = r_ref[...]
    ffn = ybuf[slot, 0] * r[:, 2:3] + ybuf[slot, 1] * r[:, 3:4]
    o_ref[...] = _layer_norm(alpha * h_ref[...] + ffn, gain_ref[...], bias_ref[...])


def _final(h, y_slots, slot_of, routing, gain, bias, alpha):
    n, dm = h.shape
    tm = min(FINAL_TM, n)
    tile = pl.BlockSpec((tm, dm), lambda i, *_: (i, 0))
    full = lambda a: pl.BlockSpec(a.shape, lambda i, *_: (0, 0))
    grid_spec = pltpu.PrefetchScalarGridSpec(
        num_scalar_prefetch=1, grid=(n // tm,),
        in_specs=[tile, pl.BlockSpec(memory_space=pl.ANY), pl.BlockSpec((tm, LANES), lambda i, *_: (i, 0)),
                  full(gain), full(bias)],
        out_specs=tile,
        scratch_shapes=[pltpu.VMEM((2, TOP_K, tm, dm), F32), pltpu.SemaphoreType.DMA((2,))])
    return pl.pallas_call(
        functools.partial(_final_kernel, alpha=alpha), grid_spec=grid_spec,
        out_shape=jax.ShapeDtypeStruct((n, dm), F32),
        compiler_params=_cparams(1), name="combine_ln",
    )(slot_of, h, y_slots, routing, gain, bias)


def _rope_frequencies():
    ret_freq = 1.0 / (RET_THETA ** (jnp.arange(0, RET_QK_DIM, 2, dtype=F32) / RET_QK_DIM))
    diff_freq = 1.0 / (ROPE_THETA ** (jnp.arange(0, ROPE_DIM, 2, dtype=F32) / ROPE_DIM))
    pad = jnp.zeros((LANES - ret_freq.shape[0] - diff_freq.shape[0],), F32)
    return jnp.concatenate([ret_freq, diff_freq, pad]).reshape(1, LANES)


def kernel(x, positions, w_in, ret_gn_gain, diff_lambda_q1, diff_lambda_k1, diff_lambda_q2, diff_lambda_k2, diff_subln_gain, w_ret_proj, w_diff_proj, w_out, ln1_gain, ln1_bias, w_group, b_group, w_router, b_router, w_expert_gate, w_expert_up, w_expert_down, ln2_gain, ln2_bias):
    batch, seq, dm = x.shape
    n = batch * seq
    depth = w_in.shape[0]
    assert seq % CHUNK == 0
    alpha = (2 * depth) ** 0.25

    ret_cos, ret_sin, diff_cos, diff_sin_lo, diff_sin_hi = _rope_tables(
        positions.reshape(n, 1), _rope_frequencies())

    log_gamma = jnp.log(1.0 - 2.0 ** (-5.0 - jnp.arange(RET_HEADS, dtype=F32)))
    block_decay = jnp.exp(log_gamma * min(RET_BLOCK, seq))

    xt = x.reshape(n, dm)
    for l in range(depth):
        lambda_init = 0.8 - 0.6 * math.exp(-0.3 * l)
        u = _matmul(xt, w_in[l], BF16, "in_proj")
        ret = _retention(u, ret_cos, ret_sin, log_gamma, block_decay,
                         ret_gn_gain[l].reshape(RET_HEADS, 1, RET_V_DIM), batch, seq)
        lam_params = jnp.stack([diff_lambda_q1[l], diff_lambda_k1[l], diff_lambda_q2[l], diff_lambda_k2[l]])
        diff = _diff_attention(u, diff_cos, diff_sin_lo, diff_sin_hi, lam_params.astype(F32),
                               diff_subln_gain[l].reshape(1, DIFF_V_DIM), batch, seq, lambda_init)
        merged = _merge(ret, diff, u, w_ret_proj[l].astype(BF16), w_diff_proj[l].astype(BF16))

        w_route = jnp.concatenate(
            [w_group[l], w_router[l], jnp.zeros((dm, LANES - N_GROUPS - N_EXPERTS), F32)], axis=1)
        b_route = jnp.concatenate(
            [b_group[l], b_router[l], jnp.zeros((LANES - N_GROUPS - N_EXPERTS,), F32)]).reshape(1, LANES)
        h, routing, counts = _route(merged, xt, w_out[l].astype(BF16), ln1_gain[l].reshape(1, dm),
                                    ln1_bias[l].reshape(1, dm), w_route, b_route, alpha)

        expert_ids = routing[:, :TOP_K].astype(jnp.int32).reshape(-1)
        ranks = routing[:, 2 * TOP_K:3 * TOP_K].astype(jnp.int32).reshape(-1)
        slot_of, tile_expert, run_next, src, n_valid = _dispatch_tables(
            expert_ids, ranks, counts[0, :N_EXPERTS].astype(jnp.int32), min(MOE_TM, n))
        y_slots = _experts(h, tile_expert, run_next, src, n_valid, w_expert_gate[l], w_expert_up[l],
                           w_expert_down[l])
        xt = _final(h, y_slots, slot_of, routing,
                    ln2_gain[l].reshape(1, dm), ln2_bias[l].reshape(1, dm), alpha)
    return xt.reshape(batch, seq, dm)
```

```python
import functools
import math

import jax
import jax.numpy as jnp
from jax import lax
from jax.experimental import pallas as pl
from jax.experimental.pallas import tpu as pltpu

F32 = jnp.float32
BF16 = jnp.bfloat16

CHUNK = 64
RET_HEADS = 8
RET_QK_DIM = 128
RET_V_DIM = 256
RET_THETA = 10000.0
DIFF_HEADS = 8
DIFF_QK_DIM = 128
DIFF_V_DIM = 256
ROPE_THETA = 500000.0
ROPE_DIM = DIFF_QK_DIM // 4
N_GROUPS = 4
EXPERTS_PER_GROUP = 8
N_EXPERTS = N_GROUPS * EXPERTS_PER_GROUP
TOP_K = 2
LN_EPS = 1e-5

LANES = 128
BF16_ROWS = 16
NEG_BIG = -1e30

VMEM_LIMIT_BYTES = 56 * 1024 * 1024
TABLE_ROWS = 512
PROJ_TM = 1024
PROJ_TN = 1024
RET_BLOCK = 256
RET_HEAD_GROUP = 8
ATT_BLOCK = 1024
MERGE_TM = 256
ROUTE_TM = 512
ROUTE_SPLITS = 2
MOE_TM = 256
DMA_GROUP = 8
WEIGHT_CAST_ROWS = 256
NOT_RUN_START = -2
NO_NEXT_EXPERT = -1
FINAL_TM = 256


def _cparams(n_axes):
    return pltpu.CompilerParams(
        dimension_semantics=("arbitrary",) * n_axes, vmem_limit_bytes=VMEM_LIMIT_BYTES)


def _sigmoid(x):
    return 1.0 / (1.0 + jnp.exp(-x))


def _chunk_of(idx):
    return jnp.right_shift(idx, CHUNK.bit_length() - 1)


def _dot(a, b):
    return jnp.dot(a, b, preferred_element_type=F32)


def _dot_nt(a, b):
    return lax.dot_general(a, b, (((1,), (1,)), ((), ())), preferred_element_type=F32)


def _table_kernel(pos_ref, freq_ref, rc_ref, rs_ref, dc_ref, dlo_ref, dhi_ref):
    rh, dh = RET_QK_DIM // 2, ROPE_DIM // 2
    ang = pos_ref[...].astype(F32) * freq_ref[...]
    c, s = jnp.cos(ang), jnp.sin(ang)
    lane = lax.broadcasted_iota(jnp.int32, ang.shape, 1)
    rc_ref[...] = jnp.where(lane < rh, c, pltpu.roll(c, rh, 1))
    rs_ref[...] = jnp.where(lane < rh, -s, pltpu.roll(s, rh, 1))
    c_lo, s_lo = pltpu.roll(c, LANES - rh, 1), pltpu.roll(s, LANES - rh, 1)
    c_hi, s_hi = pltpu.roll(c, LANES - rh + dh, 1), pltpu.roll(s, LANES - rh + dh, 1)
    dc_ref[...] = jnp.where(lane < dh, c_lo, jnp.where(lane < 2 * dh, c_hi, 1.0))
    dlo_ref[...] = jnp.where(lane < dh, -s_lo, 0.0)
    dhi_ref[...] = jnp.where((lane >= dh) & (lane < 2 * dh), s_hi, 0.0)


def _rope_tables(pos_col, freq_row):
    n = pos_col.shape[0]
    tr = min(TABLE_ROWS, n)
    out = pl.BlockSpec((tr, LANES), lambda i: (i, 0))
    shp = jax.ShapeDtypeStruct((n, LANES), F32)
    return pl.pallas_call(
        _table_kernel, grid=(n // tr,),
        in_specs=[pl.BlockSpec((tr, 1), lambda i: (i, 0)), pl.BlockSpec((1, LANES), lambda i: (0, 0))],
        out_specs=[out] * 5, out_shape=[shp] * 5,
        compiler_params=_cparams(1), name="rope_tables",
    )(pos_col, freq_row)


def _matmul_kernel(x_ref, w_ref, o_ref, wb_ref):
    @pl.when(pl.program_id(1) == 0)
    def _():
        wb_ref[...] = w_ref[...].astype(BF16)

    o_ref[...] = _dot(x_ref[...].astype(BF16), wb_ref[...]).astype(o_ref.dtype)


def _matmul(x, w, out_dtype, name):
    m, k = x.shape
    n = w.shape[1]
    tm, tn = min(PROJ_TM, m), min(PROJ_TN, n)
    return pl.pallas_call(
        _matmul_kernel, grid=(n // tn, m // tm),
        in_specs=[pl.BlockSpec((tm, k), lambda j, i: (i, 0)),
                  pl.BlockSpec((k, tn), lambda j, i: (0, j))],
        out_specs=pl.BlockSpec((tm, tn), lambda j, i: (i, j)),
        out_shape=jax.ShapeDtypeStruct((m, n), out_dtype),
        scratch_shapes=[pltpu.VMEM((k, tn), BF16)],
        compiler_params=_cparams(2), name=name,
    )(x, w)


def _retention_kernel(lg_ref, bd_ref, q_ref, k_ref, v_ref, g_ref, cos_ref, sin_ref, gain_ref,
                      o_ref, state_ref, decay_ref, qdecay_ref, kdecay_ref):
    group = state_ref.shape[0]
    first_head = pl.program_id(1) * group
    t = q_ref.shape[0]
    dk, dv = RET_QK_DIM, RET_V_DIM

    @pl.when(pl.program_id(2) == 0)
    def _():
        state_ref[...] = jnp.zeros_like(state_ref)
        ri = lax.broadcasted_iota(jnp.int32, (t, t), 0)
        ci = lax.broadcasted_iota(jnp.int32, (t, t), 1)
        dist = jnp.abs(ri - ci).astype(F32)
        visible = _chunk_of(ri) >= _chunk_of(ci)
        row = lax.broadcasted_iota(jnp.int32, (t, dk), 0).astype(F32)
        for j in range(group):
            lg = lg_ref[first_head + j]
            decay_ref[j] = jnp.where(visible, jnp.exp(lg * dist), 0.0)
            qdecay_ref[j] = jnp.exp(lg * (row + 1.0))
            kdecay_ref[j] = jnp.exp(lg * (t - 1.0 - row))

    cos, sin = cos_ref[...], sin_ref[...]

    def rot(x):
        return x * cos + pltpu.roll(x, dk // 2, 1) * sin

    for j in range(group):
        q = rot(q_ref[:, j * dk:(j + 1) * dk].astype(F32))
        k = rot(k_ref[:, j * dk:(j + 1) * dk].astype(F32)) * (dk ** -0.5)
        v = v_ref[:, j * dv:(j + 1) * dv]

        s = _dot_nt(q.astype(BF16), k.astype(BF16)) * decay_ref[j]
        intra = _dot(s.astype(BF16), v)

        state = state_ref[j]
        cross = _dot((q * qdecay_ref[j]).astype(BF16), state.astype(BF16))
        kd = k * kdecay_ref[j]
        state_ref[j] = state * bd_ref[first_head + j] + _dot(kd.T.astype(BF16), v)

        out = intra + cross
        mu = jnp.mean(out, axis=-1, keepdims=True)
        var = jnp.mean(jnp.square(out - mu), axis=-1, keepdims=True)
        y = (out - mu) * lax.rsqrt(var + LN_EPS) * gain_ref[j]
        gate = g_ref[:, j * dv:(j + 1) * dv].astype(F32)
        o_ref[:, j * dv:(j + 1) * dv] = (y * (gate * _sigmoid(gate))).astype(o_ref.dtype)


def _retention(u, cos, sin, log_gamma, block_decay, gain, batch, seq):
    t = min(RET_BLOCK, seq)
    nb = seq // t
    grp = RET_HEAD_GROUP
    n_groups = RET_HEADS // grp
    rows = lambda b, h, i: b * nb + i
    qk_w, v_w = grp * RET_QK_DIM, grp * RET_V_DIM
    k_off, v_off, g_off = n_groups, n_groups, 2 * n_groups
    smem = pl.BlockSpec(memory_space=pltpu.SMEM)
    return pl.pallas_call(
        _retention_kernel, grid=(batch, n_groups, nb),
        in_specs=[
            smem, smem,
            pl.BlockSpec((t, qk_w), lambda b, h, i: (rows(b, h, i), h)),
            pl.BlockSpec((t, qk_w), lambda b, h, i: (rows(b, h, i), k_off + h)),
            pl.BlockSpec((t, v_w), lambda b, h, i: (rows(b, h, i), v_off + h)),
            pl.BlockSpec((t, v_w), lambda b, h, i: (rows(b, h, i), g_off + h)),
            pl.BlockSpec((t, LANES), lambda b, h, i: (rows(b, h, i), 0)),
            pl.BlockSpec((t, LANES), lambda b, h, i: (rows(b, h, i), 0)),
            pl.BlockSpec((grp, 1, RET_V_DIM), lambda b, h, i: (h, 0, 0)),
        ],
        out_specs=pl.BlockSpec((t, v_w), lambda b, h, i: (rows(b, h, i), h)),
        out_shape=jax.ShapeDtypeStruct((batch * seq, RET_HEADS * RET_V_DIM), BF16),
        scratch_shapes=[pltpu.VMEM((grp, RET_QK_DIM, RET_V_DIM), F32), pltpu.VMEM((grp, t, t), F32),
                        pltpu.VMEM((grp, t, RET_QK_DIM), F32), pltpu.VMEM((grp, t, RET_QK_DIM), F32)],
        compiler_params=_cparams(3), name="retention",
    )(log_gamma, block_decay, u, u, u, u, cos, sin, gain)


def _partial_rope(x, cos, sin_lo, sin_hi):
    half = ROPE_DIM // 2
    return (x * cos + pltpu.roll(x, DIFF_QK_DIM - half, 1) * sin_lo
            + pltpu.roll(x, half, 1) * sin_hi)


def _diff_attn_kernel(lam_ref, q_ref, k_ref, v_ref, qc_ref, qa_ref, qb_ref, kc_ref, ka_ref, kb_ref,
                      gain_ref, o_ref, krot_ref, vt_ref, m_ref, acc_ref, sa_ref, sb_ref, *,
                      lambda_init):
    qblk = pl.program_id(2)
    tq = q_ref.shape[0]
    d = DIFF_QK_DIM

    @pl.when(qblk == 0)
    def _():
        def prep_rows(c, carry):
            rows = pl.ds(pl.multiple_of(c * tq, tq), tq)
            for i in range(2):
                x = k_ref[rows, i * d:(i + 1) * d].astype(F32)
                krot_ref[rows, i * d:(i + 1) * d] = _partial_rope(
                    x, kc_ref[rows, :], ka_ref[rows, :], kb_ref[rows, :]).astype(BF16)
            vt_ref[c, :DIFF_V_DIM] = v_ref[rows, :].astype(F32).T.astype(BF16)
            vt_ref[c, DIFF_V_DIM:] = jnp.ones((BF16_ROWS, tq), BF16)
            return carry
        lax.fori_loop(0, k_ref.shape[0] // tq, prep_rows, 0)

    scale = d ** -0.5
    qs = []
    for i in range(2):
        x = q_ref[:, i * d:(i + 1) * d].astype(F32)
        qs.append((_partial_rope(x, qc_ref[...], qa_ref[...], qb_ref[...]) * scale).astype(BF16))

    m_ref[...] = jnp.full_like(m_ref, NEG_BIG)
    acc_ref[...] = jnp.zeros_like(acc_ref)

    s_bufs = (sa_ref, sb_ref)

    def scores(j, buf, diagonal):
        rows = pl.ds(pl.multiple_of(j * tq, tq), tq)
        for i in range(2):
            s = _dot_nt(krot_ref[rows, i * d:(i + 1) * d], qs[i])
            if diagonal:
                key = lax.broadcasted_iota(jnp.int32, s.shape, 0)
                qry = lax.broadcasted_iota(jnp.int32, s.shape, 1)
                s = jnp.where(_chunk_of(qry) >= _chunk_of(key), s, NEG_BIG)
            s_bufs[buf][i] = s

    def absorb(j, buf):
        vt = vt_ref[j]
        for i in range(2):
            s = s_bufs[buf][i]
            m_old = m_ref[i]
            m_new = jnp.maximum(m_old, jnp.max(s, axis=0, keepdims=True))
            alpha = jnp.exp(m_old - m_new)
            p = jnp.exp(s - m_new)
            acc_ref[i] = alpha * acc_ref[i] + _dot(vt, p.astype(BF16))
            m_ref[i] = m_new

    scores(qblk, 0, True)

    def pair(p, carry):
        pending = jnp.where(p == 0, qblk, 2 * p - 1)
        scores(2 * p, 1, False)
        absorb(pending, 0)
        scores(2 * p + 1, 0, False)
        absorb(2 * p, 1)
        return carry
    n_pairs = qblk // 2
    lax.fori_loop(0, n_pairs, pair, 0)
    pending = jnp.where(n_pairs == 0, qblk, 2 * n_pairs - 1)

    @pl.when(qblk % 2 == 1)
    def _():
        scores(qblk - 1, 1, False)
        absorb(pending, 0)
        absorb(qblk - 1, 1)

    @pl.when(qblk % 2 == 0)
    def _():
        absorb(pending, 0)

    lp = lam_ref[...]
    lam = (jnp.exp(jnp.sum(lp[0:1] * lp[1:2], axis=-1, keepdims=True))
           - jnp.exp(jnp.sum(lp[2:3] * lp[3:4], axis=-1, keepdims=True)) + lambda_init)
    dv = DIFF_V_DIM
    o = (acc_ref[0, :dv] / acc_ref[0, dv:dv + 1] - lam * (acc_ref[1, :dv] / acc_ref[1, dv:dv + 1])).T
    o = o * lax.rsqrt(jnp.mean(jnp.square(o), axis=-1, keepdims=True) + LN_EPS) * gain_ref[...]
    o_ref[...] = (o * (1.0 - lambda_init)).astype(o_ref.dtype)


def _diff_attention(u, cos, sin_lo, sin_hi, lam_params, gain, batch, seq, lambda_init):
    tq = min(ATT_BLOCK, seq)
    nb = seq // tq
    width = 2 * DIFF_QK_DIM
    ret_cols = 2 * RET_HEADS * RET_QK_DIM + 2 * RET_HEADS * RET_V_DIM
    q_off = ret_cols // width
    k_off = q_off + DIFF_HEADS
    v_off = k_off + DIFF_HEADS
    qrow = lambda b, h, i: b * nb + i
    qtab = pl.BlockSpec((tq, LANES), lambda b, h, i: (qrow(b, h, i), 0))
    ktab = pl.BlockSpec((seq, LANES), lambda b, h, i: (b, 0), pipeline_mode=pl.Buffered(1))
    return pl.pallas_call(
        functools.partial(_diff_attn_kernel, lambda_init=lambda_init),
        grid=(batch, DIFF_HEADS, nb),
        in_specs=[
            pl.BlockSpec((4, DIFF_QK_DIM), lambda b, h, i: (0, 0)),
            pl.BlockSpec((tq, width), lambda b, h, i: (qrow(b, h, i), q_off + h)),
            pl.BlockSpec((seq, width), lambda b, h, i: (b, k_off + h)),
            pl.BlockSpec((seq, DIFF_V_DIM), lambda b, h, i: (b, v_off + h)),
            qtab, qtab, qtab, ktab, ktab, ktab,
            pl.BlockSpec((1, DIFF_V_DIM), lambda b, h, i: (0, 0)),
        ],
        out_specs=pl.BlockSpec((tq, DIFF_V_DIM), lambda b, h, i: (qrow(b, h, i), h)),
        out_shape=jax.ShapeDtypeStruct((batch * seq, DIFF_HEADS * DIFF_V_DIM), BF16),
        scratch_shapes=[pltpu.VMEM((seq, width), BF16),
                        pltpu.VMEM((nb, DIFF_V_DIM + BF16_ROWS, tq), BF16),
                        pltpu.VMEM((2, 1, tq), F32),
                        pltpu.VMEM((2, DIFF_V_DIM + BF16_ROWS, tq), F32),
                        pltpu.VMEM((2, tq, tq), F32), pltpu.VMEM((2, tq, tq), F32)],
        compiler_params=_cparams(3), name="diff_attention",
    )(lam_params, u, u, u, cos, sin_lo, sin_hi, cos, sin_lo, sin_hi, gain)


def _merge_kernel(ret_ref, diff_ref, gr_ref, gd_ref, wr_ref, wd_ref, o_ref):
    a = _dot(ret_ref[...], wr_ref[...])
    b = _dot(diff_ref[...], wd_ref[...])
    merged = _sigmoid(gr_ref[...].astype(F32)) * a + _sigmoid(gd_ref[...].astype(F32)) * b
    o_ref[...] = merged.astype(o_ref.dtype)


def _merge(ret, diff, u, w_ret, w_diff):
    n, dm = ret.shape[0], w_ret.shape[1]
    tm = min(MERGE_TM, n)
    gate_off = (u.shape[1] - 2 * dm) // dm
    tile = lambda c: pl.BlockSpec((tm, dm), lambda i: (i, c))
    weight = lambda w: pl.BlockSpec(w.shape, lambda i: (0, 0))
    return pl.pallas_call(
        _merge_kernel, grid=(n // tm,),
        in_specs=[pl.BlockSpec((tm, ret.shape[1]), lambda i: (i, 0)),
                  pl.BlockSpec((tm, diff.shape[1]), lambda i: (i, 0)),
                  tile(gate_off), tile(gate_off + 1), weight(w_ret), weight(w_diff)],
        out_specs=tile(0), out_shape=jax.ShapeDtypeStruct((n, dm), BF16),
        compiler_params=_cparams(1), name="gated_merge",
    )(ret, diff, u, u, w_ret, w_diff)


def _layer_norm(z, gain, bias):
    mu = jnp.mean(z, axis=-1, keepdims=True)
    var = jnp.mean(jnp.square(z - mu), axis=-1, keepdims=True)
    return (z - mu) * lax.rsqrt(var + LN_EPS) * gain + bias


def _split_bf16(a):
    hi = a.astype(BF16)
    return hi, (a - hi.astype(F32)).astype(BF16)


def _route_kernel(m_ref, x_ref, wo_ref, gain_ref, bias_ref, wr_ref, br_ref, h_ref, r_ref, count_ref,
                  *, alpha):
    @pl.when(pl.program_id(0) == 0)
    def _():
        count_ref[...] = jnp.zeros_like(count_ref)

    w_hi, w_lo = _split_bf16(wr_ref[...])
    part = h_ref.shape[0] // ROUTE_SPLITS
    for g in range(ROUTE_SPLITS):
        rows = slice(g * part, (g + 1) * part)
        h = _layer_norm(alpha * x_ref[rows, :] + _dot(m_ref[rows, :], wo_ref[...]),
                        gain_ref[...], bias_ref[...])
        h_ref[rows, :] = h
        h_hi, h_lo = _split_bf16(h)
        logits = _dot(h_hi, w_hi) + (_dot(h_hi, w_lo) + _dot(h_lo, w_hi)) + br_ref[...]
        r_ref[rows, :] = _routing_rows(logits, count_ref)


def _routing_rows(logits, count_ref):
    lane = lax.broadcasted_iota(jnp.int32, logits.shape, 1).astype(F32)

    def first_argmax(vals):
        top = jnp.max(vals, axis=-1, keepdims=True)
        idx = jnp.min(jnp.where(vals == top, lane, float(LANES)), axis=-1, keepdims=True)
        return top, idx

    group_logits = jnp.where(lane < N_GROUPS, logits, NEG_BIG)
    g_top, g_idx = first_argmax(group_logits)
    p_group = 1.0 / jnp.sum(jnp.exp(group_logits - g_top), axis=-1, keepdims=True)

    first = N_GROUPS + g_idx * EXPERTS_PER_GROUP
    within = jnp.where((lane >= first) & (lane < first + EXPERTS_PER_GROUP), logits, NEG_BIG)
    v1, i1 = first_argmax(within)
    v2, i2 = first_argmax(jnp.where(lane == i1, NEG_BIG, within))
    e2 = jnp.exp(v2 - v1)
    w1 = p_group / (1.0 + e2)
    w2 = p_group * e2 / (1.0 + e2)
    x1, x2 = i1 - N_GROUPS, i2 - N_GROUPS

    hit1 = jnp.where(lane == x1, 1.0, 0.0)
    hit2 = jnp.where(lane == x2, 1.0, 0.0)
    hits = hit1 + hit2
    tm = hits.shape[0]
    earlier = (lax.broadcasted_iota(jnp.int32, (tm, tm), 0) > lax.broadcasted_iota(jnp.int32, (tm, tm), 1))
    before = count_ref[...] + _dot(jnp.where(earlier, 1.0, 0.0).astype(BF16), hits.astype(BF16))
    rank1 = jnp.sum(before * hit1, axis=-1, keepdims=True)
    rank2 = jnp.sum(before * hit2, axis=-1, keepdims=True)
    count_ref[...] += jnp.sum(hits, axis=0, keepdims=True)

    fields = (x1, x2, w1, w2, rank1, rank2)
    row = jnp.zeros_like(logits)
    for pos, value in enumerate(fields):
        row = jnp.where(lane == pos, value, row)
    return row


def _route(merged, x, w_out, gain, bias, w_router, b_router, alpha):
    n, dm = x.shape
    tm = min(ROUTE_TM, n)
    tile = pl.BlockSpec((tm, dm), lambda i: (i, 0))
    full = lambda a: pl.BlockSpec(a.shape, lambda i: (0, 0))
    return pl.pallas_call(
        functools.partial(_route_kernel, alpha=alpha), grid=(n // tm,),
        in_specs=[tile, tile, full(w_out), full(gain), full(bias), full(w_router), full(b_router)],
        out_specs=[tile, pl.BlockSpec((tm, LANES), lambda i: (i, 0)),
                   pl.BlockSpec((1, LANES), lambda i: (0, 0))],
        out_shape=[jax.ShapeDtypeStruct((n, dm), F32), jax.ShapeDtypeStruct((n, LANES), F32),
                   jax.ShapeDtypeStruct((1, LANES), F32)],
        compiler_params=_cparams(1), name="outproj_ln_router",
    )(merged, x, w_out, gain, bias, w_router, b_router)


def _row_copies(table_ref, first, n_rows, make_copy, n_queues):
    for g in range(n_rows // DMA_GROUP):
        entries = [table_ref[first + g * DMA_GROUP + k] for k in range(DMA_GROUP)]
        for k, entry in enumerate(entries):
            r = g * DMA_GROUP + k
            make_copy(entry, r).start(priority=r % n_queues)


def _for_buffer(dynamic_buf, fn):
    for buf in range(2):
        pl.when(dynamic_buf == buf)(functools.partial(fn, buf))


def _expert_kernel(tile_expert_ref, run_next_ref, src_ref, n_valid_ref, h_hbm, wg_hbm, wu_hbm, wd_hbm,
                   y_ref, xbuf, stage_g, stage_u, stage_d, wg_ref, wu_ref, wd_ref, gather_sem,
                   weight_sem):
    i = pl.program_id(0)
    n_valid = n_valid_ref[0]
    tm = xbuf.shape[1]
    slot = i & 1
    weights = ((wg_hbm, stage_g, wg_ref), (wu_hbm, stage_u, wu_ref), (wd_hbm, stage_d, wd_ref))

    def start_weights(expert):
        for k, (hbm, stage, _) in enumerate(weights):
            pltpu.make_async_copy(hbm.at[expert], stage, weight_sem.at[k]).start(priority=1)

    def take_weights():
        for k, (hbm, stage, dst) in enumerate(weights):
            pltpu.make_async_copy(hbm.at[0], stage, weight_sem.at[k]).wait()

            def round_rows(c, carry, stage=stage, dst=dst):
                rows = pl.ds(pl.multiple_of(c * WEIGHT_CAST_ROWS, WEIGHT_CAST_ROWS), WEIGHT_CAST_ROWS)
                dst[rows, :] = stage[rows, :].astype(BF16)
                return carry
            lax.fori_loop(0, stage.shape[0] // WEIGHT_CAST_ROWS, round_rows, 0)

    def start_gather(tile, dynamic_buf):
        def issue(buf):
            _row_copies(src_ref, tile * tm, tm, lambda tok, r: pltpu.make_async_copy(
                h_hbm.at[pl.ds(tok, 1)], xbuf.at[buf, pl.ds(r, 1)], gather_sem.at[buf]), n_queues=2)
        _for_buffer(dynamic_buf, issue)

    @pl.when(i == 0)
    def _():
        start_gather(0, 0)
        start_weights(tile_expert_ref[0])

    @pl.when(i + 1 < n_valid)
    def _():
        start_gather(i + 1, 1 - slot)

    @pl.when(i < n_valid)
    def _():
        run_next = run_next_ref[i]

        @pl.when(run_next != NOT_RUN_START)
        def _():
            take_weights()

            @pl.when(run_next != NO_NEXT_EXPERT)
            def _():
                start_weights(run_next)

        pltpu.make_async_copy(h_hbm.at[pl.ds(0, tm)], xbuf.at[slot], gather_sem.at[slot]).wait()
        x = xbuf[slot].astype(BF16)
        g = _dot(x, wg_ref[...])
        act = (g * _sigmoid(g) * _dot(x, wu_ref[...])).astype(BF16)
        y_ref[...] = _dot(act, wd_ref[...])

    @pl.when(i >= n_valid)
    def _():
        y_ref[...] = jnp.zeros_like(y_ref)


def _experts(h, tile_expert, run_next, src, n_valid, w_gate, w_up, w_down):
    n, dm = h.shape
    n_tiles = tile_expert.shape[0]
    tm = src.shape[0] // n_tiles
    ff = w_gate.shape[2]
    hbm = pl.BlockSpec(memory_space=pl.ANY)
    grid_spec = pltpu.PrefetchScalarGridSpec(
        num_scalar_prefetch=4, grid=(n_tiles,),
        in_specs=[hbm, hbm, hbm, hbm],
        out_specs=pl.BlockSpec((tm, dm), lambda i, *_: (i, 0)),
        scratch_shapes=[pltpu.VMEM((2, tm, dm), F32),
                        pltpu.VMEM((dm, ff), F32), pltpu.VMEM((dm, ff), F32), pltpu.VMEM((ff, dm), F32),
                        pltpu.VMEM((dm, ff), BF16), pltpu.VMEM((dm, ff), BF16), pltpu.VMEM((ff, dm), BF16),
                        pltpu.SemaphoreType.DMA((2,)), pltpu.SemaphoreType.DMA((3,))])
    return pl.pallas_call(
        _expert_kernel, grid_spec=grid_spec,
        out_shape=jax.ShapeDtypeStruct((src.shape[0], dm), F32),
        compiler_params=_cparams(1), name="experts",
    )(tile_expert, run_next, src, n_valid, h, w_gate, w_up, w_down)


def _dispatch_tables(expert_ids, ranks, counts, tm):
    n_assign = expert_ids.shape[0]
    n_tiles = n_assign // tm + N_EXPERTS
    n_slots = n_tiles * tm
    experts = jnp.arange(N_EXPERTS, dtype=jnp.int32)
    padded = (counts + tm - 1) // tm * tm
    pad_end = jnp.cumsum(padded)
    pad_start = pad_end - padded
    slot_of = jnp.sum(jnp.where(expert_ids[:, None] == experts[None, :], pad_start[None, :], 0), axis=1) + ranks
    tile_start = jnp.arange(n_tiles, dtype=jnp.int32) * tm
    tile_expert = jnp.minimum(
        jnp.sum((pad_end[None, :] <= tile_start[:, None]).astype(jnp.int32), axis=1), N_EXPERTS - 1)
    later = lax.cummin(jnp.where(counts > 0, experts, N_EXPERTS)[::-1])[::-1]
    following = jnp.concatenate([later[1:], jnp.full((1,), N_EXPERTS, jnp.int32)])[tile_expert]
    run_start = jnp.concatenate([jnp.ones((1,), bool), tile_expert[1:] != tile_expert[:-1]])
    run_next = jnp.where(run_start, jnp.where(following < N_EXPERTS, following, NO_NEXT_EXPERT),
                         NOT_RUN_START).astype(jnp.int32)
    token = jnp.arange(n_assign, dtype=jnp.int32) // TOP_K
    src = jnp.zeros((n_slots,), jnp.int32).at[slot_of].set(token)
    n_valid = (pad_end[-1] // tm).astype(jnp.int32).reshape(1)
    return slot_of, tile_expert, run_next, src, n_valid


def _final_kernel(slot_ref, h_ref, y_hbm, r_ref, gain_ref, bias_ref, o_ref, ybuf, sem, *, alpha):
    i = pl.program_id(0)
    tm = o_ref.shape[0]
    slot = i & 1

    def start_gather(step, dynamic_buf):
        def issue(buf):
            _row_copies(slot_ref, step * (TOP_K * tm), TOP_K * tm, lambda row, r: pltpu.make_async_copy(
                y_hbm.at[pl.ds(row, 1)], ybuf.at[buf, r % TOP_K, pl.ds(r // TOP_K, 1)], sem.at[buf]),
                n_queues=2)
        _for_buffer(dynamic_buf, issue)

    @pl.when(i == 0)
    def _():
        start_gather(0, 0)

    @pl.when(i + 1 < pl.num_programs(0))
    def _():
        start_gather(i + 1, 1 - slot)

    for k in range(TOP_K):
        pltpu.make_async_copy(y_hbm.at[pl.ds(0, tm)], ybuf.at[slot, k], sem.at[slot]).wait()
    r = r_ref[...]
    ffn = ybuf[slot, 0] * r[:, 2:3] + ybuf[slot, 1] * r[:, 3:4]
    o_ref[...] = _layer_norm(alpha * h_ref[...] + ffn, gain_ref[...], bias_ref[...])


def _final(h, y_slots, slot_of, routing, gain, bias, alpha):
    n, dm = h.shape
    tm = min(FINAL_TM, n)
    tile = pl.BlockSpec((tm, dm), lambda i, *_: (i, 0))
    full = lambda a: pl.BlockSpec(a.shape, lambda i, *_: (0, 0))
    grid_spec = pltpu.PrefetchScalarGridSpec(
        num_scalar_prefetch=1, grid=(n // tm,),
        in_specs=[tile, pl.BlockSpec(memory_space=pl.ANY), pl.BlockSpec((tm, LANES), lambda i, *_: (i, 0)),
                  full(gain), full(bias)],
        out_specs=tile,
        scratch_shapes=[pltpu.VMEM((2, TOP_K, tm, dm), F32), pltpu.SemaphoreType.DMA((2,))])
    return pl.pallas_call(
        functools.partial(_final_kernel, alpha=alpha), grid_spec=grid_spec,
        out_shape=jax.ShapeDtypeStruct((n, dm), F32),
        compiler_params=_cparams(1), name="combine_ln",
    )(slot_of, h, y_slots, routing, gain, bias)


def _rope_frequencies():
    ret_freq = 1.0 / (RET_THETA ** (jnp.arange(0, RET_QK_DIM, 2, dtype=F32) / RET_QK_DIM))
    diff_freq = 1.0 / (ROPE_THETA ** (jnp.arange(0, ROPE_DIM, 2, dtype=F32) / ROPE_DIM))
    pad = jnp.zeros((LANES - ret_freq.shape[0] - diff_freq.shape[0],), F32)
    return jnp.concatenate([ret_freq, diff_freq, pad]).reshape(1, LANES)


def kernel(x, positions, w_in, ret_gn_gain, diff_lambda_q1, diff_lambda_k1, diff_lambda_q2, diff_lambda_k2, diff_subln_gain, w_ret_proj, w_diff_proj, w_out, ln1_gain, ln1_bias, w_group, b_group, w_router, b_router, w_expert_gate, w_expert_up, w_expert_down, ln2_gain, ln2_bias):
    batch, seq, dm = x.shape
    n = batch * seq
    depth = w_in.shape[0]
    assert seq % CHUNK == 0
    alpha = (2 * depth) ** 0.25

    ret_cos, ret_sin, diff_cos, diff_sin_lo, diff_sin_hi = _rope_tables(
        positions.reshape(n, 1), _rope_frequencies())

    log_gamma = jnp.log(1.0 - 2.0 ** (-5.0 - jnp.arange(RET_HEADS, dtype=F32)))
    block_decay = jnp.exp(log_gamma * min(RET_BLOCK, seq))

    xt = x.reshape(n, dm)
    for l in range(depth):
        lambda_init = 0.8 - 0.6 * math.exp(-0.3 * l)
        u = _matmul(xt, w_in[l], BF16, "in_proj")
        ret = _retention(u, ret_cos, ret_sin, log_gamma, block_decay,
                         ret_gn_gain[l].reshape(RET_HEADS, 1, RET_V_DIM), batch, seq)
        lam_params = jnp.stack([diff_lambda_q1[l], diff_lambda_k1[l], diff_lambda_q2[l], diff_lambda_k2[l]])
        diff = _diff_attention(u, diff_cos, diff_sin_lo, diff_sin_hi, lam_params.astype(F32),
                               diff_subln_gain[l].reshape(1, DIFF_V_DIM), batch, seq, lambda_init)
        merged = _merge(ret, diff, u, w_ret_proj[l].astype(BF16), w_diff_proj[l].astype(BF16))

        w_route = jnp.concatenate(
            [w_group[l], w_router[l], jnp.zeros((dm, LANES - N_GROUPS - N_EXPERTS), F32)], axis=1)
        b_route = jnp.concatenate(
            [b_group[l], b_router[l], jnp.zeros((LANES - N_GROUPS - N_EXPERTS,), F32)]).reshape(1, LANES)
        h, routing, counts = _route(merged, xt, w_out[l].astype(BF16), ln1_gain[l].reshape(1, dm),
                                    ln1_bias[l].reshape(1, dm), w_route, b_route, alpha)

        expert_ids = routing[:, :TOP_K].astype(jnp.int32).reshape(-1)
        ranks = routing[:, 2 * TOP_K:3 * TOP_K].astype(jnp.int32).reshape(-1)
        slot_of, tile_expert, run_next, src, n_valid = _dispatch_tables(
            expert_ids, ranks, counts[0, :N_EXPERTS].astype(jnp.int32), min(MOE_TM, n))
        y_slots = _experts(h, tile_expert, run_next, src, n_valid, w_expert_gate[l], w_expert_up[l],
                           w_expert_down[l])
        xt = _final(h, y_slots, slot_of, routing,
                    ln2_gain[l].reshape(1, dm), ln2_bias[l].reshape(1, dm), alpha)
    return xt.reshape(batch, seq, dm)
```

```python
import functools
import math

import jax
import jax.numpy as jnp
from jax import lax
from jax.experimental import pallas as pl
from jax.experimental.pallas import tpu as pltpu

F32 = jnp.float32
BF16 = jnp.bfloat16

CHUNK = 64
RET_HEADS = 8
RET_QK_DIM = 128
RET_V_DIM = 256
RET_THETA = 10000.0
DIFF_HEADS = 8
DIFF_QK_DIM = 128
DIFF_V_DIM = 256
ROPE_THETA = 500000.0
ROPE_DIM = DIFF_QK_DIM // 4
N_GROUPS = 4
EXPERTS_PER_GROUP = 8
N_EXPERTS = N_GROUPS * EXPERTS_PER_GROUP
TOP_K = 2
LN_EPS = 1e-5

LANES = 128
BF16_ROWS = 16
NEG_BIG = -1e30

VMEM_LIMIT_BYTES = 56 * 1024 * 1024
TABLE_ROWS = 512
PROJ_TM = 1024
PROJ_TN = 1024
RET_BLOCK = 256
RET_HEAD_GROUP = 8
ATT_BLOCK = 1024
MERGE_TM = 256
ROUTE_TM = 512
ROUTE_SPLITS = 2
MOE_TM = 256
DMA_GROUP = 8
WEIGHT_CAST_ROWS = 256
NOT_RUN_START = -2
NO_NEXT_EXPERT = -1
FINAL_TM = 256


def _cparams(n_axes):
    return pltpu.CompilerParams(
        dimension_semantics=("arbitrary",) * n_axes, vmem_limit_bytes=VMEM_LIMIT_BYTES)


def _sigmoid(x):
    return 1.0 / (1.0 + jnp.exp(-x))


def _chunk_of(idx):
    return jnp.right_shift(idx, CHUNK.bit_length() - 1)


def _dot(a, b):
    return jnp.dot(a, b, preferred_element_type=F32)


def _dot_nt(a, b):
    return lax.dot_general(a, b, (((1,), (1,)), ((), ())), preferred_element_type=F32)


def _table_kernel(pos_ref, freq_ref, rc_ref, rs_ref, dc_ref, dlo_ref, dhi_ref):
    rh, dh = RET_QK_DIM // 2, ROPE_DIM // 2
    ang = pos_ref[...].astype(F32) * freq_ref[...]
    c, s = jnp.cos(ang), jnp.sin(ang)
    lane = lax.broadcasted_iota(jnp.int32, ang.shape, 1)
    rc_ref[...] = jnp.where(lane < rh, c, pltpu.roll(c, rh, 1))
    rs_ref[...] = jnp.where(lane < rh, -s, pltpu.roll(s, rh, 1))
    c_lo, s_lo = pltpu.roll(c, LANES - rh, 1), pltpu.roll(s, LANES - rh, 1)
    c_hi, s_hi = pltpu.roll(c, LANES - rh + dh, 1), pltpu.roll(s, LANES - rh + dh, 1)
    dc_ref[...] = jnp.where(lane < dh, c_lo, jnp.where(lane < 2 * dh, c_hi, 1.0))
    dlo_ref[...] = jnp.where(lane < dh, -s_lo, 0.0)
    dhi_ref[...] = jnp.where((lane >= dh) & (lane < 2 * dh), s_hi, 0.0)


def _rope_tables(pos_col, freq_row):
    n = pos_col.shape[0]
    tr = min(TABLE_ROWS, n)
    out = pl.BlockSpec((tr, LANES), lambda i: (i, 0))
    shp = jax.ShapeDtypeStruct((n, LANES), F32)
    return pl.pallas_call(
        _table_kernel, grid=(n // tr,),
        in_specs=[pl.BlockSpec((tr, 1), lambda i: (i, 0)), pl.BlockSpec((1, LANES), lambda i: (0, 0))],
        out_specs=[out] * 5, out_shape=[shp] * 5,
        compiler_params=_cparams(1), name="rope_tables",
    )(pos_col, freq_row)


def _matmul_kernel(x_ref, w_ref, o_ref, wb_ref):
    @pl.when(pl.program_id(1) == 0)
    def _():
        wb_ref[...] = w_ref[...].astype(BF16)

    o_ref[...] = _dot(x_ref[...].astype(BF16), wb_ref[...]).astype(o_ref.dtype)


def _matmul(x, w, out_dtype, name):
    m, k = x.shape
    n = w.shape[1]
    tm, tn = min(PROJ_TM, m), min(PROJ_TN, n)
    return pl.pallas_call(
        _matmul_kernel, grid=(n // tn, m // tm),
        in_specs=[pl.BlockSpec((tm, k), lambda j, i: (i, 0)),
                  pl.BlockSpec((k, tn), lambda j, i: (0, j))],
        out_specs=pl.BlockSpec((tm, tn), lambda j, i: (i, j)),
        out_shape=jax.ShapeDtypeStruct((m, n), out_dtype),
        scratch_shapes=[pltpu.VMEM((k, tn), BF16)],
        compiler_params=_cparams(2), name=name,
    )(x, w)


def _retention_kernel(lg_ref, bd_ref, q_ref, k_ref, v_ref, g_ref, cos_ref, sin_ref, gain_ref,
                      o_ref, state_ref, decay_ref, qdecay_ref, kdecay_ref):
    group = state_ref.shape[0]
    first_head = pl.program_id(1) * group
    t = q_ref.shape[0]
    dk, dv = RET_QK_DIM, RET_V_DIM

    @pl.when(pl.program_id(2) == 0)
    def _():
        state_ref[...] = jnp.zeros_like(state_ref)
        ri = lax.broadcasted_iota(jnp.int32, (t, t), 0)
        ci = lax.broadcasted_iota(jnp.int32, (t, t), 1)
        dist = jnp.abs(ri - ci).astype(F32)
        visible = _chunk_of(ri) >= _chunk_of(ci)
        row = lax.broadcasted_iota(jnp.int32, (t, dk), 0).astype(F32)
        for j in range(group):
            lg = lg_ref[first_head + j]
            decay_ref[j] = jnp.where(visible, jnp.exp(lg * dist), 0.0)
            qdecay_ref[j] = jnp.exp(lg * (row + 1.0))
            kdecay_ref[j] = jnp.exp(lg * (t - 1.0 - row))

    cos, sin = cos_ref[...], sin_ref[...]

    def rot(x):
        return x * cos + pltpu.roll(x, dk // 2, 1) * sin

    for j in range(group):
        q = rot(q_ref[:, j * dk:(j + 1) * dk].astype(F32))
        k = rot(k_ref[:, j * dk:(j + 1) * dk].astype(F32)) * (dk ** -0.5)
        v = v_ref[:, j * dv:(j + 1) * dv]

        s = _dot_nt(q.astype(BF16), k.astype(BF16)) * decay_ref[j]
        intra = _dot(s.astype(BF16), v)

        state = state_ref[j]
        cross = _dot((q * qdecay_ref[j]).astype(BF16), state.astype(BF16))
        kd = k * kdecay_ref[j]
        state_ref[j] = state * bd_ref[first_head + j] + _dot(kd.T.astype(BF16), v)

        out = intra + cross
        mu = jnp.mean(out, axis=-1, keepdims=True)
        var = jnp.mean(jnp.square(out - mu), axis=-1, keepdims=True)
        y = (out - mu) * lax.rsqrt(var + LN_EPS) * gain_ref[j]
        gate = g_ref[:, j * dv:(j + 1) * dv].astype(F32)
        o_ref[:, j * dv:(j + 1) * dv] = (y * (gate * _sigmoid(gate))).astype(o_ref.dtype)


def _retention(u, cos, sin, log_gamma, block_decay, gain, batch, seq):
    t = min(RET_BLOCK, seq)
    nb = seq // t
    grp = RET_HEAD_GROUP
    n_groups = RET_HEADS // grp
    rows = lambda b, h, i: b * nb + i
    qk_w, v_w = grp * RET_QK_DIM, grp * RET_V_DIM
    k_off, v_off, g_off = n_groups, n_groups, 2 * n_groups
    smem = pl.BlockSpec(memory_space=pltpu.SMEM)
    return pl.pallas_call(
        _retention_kernel, grid=(batch, n_groups, nb),
        in_specs=[
            smem, smem,
            pl.BlockSpec((t, qk_w), lambda b, h, i: (rows(b, h, i), h)),
            pl.BlockSpec((t, qk_w), lambda b, h, i: (rows(b, h, i), k_off + h)),
            pl.BlockSpec((t, v_w), lambda b, h, i: (rows(b, h, i), v_off + h)),
            pl.BlockSpec((t, v_w), lambda b, h, i: (rows(b, h, i), g_off + h)),
            pl.BlockSpec((t, LANES), lambda b, h, i: (rows(b, h, i), 0)),
            pl.BlockSpec((t, LANES), lambda b, h, i: (rows(b, h, i), 0)),
            pl.BlockSpec((grp, 1, RET_V_DIM), lambda b, h, i: (h, 0, 0)),
        ],
        out_specs=pl.BlockSpec((t, v_w), lambda b, h, i: (rows(b, h, i), h)),
        out_shape=jax.ShapeDtypeStruct((batch * seq, RET_HEADS * RET_V_DIM), BF16),
        scratch_shapes=[pltpu.VMEM((grp, RET_QK_DIM, RET_V_DIM), F32), pltpu.VMEM((grp, t, t), F32),
                        pltpu.VMEM((grp, t, RET_QK_DIM), F32), pltpu.VMEM((grp, t, RET_QK_DIM), F32)],
        compiler_params=_cparams(3), name="retention",
    )(log_gamma, block_decay, u, u, u, u, cos, sin, gain)


def _partial_rope(x, cos, sin_lo, sin_hi):
    half = ROPE_DIM // 2
    return (x * cos + pltpu.roll(x, DIFF_QK_DIM - half, 1) * sin_lo
            + pltpu.roll(x, half, 1) * sin_hi)


def _diff_attn_kernel(lam_ref, q_ref, k_ref, v_ref, qc_ref, qa_ref, qb_ref, kc_ref, ka_ref, kb_ref,
                      gain_ref, o_ref, krot_ref, vt_ref, m_ref, acc_ref, sa_ref, sb_ref, *,
                      lambda_init):
    qblk = pl.program_id(2)
    tq = q_ref.shape[0]
    d = DIFF_QK_DIM

    def key_rows(j):
        return pl.ds(pl.multiple_of(j * tq, tq), tq)

    @pl.when(qblk == 0)
    def _():
        def prep_rows(c, carry):
            rows = key_rows(c)
            for i in range(2):
                x = k_ref[rows, i * d:(i + 1) * d].astype(F32)
                krot_ref[rows, i * d:(i + 1) * d] = _partial_rope(
                    x, kc_ref[rows, :], ka_ref[rows, :], kb_ref[rows, :]).astype(BF16)
            vt_ref[c, :DIFF_V_DIM] = v_ref[rows, :].astype(F32).T.astype(BF16)
            vt_ref[c, DIFF_V_DIM:] = jnp.ones((BF16_ROWS, tq), BF16)
            return carry
        lax.fori_loop(0, k_ref.shape[0] // tq, prep_rows, 0)

    scale = d ** -0.5
    qs = []
    for i in range(2):
        x = q_ref[:, i * d:(i + 1) * d].astype(F32)
        qs.append((_partial_rope(x, qc_ref[...], qa_ref[...], qb_ref[...]) * scale).astype(BF16))

    m_ref[...] = jnp.full_like(m_ref, NEG_BIG)
    acc_ref[...] = jnp.zeros_like(acc_ref)

    s_bufs = (sa_ref, sb_ref)

    def scores(j, buf, diagonal):
        for i in range(2):
            s = _dot_nt(krot_ref[key_rows(j), i * d:(i + 1) * d], qs[i])
            if diagonal:
                key = lax.broadcasted_iota(jnp.int32, s.shape, 0)
                qry = lax.broadcasted_iota(jnp.int32, s.shape, 1)
                s = jnp.where(_chunk_of(qry) >= _chunk_of(key), s, NEG_BIG)
            s_bufs[buf][i] = s

    def absorb(j, buf):
        vt = vt_ref[j]
        for i in range(2):
            s = s_bufs[buf][i]
            m_old = m_ref[i]
            m_new = jnp.maximum(m_old, jnp.max(s, axis=0, keepdims=True))
            alpha = jnp.exp(m_old - m_new)
            p = jnp.exp(s - m_new)
            acc_ref[i] = alpha * acc_ref[i] + _dot(vt, p.astype(BF16))
            m_ref[i] = m_new

    scores(qblk, 0, True)

    def pair(p, carry):
        pending = jnp.where(p == 0, qblk, 2 * p - 1)
        scores(2 * p, 1, False)
        absorb(pending, 0)
        scores(2 * p + 1, 0, False)
        absorb(2 * p, 1)
        return carry
    n_pairs = qblk // 2
    lax.fori_loop(0, n_pairs, pair, 0)
    pending = jnp.where(n_pairs == 0, qblk, 2 * n_pairs - 1)

    @pl.when(qblk % 2 == 1)
    def _():
        scores(qblk - 1, 1, False)
        absorb(pending, 0)
        absorb(qblk - 1, 1)

    @pl.when(qblk % 2 == 0)
    def _():
        absorb(pending, 0)

    lp = lam_ref[...]
    lam = (jnp.exp(jnp.sum(lp[0:1] * lp[1:2], axis=-1, keepdims=True))
           - jnp.exp(jnp.sum(lp[2:3] * lp[3:4], axis=-1, keepdims=True)) + lambda_init)
    dv = DIFF_V_DIM
    o = (acc_ref[0, :dv] / acc_ref[0, dv:dv + 1] - lam * (acc_ref[1, :dv] / acc_ref[1, dv:dv + 1])).T
    o = o * lax.rsqrt(jnp.mean(jnp.square(o), axis=-1, keepdims=True) + LN_EPS) * gain_ref[...]
    o_ref[...] = (o * (1.0 - lambda_init)).astype(o_ref.dtype)


def _diff_attention(u, cos, sin_lo, sin_hi, lam_params, gain, batch, seq, lambda_init):
    tq = min(ATT_BLOCK, seq)
    nb = seq // tq
    width = 2 * DIFF_QK_DIM
    ret_cols = 2 * RET_HEADS * RET_QK_DIM + 2 * RET_HEADS * RET_V_DIM
    q_off = ret_cols // width
    k_off = q_off + DIFF_HEADS
    v_off = k_off + DIFF_HEADS
    qrow = lambda b, h, i: b * nb + i
    qtab = pl.BlockSpec((tq, LANES), lambda b, h, i: (qrow(b, h, i), 0))
    ktab = pl.BlockSpec((seq, LANES), lambda b, h, i: (b, 0), pipeline_mode=pl.Buffered(1))
    return pl.pallas_call(
        functools.partial(_diff_attn_kernel, lambda_init=lambda_init),
        grid=(batch, DIFF_HEADS, nb),
        in_specs=[
            pl.BlockSpec((4, DIFF_QK_DIM), lambda b, h, i: (0, 0)),
            pl.BlockSpec((tq, width), lambda b, h, i: (qrow(b, h, i), q_off + h)),
            pl.BlockSpec((seq, width), lambda b, h, i: (b, k_off + h)),
            pl.BlockSpec((seq, DIFF_V_DIM), lambda b, h, i: (b, v_off + h)),
            qtab, qtab, qtab, ktab, ktab, ktab,
            pl.BlockSpec((1, DIFF_V_DIM), lambda b, h, i: (0, 0)),
        ],
        out_specs=pl.BlockSpec((tq, DIFF_V_DIM), lambda b, h, i: (qrow(b, h, i), h)),
        out_shape=jax.ShapeDtypeStruct((batch * seq, DIFF_HEADS * DIFF_V_DIM), BF16),
        scratch_shapes=[pltpu.VMEM((seq, width), BF16),
                        pltpu.VMEM((nb, DIFF_V_DIM + BF16_ROWS, tq), BF16),
                        pltpu.VMEM((2, 1, tq), F32),
                        pltpu.VMEM((2, DIFF_V_DIM + BF16_ROWS, tq), F32),
                        pltpu.VMEM((2, tq, tq), F32), pltpu.VMEM((2, tq, tq), F32)],
        compiler_params=_cparams(3), name="diff_attention",
    )(lam_params, u, u, u, cos, sin_lo, sin_hi, cos, sin_lo, sin_hi, gain)


def _merge_kernel(ret_ref, diff_ref, gr_ref, gd_ref, wr_ref, wd_ref, o_ref):
    a = _dot(ret_ref[...], wr_ref[...])
    b = _dot(diff_ref[...], wd_ref[...])
    merged = _sigmoid(gr_ref[...].astype(F32)) * a + _sigmoid(gd_ref[...].astype(F32)) * b
    o_ref[...] = merged.astype(o_ref.dtype)


def _merge(ret, diff, u, w_ret, w_diff):
    n, dm = ret.shape[0], w_ret.shape[1]
    tm = min(MERGE_TM, n)
    gate_off = (u.shape[1] - 2 * dm) // dm
    tile = lambda c: pl.BlockSpec((tm, dm), lambda i: (i, c))
    weight = lambda w: pl.BlockSpec(w.shape, lambda i: (0, 0))
    return pl.pallas_call(
        _merge_kernel, grid=(n // tm,),
        in_specs=[pl.BlockSpec((tm, ret.shape[1]), lambda i: (i, 0)),
                  pl.BlockSpec((tm, diff.shape[1]), lambda i: (i, 0)),
                  tile(gate_off), tile(gate_off + 1), weight(w_ret), weight(w_diff)],
        out_specs=tile(0), out_shape=jax.ShapeDtypeStruct((n, dm), BF16),
        compiler_params=_cparams(1), name="gated_merge",
    )(ret, diff, u, u, w_ret, w_diff)


def _layer_norm(z, gain, bias):
    mu = jnp.mean(z, axis=-1, keepdims=True)
    var = jnp.mean(jnp.square(z - mu), axis=-1, keepdims=True)
    return (z - mu) * lax.rsqrt(var + LN_EPS) * gain + bias


def _split_bf16(a):
    hi = a.astype(BF16)
    return hi, (a - hi.astype(F32)).astype(BF16)


def _route_kernel(m_ref, x_ref, wo_ref, gain_ref, bias_ref, wr_ref, br_ref, h_ref, r_ref, count_ref,
                  *, alpha):
    @pl.when(pl.program_id(0) == 0)
    def _():
        count_ref[...] = jnp.zeros_like(count_ref)

    w_hi, w_lo = _split_bf16(wr_ref[...])
    part = h_ref.shape[0] // ROUTE_SPLITS
    for g in range(ROUTE_SPLITS):
        rows = slice(g * part, (g + 1) * part)
        h = _layer_norm(alpha * x_ref[rows, :] + _dot(m_ref[rows, :], wo_ref[...]),
                        gain_ref[...], bias_ref[...])
        h_ref[rows, :] = h
        h_hi, h_lo = _split_bf16(h)
        logits = _dot(h_hi, w_hi) + (_dot(h_hi, w_lo) + _dot(h_lo, w_hi)) + br_ref[...]
        r_ref[rows, :] = _routing_rows(logits, count_ref)


def _routing_rows(logits, count_ref):
    lane = lax.broadcasted_iota(jnp.int32, logits.shape, 1).astype(F32)

    def first_argmax(vals):
        top = jnp.max(vals, axis=-1, keepdims=True)
        idx = jnp.min(jnp.where(vals == top, lane, float(LANES)), axis=-1, keepdims=True)
        return top, idx

    group_logits = jnp.where(lane < N_GROUPS, logits, NEG_BIG)
    g_top, g_idx = first_argmax(group_logits)
    p_group = 1.0 / jnp.sum(jnp.exp(group_logits - g_top), axis=-1, keepdims=True)

    first = N_GROUPS + g_idx * EXPERTS_PER_GROUP
    within = jnp.where((lane >= first) & (lane < first + EXPERTS_PER_GROUP), logits, NEG_BIG)
    v1, i1 = first_argmax(within)
    v2, i2 = first_argmax(jnp.where(lane == i1, NEG_BIG, within))
    e2 = jnp.exp(v2 - v1)
    w1 = p_group / (1.0 + e2)
    w2 = p_group * e2 / (1.0 + e2)
    x1, x2 = i1 - N_GROUPS, i2 - N_GROUPS

    hit1 = jnp.where(lane == x1, 1.0, 0.0)
    hit2 = jnp.where(lane == x2, 1.0, 0.0)
    hits = hit1 + hit2
    tm = hits.shape[0]
    earlier = (lax.broadcasted_iota(jnp.int32, (tm, tm), 0) > lax.broadcasted_iota(jnp.int32, (tm, tm), 1))
    before = count_ref[...] + _dot(jnp.where(earlier, 1.0, 0.0).astype(BF16), hits.astype(BF16))
    rank1 = jnp.sum(before * hit1, axis=-1, keepdims=True)
    rank2 = jnp.sum(before * hit2, axis=-1, keepdims=True)
    count_ref[...] += jnp.sum(hits, axis=0, keepdims=True)

    fields = (x1, x2, w1, w2, rank1, rank2)
    row = jnp.zeros_like(logits)
    for pos, value in enumerate(fields):
        row = jnp.where(lane == pos, value, row)
    return row


def _route(merged, x, w_out, gain, bias, w_router, b_router, alpha):
    n, dm = x.shape
    tm = min(ROUTE_TM, n)
    tile = pl.BlockSpec((tm, dm), lambda i: (i, 0))
    full = lambda a: pl.BlockSpec(a.shape, lambda i: (0, 0))
    return pl.pallas_call(
        functools.partial(_route_kernel, alpha=alpha), grid=(n // tm,),
        in_specs=[tile, tile, full(w_out), full(gain), full(bias), full(w_router), full(b_router)],
        out_specs=[tile, pl.BlockSpec((tm, LANES), lambda i: (i, 0)),
                   pl.BlockSpec((1, LANES), lambda i: (0, 0))],
        out_shape=[jax.ShapeDtypeStruct((n, dm), F32), jax.ShapeDtypeStruct((n, LANES), F32),
                   jax.ShapeDtypeStruct((1, LANES), F32)],
        compiler_params=_cparams(1), name="outproj_ln_router",
    )(merged, x, w_out, gain, bias, w_router, b_router)


def _row_copies(table_ref, first, n_rows, make_copy):
    for g in range(n_rows // DMA_GROUP):
        entries = [table_ref[first + g * DMA_GROUP + k] for k in range(DMA_GROUP)]
        for k, entry in enumerate(entries):
            make_copy(entry, g * DMA_GROUP + k).start()


def _for_buffer(dynamic_buf, fn):
    for buf in range(2):
        pl.when(dynamic_buf == buf)(functools.partial(fn, buf))


def _expert_kernel(tile_expert_ref, run_next_ref, src_ref, n_valid_ref, h_hbm, wg_hbm, wu_hbm, wd_hbm,
                   y_ref, xbuf, stage_g, stage_u, stage_d, wg_ref, wu_ref, wd_ref, gather_sem,
                   weight_sem):
    i = pl.program_id(0)
    n_valid = n_valid_ref[0]
    tm = xbuf.shape[1]
    slot = i & 1
    weights = ((wg_hbm, stage_g, wg_ref), (wu_hbm, stage_u, wu_ref), (wd_hbm, stage_d, wd_ref))

    def start_weights(expert):
        for k, (hbm, stage, _) in enumerate(weights):
            pltpu.make_async_copy(hbm.at[expert], stage, weight_sem.at[k]).start(priority=1)

    def take_weights():
        for k, (hbm, stage, dst) in enumerate(weights):
            pltpu.make_async_copy(hbm.at[0], stage, weight_sem.at[k]).wait()

            def round_rows(c, carry, stage=stage, dst=dst):
                rows = pl.ds(pl.multiple_of(c * WEIGHT_CAST_ROWS, WEIGHT_CAST_ROWS), WEIGHT_CAST_ROWS)
                dst[rows, :] = stage[rows, :].astype(BF16)
                return carry
            lax.fori_loop(0, stage.shape[0] // WEIGHT_CAST_ROWS, round_rows, 0)

    def start_gather(tile, dynamic_buf):
        def issue(buf):
            _row_copies(src_ref, tile * tm, tm, lambda tok, r: pltpu.make_async_copy(
                h_hbm.at[pl.ds(tok, 1)], xbuf.at[buf, pl.ds(r, 1)], gather_sem.at[buf]))
        _for_buffer(dynamic_buf, issue)

    @pl.when(i == 0)
    def _():
        start_gather(0, 0)
        start_weights(tile_expert_ref[0])

    @pl.when(i + 1 < n_valid)
    def _():
        start_gather(i + 1, 1 - slot)

    @pl.when(i < n_valid)
    def _():
        run_next = run_next_ref[i]

        @pl.when(run_next != NOT_RUN_START)
        def _():
            take_weights()

            @pl.when(run_next != NO_NEXT_EXPERT)
            def _():
                start_weights(run_next)

        pltpu.make_async_copy(h_hbm.at[pl.ds(0, tm)], xbuf.at[slot], gather_sem.at[slot]).wait()
        x = xbuf[slot].astype(BF16)
        g = _dot(x, wg_ref[...])
        act = (g * _sigmoid(g) * _dot(x, wu_ref[...])).astype(BF16)
        y_ref[...] = _dot(act, wd_ref[...])

    @pl.when(i >= n_valid)
    def _():
        y_ref[...] = jnp.zeros_like(y_ref)


def _experts(h, tile_expert, run_next, src, n_valid, w_gate, w_up, w_down):
    dm = h.shape[1]
    n_tiles = tile_expert.shape[0]
    tm = src.shape[0] // n_tiles
    ff = w_gate.shape[2]
    hbm = pl.BlockSpec(memory_space=pl.ANY)
    grid_spec = pltpu.PrefetchScalarGridSpec(
        num_scalar_prefetch=4, grid=(n_tiles,),
        in_specs=[hbm, hbm, hbm, hbm],
        out_specs=pl.BlockSpec((tm, dm), lambda i, *_: (i, 0)),
        scratch_shapes=[pltpu.VMEM((2, tm, dm), F32),
                        pltpu.VMEM((dm, ff), F32), pltpu.VMEM((dm, ff), F32), pltpu.VMEM((ff, dm), F32),
                        pltpu.VMEM((dm, ff), BF16), pltpu.VMEM((dm, ff), BF16), pltpu.VMEM((ff, dm), BF16),
                        pltpu.SemaphoreType.DMA((2,)), pltpu.SemaphoreType.DMA((3,))])
    return pl.pallas_call(
        _expert_kernel, grid_spec=grid_spec,
        out_shape=jax.ShapeDtypeStruct((src.shape[0], dm), F32),
        compiler_params=_cparams(1), name="experts",
    )(tile_expert, run_next, src, n_valid, h, w_gate, w_up, w_down)


def _dispatch_tables(expert_ids, ranks, counts, tm):
    n_assign = expert_ids.shape[0]
    n_tiles = n_assign // tm + N_EXPERTS
    n_slots = n_tiles * tm
    experts = jnp.arange(N_EXPERTS, dtype=jnp.int32)
    padded = (counts + tm - 1) // tm * tm
    pad_end = jnp.cumsum(padded)
    pad_start = pad_end - padded
    slot_of = jnp.sum(jnp.where(expert_ids[:, None] == experts[None, :], pad_start[None, :], 0), axis=1) + ranks
    tile_start = jnp.arange(n_tiles, dtype=jnp.int32) * tm
    tile_expert = jnp.minimum(
        jnp.sum((pad_end[None, :] <= tile_start[:, None]).astype(jnp.int32), axis=1), N_EXPERTS - 1)
    later = lax.cummin(jnp.where(counts > 0, experts, N_EXPERTS)[::-1])[::-1]
    following = jnp.concatenate([later[1:], jnp.full((1,), N_EXPERTS, jnp.int32)])[tile_expert]
    run_start = jnp.concatenate([jnp.ones((1,), bool), tile_expert[1:] != tile_expert[:-1]])
    run_next = jnp.where(run_start, jnp.where(following < N_EXPERTS, following, NO_NEXT_EXPERT),
                         NOT_RUN_START).astype(jnp.int32)
    token = jnp.arange(n_assign, dtype=jnp.int32) // TOP_K
    src = jnp.zeros((n_slots,), jnp.int32).at[slot_of].set(token)
    n_valid = (pad_end[-1] // tm).astype(jnp.int32).reshape(1)
    return slot_of, tile_expert, run_next, src, n_valid


def _final_kernel(slot_ref, h_ref, y_hbm, r_ref, gain_ref, bias_ref, o_ref, ybuf, sem, *, alpha):
    i = pl.program_id(0)
    tm = o_ref.shape[0]
    slot = i & 1

    def start_gather(step, dynamic_buf):
        def issue(buf):
            _row_copies(slot_ref, step * (TOP_K * tm), TOP_K * tm, lambda row, r: pltpu.make_async_copy(
                y_hbm.at[pl.ds(row, 1)], ybuf.at[buf, r % TOP_K, pl.ds(r // TOP_K, 1)], sem.at[buf]))
        _for_buffer(dynamic_buf, issue)

    @pl.when(i == 0)
    def _():
        start_gather(0, 0)

    @pl.when(i + 1 < pl.num_programs(0))
    def _():
        start_gather(i + 1, 1 - slot)

    for k in range(TOP_K):
        pltpu.make_async_copy(y_hbm.at[pl.ds(0, tm)], ybuf.at[slot, k], sem.at[slot]).wait()
    r = r_ref[...]
    ffn = ybuf[slot, 0] * r[:, 2:3] + ybuf[slot, 1] * r[:, 3:4]
    o_ref[...] = _layer_norm(alpha * h_ref[...] + ffn, gain_ref[...], bias_ref[...])


def _final(h, y_slots, slot_of, routing, gain, bias, alpha):
    n, dm = h.shape
    tm = min(FINAL_TM, n)
    tile = pl.BlockSpec((tm, dm), lambda i, *_: (i, 0))
    full = lambda a: pl.BlockSpec(a.shape, lambda i, *_: (0, 0))
    grid_spec = pltpu.PrefetchScalarGridSpec(
        num_scalar_prefetch=1, grid=(n // tm,),
        in_specs=[tile, pl.BlockSpec(memory_space=pl.ANY), pl.BlockSpec((tm, LANES), lambda i, *_: (i, 0)),
                  full(gain), full(bias)],
        out_specs=tile,
        scratch_shapes=[pltpu.VMEM((2, TOP_K, tm, dm), F32), pltpu.SemaphoreType.DMA((2,))])
    return pl.pallas_call(
        functools.partial(_final_kernel, alpha=alpha), grid_spec=grid_spec,
        out_shape=jax.ShapeDtypeStruct((n, dm), F32),
        compiler_params=_cparams(1), name="combine_ln",
    )(slot_of, h, y_slots, routing, gain, bias)


def _rope_frequencies():
    ret_freq = 1.0 / (RET_THETA ** (jnp.arange(0, RET_QK_DIM, 2, dtype=F32) / RET_QK_DIM))
    diff_freq = 1.0 / (ROPE_THETA ** (jnp.arange(0, ROPE_DIM, 2, dtype=F32) / ROPE_DIM))
    pad = jnp.zeros((LANES - ret_freq.shape[0] - diff_freq.shape[0],), F32)
    return jnp.concatenate([ret_freq, diff_freq, pad]).reshape(1, LANES)


def kernel(x, positions, w_in, ret_gn_gain, diff_lambda_q1, diff_lambda_k1, diff_lambda_q2, diff_lambda_k2, diff_subln_gain, w_ret_proj, w_diff_proj, w_out, ln1_gain, ln1_bias, w_group, b_group, w_router, b_router, w_expert_gate, w_expert_up, w_expert_down, ln2_gain, ln2_bias):
    batch, seq, dm = x.shape
    n = batch * seq
    depth = w_in.shape[0]
    assert seq % CHUNK == 0
    alpha = (2 * depth) ** 0.25

    ret_cos, ret_sin, diff_cos, diff_sin_lo, diff_sin_hi = _rope_tables(
        positions.reshape(n, 1), _rope_frequencies())

    log_gamma = jnp.log(1.0 - 2.0 ** (-5.0 - jnp.arange(RET_HEADS, dtype=F32)))
    block_decay = jnp.exp(log_gamma * min(RET_BLOCK, seq))

    xt = x.reshape(n, dm)
    for l in range(depth):
        lambda_init = 0.8 - 0.6 * math.exp(-0.3 * l)
        u = _matmul(xt, w_in[l], BF16, "in_proj")
        ret = _retention(u, ret_cos, ret_sin, log_gamma, block_decay,
                         ret_gn_gain[l].reshape(RET_HEADS, 1, RET_V_DIM), batch, seq)
        lam_params = jnp.stack([diff_lambda_q1[l], diff_lambda_k1[l], diff_lambda_q2[l], diff_lambda_k2[l]])
        diff = _diff_attention(u, diff_cos, diff_sin_lo, diff_sin_hi, lam_params.astype(F32),
                               diff_subln_gain[l].reshape(1, DIFF_V_DIM), batch, seq, lambda_init)
        merged = _merge(ret, diff, u, w_ret_proj[l].astype(BF16), w_diff_proj[l].astype(BF16))

        w_route = jnp.concatenate(
            [w_group[l], w_router[l], jnp.zeros((dm, LANES - N_GROUPS - N_EXPERTS), F32)], axis=1)
        b_route = jnp.concatenate(
            [b_group[l], b_router[l], jnp.zeros((LANES - N_GROUPS - N_EXPERTS,), F32)]).reshape(1, LANES)
        h, routing, counts = _route(merged, xt, w_out[l].astype(BF16), ln1_gain[l].reshape(1, dm),
                                    ln1_bias[l].reshape(1, dm), w_route, b_route, alpha)

        expert_ids = routing[:, :TOP_K].astype(jnp.int32).reshape(-1)
        ranks = routing[:, 2 * TOP_K:3 * TOP_K].astype(jnp.int32).reshape(-1)
        slot_of, tile_expert, run_next, src, n_valid = _dispatch_tables(
            expert_ids, ranks, counts[0, :N_EXPERTS].astype(jnp.int32), min(MOE_TM, n))
        y_slots = _experts(h, tile_expert, run_next, src, n_valid, w_expert_gate[l], w_expert_up[l],
                           w_expert_down[l])
        xt = _final(h, y_slots, slot_of, routing,
                    ln2_gain[l].reshape(1, dm), ln2_bias[l].reshape(1, dm), alpha)
    return xt.reshape(batch, seq, dm)
```

```python
import functools
import math

import jax
import jax.numpy as jnp
from jax import lax
from jax.experimental import pallas as pl
from jax.experimental.pallas import tpu as pltpu

F32 = jnp.float32
BF16 = jnp.bfloat16

CHUNK = 64
RET_HEADS = 8
RET_QK_DIM = 128
RET_V_DIM = 256
RET_THETA = 10000.0
DIFF_HEADS = 8
DIFF_QK_DIM = 128
DIFF_V_DIM = 256
ROPE_THETA = 500000.0
ROPE_DIM = DIFF_QK_DIM // 4
N_GROUPS = 4
EXPERTS_PER_GROUP = 8
N_EXPERTS = N_GROUPS * EXPERTS_PER_GROUP
TOP_K = 2
LN_EPS = 1e-5

LANES = 128
BF16_ROWS = 16
NEG_BIG = -1e30

VMEM_LIMIT_BYTES = 56 * 1024 * 1024
TABLE_ROWS = 512
PROJ_TM = 1024
PROJ_TN = 1024
RET_BLOCK = 256
RET_HEAD_GROUP = 8
ATT_BLOCK = 1024
MERGE_TM = 256
ROUTE_TM = 512
ROUTE_SPLITS = 2
MOE_TM = 256
DMA_GROUP = 8
WEIGHT_CAST_ROWS = 256
NOT_RUN_START = -2
NO_NEXT_EXPERT = -1
FINAL_TM = 256


def _cparams(n_axes):
    return pltpu.CompilerParams(
        dimension_semantics=("arbitrary",) * n_axes, vmem_limit_bytes=VMEM_LIMIT_BYTES)


def _sigmoid(x):
    return 1.0 / (1.0 + jnp.exp(-x))


def _chunk_of(idx):
    return jnp.right_shift(idx, CHUNK.bit_length() - 1)


def _dot(a, b):
    return jnp.dot(a, b, preferred_element_type=F32)


def _dot_nt(a, b):
    return lax.dot_general(a, b, (((1,), (1,)), ((), ())), preferred_element_type=F32)


def _table_kernel(pos_ref, freq_ref, rc_ref, rs_ref, dc_ref, dlo_ref, dhi_ref):
    rh, dh = RET_QK_DIM // 2, ROPE_DIM // 2
    ang = pos_ref[...].astype(F32) * freq_ref[...]
    c, s = jnp.cos(ang), jnp.sin(ang)
    lane = lax.broadcasted_iota(jnp.int32, ang.shape, 1)
    rc_ref[...] = jnp.where(lane < rh, c, pltpu.roll(c, rh, 1))
    rs_ref[...] = jnp.where(lane < rh, -s, pltpu.roll(s, rh, 1))
    c_lo, s_lo = pltpu.roll(c, LANES - rh, 1), pltpu.roll(s, LANES - rh, 1)
    c_hi, s_hi = pltpu.roll(c, LANES - rh + dh, 1), pltpu.roll(s, LANES - rh + dh, 1)
    dc_ref[...] = jnp.where(lane < dh, c_lo, jnp.where(lane < 2 * dh, c_hi, 1.0))
    dlo_ref[...] = jnp.where(lane < dh, -s_lo, 0.0)
    dhi_ref[...] = jnp.where((lane >= dh) & (lane < 2 * dh), s_hi, 0.0)


def _rope_tables(pos_col, freq_row):
    n = pos_col.shape[0]
    tr = min(TABLE_ROWS, n)
    out = pl.BlockSpec((tr, LANES), lambda i: (i, 0))
    shp = jax.ShapeDtypeStruct((n, LANES), F32)
    return pl.pallas_call(
        _table_kernel, grid=(n // tr,),
        in_specs=[pl.BlockSpec((tr, 1), lambda i: (i, 0)), pl.BlockSpec((1, LANES), lambda i: (0, 0))],
        out_specs=[out] * 5, out_shape=[shp] * 5,
        compiler_params=_cparams(1), name="rope_tables",
    )(pos_col, freq_row)


def _matmul_kernel(x_ref, w_ref, o_ref, wb_ref):
    @pl.when(pl.program_id(1) == 0)
    def _():
        wb_ref[...] = w_ref[...].astype(BF16)

    o_ref[...] = _dot(x_ref[...].astype(BF16), wb_ref[...]).astype(o_ref.dtype)


def _matmul(x, w, out_dtype, name):
    m, k = x.shape
    n = w.shape[1]
    tm, tn = min(PROJ_TM, m), min(PROJ_TN, n)
    return pl.pallas_call(
        _matmul_kernel, grid=(n // tn, m // tm),
        in_specs=[pl.BlockSpec((tm, k), lambda j, i: (i, 0)),
                  pl.BlockSpec((k, tn), lambda j, i: (0, j))],
        out_specs=pl.BlockSpec((tm, tn), lambda j, i: (i, j)),
        out_shape=jax.ShapeDtypeStruct((m, n), out_dtype),
        scratch_shapes=[pltpu.VMEM((k, tn), BF16)],
        compiler_params=_cparams(2), name=name,
    )(x, w)


def _retention_kernel(lg_ref, bd_ref, q_ref, k_ref, v_ref, g_ref, cos_ref, sin_ref, gain_ref,
                      o_ref, state_ref, decay_ref, qdecay_ref, kdecay_ref):
    group = state_ref.shape[0]
    first_head = pl.program_id(1) * group
    t = q_ref.shape[0]
    dk, dv = RET_QK_DIM, RET_V_DIM

    @pl.when(pl.program_id(2) == 0)
    def _():
        state_ref[...] = jnp.zeros_like(state_ref)
        ri = lax.broadcasted_iota(jnp.int32, (t, t), 0)
        ci = lax.broadcasted_iota(jnp.int32, (t, t), 1)
        dist = jnp.abs(ri - ci).astype(F32)
        visible = _chunk_of(ri) >= _chunk_of(ci)
        row = lax.broadcasted_iota(jnp.int32, (t, dk), 0).astype(F32)
        for j in range(group):
            lg = lg_ref[first_head + j]
            decay_ref[j] = jnp.where(visible, jnp.exp(lg * dist), 0.0)
            qdecay_ref[j] = jnp.exp(lg * (row + 1.0))
            kdecay_ref[j] = jnp.exp(lg * (t - 1.0 - row))

    cos, sin = cos_ref[...], sin_ref[...]

    def rot(x):
        return x * cos + pltpu.roll(x, dk // 2, 1) * sin

    for j in range(group):
        q = rot(q_ref[:, j * dk:(j + 1) * dk].astype(F32))
        k = rot(k_ref[:, j * dk:(j + 1) * dk].astype(F32)) * (dk ** -0.5)
        v = v_ref[:, j * dv:(j + 1) * dv]

        s = _dot_nt(q.astype(BF16), k.astype(BF16)) * decay_ref[j]
        intra = _dot(s.astype(BF16), v)

        state = state_ref[j]
        cross = _dot((q * qdecay_ref[j]).astype(BF16), state.astype(BF16))
        kd = k * kdecay_ref[j]
        state_ref[j] = state * bd_ref[first_head + j] + _dot(kd.T.astype(BF16), v)

        out = intra + cross
        mu = jnp.mean(out, axis=-1, keepdims=True)
        var = jnp.mean(jnp.square(out - mu), axis=-1, keepdims=True)
        y = (out - mu) * lax.rsqrt(var + LN_EPS) * gain_ref[j]
        gate = g_ref[:, j * dv:(j + 1) * dv].astype(F32)
        o_ref[:, j * dv:(j + 1) * dv] = (y * (gate * _sigmoid(gate))).astype(o_ref.dtype)


def _retention(u, cos, sin, log_gamma, block_decay, gain, batch, seq):
    t = min(RET_BLOCK, seq)
    nb = seq // t
    grp = RET_HEAD_GROUP
    n_groups = RET_HEADS // grp
    rows = lambda b, h, i: b * nb + i
    qk_w, v_w = grp * RET_QK_DIM, grp * RET_V_DIM
    k_off, v_off, g_off = n_groups, n_groups, 2 * n_groups
    smem = pl.BlockSpec(memory_space=pltpu.SMEM)
    return pl.pallas_call(
        _retention_kernel, grid=(batch, n_groups, nb),
        in_specs=[
            smem, smem,
            pl.BlockSpec((t, qk_w), lambda b, h, i: (rows(b, h, i), h)),
            pl.BlockSpec((t, qk_w), lambda b, h, i: (rows(b, h, i), k_off + h)),
            pl.BlockSpec((t, v_w), lambda b, h, i: (rows(b, h, i), v_off + h)),
            pl.BlockSpec((t, v_w), lambda b, h, i: (rows(b, h, i), g_off + h)),
            pl.BlockSpec((t, LANES), lambda b, h, i: (rows(b, h, i), 0)),
            pl.BlockSpec((t, LANES), lambda b, h, i: (rows(b, h, i), 0)),
            pl.BlockSpec((grp, 1, RET_V_DIM), lambda b, h, i: (h, 0, 0)),
        ],
        out_specs=pl.BlockSpec((t, v_w), lambda b, h, i: (rows(b, h, i), h)),
        out_shape=jax.ShapeDtypeStruct((batch * seq, RET_HEADS * RET_V_DIM), BF16),
        scratch_shapes=[pltpu.VMEM((grp, RET_QK_DIM, RET_V_DIM), F32), pltpu.VMEM((grp, t, t), F32),
                        pltpu.VMEM((grp, t, RET_QK_DIM), F32), pltpu.VMEM((grp, t, RET_QK_DIM), F32)],
        compiler_params=_cparams(3), name="retention",
    )(log_gamma, block_decay, u, u, u, u, cos, sin, gain)


def _partial_rope(x, cos, sin_lo, sin_hi):
    half = ROPE_DIM // 2
    return (x * cos + pltpu.roll(x, DIFF_QK_DIM - half, 1) * sin_lo
            + pltpu.roll(x, half, 1) * sin_hi)


def _diff_attn_kernel(lam_ref, q_ref, k_ref, v_ref, qc_ref, qa_ref, qb_ref, kc_ref, ka_ref, kb_ref,
                      gain_ref, o_ref, krot_ref, vt_ref, m_ref, acc_ref, sa_ref, sb_ref, *,
                      lambda_init):
    qblk = pl.program_id(2)
    tq = q_ref.shape[0]
    d = DIFF_QK_DIM

    def key_rows(j):
        return pl.ds(pl.multiple_of(j * tq, tq), tq)

    @pl.when(qblk == 0)
    def _():
        def prep_rows(c, carry):
            rows = key_rows(c)
            for i in range(2):
                x = k_ref[rows, i * d:(i + 1) * d].astype(F32)
                krot_ref[rows, i * d:(i + 1) * d] = _partial_rope(
                    x, kc_ref[rows, :], ka_ref[rows, :], kb_ref[rows, :]).astype(BF16)
            vt_ref[c, :DIFF_V_DIM] = v_ref[rows, :].astype(F32).T.astype(BF16)
            vt_ref[c, DIFF_V_DIM:] = jnp.ones((BF16_ROWS, tq), BF16)
            return carry
        lax.fori_loop(0, k_ref.shape[0] // tq, prep_rows, 0)

    scale = d ** -0.5
    qs = []
    for i in range(2):
        x = q_ref[:, i * d:(i + 1) * d].astype(F32)
        qs.append((_partial_rope(x, qc_ref[...], qa_ref[...], qb_ref[...]) * scale).astype(BF16))

    m_ref[...] = jnp.full_like(m_ref, NEG_BIG)
    acc_ref[...] = jnp.zeros_like(acc_ref)

    s_bufs = (sa_ref, sb_ref)

    def scores(j, buf, diagonal):
        for i in range(2):
            s = _dot_nt(krot_ref[key_rows(j), i * d:(i + 1) * d], qs[i])
            if diagonal:
                key = lax.broadcasted_iota(jnp.int32, s.shape, 0)
                qry = lax.broadcasted_iota(jnp.int32, s.shape, 1)
                s = jnp.where(_chunk_of(qry) >= _chunk_of(key), s, NEG_BIG)
            s_bufs[buf][i] = s

    def absorb(j, buf):
        vt = vt_ref[j]
        for i in range(2):
            s = s_bufs[buf][i]
            m_old = m_ref[i]
            m_new = jnp.maximum(m_old, jnp.max(s, axis=0, keepdims=True))
            alpha = jnp.exp(m_old - m_new)
            p = jnp.exp(s - m_new)
            acc_ref[i] = alpha * acc_ref[i] + _dot(vt, p.astype(BF16))
            m_ref[i] = m_new

    scores(qblk, 0, True)

    def pair(p, carry):
        pending = jnp.where(p == 0, qblk, 2 * p - 1)
        scores(2 * p, 1, False)
        absorb(pending, 0)
        scores(2 * p + 1, 0, False)
        absorb(2 * p, 1)
        return carry
    n_pairs = qblk // 2
    lax.fori_loop(0, n_pairs, pair, 0)
    pending = jnp.where(n_pairs == 0, qblk, 2 * n_pairs - 1)

    @pl.when(qblk % 2 == 1)
    def _():
        scores(qblk - 1, 1, False)
        absorb(pending, 0)
        absorb(qblk - 1, 1)

    @pl.when(qblk % 2 == 0)
    def _():
        absorb(pending, 0)

    lp = lam_ref[...]
    lam = (jnp.exp(jnp.sum(lp[0:1] * lp[1:2], axis=-1, keepdims=True))
           - jnp.exp(jnp.sum(lp[2:3] * lp[3:4], axis=-1, keepdims=True)) + lambda_init)
    dv = DIFF_V_DIM
    o = (acc_ref[0, :dv] / acc_ref[0, dv:dv + 1] - lam * (acc_ref[1, :dv] / acc_ref[1, dv:dv + 1])).T
    o = o * lax.rsqrt(jnp.mean(jnp.square(o), axis=-1, keepdims=True) + LN_EPS) * gain_ref[...]
    o_ref[...] = (o * (1.0 - lambda_init)).astype(o_ref.dtype)


def _diff_attention(u, cos, sin_lo, sin_hi, lam_params, gain, batch, seq, lambda_init):
    tq = min(ATT_BLOCK, seq)
    nb = seq // tq
    width = 2 * DIFF_QK_DIM
    ret_cols = 2 * RET_HEADS * RET_QK_DIM + 2 * RET_HEADS * RET_V_DIM
    q_off = ret_cols // width
    k_off = q_off + DIFF_HEADS
    v_off = k_off + DIFF_HEADS
    qrow = lambda b, h, i: b * nb + i
    qtab = pl.BlockSpec((tq, LANES), lambda b, h, i: (qrow(b, h, i), 0))
    ktab = pl.BlockSpec((seq, LANES), lambda b, h, i: (b, 0), pipeline_mode=pl.Buffered(1))
    return pl.pallas_call(
        functools.partial(_diff_attn_kernel, lambda_init=lambda_init),
        grid=(batch, DIFF_HEADS, nb),
        in_specs=[
            pl.BlockSpec((4, DIFF_QK_DIM), lambda b, h, i: (0, 0)),
            pl.BlockSpec((tq, width), lambda b, h, i: (qrow(b, h, i), q_off + h)),
            pl.BlockSpec((seq, width), lambda b, h, i: (b, k_off + h)),
            pl.BlockSpec((seq, DIFF_V_DIM), lambda b, h, i: (b, v_off + h)),
            qtab, qtab, qtab, ktab, ktab, ktab,
            pl.BlockSpec((1, DIFF_V_DIM), lambda b, h, i: (0, 0)),
        ],
        out_specs=pl.BlockSpec((tq, DIFF_V_DIM), lambda b, h, i: (qrow(b, h, i), h)),
        out_shape=jax.ShapeDtypeStruct((batch * seq, DIFF_HEADS * DIFF_V_DIM), BF16),
        scratch_shapes=[pltpu.VMEM((seq, width), BF16),
                        pltpu.VMEM((nb, DIFF_V_DIM + BF16_ROWS, tq), BF16),
                        pltpu.VMEM((2, 1, tq), F32),
                        pltpu.VMEM((2, DIFF_V_DIM + BF16_ROWS, tq), F32),
                        pltpu.VMEM((2, tq, tq), F32), pltpu.VMEM((2, tq, tq), F32)],
        compiler_params=_cparams(3), name="diff_attention",
    )(lam_params, u, u, u, cos, sin_lo, sin_hi, cos, sin_lo, sin_hi, gain)


def _merge_kernel(ret_ref, diff_ref, gr_ref, gd_ref, wr_ref, wd_ref, o_ref):
    a = _dot(ret_ref[...], wr_ref[...])
    b = _dot(diff_ref[...], wd_ref[...])
    merged = _sigmoid(gr_ref[...].astype(F32)) * a + _sigmoid(gd_ref[...].astype(F32)) * b
    o_ref[...] = merged.astype(o_ref.dtype)


def _merge(ret, diff, u, w_ret, w_diff):
    n, dm = ret.shape[0], w_ret.shape[1]
    tm = min(MERGE_TM, n)
    gate_off = (u.shape[1] - 2 * dm) // dm
    tile = lambda c: pl.BlockSpec((tm, dm), lambda i: (i, c))
    weight = lambda w: pl.BlockSpec(w.shape, lambda i: (0, 0))
    return pl.pallas_call(
        _merge_kernel, grid=(n // tm,),
        in_specs=[pl.BlockSpec((tm, ret.shape[1]), lambda i: (i, 0)),
                  pl.BlockSpec((tm, diff.shape[1]), lambda i: (i, 0)),
                  tile(gate_off), tile(gate_off + 1), weight(w_ret), weight(w_diff)],
        out_specs=tile(0), out_shape=jax.ShapeDtypeStruct((n, dm), BF16),
        compiler_params=_cparams(1), name="gated_merge",
    )(ret, diff, u, u, w_ret, w_diff)


def _layer_norm(z, gain, bias):
    mu = jnp.mean(z, axis=-1, keepdims=True)
    var = jnp.mean(jnp.square(z - mu), axis=-1, keepdims=True)
    return (z - mu) * lax.rsqrt(var + LN_EPS) * gain + bias


def _split_bf16(a):
    hi = a.astype(BF16)
    return hi, (a - hi.astype(F32)).astype(BF16)


def _pack_bf16_halves(a):
    bits = pltpu.bitcast(a.astype(F32), jnp.uint32)
    c = bits.shape[1] // 2
    return (bits[:, :c] & jnp.uint32(0xFFFF0000)) | (bits[:, c:] >> 16)


def _unpack_bf16_halves(words):
    hi = pltpu.bitcast(words & jnp.uint32(0xFFFF0000), F32)
    lo = pltpu.bitcast(words << 16, F32)
    return jnp.concatenate([hi.astype(BF16), lo.astype(BF16)], axis=1)


def _route_kernel(m_ref, x_ref, wo_ref, gain_ref, bias_ref, wr_ref, br_ref, h_ref, hp_ref, r_ref,
                  count_ref, *, alpha):
    @pl.when(pl.program_id(0) == 0)
    def _():
        count_ref[...] = jnp.zeros_like(count_ref)

    w_hi, w_lo = _split_bf16(wr_ref[...])
    part = h_ref.shape[0] // ROUTE_SPLITS
    for g in range(ROUTE_SPLITS):
        rows = slice(g * part, (g + 1) * part)
        h = _layer_norm(alpha * x_ref[rows, :] + _dot(m_ref[rows, :], wo_ref[...]),
                        gain_ref[...], bias_ref[...])
        h_ref[rows, :] = h
        h_hi, h_lo = _split_bf16(h)
        hp_ref[rows, :] = _pack_bf16_halves(h_hi)
        logits = _dot(h_hi, w_hi) + (_dot(h_hi, w_lo) + _dot(h_lo, w_hi)) + br_ref[...]
        r_ref[rows, :] = _routing_rows(logits, count_ref)


def _routing_rows(logits, count_ref):
    lane = lax.broadcasted_iota(jnp.int32, logits.shape, 1).astype(F32)

    def first_argmax(vals):
        top = jnp.max(vals, axis=-1, keepdims=True)
        idx = jnp.min(jnp.where(vals == top, lane, float(LANES)), axis=-1, keepdims=True)
        return top, idx

    group_logits = jnp.where(lane < N_GROUPS, logits, NEG_BIG)
    g_top, g_idx = first_argmax(group_logits)
    p_group = 1.0 / jnp.sum(jnp.exp(group_logits - g_top), axis=-1, keepdims=True)

    first = N_GROUPS + g_idx * EXPERTS_PER_GROUP
    within = jnp.where((lane >= first) & (lane < first + EXPERTS_PER_GROUP), logits, NEG_BIG)
    v1, i1 = first_argmax(within)
    v2, i2 = first_argmax(jnp.where(lane == i1, NEG_BIG, within))
    e2 = jnp.exp(v2 - v1)
    w1 = p_group / (1.0 + e2)
    w2 = p_group * e2 / (1.0 + e2)
    x1, x2 = i1 - N_GROUPS, i2 - N_GROUPS

    hit1 = jnp.where(lane == x1, 1.0, 0.0)
    hit2 = jnp.where(lane == x2, 1.0, 0.0)
    hits = hit1 + hit2
    tm = hits.shape[0]
    earlier = (lax.broadcasted_iota(jnp.int32, (tm, tm), 0) > lax.broadcasted_iota(jnp.int32, (tm, tm), 1))
    before = count_ref[...] + _dot(jnp.where(earlier, 1.0, 0.0).astype(BF16), hits.astype(BF16))
    rank1 = jnp.sum(before * hit1, axis=-1, keepdims=True)
    rank2 = jnp.sum(before * hit2, axis=-1, keepdims=True)
    count_ref[...] += jnp.sum(hits, axis=0, keepdims=True)

    fields = (x1, x2, w1, w2, rank1, rank2)
    row = jnp.zeros_like(logits)
    for pos, value in enumerate(fields):
        row = jnp.where(lane == pos, value, row)
    return row


def _route(merged, x, w_out, gain, bias, w_router, b_router, alpha):
    n, dm = x.shape
    tm = min(ROUTE_TM, n)
    tile = pl.BlockSpec((tm, dm), lambda i: (i, 0))
    full = lambda a: pl.BlockSpec(a.shape, lambda i: (0, 0))
    return pl.pallas_call(
        functools.partial(_route_kernel, alpha=alpha), grid=(n // tm,),
        in_specs=[tile, tile, full(w_out), full(gain), full(bias), full(w_router), full(b_router)],
        out_specs=[tile, pl.BlockSpec((tm, dm // 2), lambda i: (i, 0)),
                   pl.BlockSpec((tm, LANES), lambda i: (i, 0)), pl.BlockSpec((1, LANES), lambda i: (0, 0))],
        out_shape=[jax.ShapeDtypeStruct((n, dm), F32), jax.ShapeDtypeStruct((n, dm // 2), jnp.uint32),
                   jax.ShapeDtypeStruct((n, LANES), F32), jax.ShapeDtypeStruct((1, LANES), F32)],
        compiler_params=_cparams(1), name="outproj_ln_router",
    )(merged, x, w_out, gain, bias, w_router, b_router)


def _row_copies(table_ref, first, n_rows, make_copy):
    for g in range(n_rows // DMA_GROUP):
        entries = [table_ref[first + g * DMA_GROUP + k] for k in range(DMA_GROUP)]
        for k, entry in enumerate(entries):
            make_copy(entry, g * DMA_GROUP + k).start()


def _for_buffer(dynamic_buf, fn):
    for buf in range(2):
        pl.when(dynamic_buf == buf)(functools.partial(fn, buf))


def _expert_kernel(tile_expert_ref, run_next_ref, src_ref, n_valid_ref, h_hbm, wg_hbm, wu_hbm, wd_hbm,
                   y_ref, xbuf, stage_g, stage_u, stage_d, wg_ref, wu_ref, wd_ref, gather_sem,
                   weight_sem):
    i = pl.program_id(0)
    n_valid = n_valid_ref[0]
    tm = xbuf.shape[1]
    slot = i & 1
    weights = ((wg_hbm, stage_g, wg_ref), (wu_hbm, stage_u, wu_ref), (wd_hbm, stage_d, wd_ref))

    def start_weights(expert):
        for k, (hbm, stage, _) in enumerate(weights):
            pltpu.make_async_copy(hbm.at[expert], stage, weight_sem.at[k]).start(priority=1)

    def take_weights():
        for k, (hbm, stage, dst) in enumerate(weights):
            pltpu.make_async_copy(hbm.at[0], stage, weight_sem.at[k]).wait()

            def round_rows(c, carry, stage=stage, dst=dst):
                rows = pl.ds(pl.multiple_of(c * WEIGHT_CAST_ROWS, WEIGHT_CAST_ROWS), WEIGHT_CAST_ROWS)
                dst[rows, :] = stage[rows, :].astype(BF16)
                return carry
            lax.fori_loop(0, stage.shape[0] // WEIGHT_CAST_ROWS, round_rows, 0)

    def start_gather(tile, dynamic_buf):
        def issue(buf):
            _row_copies(src_ref, tile * tm, tm, lambda tok, r: pltpu.make_async_copy(
                h_hbm.at[pl.ds(tok, 1)], xbuf.at[buf, pl.ds(r, 1)], gather_sem.at[buf]))
        _for_buffer(dynamic_buf, issue)

    @pl.when(i == 0)
    def _():
        start_gather(0, 0)
        start_weights(tile_expert_ref[0])

    @pl.when(i + 1 < n_valid)
    def _():
        start_gather(i + 1, 1 - slot)

    @pl.when(i < n_valid)
    def _():
        run_next = run_next_ref[i]

        @pl.when(run_next != NOT_RUN_START)
        def _():
            take_weights()

            @pl.when(run_next != NO_NEXT_EXPERT)
            def _():
                start_weights(run_next)

        pltpu.make_async_copy(h_hbm.at[pl.ds(0, tm)], xbuf.at[slot], gather_sem.at[slot]).wait()
        x = _unpack_bf16_halves(xbuf[slot])
        g = _dot(x, wg_ref[...])
        act = (g * _sigmoid(g) * _dot(x, wu_ref[...])).astype(BF16)
        y_ref[...] = _dot(act, wd_ref[...])

    @pl.when(i >= n_valid)
    def _():
        y_ref[...] = jnp.zeros_like(y_ref)


def _experts(h, tile_expert, run_next, src, n_valid, w_gate, w_up, w_down):
    dm = w_gate.shape[1]
    n_tiles = tile_expert.shape[0]
    tm = src.shape[0] // n_tiles
    ff = w_gate.shape[2]
    hbm = pl.BlockSpec(memory_space=pl.ANY)
    grid_spec = pltpu.PrefetchScalarGridSpec(
        num_scalar_prefetch=4, grid=(n_tiles,),
        in_specs=[hbm, hbm, hbm, hbm],
        out_specs=pl.BlockSpec((tm, dm), lambda i, *_: (i, 0)),
        scratch_shapes=[pltpu.VMEM((2, tm, h.shape[1]), h.dtype),
                        pltpu.VMEM((dm, ff), F32), pltpu.VMEM((dm, ff), F32), pltpu.VMEM((ff, dm), F32),
                        pltpu.VMEM((dm, ff), BF16), pltpu.VMEM((dm, ff), BF16), pltpu.VMEM((ff, dm), BF16),
                        pltpu.SemaphoreType.DMA((2,)), pltpu.SemaphoreType.DMA((3,))])
    return pl.pallas_call(
        _expert_kernel, grid_spec=grid_spec,
        out_shape=jax.ShapeDtypeStruct((src.shape[0], dm), F32),
        compiler_params=_cparams(1), name="experts",
    )(tile_expert, run_next, src, n_valid, h, w_gate, w_up, w_down)


def _dispatch_tables(expert_ids, ranks, counts, tm):
    n_assign = expert_ids.shape[0]
    n_tiles = n_assign // tm + N_EXPERTS
    n_slots = n_tiles * tm
    experts = jnp.arange(N_EXPERTS, dtype=jnp.int32)
    padded = (counts + tm - 1) // tm * tm
    pad_end = jnp.cumsum(padded)
    pad_start = pad_end - padded
    slot_of = jnp.sum(jnp.where(expert_ids[:, None] == experts[None, :], pad_start[None, :], 0), axis=1) + ranks
    tile_start = jnp.arange(n_tiles, dtype=jnp.int32) * tm
    tile_expert = jnp.minimum(
        jnp.sum((pad_end[None, :] <= tile_start[:, None]).astype(jnp.int32), axis=1), N_EXPERTS - 1)
    later = lax.cummin(jnp.where(counts > 0, experts, N_EXPERTS)[::-1])[::-1]
    following = jnp.concatenate([later[1:], jnp.full((1,), N_EXPERTS, jnp.int32)])[tile_expert]
    run_start = jnp.concatenate([jnp.ones((1,), bool), tile_expert[1:] != tile_expert[:-1]])
    run_next = jnp.where(run_start, jnp.where(following < N_EXPERTS, following, NO_NEXT_EXPERT),
                         NOT_RUN_START).astype(jnp.int32)
    token = jnp.arange(n_assign, dtype=jnp.int32) // TOP_K
    src = jnp.zeros((n_slots,), jnp.int32).at[slot_of].set(token)
    n_valid = (pad_end[-1] // tm).astype(jnp.int32).reshape(1)
    return slot_of, tile_expert, run_next, src, n_valid


def _final_kernel(slot_ref, h_ref, y_hbm, r_ref, gain_ref, bias_ref, o_ref, ybuf, sem, *, alpha):
    i = pl.program_id(0)
    tm = o_ref.shape[0]
    slot = i & 1

    def start_gather(step, dynamic_buf):
        def issue(buf):
            _row_copies(slot_ref, step * (TOP_K * tm), TOP_K * tm, lambda row, r: pltpu.make_async_copy(
                y_hbm.at[pl.ds(row, 1)], ybuf.at[buf, r % TOP_K, pl.ds(r // TOP_K, 1)], sem.at[buf]))
        _for_buffer(dynamic_buf, issue)

    @pl.when(i == 0)
    def _():
        start_gather(0, 0)

    @pl.when(i + 1 < pl.num_programs(0))
    def _():
        start_gather(i + 1, 1 - slot)

    for k in range(TOP_K):
        pltpu.make_async_copy(y_hbm.at[pl.ds(0, tm)], ybuf.at[slot, k], sem.at[slot]).wait()
    r = r_ref[...]
    ffn = ybuf[slot, 0] * r[:, 2:3] + ybuf[slot, 1] * r[:, 3:4]
    o_ref[...] = _layer_norm(alpha * h_ref[...] + ffn, gain_ref[...], bias_ref[...])


def _final(h, y_slots, slot_of, routing, gain, bias, alpha):
    n, dm = h.shape
    tm = min(FINAL_TM, n)
    tile = pl.BlockSpec((tm, dm), lambda i, *_: (i, 0))
    full = lambda a: pl.BlockSpec(a.shape, lambda i, *_: (0, 0))
    grid_spec = pltpu.PrefetchScalarGridSpec(
        num_scalar_prefetch=1, grid=(n // tm,),
        in_specs=[tile, pl.BlockSpec(memory_space=pl.ANY), pl.BlockSpec((tm, LANES), lambda i, *_: (i, 0)),
                  full(gain), full(bias)],
        out_specs=tile,
        scratch_shapes=[pltpu.VMEM((2, TOP_K, tm, dm), F32), pltpu.SemaphoreType.DMA((2,))])
    return pl.pallas_call(
        functools.partial(_final_kernel, alpha=alpha), grid_spec=grid_spec,
        out_shape=jax.ShapeDtypeStruct((n, dm), F32),
        compiler_params=_cparams(1), name="combine_ln",
    )(slot_of, h, y_slots, routing, gain, bias)


def _rope_frequencies():
    ret_freq = 1.0 / (RET_THETA ** (jnp.arange(0, RET_QK_DIM, 2, dtype=F32) / RET_QK_DIM))
    diff_freq = 1.0 / (ROPE_THETA ** (jnp.arange(0, ROPE_DIM, 2, dtype=F32) / ROPE_DIM))
    pad = jnp.zeros((LANES - ret_freq.shape[0] - diff_freq.shape[0],), F32)
    return jnp.concatenate([ret_freq, diff_freq, pad]).reshape(1, LANES)


def kernel(x, positions, w_in, ret_gn_gain, diff_lambda_q1, diff_lambda_k1, diff_lambda_q2, diff_lambda_k2, diff_subln_gain, w_ret_proj, w_diff_proj, w_out, ln1_gain, ln1_bias, w_group, b_group, w_router, b_router, w_expert_gate, w_expert_up, w_expert_down, ln2_gain, ln2_bias):
    batch, seq, dm = x.shape
    n = batch * seq
    depth = w_in.shape[0]
    assert seq % CHUNK == 0
    alpha = (2 * depth) ** 0.25

    ret_cos, ret_sin, diff_cos, diff_sin_lo, diff_sin_hi = _rope_tables(
        positions.reshape(n, 1), _rope_frequencies())

    log_gamma = jnp.log(1.0 - 2.0 ** (-5.0 - jnp.arange(RET_HEADS, dtype=F32)))
    block_decay = jnp.exp(log_gamma * min(RET_BLOCK, seq))

    xt = x.reshape(n, dm)
    for l in range(depth):
        lambda_init = 0.8 - 0.6 * math.exp(-0.3 * l)
        u = _matmul(xt, w_in[l], BF16, "in_proj")
        ret = _retention(u, ret_cos, ret_sin, log_gamma, block_decay,
                         ret_gn_gain[l].reshape(RET_HEADS, 1, RET_V_DIM), batch, seq)
        lam_params = jnp.stack([diff_lambda_q1[l], diff_lambda_k1[l], diff_lambda_q2[l], diff_lambda_k2[l]])
        diff = _diff_attention(u, diff_cos, diff_sin_lo, diff_sin_hi, lam_params.astype(F32),
                               diff_subln_gain[l].reshape(1, DIFF_V_DIM), batch, seq, lambda_init)
        merged = _merge(ret, diff, u, w_ret_proj[l].astype(BF16), w_diff_proj[l].astype(BF16))

        w_route = jnp.concatenate(
            [w_group[l], w_router[l], jnp.zeros((dm, LANES - N_GROUPS - N_EXPERTS), F32)], axis=1)
        b_route = jnp.concatenate(
            [b_group[l], b_router[l], jnp.zeros((LANES - N_GROUPS - N_EXPERTS,), F32)]).reshape(1, LANES)
        h, h_packed, routing, counts = _route(
            merged, xt, w_out[l].astype(BF16), ln1_gain[l].reshape(1, dm), ln1_bias[l].reshape(1, dm),
            w_route, b_route, alpha)

        expert_ids = routing[:, :TOP_K].astype(jnp.int32).reshape(-1)
        ranks = routing[:, 2 * TOP_K:3 * TOP_K].astype(jnp.int32).reshape(-1)
        slot_of, tile_expert, run_next, src, n_valid = _dispatch_tables(
            expert_ids, ranks, counts[0, :N_EXPERTS].astype(jnp.int32), min(MOE_TM, n))
        y_slots = _experts(h_packed, tile_expert, run_next, src, n_valid, w_expert_gate[l],
                           w_expert_up[l], w_expert_down[l])
        xt = _final(h, y_slots, slot_of, routing,
                    ln2_gain[l].reshape(1, dm), ln2_bias[l].reshape(1, dm), alpha)
    return xt.reshape(batch, seq, dm)
```

```python
import functools
import math

import jax
import jax.numpy as jnp
from jax import lax
from jax.experimental import pallas as pl
from jax.experimental.pallas import tpu as pltpu

F32 = jnp.float32
BF16 = jnp.bfloat16

CHUNK = 64
RET_HEADS = 8
RET_QK_DIM = 128
RET_V_DIM = 256
RET_THETA = 10000.0
DIFF_HEADS = 8
DIFF_QK_DIM = 128
DIFF_V_DIM = 256
ROPE_THETA = 500000.0
ROPE_DIM = DIFF_QK_DIM // 4
N_GROUPS = 4
EXPERTS_PER_GROUP = 8
N_EXPERTS = N_GROUPS * EXPERTS_PER_GROUP
TOP_K = 2
LN_EPS = 1e-5

LANES = 128
BF16_ROWS = 16
NEG_BIG = -1e30

VMEM_LIMIT_BYTES = 56 * 1024 * 1024
TABLE_ROWS = 512
PROJ_TM = 1024
PROJ_TN = 1024
RET_BLOCK = 256
RET_HEAD_GROUP = 8
ATT_BLOCK = 1024
MERGE_TM = 256
ROUTE_TM = 512
ROUTE_SPLITS = 2
MOE_TM = 256
DMA_GROUP = 8
WEIGHT_CAST_ROWS = 256
NOT_RUN_START = -2
NO_NEXT_EXPERT = -1
FINAL_TM = 256


def _cparams(n_axes):
    return pltpu.CompilerParams(
        dimension_semantics=("arbitrary",) * n_axes, vmem_limit_bytes=VMEM_LIMIT_BYTES)


def _sigmoid(x):
    return 1.0 / (1.0 + jnp.exp(-x))


def _chunk_of(idx):
    return jnp.right_shift(idx, CHUNK.bit_length() - 1)


def _dot(a, b):
    return jnp.dot(a, b, preferred_element_type=F32)


def _dot_nt(a, b):
    return lax.dot_general(a, b, (((1,), (1,)), ((), ())), preferred_element_type=F32)


def _table_kernel(pos_ref, freq_ref, rc_ref, rs_ref, dc_ref, dlo_ref, dhi_ref):
    rh, dh = RET_QK_DIM // 2, ROPE_DIM // 2
    ang = pos_ref[...].astype(F32) * freq_ref[...]
    c, s = jnp.cos(ang), jnp.sin(ang)
    lane = lax.broadcasted_iota(jnp.int32, ang.shape, 1)
    rc_ref[...] = jnp.where(lane < rh, c, pltpu.roll(c, rh, 1))
    rs_ref[...] = jnp.where(lane < rh, -s, pltpu.roll(s, rh, 1))
    c_lo, s_lo = pltpu.roll(c, LANES - rh, 1), pltpu.roll(s, LANES - rh, 1)
    c_hi, s_hi = pltpu.roll(c, LANES - rh + dh, 1), pltpu.roll(s, LANES - rh + dh, 1)
    dc_ref[...] = jnp.where(lane < dh, c_lo, jnp.where(lane < 2 * dh, c_hi, 1.0))
    dlo_ref[...] = jnp.where(lane < dh, -s_lo, 0.0)
    dhi_ref[...] = jnp.where((lane >= dh) & (lane < 2 * dh), s_hi, 0.0)


def _rope_tables(pos_col, freq_row):
    n = pos_col.shape[0]
    tr = min(TABLE_ROWS, n)
    out = pl.BlockSpec((tr, LANES), lambda i: (i, 0))
    shp = jax.ShapeDtypeStruct((n, LANES), F32)
    return pl.pallas_call(
        _table_kernel, grid=(n // tr,),
        in_specs=[pl.BlockSpec((tr, 1), lambda i: (i, 0)), pl.BlockSpec((1, LANES), lambda i: (0, 0))],
        out_specs=[out] * 5, out_shape=[shp] * 5,
        compiler_params=_cparams(1), name="rope_tables",
    )(pos_col, freq_row)


def _matmul_kernel(x_ref, w_ref, o_ref, wb_ref):
    @pl.when(pl.program_id(1) == 0)
    def _():
        wb_ref[...] = w_ref[...].astype(BF16)

    o_ref[...] = _dot(x_ref[...].astype(BF16), wb_ref[...]).astype(o_ref.dtype)


def _matmul(x, w, out_dtype, name):
    m, k = x.shape
    n = w.shape[1]
    tm, tn = min(PROJ_TM, m), min(PROJ_TN, n)
    return pl.pallas_call(
        _matmul_kernel, grid=(n // tn, m // tm),
        in_specs=[pl.BlockSpec((tm, k), lambda j, i: (i, 0)),
                  pl.BlockSpec((k, tn), lambda j, i: (0, j))],
        out_specs=pl.BlockSpec((tm, tn), lambda j, i: (i, j)),
        out_shape=jax.ShapeDtypeStruct((m, n), out_dtype),
        scratch_shapes=[pltpu.VMEM((k, tn), BF16)],
        compiler_params=_cparams(2), name=name,
    )(x, w)


def _retention_kernel(lg_ref, bd_ref, q_ref, k_ref, v_ref, g_ref, cos_ref, sin_ref, gain_ref,
                      o_ref, state_ref, decay_ref, qdecay_ref, kdecay_ref):
    group = state_ref.shape[0]
    first_head = pl.program_id(1) * group
    t = q_ref.shape[0]
    dk, dv = RET_QK_DIM, RET_V_DIM

    @pl.when(pl.program_id(2) == 0)
    def _():
        state_ref[...] = jnp.zeros_like(state_ref)
        ri = lax.broadcasted_iota(jnp.int32, (t, t), 0)
        ci = lax.broadcasted_iota(jnp.int32, (t, t), 1)
        dist = jnp.abs(ri - ci).astype(F32)
        visible = _chunk_of(ri) >= _chunk_of(ci)
        row = lax.broadcasted_iota(jnp.int32, (t, dk), 0).astype(F32)
        for j in range(group):
            lg = lg_ref[first_head + j]
            decay_ref[j] = jnp.where(visible, jnp.exp(lg * dist), 0.0)
            qdecay_ref[j] = jnp.exp(lg * (row + 1.0))
            kdecay_ref[j] = jnp.exp(lg * (t - 1.0 - row))

    cos, sin = cos_ref[...], sin_ref[...]

    def rot(x):
        return x * cos + pltpu.roll(x, dk // 2, 1) * sin

    for j in range(group):
        q = rot(q_ref[:, j * dk:(j + 1) * dk].astype(F32))
        k = rot(k_ref[:, j * dk:(j + 1) * dk].astype(F32)) * (dk ** -0.5)
        v = v_ref[:, j * dv:(j + 1) * dv]

        s = _dot_nt(q.astype(BF16), k.astype(BF16)) * decay_ref[j]
        intra = _dot(s.astype(BF16), v)

        state = state_ref[j]
        cross = _dot((q * qdecay_ref[j]).astype(BF16), state.astype(BF16))
        kd = k * kdecay_ref[j]
        state_ref[j] = state * bd_ref[first_head + j] + _dot(kd.T.astype(BF16), v)

        out = intra + cross
        mu = jnp.mean(out, axis=-1, keepdims=True)
        var = jnp.mean(jnp.square(out - mu), axis=-1, keepdims=True)
        y = (out - mu) * lax.rsqrt(var + LN_EPS) * gain_ref[j]
        gate = g_ref[:, j * dv:(j + 1) * dv].astype(F32)
        o_ref[:, j * dv:(j + 1) * dv] = (y * (gate * _sigmoid(gate))).astype(o_ref.dtype)


def _retention(u, cos, sin, log_gamma, block_decay, gain, batch, seq):
    t = min(RET_BLOCK, seq)
    nb = seq // t
    grp = RET_HEAD_GROUP
    n_groups = RET_HEADS // grp
    rows = lambda b, h, i: b * nb + i
    qk_w, v_w = grp * RET_QK_DIM, grp * RET_V_DIM
    k_off, v_off, g_off = n_groups, n_groups, 2 * n_groups
    smem = pl.BlockSpec(memory_space=pltpu.SMEM)
    return pl.pallas_call(
        _retention_kernel, grid=(batch, n_groups, nb),
        in_specs=[
            smem, smem,
            pl.BlockSpec((t, qk_w), lambda b, h, i: (rows(b, h, i), h)),
            pl.BlockSpec((t, qk_w), lambda b, h, i: (rows(b, h, i), k_off + h)),
            pl.BlockSpec((t, v_w), lambda b, h, i: (rows(b, h, i), v_off + h)),
            pl.BlockSpec((t, v_w), lambda b, h, i: (rows(b, h, i), g_off + h)),
            pl.BlockSpec((t, LANES), lambda b, h, i: (rows(b, h, i), 0)),
            pl.BlockSpec((t, LANES), lambda b, h, i: (rows(b, h, i), 0)),
            pl.BlockSpec((grp, 1, RET_V_DIM), lambda b, h, i: (h, 0, 0)),
        ],
        out_specs=pl.BlockSpec((t, v_w), lambda b, h, i: (rows(b, h, i), h)),
        out_shape=jax.ShapeDtypeStruct((batch * seq, RET_HEADS * RET_V_DIM), BF16),
        scratch_shapes=[pltpu.VMEM((grp, RET_QK_DIM, RET_V_DIM), F32), pltpu.VMEM((grp, t, t), F32),
                        pltpu.VMEM((grp, t, RET_QK_DIM), F32), pltpu.VMEM((grp, t, RET_QK_DIM), F32)],
        compiler_params=_cparams(3), name="retention",
    )(log_gamma, block_decay, u, u, u, u, cos, sin, gain)


def _partial_rope(x, cos, sin_lo, sin_hi):
    half = ROPE_DIM // 2
    return (x * cos + pltpu.roll(x, DIFF_QK_DIM - half, 1) * sin_lo
            + pltpu.roll(x, half, 1) * sin_hi)


def _diff_attn_kernel(lam_ref, q_ref, k_ref, v_ref, qc_ref, qa_ref, qb_ref, kc_ref, ka_ref, kb_ref,
                      gain_ref, o_ref, krot_ref, vt_ref, m_ref, acc_ref, sa_ref, sb_ref, *,
                      lambda_init):
    qblk = pl.program_id(2)
    tq = q_ref.shape[0]
    d = DIFF_QK_DIM

    def key_rows(j):
        return pl.ds(pl.multiple_of(j * tq, tq), tq)

    @pl.when(qblk == 0)
    def _():
        def prep_rows(c, carry):
            rows = key_rows(c)
            for i in range(2):
                x = k_ref[rows, i * d:(i + 1) * d].astype(F32)
                krot_ref[rows, i * d:(i + 1) * d] = _partial_rope(
                    x, kc_ref[rows, :], ka_ref[rows, :], kb_ref[rows, :]).astype(BF16)
            vt_ref[c, :DIFF_V_DIM] = v_ref[rows, :].astype(F32).T.astype(BF16)
            vt_ref[c, DIFF_V_DIM:] = jnp.ones((BF16_ROWS, tq), BF16)
            return carry
        lax.fori_loop(0, k_ref.shape[0] // tq, prep_rows, 0)

    scale = d ** -0.5
    qs = []
    for i in range(2):
        x = q_ref[:, i * d:(i + 1) * d].astype(F32)
        qs.append((_partial_rope(x, qc_ref[...], qa_ref[...], qb_ref[...]) * scale).astype(BF16))

    m_ref[...] = jnp.full_like(m_ref, NEG_BIG)
    acc_ref[...] = jnp.zeros_like(acc_ref)

    s_bufs = (sa_ref, sb_ref)

    def scores(j, buf, diagonal):
        for i in range(2):
            s = _dot_nt(krot_ref[key_rows(j), i * d:(i + 1) * d], qs[i])
            if diagonal:
                key = lax.broadcasted_iota(jnp.int32, s.shape, 0)
                qry = lax.broadcasted_iota(jnp.int32, s.shape, 1)
                s = jnp.where(_chunk_of(qry) >= _chunk_of(key), s, NEG_BIG)
            s_bufs[buf][i] = s

    def absorb(j, buf):
        vt = vt_ref[j]
        for i in range(2):
            s = s_bufs[buf][i]
            m_old = m_ref[i]
            m_new = jnp.maximum(m_old, jnp.max(s, axis=0, keepdims=True))
            alpha = jnp.exp(m_old - m_new)
            p = jnp.exp(s - m_new)
            acc_ref[i] = alpha * acc_ref[i] + _dot(vt, p.astype(BF16))
            m_ref[i] = m_new

    scores(qblk, 0, True)

    def pair(p, carry):
        pending = jnp.where(p == 0, qblk, 2 * p - 1)
        scores(2 * p, 1, False)
        absorb(pending, 0)
        scores(2 * p + 1, 0, False)
        absorb(2 * p, 1)
        return carry
    n_pairs = qblk // 2
    lax.fori_loop(0, n_pairs, pair, 0)
    pending = jnp.where(n_pairs == 0, qblk, 2 * n_pairs - 1)

    @pl.when(qblk % 2 == 1)
    def _():
        scores(qblk - 1, 1, False)
        absorb(pending, 0)
        absorb(qblk - 1, 1)

    @pl.when(qblk % 2 == 0)
    def _():
        absorb(pending, 0)

    lp = lam_ref[...]
    lam = (jnp.exp(jnp.sum(lp[0:1] * lp[1:2], axis=-1, keepdims=True))
           - jnp.exp(jnp.sum(lp[2:3] * lp[3:4], axis=-1, keepdims=True)) + lambda_init)
    dv = DIFF_V_DIM
    o = (acc_ref[0, :dv] / acc_ref[0, dv:dv + 1] - lam * (acc_ref[1, :dv] / acc_ref[1, dv:dv + 1])).T
    o = o * lax.rsqrt(jnp.mean(jnp.square(o), axis=-1, keepdims=True) + LN_EPS) * gain_ref[...]
    o_ref[...] = (o * (1.0 - lambda_init)).astype(o_ref.dtype)


def _diff_attention(u, cos, sin_lo, sin_hi, lam_params, gain, batch, seq, lambda_init):
    tq = min(ATT_BLOCK, seq)
    nb = seq // tq
    width = 2 * DIFF_QK_DIM
    ret_cols = 2 * RET_HEADS * RET_QK_DIM + 2 * RET_HEADS * RET_V_DIM
    q_off = ret_cols // width
    k_off = q_off + DIFF_HEADS
    v_off = k_off + DIFF_HEADS
    qrow = lambda b, h, i: b * nb + i
    qtab = pl.BlockSpec((tq, LANES), lambda b, h, i: (qrow(b, h, i), 0))
    ktab = pl.BlockSpec((seq, LANES), lambda b, h, i: (b, 0), pipeline_mode=pl.Buffered(1))
    return pl.pallas_call(
        functools.partial(_diff_attn_kernel, lambda_init=lambda_init),
        grid=(batch, DIFF_HEADS, nb),
        in_specs=[
            pl.BlockSpec((4, DIFF_QK_DIM), lambda b, h, i: (0, 0)),
            pl.BlockSpec((tq, width), lambda b, h, i: (qrow(b, h, i), q_off + h)),
            pl.BlockSpec((seq, width), lambda b, h, i: (b, k_off + h)),
            pl.BlockSpec((seq, DIFF_V_DIM), lambda b, h, i: (b, v_off + h)),
            qtab, qtab, qtab, ktab, ktab, ktab,
            pl.BlockSpec((1, DIFF_V_DIM), lambda b, h, i: (0, 0)),
        ],
        out_specs=pl.BlockSpec((tq, DIFF_V_DIM), lambda b, h, i: (qrow(b, h, i), h)),
        out_shape=jax.ShapeDtypeStruct((batch * seq, DIFF_HEADS * DIFF_V_DIM), BF16),
        scratch_shapes=[pltpu.VMEM((seq, width), BF16),
                        pltpu.VMEM((nb, DIFF_V_DIM + BF16_ROWS, tq), BF16),
                        pltpu.VMEM((2, 1, tq), F32),
                        pltpu.VMEM((2, DIFF_V_DIM + BF16_ROWS, tq), F32),
                        pltpu.VMEM((2, tq, tq), F32), pltpu.VMEM((2, tq, tq), F32)],
        compiler_params=_cparams(3), name="diff_attention",
    )(lam_params, u, u, u, cos, sin_lo, sin_hi, cos, sin_lo, sin_hi, gain)


def _merge_kernel(ret_ref, diff_ref, gr_ref, gd_ref, wr_ref, wd_ref, o_ref):
    a = _dot(ret_ref[...], wr_ref[...])
    b = _dot(diff_ref[...], wd_ref[...])
    merged = _sigmoid(gr_ref[...].astype(F32)) * a + _sigmoid(gd_ref[...].astype(F32)) * b
    o_ref[...] = merged.astype(o_ref.dtype)


def _merge(ret, diff, u, w_ret, w_diff):
    n, dm = ret.shape[0], w_ret.shape[1]
    tm = min(MERGE_TM, n)
    gate_off = (u.shape[1] - 2 * dm) // dm
    tile = lambda c: pl.BlockSpec((tm, dm), lambda i: (i, c))
    weight = lambda w: pl.BlockSpec(w.shape, lambda i: (0, 0))
    return pl.pallas_call(
        _merge_kernel, grid=(n // tm,),
        in_specs=[pl.BlockSpec((tm, ret.shape[1]), lambda i: (i, 0)),
                  pl.BlockSpec((tm, diff.shape[1]), lambda i: (i, 0)),
                  tile(gate_off), tile(gate_off + 1), weight(w_ret), weight(w_diff)],
        out_specs=tile(0), out_shape=jax.ShapeDtypeStruct((n, dm), BF16),
        compiler_params=_cparams(1), name="gated_merge",
    )(ret, diff, u, u, w_ret, w_diff)


def _layer_norm(z, gain, bias):
    mu = jnp.mean(z, axis=-1, keepdims=True)
    var = jnp.mean(jnp.square(z - mu), axis=-1, keepdims=True)
    return (z - mu) * lax.rsqrt(var + LN_EPS) * gain + bias


def _split_bf16(a):
    hi = a.astype(BF16)
    return hi, (a - hi.astype(F32)).astype(BF16)


def _route_kernel(m_ref, x_ref, wo_ref, gain_ref, bias_ref, wr_ref, br_ref, h_ref, r_ref, count_ref,
                  *, alpha):
    @pl.when(pl.program_id(0) == 0)
    def _():
        count_ref[...] = jnp.zeros_like(count_ref)

    w_hi, w_lo = _split_bf16(wr_ref[...])
    part = h_ref.shape[0] // ROUTE_SPLITS
    for g in range(ROUTE_SPLITS):
        rows = slice(g * part, (g + 1) * part)
        h = _layer_norm(alpha * x_ref[rows, :] + _dot(m_ref[rows, :], wo_ref[...]),
                        gain_ref[...], bias_ref[...])
        h_ref[rows, :] = h
        h_hi, h_lo = _split_bf16(h)
        logits = _dot(h_hi, w_hi) + (_dot(h_hi, w_lo) + _dot(h_lo, w_hi)) + br_ref[...]
        r_ref[rows, :] = _routing_rows(logits, count_ref)


def _routing_rows(logits, count_ref):
    lane = lax.broadcasted_iota(jnp.int32, logits.shape, 1).astype(F32)

    def first_argmax(vals):
        top = jnp.max(vals, axis=-1, keepdims=True)
        idx = jnp.min(jnp.where(vals == top, lane, float(LANES)), axis=-1, keepdims=True)
        return top, idx

    group_logits = jnp.where(lane < N_GROUPS, logits, NEG_BIG)
    g_top, g_idx = first_argmax(group_logits)
    p_group = 1.0 / jnp.sum(jnp.exp(group_logits - g_top), axis=-1, keepdims=True)

    first = N_GROUPS + g_idx * EXPERTS_PER_GROUP
    within = jnp.where((lane >= first) & (lane < first + EXPERTS_PER_GROUP), logits, NEG_BIG)
    v1, i1 = first_argmax(within)
    v2, i2 = first_argmax(jnp.where(lane == i1, NEG_BIG, within))
    e2 = jnp.exp(v2 - v1)
    w1 = p_group / (1.0 + e2)
    w2 = p_group * e2 / (1.0 + e2)
    x1, x2 = i1 - N_GROUPS, i2 - N_GROUPS

    hit1 = jnp.where(lane == x1, 1.0, 0.0)
    hit2 = jnp.where(lane == x2, 1.0, 0.0)
    hits = hit1 + hit2
    tm = hits.shape[0]
    earlier = (lax.broadcasted_iota(jnp.int32, (tm, tm), 0) > lax.broadcasted_iota(jnp.int32, (tm, tm), 1))
    before = count_ref[...] + _dot(jnp.where(earlier, 1.0, 0.0).astype(BF16), hits.astype(BF16))
    rank1 = jnp.sum(before * hit1, axis=-1, keepdims=True)
    rank2 = jnp.sum(before * hit2, axis=-1, keepdims=True)
    count_ref[...] += jnp.sum(hits, axis=0, keepdims=True)

    fields = (x1, x2, w1, w2, rank1, rank2)
    row = jnp.zeros_like(logits)
    for pos, value in enumerate(fields):
        row = jnp.where(lane == pos, value, row)
    return row


def _route(merged, x, w_out, gain, bias, w_router, b_router, alpha):
    n, dm = x.shape
    tm = min(ROUTE_TM, n)
    tile = pl.BlockSpec((tm, dm), lambda i: (i, 0))
    full = lambda a: pl.BlockSpec(a.shape, lambda i: (0, 0))
    return pl.pallas_call(
        functools.partial(_route_kernel, alpha=alpha), grid=(n // tm,),
        in_specs=[tile, tile, full(w_out), full(gain), full(bias), full(w_router), full(b_router)],
        out_specs=[tile, pl.BlockSpec((tm, LANES), lambda i: (i, 0)),
                   pl.BlockSpec((1, LANES), lambda i: (0, 0))],
        out_shape=[jax.ShapeDtypeStruct((n, dm), F32), jax.ShapeDtypeStruct((n, LANES), F32),
                   jax.ShapeDtypeStruct((1, LANES), F32)],
        compiler_params=_cparams(1), name="outproj_ln_router",
    )(merged, x, w_out, gain, bias, w_router, b_router)


def _row_copies(table_ref, first, n_rows, make_copy):
    for g in range(n_rows // DMA_GROUP):
        entries = [table_ref[first + g * DMA_GROUP + k] for k in range(DMA_GROUP)]
        for k, entry in enumerate(entries):
            make_copy(entry, g * DMA_GROUP + k).start()


def _for_buffer(dynamic_buf, fn):
    for buf in range(2):
        pl.when(dynamic_buf == buf)(functools.partial(fn, buf))


def _expert_kernel(tile_expert_ref, run_next_ref, src_ref, n_valid_ref, h_hbm, wg_hbm, wu_hbm, wd_hbm,
                   y_ref, xbuf, stage_g, stage_u, stage_d, wg_ref, wu_ref, wd_ref, gather_sem,
                   weight_sem):
    i = pl.program_id(0)
    n_valid = n_valid_ref[0]
    tm = xbuf.shape[1]
    slot = i & 1
    weights = ((wg_hbm, stage_g, wg_ref), (wu_hbm, stage_u, wu_ref), (wd_hbm, stage_d, wd_ref))

    def start_weights(expert):
        for k, (hbm, stage, _) in enumerate(weights):
            pltpu.make_async_copy(hbm.at[expert], stage, weight_sem.at[k]).start(priority=1)

    def take_weights():
        for k, (hbm, stage, dst) in enumerate(weights):
            pltpu.make_async_copy(hbm.at[0], stage, weight_sem.at[k]).wait()

            def round_rows(c, carry, stage=stage, dst=dst):
                rows = pl.ds(pl.multiple_of(c * WEIGHT_CAST_ROWS, WEIGHT_CAST_ROWS), WEIGHT_CAST_ROWS)
                dst[rows, :] = stage[rows, :].astype(BF16)
                return carry
            lax.fori_loop(0, stage.shape[0] // WEIGHT_CAST_ROWS, round_rows, 0)

    def start_gather(tile, dynamic_buf):
        def issue(buf):
            _row_copies(src_ref, tile * tm, tm, lambda tok, r: pltpu.make_async_copy(
                h_hbm.at[pl.ds(tok, 1)], xbuf.at[buf, pl.ds(r, 1)], gather_sem.at[buf]))
        _for_buffer(dynamic_buf, issue)

    @pl.when(i == 0)
    def _():
        start_gather(0, 0)
        start_weights(tile_expert_ref[0])

    @pl.when(i + 1 < n_valid)
    def _():
        start_gather(i + 1, 1 - slot)

    @pl.when(i < n_valid)
    def _():
        run_next = run_next_ref[i]

        @pl.when(run_next != NOT_RUN_START)
        def _():
            take_weights()

            @pl.when(run_next != NO_NEXT_EXPERT)
            def _():
                start_weights(run_next)

        pltpu.make_async_copy(h_hbm.at[pl.ds(0, tm)], xbuf.at[slot], gather_sem.at[slot]).wait()
        x = xbuf[slot].astype(BF16)
        g = _dot(x, wg_ref[...])
        act = (g * _sigmoid(g) * _dot(x, wu_ref[...])).astype(BF16)
        y_ref[...] = _dot(act, wd_ref[...])

    @pl.when(i >= n_valid)
    def _():
        y_ref[...] = jnp.zeros_like(y_ref)


def _experts(h, tile_expert, run_next, src, n_valid, w_gate, w_up, w_down):
    dm = h.shape[1]
    n_tiles = tile_expert.shape[0]
    tm = src.shape[0] // n_tiles
    ff = w_gate.shape[2]
    hbm = pl.BlockSpec(memory_space=pl.ANY)
    grid_spec = pltpu.PrefetchScalarGridSpec(
        num_scalar_prefetch=4, grid=(n_tiles,),
        in_specs=[hbm, hbm, hbm, hbm],
        out_specs=pl.BlockSpec((tm, dm), lambda i, *_: (i, 0)),
        scratch_shapes=[pltpu.VMEM((2, tm, dm), F32),
                        pltpu.VMEM((dm, ff), F32), pltpu.VMEM((dm, ff), F32), pltpu.VMEM((ff, dm), F32),
                        pltpu.VMEM((dm, ff), BF16), pltpu.VMEM((dm, ff), BF16), pltpu.VMEM((ff, dm), BF16),
                        pltpu.SemaphoreType.DMA((2,)), pltpu.SemaphoreType.DMA((3,))])
    return pl.pallas_call(
        _expert_kernel, grid_spec=grid_spec,
        out_shape=jax.ShapeDtypeStruct((src.shape[0], dm), F32),
        compiler_params=_cparams(1), name="experts",
    )(tile_expert, run_next, src, n_valid, h, w_gate, w_up, w_down)


def _dispatch_tables(expert_ids, ranks, counts, tm):
    n_assign = expert_ids.shape[0]
    n_tiles = n_assign // tm + N_EXPERTS
    n_slots = n_tiles * tm
    experts = jnp.arange(N_EXPERTS, dtype=jnp.int32)
    padded = (counts + tm - 1) // tm * tm
    pad_end = jnp.cumsum(padded)
    pad_start = pad_end - padded
    slot_of = jnp.sum(jnp.where(expert_ids[:, None] == experts[None, :], pad_start[None, :], 0), axis=1) + ranks
    tile_start = jnp.arange(n_tiles, dtype=jnp.int32) * tm
    tile_expert = jnp.minimum(
        jnp.sum((pad_end[None, :] <= tile_start[:, None]).astype(jnp.int32), axis=1), N_EXPERTS - 1)
    later = lax.cummin(jnp.where(counts > 0, experts, N_EXPERTS)[::-1])[::-1]
    following = jnp.concatenate([later[1:], jnp.full((1,), N_EXPERTS, jnp.int32)])[tile_expert]
    run_start = jnp.concatenate([jnp.ones((1,), bool), tile_expert[1:] != tile_expert[:-1]])
    run_next = jnp.where(run_start, jnp.where(following < N_EXPERTS, following, NO_NEXT_EXPERT),
                         NOT_RUN_START).astype(jnp.int32)
    token = jnp.arange(n_assign, dtype=jnp.int32) // TOP_K
    by_slot = token[jnp.argsort(slot_of)]
    dense_start = jnp.cumsum(counts) - counts
    slot_expert = jnp.repeat(tile_expert, tm)
    dense_index = dense_start[slot_expert] + jnp.arange(n_slots, dtype=jnp.int32) - pad_start[slot_expert]
    src = by_slot[jnp.clip(dense_index, 0, n_assign - 1)]
    n_valid = (pad_end[-1] // tm).astype(jnp.int32).reshape(1)
    return slot_of, tile_expert, run_next, src, n_valid


def _final_kernel(slot_ref, h_ref, y_hbm, r_ref, gain_ref, bias_ref, o_ref, ybuf, sem, *, alpha):
    i = pl.program_id(0)
    tm = o_ref.shape[0]
    slot = i & 1

    def start_gather(step, dynamic_buf):
        def issue(buf):
            _row_copies(slot_ref, step * (TOP_K * tm), TOP_K * tm, lambda row, r: pltpu.make_async_copy(
                y_hbm.at[pl.ds(row, 1)], ybuf.at[buf, r % TOP_K, pl.ds(r // TOP_K, 1)], sem.at[buf]))
        _for_buffer(dynamic_buf, issue)

    @pl.when(i == 0)
    def _():
        start_gather(0, 0)

    @pl.when(i + 1 < pl.num_programs(0))
    def _():
        start_gather(i + 1, 1 - slot)

    for k in range(TOP_K):
        pltpu.make_async_copy(y_hbm.at[pl.ds(0, tm)], ybuf.at[slot, k], sem.at[slot]).wait()
    r = r_ref[...]
    ffn = ybuf[slot, 0] * r[:, 2:3] + ybuf[slot, 1] * r[:, 3:4]
    o_ref[...] = _layer_norm(alpha * h_ref[...] + ffn, gain_ref[...], bias_ref[...])


def _final(h, y_slots, slot_of, routing, gain, bias, alpha):
    n, dm = h.shape
    tm = min(FINAL_TM, n)
    tile = pl.BlockSpec((tm, dm), lambda i, *_: (i, 0))
    full = lambda a: pl.BlockSpec(a.shape, lambda i, *_: (0, 0))
    grid_spec = pltpu.PrefetchScalarGridSpec(
        num_scalar_prefetch=1, grid=(n // tm,),
        in_specs=[tile, pl.BlockSpec(memory_space=pl.ANY), pl.BlockSpec((tm, LANES), lambda i, *_: (i, 0)),
                  full(gain), full(bias)],
        out_specs=tile,
        scratch_shapes=[pltpu.VMEM((2, TOP_K, tm, dm), F32), pltpu.SemaphoreType.DMA((2,))])
    return pl.pallas_call(
        functools.partial(_final_kernel, alpha=alpha), grid_spec=grid_spec,
        out_shape=jax.ShapeDtypeStruct((n, dm), F32),
        compiler_params=_cparams(1), name="combine_ln",
    )(slot_of, h, y_slots, routing, gain, bias)


def _rope_frequencies():
    ret_freq = 1.0 / (RET_THETA ** (jnp.arange(0, RET_QK_DIM, 2, dtype=F32) / RET_QK_DIM))
    diff_freq = 1.0 / (ROPE_THETA ** (jnp.arange(0, ROPE_DIM, 2, dtype=F32) / ROPE_DIM))
    pad = jnp.zeros((LANES - ret_freq.shape[0] - diff_freq.shape[0],), F32)
    return jnp.concatenate([ret_freq, diff_freq, pad]).reshape(1, LANES)


def kernel(x, positions, w_in, ret_gn_gain, diff_lambda_q1, diff_lambda_k1, diff_lambda_q2, diff_lambda_k2, diff_subln_gain, w_ret_proj, w_diff_proj, w_out, ln1_gain, ln1_bias, w_group, b_group, w_router, b_router, w_expert_gate, w_expert_up, w_expert_down, ln2_gain, ln2_bias):
    batch, seq, dm = x.shape
    n = batch * seq
    depth = w_in.shape[0]
    assert seq % CHUNK == 0
    alpha = (2 * depth) ** 0.25

    ret_cos, ret_sin, diff_cos, diff_sin_lo, diff_sin_hi = _rope_tables(
        positions.reshape(n, 1), _rope_frequencies())

    log_gamma = jnp.log(1.0 - 2.0 ** (-5.0 - jnp.arange(RET_HEADS, dtype=F32)))
    block_decay = jnp.exp(log_gamma * min(RET_BLOCK, seq))

    xt = x.reshape(n, dm)
    for l in range(depth):
        lambda_init = 0.8 - 0.6 * math.exp(-0.3 * l)
        u = _matmul(xt, w_in[l], BF16, "in_proj")
        ret = _retention(u, ret_cos, ret_sin, log_gamma, block_decay,
                         ret_gn_gain[l].reshape(RET_HEADS, 1, RET_V_DIM), batch, seq)
        lam_params = jnp.stack([diff_lambda_q1[l], diff_lambda_k1[l], diff_lambda_q2[l], diff_lambda_k2[l]])
        diff = _diff_attention(u, diff_cos, diff_sin_lo, diff_sin_hi, lam_params.astype(F32),
                               diff_subln_gain[l].reshape(1, DIFF_V_DIM), batch, seq, lambda_init)
        merged = _merge(ret, diff, u, w_ret_proj[l].astype(BF16), w_diff_proj[l].astype(BF16))

        w_route = jnp.concatenate(
            [w_group[l], w_router[l], jnp.zeros((dm, LANES - N_GROUPS - N_EXPERTS), F32)], axis=1)
        b_route = jnp.concatenate(
            [b_group[l], b_router[l], jnp.zeros((LANES - N_GROUPS - N_EXPERTS,), F32)]).reshape(1, LANES)
        h, routing, counts = _route(merged, xt, w_out[l].astype(BF16), ln1_gain[l].reshape(1, dm),
                                    ln1_bias[l].reshape(1, dm), w_route, b_route, alpha)

        expert_ids = routing[:, :TOP_K].astype(jnp.int32).reshape(-1)
        ranks = routing[:, 2 * TOP_K:3 * TOP_K].astype(jnp.int32).reshape(-1)
        slot_of, tile_expert, run_next, src, n_valid = _dispatch_tables(
            expert_ids, ranks, counts[0, :N_EXPERTS].astype(jnp.int32), min(MOE_TM, n))
        y_slots = _experts(h, tile_expert, run_next, src, n_valid, w_expert_gate[l], w_expert_up[l],
                           w_expert_down[l])
        xt = _final(h, y_slots, slot_of, routing,
                    ln2_gain[l].reshape(1, dm), ln2_bias[l].reshape(1, dm), alpha)
    return xt.reshape(batch, seq, dm)
```

```python
import functools
import math

import jax
import jax.numpy as jnp
from jax import lax
from jax.experimental import pallas as pl
from jax.experimental.pallas import tpu as pltpu

F32 = jnp.float32
BF16 = jnp.bfloat16

CHUNK = 64
RET_HEADS = 8
RET_QK_DIM = 128
RET_V_DIM = 256
RET_THETA = 10000.0
DIFF_HEADS = 8
DIFF_QK_DIM = 128
DIFF_V_DIM = 256
ROPE_THETA = 500000.0
ROPE_DIM = DIFF_QK_DIM // 4
N_GROUPS = 4
EXPERTS_PER_GROUP = 8
N_EXPERTS = N_GROUPS * EXPERTS_PER_GROUP
TOP_K = 2
LN_EPS = 1e-5

LANES = 128
BF16_ROWS = 16
NEG_BIG = -1e30

VMEM_LIMIT_BYTES = 56 * 1024 * 1024
TABLE_ROWS = 512
PROJ_TM = 1024
PROJ_TN = 1024
RET_BLOCK = 256
RET_HEAD_GROUP = 8
ATT_BLOCK = 1024
ROUTE_TM = 256
ROUTE_SPLITS = 1
MOE_TM = 256
DMA_GROUP = 8
WEIGHT_CAST_ROWS = 256
NOT_RUN_START = -2
NO_NEXT_EXPERT = -1
FINAL_TM = 256


def _cparams(n_axes):
    return pltpu.CompilerParams(
        dimension_semantics=("arbitrary",) * n_axes, vmem_limit_bytes=VMEM_LIMIT_BYTES)


def _sigmoid(x):
    return 1.0 / (1.0 + jnp.exp(-x))


def _chunk_of(idx):
    return jnp.right_shift(idx, CHUNK.bit_length() - 1)


def _dot(a, b):
    return jnp.dot(a, b, preferred_element_type=F32)


def _dot_nt(a, b):
    return lax.dot_general(a, b, (((1,), (1,)), ((), ())), preferred_element_type=F32)


def _table_kernel(pos_ref, freq_ref, rc_ref, rs_ref, dc_ref, dlo_ref, dhi_ref):
    rh, dh = RET_QK_DIM // 2, ROPE_DIM // 2
    ang = pos_ref[...].astype(F32) * freq_ref[...]
    c, s = jnp.cos(ang), jnp.sin(ang)
    lane = lax.broadcasted_iota(jnp.int32, ang.shape, 1)
    rc_ref[...] = jnp.where(lane < rh, c, pltpu.roll(c, rh, 1))
    rs_ref[...] = jnp.where(lane < rh, -s, pltpu.roll(s, rh, 1))
    c_lo, s_lo = pltpu.roll(c, LANES - rh, 1), pltpu.roll(s, LANES - rh, 1)
    c_hi, s_hi = pltpu.roll(c, LANES - rh + dh, 1), pltpu.roll(s, LANES - rh + dh, 1)
    dc_ref[...] = jnp.where(lane < dh, c_lo, jnp.where(lane < 2 * dh, c_hi, 1.0))
    dlo_ref[...] = jnp.where(lane < dh, -s_lo, 0.0)
    dhi_ref[...] = jnp.where((lane >= dh) & (lane < 2 * dh), s_hi, 0.0)


def _rope_tables(pos_col, freq_row):
    n = pos_col.shape[0]
    tr = min(TABLE_ROWS, n)
    out = pl.BlockSpec((tr, LANES), lambda i: (i, 0))
    shp = jax.ShapeDtypeStruct((n, LANES), F32)
    return pl.pallas_call(
        _table_kernel, grid=(n // tr,),
        in_specs=[pl.BlockSpec((tr, 1), lambda i: (i, 0)), pl.BlockSpec((1, LANES), lambda i: (0, 0))],
        out_specs=[out] * 5, out_shape=[shp] * 5,
        compiler_params=_cparams(1), name="rope_tables",
    )(pos_col, freq_row)


def _matmul_kernel(x_ref, w_ref, o_ref, wb_ref):
    @pl.when(pl.program_id(1) == 0)
    def _():
        wb_ref[...] = w_ref[...].astype(BF16)

    o_ref[...] = _dot(x_ref[...].astype(BF16), wb_ref[...]).astype(o_ref.dtype)


def _matmul(x, w, out_dtype, name):
    m, k = x.shape
    n = w.shape[1]
    tm, tn = min(PROJ_TM, m), min(PROJ_TN, n)
    return pl.pallas_call(
        _matmul_kernel, grid=(n // tn, m // tm),
        in_specs=[pl.BlockSpec((tm, k), lambda j, i: (i, 0)),
                  pl.BlockSpec((k, tn), lambda j, i: (0, j))],
        out_specs=pl.BlockSpec((tm, tn), lambda j, i: (i, j)),
        out_shape=jax.ShapeDtypeStruct((m, n), out_dtype),
        scratch_shapes=[pltpu.VMEM((k, tn), BF16)],
        compiler_params=_cparams(2), name=name,
    )(x, w)


def _retention_kernel(lg_ref, bd_ref, q_ref, k_ref, v_ref, g_ref, cos_ref, sin_ref, gain_ref,
                      o_ref, state_ref, decay_ref, qdecay_ref, kdecay_ref):
    group = state_ref.shape[0]
    first_head = pl.program_id(1) * group
    t = q_ref.shape[0]
    dk, dv = RET_QK_DIM, RET_V_DIM

    @pl.when(pl.program_id(2) == 0)
    def _():
        state_ref[...] = jnp.zeros_like(state_ref)
        ri = lax.broadcasted_iota(jnp.int32, (t, t), 0)
        ci = lax.broadcasted_iota(jnp.int32, (t, t), 1)
        dist = jnp.abs(ri - ci).astype(F32)
        visible = _chunk_of(ri) >= _chunk_of(ci)
        row = lax.broadcasted_iota(jnp.int32, (t, dk), 0).astype(F32)
        for j in range(group):
            lg = lg_ref[first_head + j]
            decay_ref[j] = jnp.where(visible, jnp.exp(lg * dist), 0.0)
            qdecay_ref[j] = jnp.exp(lg * (row + 1.0))
            kdecay_ref[j] = jnp.exp(lg * (t - 1.0 - row))

    cos, sin = cos_ref[...], sin_ref[...]

    def rot(x):
        return x * cos + pltpu.roll(x, dk // 2, 1) * sin

    for j in range(group):
        q = rot(q_ref[:, j * dk:(j + 1) * dk].astype(F32))
        k = rot(k_ref[:, j * dk:(j + 1) * dk].astype(F32)) * (dk ** -0.5)
        v = v_ref[:, j * dv:(j + 1) * dv]

        s = _dot_nt(q.astype(BF16), k.astype(BF16)) * decay_ref[j]
        intra = _dot(s.astype(BF16), v)

        state = state_ref[j]
        cross = _dot((q * qdecay_ref[j]).astype(BF16), state.astype(BF16))
        kd = k * kdecay_ref[j]
        state_ref[j] = state * bd_ref[first_head + j] + _dot(kd.T.astype(BF16), v)

        out = intra + cross
        mu = jnp.mean(out, axis=-1, keepdims=True)
        var = jnp.mean(jnp.square(out - mu), axis=-1, keepdims=True)
        y = (out - mu) * lax.rsqrt(var + LN_EPS) * gain_ref[j]
        gate = g_ref[:, j * dv:(j + 1) * dv].astype(F32)
        o_ref[:, j * dv:(j + 1) * dv] = (y * (gate * _sigmoid(gate))).astype(o_ref.dtype)


def _retention(u, cos, sin, log_gamma, block_decay, gain, batch, seq):
    t = min(RET_BLOCK, seq)
    nb = seq // t
    grp = RET_HEAD_GROUP
    n_groups = RET_HEADS // grp
    rows = lambda b, h, i: b * nb + i
    qk_w, v_w = grp * RET_QK_DIM, grp * RET_V_DIM
    k_off, v_off, g_off = n_groups, n_groups, 2 * n_groups
    smem = pl.BlockSpec(memory_space=pltpu.SMEM)
    return pl.pallas_call(
        _retention_kernel, grid=(batch, n_groups, nb),
        in_specs=[
            smem, smem,
            pl.BlockSpec((t, qk_w), lambda b, h, i: (rows(b, h, i), h)),
            pl.BlockSpec((t, qk_w), lambda b, h, i: (rows(b, h, i), k_off + h)),
            pl.BlockSpec((t, v_w), lambda b, h, i: (rows(b, h, i), v_off + h)),
            pl.BlockSpec((t, v_w), lambda b, h, i: (rows(b, h, i), g_off + h)),
            pl.BlockSpec((t, LANES), lambda b, h, i: (rows(b, h, i), 0)),
            pl.BlockSpec((t, LANES), lambda b, h, i: (rows(b, h, i), 0)),
            pl.BlockSpec((grp, 1, RET_V_DIM), lambda b, h, i: (h, 0, 0)),
        ],
        out_specs=pl.BlockSpec((t, v_w), lambda b, h, i: (rows(b, h, i), h)),
        out_shape=jax.ShapeDtypeStruct((batch * seq, RET_HEADS * RET_V_DIM), BF16),
        scratch_shapes=[pltpu.VMEM((grp, RET_QK_DIM, RET_V_DIM), F32), pltpu.VMEM((grp, t, t), F32),
                        pltpu.VMEM((grp, t, RET_QK_DIM), F32), pltpu.VMEM((grp, t, RET_QK_DIM), F32)],
        compiler_params=_cparams(3), name="retention",
    )(log_gamma, block_decay, u, u, u, u, cos, sin, gain)


def _partial_rope(x, cos, sin_lo, sin_hi):
    half = ROPE_DIM // 2
    return (x * cos + pltpu.roll(x, DIFF_QK_DIM - half, 1) * sin_lo
            + pltpu.roll(x, half, 1) * sin_hi)


def _diff_attn_kernel(lam_ref, q_ref, k_ref, v_ref, qc_ref, qa_ref, qb_ref, kc_ref, ka_ref, kb_ref,
                      gain_ref, o_ref, krot_ref, vt_ref, m_ref, acc_ref, sa_ref, sb_ref, *,
                      lambda_init):
    qblk = pl.program_id(2)
    tq = q_ref.shape[0]
    d = DIFF_QK_DIM

    def key_rows(j):
        return pl.ds(pl.multiple_of(j * tq, tq), tq)

    @pl.when(qblk == 0)
    def _():
        def prep_rows(c, carry):
            rows = key_rows(c)
            for i in range(2):
                x = k_ref[rows, i * d:(i + 1) * d].astype(F32)
                krot_ref[rows, i * d:(i + 1) * d] = _partial_rope(
                    x, kc_ref[rows, :], ka_ref[rows, :], kb_ref[rows, :]).astype(BF16)
            vt_ref[c, :DIFF_V_DIM] = v_ref[rows, :].astype(F32).T.astype(BF16)
            vt_ref[c, DIFF_V_DIM:] = jnp.ones((BF16_ROWS, tq), BF16)
            return carry
        lax.fori_loop(0, k_ref.shape[0] // tq, prep_rows, 0)

    scale = d ** -0.5
    qs = []
    for i in range(2):
        x = q_ref[:, i * d:(i + 1) * d].astype(F32)
        qs.append((_partial_rope(x, qc_ref[...], qa_ref[...], qb_ref[...]) * scale).astype(BF16))

    m_ref[...] = jnp.full_like(m_ref, NEG_BIG)
    acc_ref[...] = jnp.zeros_like(acc_ref)

    s_bufs = (sa_ref, sb_ref)

    def scores(j, buf, diagonal):
        for i in range(2):
            s = _dot_nt(krot_ref[key_rows(j), i * d:(i + 1) * d], qs[i])
            if diagonal:
                key = lax.broadcasted_iota(jnp.int32, s.shape, 0)
                qry = lax.broadcasted_iota(jnp.int32, s.shape, 1)
                s = jnp.where(_chunk_of(qry) >= _chunk_of(key), s, NEG_BIG)
            s_bufs[buf][i] = s

    def absorb(j, buf):
        vt = vt_ref[j]
        for i in range(2):
            s = s_bufs[buf][i]
            m_old = m_ref[i]
            m_new = jnp.maximum(m_old, jnp.max(s, axis=0, keepdims=True))
            alpha = jnp.exp(m_old - m_new)
            p = jnp.exp(s - m_new)
            acc_ref[i] = alpha * acc_ref[i] + _dot(vt, p.astype(BF16))
            m_ref[i] = m_new

    scores(qblk, 0, True)

    def pair(p, carry):
        pending = jnp.where(p == 0, qblk, 2 * p - 1)
        scores(2 * p, 1, False)
        absorb(pending, 0)
        scores(2 * p + 1, 0, False)
        absorb(2 * p, 1)
        return carry
    n_pairs = qblk // 2
    lax.fori_loop(0, n_pairs, pair, 0)
    pending = jnp.where(n_pairs == 0, qblk, 2 * n_pairs - 1)

    @pl.when(qblk % 2 == 1)
    def _():
        scores(qblk - 1, 1, False)
        absorb(pending, 0)
        absorb(qblk - 1, 1)

    @pl.when(qblk % 2 == 0)
    def _():
        absorb(pending, 0)

    lp = lam_ref[...]
    lam = (jnp.exp(jnp.sum(lp[0:1] * lp[1:2], axis=-1, keepdims=True))
           - jnp.exp(jnp.sum(lp[2:3] * lp[3:4], axis=-1, keepdims=True)) + lambda_init)
    dv = DIFF_V_DIM
    o = (acc_ref[0, :dv] / acc_ref[0, dv:dv + 1] - lam * (acc_ref[1, :dv] / acc_ref[1, dv:dv + 1])).T
    o = o * lax.rsqrt(jnp.mean(jnp.square(o), axis=-1, keepdims=True) + LN_EPS) * gain_ref[...]
    o_ref[...] = (o * (1.0 - lambda_init)).astype(o_ref.dtype)


def _diff_attention(u, cos, sin_lo, sin_hi, lam_params, gain, batch, seq, lambda_init):
    tq = min(ATT_BLOCK, seq)
    nb = seq // tq
    width = 2 * DIFF_QK_DIM
    ret_cols = 2 * RET_HEADS * RET_QK_DIM + 2 * RET_HEADS * RET_V_DIM
    q_off = ret_cols // width
    k_off = q_off + DIFF_HEADS
    v_off = k_off + DIFF_HEADS
    qrow = lambda b, h, i: b * nb + i
    qtab = pl.BlockSpec((tq, LANES), lambda b, h, i: (qrow(b, h, i), 0))
    ktab = pl.BlockSpec((seq, LANES), lambda b, h, i: (b, 0), pipeline_mode=pl.Buffered(1))
    return pl.pallas_call(
        functools.partial(_diff_attn_kernel, lambda_init=lambda_init),
        grid=(batch, DIFF_HEADS, nb),
        in_specs=[
            pl.BlockSpec((4, DIFF_QK_DIM), lambda b, h, i: (0, 0)),
            pl.BlockSpec((tq, width), lambda b, h, i: (qrow(b, h, i), q_off + h)),
            pl.BlockSpec((seq, width), lambda b, h, i: (b, k_off + h)),
            pl.BlockSpec((seq, DIFF_V_DIM), lambda b, h, i: (b, v_off + h)),
            qtab, qtab, qtab, ktab, ktab, ktab,
            pl.BlockSpec((1, DIFF_V_DIM), lambda b, h, i: (0, 0)),
        ],
        out_specs=pl.BlockSpec((tq, DIFF_V_DIM), lambda b, h, i: (qrow(b, h, i), h)),
        out_shape=jax.ShapeDtypeStruct((batch * seq, DIFF_HEADS * DIFF_V_DIM), BF16),
        scratch_shapes=[pltpu.VMEM((seq, width), BF16),
                        pltpu.VMEM((nb, DIFF_V_DIM + BF16_ROWS, tq), BF16),
                        pltpu.VMEM((2, 1, tq), F32),
                        pltpu.VMEM((2, DIFF_V_DIM + BF16_ROWS, tq), F32),
                        pltpu.VMEM((2, tq, tq), F32), pltpu.VMEM((2, tq, tq), F32)],
        compiler_params=_cparams(3), name="diff_attention",
    )(lam_params, u, u, u, cos, sin_lo, sin_hi, cos, sin_lo, sin_hi, gain)


def _layer_norm(z, gain, bias):
    mu = jnp.mean(z, axis=-1, keepdims=True)
    var = jnp.mean(jnp.square(z - mu), axis=-1, keepdims=True)
    return (z - mu) * lax.rsqrt(var + LN_EPS) * gain + bias


def _split_bf16(a):
    hi = a.astype(BF16)
    return hi, (a - hi.astype(F32)).astype(BF16)


def _route_kernel(ret_ref, diff_ref, gr_ref, gd_ref, x_ref, wret_ref, wdiff_ref, wo_ref, gain_ref,
                  bias_ref, wr_ref, br_ref, h_ref, r_ref, count_ref, *, alpha):
    @pl.when(pl.program_id(0) == 0)
    def _():
        count_ref[...] = jnp.zeros_like(count_ref)

    w_hi, w_lo = _split_bf16(wr_ref[...])
    w_both = jnp.concatenate([w_hi, w_lo], axis=1)
    part = h_ref.shape[0] // ROUTE_SPLITS
    for g in range(ROUTE_SPLITS):
        rows = slice(g * part, (g + 1) * part)
        merged = (_sigmoid(gr_ref[rows, :].astype(F32)) * _dot(ret_ref[rows, :], wret_ref[...])
                  + _sigmoid(gd_ref[rows, :].astype(F32)) * _dot(diff_ref[rows, :], wdiff_ref[...]))
        h = _layer_norm(alpha * x_ref[rows, :] + _dot(merged.astype(BF16), wo_ref[...]),
                        gain_ref[...], bias_ref[...])
        h_ref[rows, :] = h
        h_hi, h_lo = _split_bf16(h)
        both = _dot(h_hi, w_both)
        logits = both[:, :LANES] + (both[:, LANES:] + _dot(h_lo, w_hi)) + br_ref[...]
        r_ref[rows, :] = _routing_rows(logits, count_ref)


def _routing_rows(logits, count_ref):
    lane = lax.broadcasted_iota(jnp.int32, logits.shape, 1).astype(F32)

    def first_argmax(vals):
        top = jnp.max(vals, axis=-1, keepdims=True)
        idx = jnp.min(jnp.where(vals == top, lane, float(LANES)), axis=-1, keepdims=True)
        return top, idx

    group_logits = jnp.where(lane < N_GROUPS, logits, NEG_BIG)
    g_top, g_idx = first_argmax(group_logits)
    p_group = 1.0 / jnp.sum(jnp.exp(group_logits - g_top), axis=-1, keepdims=True)

    first = N_GROUPS + g_idx * EXPERTS_PER_GROUP
    within = jnp.where((lane >= first) & (lane < first + EXPERTS_PER_GROUP), logits, NEG_BIG)
    v1, i1 = first_argmax(within)
    v2, i2 = first_argmax(jnp.where(lane == i1, NEG_BIG, within))
    e2 = jnp.exp(v2 - v1)
    w1 = p_group / (1.0 + e2)
    w2 = p_group * e2 / (1.0 + e2)
    x1, x2 = i1 - N_GROUPS, i2 - N_GROUPS

    hit1 = jnp.where(lane == x1, 1.0, 0.0)
    hit2 = jnp.where(lane == x2, 1.0, 0.0)
    hits = hit1 + hit2
    tm = hits.shape[0]
    earlier = (lax.broadcasted_iota(jnp.int32, (tm, tm), 0) > lax.broadcasted_iota(jnp.int32, (tm, tm), 1))
    before = count_ref[...] + _dot(jnp.where(earlier, 1.0, 0.0).astype(BF16), hits.astype(BF16))
    rank1 = jnp.sum(before * hit1, axis=-1, keepdims=True)
    rank2 = jnp.sum(before * hit2, axis=-1, keepdims=True)
    count_ref[...] += jnp.sum(hits, axis=0, keepdims=True)

    fields = (x1, x2, w1, w2, rank1, rank2)
    row = jnp.zeros_like(logits)
    for pos, value in enumerate(fields):
        row = jnp.where(lane == pos, value, row)
    return row


def _route(ret, diff, u, x, w_ret, w_diff, w_out, gain, bias, w_router, b_router, alpha):
    n, dm = x.shape
    tm = min(ROUTE_TM, n)
    gate_off = (u.shape[1] - 2 * dm) // dm
    tile = pl.BlockSpec((tm, dm), lambda i: (i, 0))
    gate = lambda c: pl.BlockSpec((tm, dm), lambda i: (i, c))
    full = lambda a: pl.BlockSpec(a.shape, lambda i: (0, 0), pipeline_mode=pl.Buffered(1))
    return pl.pallas_call(
        functools.partial(_route_kernel, alpha=alpha), grid=(n // tm,),
        in_specs=[pl.BlockSpec((tm, ret.shape[1]), lambda i: (i, 0)),
                  pl.BlockSpec((tm, diff.shape[1]), lambda i: (i, 0)),
                  gate(gate_off), gate(gate_off + 1), tile, full(w_ret), full(w_diff), full(w_out),
                  full(gain), full(bias), full(w_router), full(b_router)],
        out_specs=[tile, pl.BlockSpec((tm, LANES), lambda i: (i, 0)),
                   pl.BlockSpec((1, LANES), lambda i: (0, 0))],
        out_shape=[jax.ShapeDtypeStruct((n, dm), F32), jax.ShapeDtypeStruct((n, LANES), F32),
                   jax.ShapeDtypeStruct((1, LANES), F32)],
        compiler_params=_cparams(1), name="merge_outproj_ln_router",
    )(ret, diff, u, u, x, w_ret, w_diff, w_out, gain, bias, w_router, b_router)


def _row_copies(table_ref, first, n_rows, make_copy):
    for g in range(n_rows // DMA_GROUP):
        entries = [table_ref[first + g * DMA_GROUP + k] for k in range(DMA_GROUP)]
        for k, entry in enumerate(entries):
            make_copy(entry, g * DMA_GROUP + k).start()


def _for_buffer(dynamic_buf, fn):
    for buf in range(2):
        pl.when(dynamic_buf == buf)(functools.partial(fn, buf))


def _expert_kernel(tile_expert_ref, run_next_ref, src_ref, n_valid_ref, h_hbm, wg_hbm, wu_hbm, wd_hbm,
                   y_ref, xbuf, stage_g, stage_u, stage_d, wg_ref, wu_ref, wd_ref, gather_sem,
                   weight_sem):
    i = pl.program_id(0)
    n_valid = n_valid_ref[0]
    tm = xbuf.shape[1]
    slot = i & 1
    weights = ((wg_hbm, stage_g, wg_ref), (wu_hbm, stage_u, wu_ref), (wd_hbm, stage_d, wd_ref))

    def start_weights(expert):
        for k, (hbm, stage, _) in enumerate(weights):
            pltpu.make_async_copy(hbm.at[expert], stage, weight_sem.at[k]).start(priority=1)

    def take_weights():
        for k, (hbm, stage, dst) in enumerate(weights):
            pltpu.make_async_copy(hbm.at[0], stage, weight_sem.at[k]).wait()

            def round_rows(c, carry, stage=stage, dst=dst):
                rows = pl.ds(pl.multiple_of(c * WEIGHT_CAST_ROWS, WEIGHT_CAST_ROWS), WEIGHT_CAST_ROWS)
                dst[rows, :] = stage[rows, :].astype(BF16)
                return carry
            lax.fori_loop(0, stage.shape[0] // WEIGHT_CAST_ROWS, round_rows, 0)

    def start_gather(tile, dynamic_buf):
        def issue(buf):
            _row_copies(src_ref, tile * tm, tm, lambda tok, r: pltpu.make_async_copy(
                h_hbm.at[pl.ds(tok, 1)], xbuf.at[buf, pl.ds(r, 1)], gather_sem.at[buf]))
        _for_buffer(dynamic_buf, issue)

    @pl.when(i == 0)
    def _():
        start_gather(0, 0)
        start_weights(tile_expert_ref[0])

    @pl.when(i + 1 < n_valid)
    def _():
        start_gather(i + 1, 1 - slot)

    @pl.when(i < n_valid)
    def _():
        run_next = run_next_ref[i]

        @pl.when(run_next != NOT_RUN_START)
        def _():
            take_weights()

            @pl.when(run_next != NO_NEXT_EXPERT)
            def _():
                start_weights(run_next)

        pltpu.make_async_copy(h_hbm.at[pl.ds(0, tm)], xbuf.at[slot], gather_sem.at[slot]).wait()
        x = xbuf[slot].astype(BF16)
        g = _dot(x, wg_ref[...])
        act = (g * _sigmoid(g) * _dot(x, wu_ref[...])).astype(BF16)
        y_ref[...] = _dot(act, wd_ref[...])

    @pl.when(i >= n_valid)
    def _():
        y_ref[...] = jnp.zeros_like(y_ref)


def _experts(h, tile_expert, run_next, src, n_valid, w_gate, w_up, w_down):
    dm = h.shape[1]
    n_tiles = tile_expert.shape[0]
    tm = src.shape[0] // n_tiles
    ff = w_gate.shape[2]
    hbm = pl.BlockSpec(memory_space=pl.ANY)
    grid_spec = pltpu.PrefetchScalarGridSpec(
        num_scalar_prefetch=4, grid=(n_tiles,),
        in_specs=[hbm, hbm, hbm, hbm],
        out_specs=pl.BlockSpec((tm, dm), lambda i, *_: (i, 0)),
        scratch_shapes=[pltpu.VMEM((2, tm, dm), F32),
                        pltpu.VMEM((dm, ff), F32), pltpu.VMEM((dm, ff), F32), pltpu.VMEM((ff, dm), F32),
                        pltpu.VMEM((dm, ff), BF16), pltpu.VMEM((dm, ff), BF16), pltpu.VMEM((ff, dm), BF16),
                        pltpu.SemaphoreType.DMA((2,)), pltpu.SemaphoreType.DMA((3,))])
    return pl.pallas_call(
        _expert_kernel, grid_spec=grid_spec,
        out_shape=jax.ShapeDtypeStruct((src.shape[0], dm), F32),
        compiler_params=_cparams(1), name="experts",
    )(tile_expert, run_next, src, n_valid, h, w_gate, w_up, w_down)


def _dispatch_tables(expert_ids, ranks, counts, tm):
    n_assign = expert_ids.shape[0]
    n_tiles = n_assign // tm + N_EXPERTS
    n_slots = n_tiles * tm
    experts = jnp.arange(N_EXPERTS, dtype=jnp.int32)
    padded = (counts + tm - 1) // tm * tm
    pad_end = jnp.cumsum(padded)
    pad_start = pad_end - padded
    slot_of = jnp.sum(jnp.where(expert_ids[:, None] == experts[None, :], pad_start[None, :], 0), axis=1) + ranks
    tile_start = jnp.arange(n_tiles, dtype=jnp.int32) * tm
    tile_expert = jnp.minimum(
        jnp.sum((pad_end[None, :] <= tile_start[:, None]).astype(jnp.int32), axis=1), N_EXPERTS - 1)
    later = lax.cummin(jnp.where(counts > 0, experts, N_EXPERTS)[::-1])[::-1]
    following = jnp.concatenate([later[1:], jnp.full((1,), N_EXPERTS, jnp.int32)])[tile_expert]
    run_start = jnp.concatenate([jnp.ones((1,), bool), tile_expert[1:] != tile_expert[:-1]])
    run_next = jnp.where(run_start, jnp.where(following < N_EXPERTS, following, NO_NEXT_EXPERT),
                         NOT_RUN_START).astype(jnp.int32)
    token = jnp.arange(n_assign, dtype=jnp.int32) // TOP_K
    by_slot = token[jnp.argsort(slot_of)]
    dense_start = jnp.cumsum(counts) - counts
    slot_expert = jnp.repeat(tile_expert, tm)
    dense_index = dense_start[slot_expert] + jnp.arange(n_slots, dtype=jnp.int32) - pad_start[slot_expert]
    src = by_slot[jnp.clip(dense_index, 0, n_assign - 1)]
    n_valid = (pad_end[-1] // tm).astype(jnp.int32).reshape(1)
    return slot_of, tile_expert, run_next, src, n_valid


def _final_kernel(slot_ref, h_ref, y_hbm, r_ref, gain_ref, bias_ref, o_ref, ybuf, sem, *, alpha):
    i = pl.program_id(0)
    tm = o_ref.shape[0]
    slot = i & 1

    def start_gather(step, dynamic_buf):
        def issue(buf):
            _row_copies(slot_ref, step * (TOP_K * tm), TOP_K * tm, lambda row, r: pltpu.make_async_copy(
                y_hbm.at[pl.ds(row, 1)], ybuf.at[buf, r % TOP_K, pl.ds(r // TOP_K, 1)], sem.at[buf]))
        _for_buffer(dynamic_buf, issue)

    @pl.when(i == 0)
    def _():
        start_gather(0, 0)

    @pl.when(i + 1 < pl.num_programs(0))
    def _():
        start_gather(i + 1, 1 - slot)

    for k in range(TOP_K):
        pltpu.make_async_copy(y_hbm.at[pl.ds(0, tm)], ybuf.at[slot, k], sem.at[slot]).wait()
    r = r_ref[...]
    ffn = ybuf[slot, 0] * r[:, 2:3] + ybuf[slot, 1] * r[:, 3:4]
    o_ref[...] = _layer_norm(alpha * h_ref[...] + ffn, gain_ref[...], bias_ref[...])


def _final(h, y_slots, slot_of, routing, gain, bias, alpha):
    n, dm = h.shape
    tm = min(FINAL_TM, n)
    tile = pl.BlockSpec((tm, dm), lambda i, *_: (i, 0))
    full = lambda a: pl.BlockSpec(a.shape, lambda i, *_: (0, 0))
    grid_spec = pltpu.PrefetchScalarGridSpec(
        num_scalar_prefetch=1, grid=(n // tm,),
        in_specs=[tile, pl.BlockSpec(memory_space=pl.ANY), pl.BlockSpec((tm, LANES), lambda i, *_: (i, 0)),
                  full(gain), full(bias)],
        out_specs=tile,
        scratch_shapes=[pltpu.VMEM((2, TOP_K, tm, dm), F32), pltpu.SemaphoreType.DMA((2,))])
    return pl.pallas_call(
        functools.partial(_final_kernel, alpha=alpha), grid_spec=grid_spec,
        out_shape=jax.ShapeDtypeStruct((n, dm), F32),
        compiler_params=_cparams(1), name="combine_ln",
    )(slot_of, h, y_slots, routing, gain, bias)


def _rope_frequencies():
    ret_freq = 1.0 / (RET_THETA ** (jnp.arange(0, RET_QK_DIM, 2, dtype=F32) / RET_QK_DIM))
    diff_freq = 1.0 / (ROPE_THETA ** (jnp.arange(0, ROPE_DIM, 2, dtype=F32) / ROPE_DIM))
    pad = jnp.zeros((LANES - ret_freq.shape[0] - diff_freq.shape[0],), F32)
    return jnp.concatenate([ret_freq, diff_freq, pad]).reshape(1, LANES)


def kernel(x, positions, w_in, ret_gn_gain, diff_lambda_q1, diff_lambda_k1, diff_lambda_q2, diff_lambda_k2, diff_subln_gain, w_ret_proj, w_diff_proj, w_out, ln1_gain, ln1_bias, w_group, b_group, w_router, b_router, w_expert_gate, w_expert_up, w_expert_down, ln2_gain, ln2_bias):
    batch, seq, dm = x.shape
    n = batch * seq
    depth = w_in.shape[0]
    assert seq % CHUNK == 0
    alpha = (2 * depth) ** 0.25

    ret_cos, ret_sin, diff_cos, diff_sin_lo, diff_sin_hi = _rope_tables(
        positions.reshape(n, 1), _rope_frequencies())

    log_gamma = jnp.log(1.0 - 2.0 ** (-5.0 - jnp.arange(RET_HEADS, dtype=F32)))
    block_decay = jnp.exp(log_gamma * min(RET_BLOCK, seq))

    xt = x.reshape(n, dm)
    for l in range(depth):
        lambda_init = 0.8 - 0.6 * math.exp(-0.3 * l)
        u = _matmul(xt, w_in[l], BF16, "in_proj")
        ret = _retention(u, ret_cos, ret_sin, log_gamma, block_decay,
                         ret_gn_gain[l].reshape(RET_HEADS, 1, RET_V_DIM), batch, seq)
        lam_params = jnp.stack([diff_lambda_q1[l], diff_lambda_k1[l], diff_lambda_q2[l], diff_lambda_k2[l]])
        diff = _diff_attention(u, diff_cos, diff_sin_lo, diff_sin_hi, lam_params.astype(F32),
                               diff_subln_gain[l].reshape(1, DIFF_V_DIM), batch, seq, lambda_init)
        w_route = jnp.concatenate(
            [w_group[l], w_router[l], jnp.zeros((dm, LANES - N_GROUPS - N_EXPERTS), F32)], axis=1)
        b_route = jnp.concatenate(
            [b_group[l], b_router[l], jnp.zeros((LANES - N_GROUPS - N_EXPERTS,), F32)]).reshape(1, LANES)
        h, routing, counts = _route(
            ret, diff, u, xt, w_ret_proj[l].astype(BF16), w_diff_proj[l].astype(BF16),
            w_out[l].astype(BF16), ln1_gain[l].reshape(1, dm), ln1_bias[l].reshape(1, dm),
            w_route, b_route, alpha)

        expert_ids = routing[:, :TOP_K].astype(jnp.int32).reshape(-1)
        ranks = routing[:, 2 * TOP_K:3 * TOP_K].astype(jnp.int32).reshape(-1)
        slot_of, tile_expert, run_next, src, n_valid = _dispatch_tables(
            expert_ids, ranks, counts[0, :N_EXPERTS].astype(jnp.int32), min(MOE_TM, n))
        y_slots = _experts(h, tile_expert, run_next, src, n_valid, w_expert_gate[l], w_expert_up[l],
                           w_expert_down[l])
        xt = _final(h, y_slots, slot_of, routing,
                    ln2_gain[l].reshape(1, dm), ln2_bias[l].reshape(1, dm), alpha)
    return xt.reshape(batch, seq, dm)
```

```python
import functools
import math

import jax
import jax.numpy as jnp
from jax import lax
from jax.experimental import pallas as pl
from jax.experimental.pallas import tpu as pltpu

F32 = jnp.float32
BF16 = jnp.bfloat16

CHUNK = 64
RET_HEADS = 8
RET_QK_DIM = 128
RET_V_DIM = 256
RET_THETA = 10000.0
DIFF_HEADS = 8
DIFF_QK_DIM = 128
DIFF_V_DIM = 256
ROPE_THETA = 500000.0
ROPE_DIM = DIFF_QK_DIM // 4
N_GROUPS = 4
EXPERTS_PER_GROUP = 8
N_EXPERTS = N_GROUPS * EXPERTS_PER_GROUP
TOP_K = 2
LN_EPS = 1e-5

LANES = 128
BF16_ROWS = 16
NEG_BIG = -1e30

VMEM_LIMIT_BYTES = 56 * 1024 * 1024
TABLE_ROWS = 512
PROJ_TM = 1024
PROJ_TN = 1024
RET_BLOCK = 256
RET_HEAD_GROUP = 8
ATT_BLOCK = 1024
ROUTE_TM = 256
ROUTE_SPLITS = 1
MOE_TM = 256
DMA_GROUP = 8
WEIGHT_CAST_ROWS = 256
NOT_RUN_START = -2
NO_NEXT_EXPERT = -1
FINAL_TM = 256


def _cparams(n_axes):
    return pltpu.CompilerParams(
        dimension_semantics=("arbitrary",) * n_axes, vmem_limit_bytes=VMEM_LIMIT_BYTES)


def _sigmoid(x):
    return 1.0 / (1.0 + jnp.exp(-x))


def _chunk_of(idx):
    return jnp.right_shift(idx, CHUNK.bit_length() - 1)


def _dot(a, b):
    return jnp.dot(a, b, preferred_element_type=F32)


def _dot_nt(a, b):
    return lax.dot_general(a, b, (((1,), (1,)), ((), ())), preferred_element_type=F32)


def _table_kernel(pos_ref, freq_ref, rc_ref, rs_ref, dc_ref, dlo_ref, dhi_ref):
    rh, dh = RET_QK_DIM // 2, ROPE_DIM // 2
    ang = pos_ref[...].astype(F32) * freq_ref[...]
    c, s = jnp.cos(ang), jnp.sin(ang)
    lane = lax.broadcasted_iota(jnp.int32, ang.shape, 1)
    rc_ref[...] = jnp.where(lane < rh, c, pltpu.roll(c, rh, 1))
    rs_ref[...] = jnp.where(lane < rh, -s, pltpu.roll(s, rh, 1))
    c_lo, s_lo = pltpu.roll(c, LANES - rh, 1), pltpu.roll(s, LANES - rh, 1)
    c_hi, s_hi = pltpu.roll(c, LANES - rh + dh, 1), pltpu.roll(s, LANES - rh + dh, 1)
    dc_ref[...] = jnp.where(lane < dh, c_lo, jnp.where(lane < 2 * dh, c_hi, 1.0))
    dlo_ref[...] = jnp.where(lane < dh, -s_lo, 0.0)
    dhi_ref[...] = jnp.where((lane >= dh) & (lane < 2 * dh), s_hi, 0.0)


def _rope_tables(pos_col, freq_row):
    n = pos_col.shape[0]
    tr = min(TABLE_ROWS, n)
    out = pl.BlockSpec((tr, LANES), lambda i: (i, 0))
    shp = jax.ShapeDtypeStruct((n, LANES), F32)
    return pl.pallas_call(
        _table_kernel, grid=(n // tr,),
        in_specs=[pl.BlockSpec((tr, 1), lambda i: (i, 0)), pl.BlockSpec((1, LANES), lambda i: (0, 0))],
        out_specs=[out] * 5, out_shape=[shp] * 5,
        compiler_params=_cparams(1), name="rope_tables",
    )(pos_col, freq_row)


def _matmul_kernel(x_ref, w_ref, o_ref, wb_ref):
    @pl.when(pl.program_id(1) == 0)
    def _():
        wb_ref[...] = w_ref[...].astype(BF16)

    o_ref[...] = _dot(x_ref[...].astype(BF16), wb_ref[...]).astype(o_ref.dtype)


def _matmul(x, w, out_dtype, name):
    m, k = x.shape
    n = w.shape[1]
    tm, tn = min(PROJ_TM, m), min(PROJ_TN, n)
    return pl.pallas_call(
        _matmul_kernel, grid=(n // tn, m // tm),
        in_specs=[pl.BlockSpec((tm, k), lambda j, i: (i, 0)),
                  pl.BlockSpec((k, tn), lambda j, i: (0, j))],
        out_specs=pl.BlockSpec((tm, tn), lambda j, i: (i, j)),
        out_shape=jax.ShapeDtypeStruct((m, n), out_dtype),
        scratch_shapes=[pltpu.VMEM((k, tn), BF16)],
        compiler_params=_cparams(2), name=name,
    )(x, w)


def _retention_kernel(lg_ref, bd_ref, q_ref, k_ref, v_ref, g_ref, cos_ref, sin_ref, gain_ref,
                      o_ref, state_ref, decay_ref, qdecay_ref, kdecay_ref):
    group = state_ref.shape[0]
    first_head = pl.program_id(1) * group
    t = q_ref.shape[0]
    dk, dv = RET_QK_DIM, RET_V_DIM

    @pl.when(pl.program_id(2) == 0)
    def _():
        state_ref[...] = jnp.zeros_like(state_ref)
        ri = lax.broadcasted_iota(jnp.int32, (t, t), 0)
        ci = lax.broadcasted_iota(jnp.int32, (t, t), 1)
        dist = jnp.abs(ri - ci).astype(F32)
        visible = _chunk_of(ri) >= _chunk_of(ci)
        row = lax.broadcasted_iota(jnp.int32, (t, dk), 0).astype(F32)
        for j in range(group):
            lg = lg_ref[first_head + j]
            decay_ref[j] = jnp.where(visible, jnp.exp(lg * dist), 0.0)
            qdecay_ref[j] = jnp.exp(lg * (row + 1.0))
            kdecay_ref[j] = jnp.exp(lg * (t - 1.0 - row))

    cos, sin = cos_ref[...], sin_ref[...]

    def rot(x):
        return x * cos + pltpu.roll(x, dk // 2, 1) * sin

    for j in range(group):
        q = rot(q_ref[:, j * dk:(j + 1) * dk].astype(F32))
        k = rot(k_ref[:, j * dk:(j + 1) * dk].astype(F32)) * (dk ** -0.5)
        v = v_ref[:, j * dv:(j + 1) * dv]

        s = _dot_nt(q.astype(BF16), k.astype(BF16)) * decay_ref[j]
        intra = _dot(s.astype(BF16), v)

        state = state_ref[j]
        cross = _dot((q * qdecay_ref[j]).astype(BF16), state.astype(BF16))
        kd = k * kdecay_ref[j]
        state_ref[j] = state * bd_ref[first_head + j] + _dot(kd.T.astype(BF16), v)

        out = intra + cross
        mu = jnp.mean(out, axis=-1, keepdims=True)
        var = jnp.mean(jnp.square(out - mu), axis=-1, keepdims=True)
        y = (out - mu) * lax.rsqrt(var + LN_EPS) * gain_ref[j]
        gate = g_ref[:, j * dv:(j + 1) * dv].astype(F32)
        o_ref[:, j * dv:(j + 1) * dv] = (y * (gate * _sigmoid(gate))).astype(o_ref.dtype)


def _retention(u, cos, sin, log_gamma, block_decay, gain, batch, seq):
    t = min(RET_BLOCK, seq)
    nb = seq // t
    grp = RET_HEAD_GROUP
    n_groups = RET_HEADS // grp
    rows = lambda b, h, i: b * nb + i
    qk_w, v_w = grp * RET_QK_DIM, grp * RET_V_DIM
    k_off, v_off, g_off = n_groups, n_groups, 2 * n_groups
    smem = pl.BlockSpec(memory_space=pltpu.SMEM)
    return pl.pallas_call(
        _retention_kernel, grid=(batch, n_groups, nb),
        in_specs=[
            smem, smem,
            pl.BlockSpec((t, qk_w), lambda b, h, i: (rows(b, h, i), h)),
            pl.BlockSpec((t, qk_w), lambda b, h, i: (rows(b, h, i), k_off + h)),
            pl.BlockSpec((t, v_w), lambda b, h, i: (rows(b, h, i), v_off + h)),
            pl.BlockSpec((t, v_w), lambda b, h, i: (rows(b, h, i), g_off + h)),
            pl.BlockSpec((t, LANES), lambda b, h, i: (rows(b, h, i), 0)),
            pl.BlockSpec((t, LANES), lambda b, h, i: (rows(b, h, i), 0)),
            pl.BlockSpec((grp, 1, RET_V_DIM), lambda b, h, i: (h, 0, 0)),
        ],
        out_specs=pl.BlockSpec((t, v_w), lambda b, h, i: (rows(b, h, i), h)),
        out_shape=jax.ShapeDtypeStruct((batch * seq, RET_HEADS * RET_V_DIM), BF16),
        scratch_shapes=[pltpu.VMEM((grp, RET_QK_DIM, RET_V_DIM), F32), pltpu.VMEM((grp, t, t), F32),
                        pltpu.VMEM((grp, t, RET_QK_DIM), F32), pltpu.VMEM((grp, t, RET_QK_DIM), F32)],
        compiler_params=_cparams(3), name="retention",
    )(log_gamma, block_decay, u, u, u, u, cos, sin, gain)


def _partial_rope(x, cos, sin_lo, sin_hi):
    half = ROPE_DIM // 2
    return (x * cos + pltpu.roll(x, DIFF_QK_DIM - half, 1) * sin_lo
            + pltpu.roll(x, half, 1) * sin_hi)


def _diff_attn_kernel(lam_ref, q_ref, k_ref, v_ref, qc_ref, qa_ref, qb_ref, kc_ref, ka_ref, kb_ref,
                      gain_ref, o_ref, krot_ref, vt_ref, m_ref, acc_ref, sa_ref, sb_ref, *,
                      lambda_init):
    qblk = pl.program_id(2)
    tq = q_ref.shape[0]
    d = DIFF_QK_DIM

    def key_rows(j):
        return pl.ds(pl.multiple_of(j * tq, tq), tq)

    @pl.when(qblk == 0)
    def _():
        def prep_rows(c, carry):
            rows = key_rows(c)
            for i in range(2):
                x = k_ref[rows, i * d:(i + 1) * d].astype(F32)
                krot_ref[rows, i * d:(i + 1) * d] = _partial_rope(
                    x, kc_ref[rows, :], ka_ref[rows, :], kb_ref[rows, :]).astype(BF16)
            vt_ref[c, :DIFF_V_DIM] = v_ref[rows, :].astype(F32).T.astype(BF16)
            vt_ref[c, DIFF_V_DIM:] = jnp.ones((BF16_ROWS, tq), BF16)
            return carry
        lax.fori_loop(0, k_ref.shape[0] // tq, prep_rows, 0)

    scale = d ** -0.5
    qs = []
    for i in range(2):
        x = q_ref[:, i * d:(i + 1) * d].astype(F32)
        qs.append((_partial_rope(x, qc_ref[...], qa_ref[...], qb_ref[...]) * scale).astype(BF16))

    m_ref[...] = jnp.full_like(m_ref, NEG_BIG)
    acc_ref[...] = jnp.zeros_like(acc_ref)

    s_bufs = (sa_ref, sb_ref)

    def scores(j, buf, diagonal):
        for i in range(2):
            s = _dot_nt(krot_ref[key_rows(j), i * d:(i + 1) * d], qs[i])
            if diagonal:
                key = lax.broadcasted_iota(jnp.int32, s.shape, 0)
                qry = lax.broadcasted_iota(jnp.int32, s.shape, 1)
                s = jnp.where(_chunk_of(qry) >= _chunk_of(key), s, NEG_BIG)
            s_bufs[buf][i] = s

    def absorb(j, buf):
        vt = vt_ref[j]
        for i in range(2):
            s = s_bufs[buf][i]
            m_old = m_ref[i]
            m_new = jnp.maximum(m_old, jnp.max(s, axis=0, keepdims=True))
            alpha = jnp.exp(m_old - m_new)
            p = jnp.exp(s - m_new)
            acc_ref[i] = alpha * acc_ref[i] + _dot(vt, p.astype(BF16))
            m_ref[i] = m_new

    scores(qblk, 0, True)

    def pair(p, carry):
        pending = jnp.where(p == 0, qblk, 2 * p - 1)
        scores(2 * p, 1, False)
        absorb(pending, 0)
        scores(2 * p + 1, 0, False)
        absorb(2 * p, 1)
        return carry
    n_pairs = qblk // 2
    lax.fori_loop(0, n_pairs, pair, 0)
    pending = jnp.where(n_pairs == 0, qblk, 2 * n_pairs - 1)

    @pl.when(qblk % 2 == 1)
    def _():
        scores(qblk - 1, 1, False)
        absorb(pending, 0)
        absorb(qblk - 1, 1)

    @pl.when(qblk % 2 == 0)
    def _():
        absorb(pending, 0)

    lp = lam_ref[...]
    lam = (jnp.exp(jnp.sum(lp[0:1] * lp[1:2], axis=-1, keepdims=True))
           - jnp.exp(jnp.sum(lp[2:3] * lp[3:4], axis=-1, keepdims=True)) + lambda_init)
    dv = DIFF_V_DIM
    o = (acc_ref[0, :dv] / acc_ref[0, dv:dv + 1] - lam * (acc_ref[1, :dv] / acc_ref[1, dv:dv + 1])).T
    o = o * lax.rsqrt(jnp.mean(jnp.square(o), axis=-1, keepdims=True) + LN_EPS) * gain_ref[...]
    o_ref[...] = (o * (1.0 - lambda_init)).astype(o_ref.dtype)


def _diff_attention(u, cos, sin_lo, sin_hi, lam_params, gain, batch, seq, lambda_init):
    tq = min(ATT_BLOCK, seq)
    nb = seq // tq
    width = 2 * DIFF_QK_DIM
    ret_cols = 2 * RET_HEADS * RET_QK_DIM + 2 * RET_HEADS * RET_V_DIM
    q_off = ret_cols // width
    k_off = q_off + DIFF_HEADS
    v_off = k_off + DIFF_HEADS
    qrow = lambda b, h, i: b * nb + i
    qtab = pl.BlockSpec((tq, LANES), lambda b, h, i: (qrow(b, h, i), 0))
    ktab = pl.BlockSpec((seq, LANES), lambda b, h, i: (b, 0), pipeline_mode=pl.Buffered(1))
    return pl.pallas_call(
        functools.partial(_diff_attn_kernel, lambda_init=lambda_init),
        grid=(batch, DIFF_HEADS, nb),
        in_specs=[
            pl.BlockSpec((4, DIFF_QK_DIM), lambda b, h, i: (0, 0)),
            pl.BlockSpec((tq, width), lambda b, h, i: (qrow(b, h, i), q_off + h)),
            pl.BlockSpec((seq, width), lambda b, h, i: (b, k_off + h)),
            pl.BlockSpec((seq, DIFF_V_DIM), lambda b, h, i: (b, v_off + h)),
            qtab, qtab, qtab, ktab, ktab, ktab,
            pl.BlockSpec((1, DIFF_V_DIM), lambda b, h, i: (0, 0)),
        ],
        out_specs=pl.BlockSpec((tq, DIFF_V_DIM), lambda b, h, i: (qrow(b, h, i), h)),
        out_shape=jax.ShapeDtypeStruct((batch * seq, DIFF_HEADS * DIFF_V_DIM), BF16),
        scratch_shapes=[pltpu.VMEM((seq, width), BF16),
                        pltpu.VMEM((nb, DIFF_V_DIM + BF16_ROWS, tq), BF16),
                        pltpu.VMEM((2, 1, tq), F32),
                        pltpu.VMEM((2, DIFF_V_DIM + BF16_ROWS, tq), F32),
                        pltpu.VMEM((2, tq, tq), F32), pltpu.VMEM((2, tq, tq), F32)],
        compiler_params=_cparams(3), name="diff_attention",
    )(lam_params, u, u, u, cos, sin_lo, sin_hi, cos, sin_lo, sin_hi, gain)


def _layer_norm(z, gain, bias):
    mu = jnp.mean(z, axis=-1, keepdims=True)
    var = jnp.mean(jnp.square(z - mu), axis=-1, keepdims=True)
    return (z - mu) * lax.rsqrt(var + LN_EPS) * gain + bias


def _split_bf16(a):
    hi = a.astype(BF16)
    return hi, (a - hi.astype(F32)).astype(BF16)


def _route_kernel(ret_ref, diff_ref, gr_ref, gd_ref, x_ref, wret_ref, wdiff_ref, wo_ref, gain_ref,
                  bias_ref, wr_ref, br_ref, h_ref, r_ref, count_ref, *, alpha):
    @pl.when(pl.program_id(0) == 0)
    def _():
        count_ref[...] = jnp.zeros_like(count_ref)

    w_hi, w_lo = _split_bf16(wr_ref[...])
    w_both = jnp.concatenate([w_hi, w_lo], axis=1)
    part = h_ref.shape[0] // ROUTE_SPLITS
    for g in range(ROUTE_SPLITS):
        rows = slice(g * part, (g + 1) * part)
        merged = (_sigmoid(gr_ref[rows, :].astype(F32)) * _dot(ret_ref[rows, :], wret_ref[...])
                  + _sigmoid(gd_ref[rows, :].astype(F32)) * _dot(diff_ref[rows, :], wdiff_ref[...]))
        h = _layer_norm(alpha * x_ref[rows, :] + _dot(merged.astype(BF16), wo_ref[...]),
                        gain_ref[...], bias_ref[...])
        h_ref[rows, :] = h
        h_hi, h_lo = _split_bf16(h)
        both = _dot(h_hi, w_both)
        logits = both[:, :LANES] + (both[:, LANES:] + _dot(h_lo, w_hi)) + br_ref[...]
        r_ref[rows, :] = _routing_rows(logits, count_ref)


def _routing_rows(logits, count_ref):
    lane = lax.broadcasted_iota(jnp.int32, logits.shape, 1).astype(F32)

    def first_argmax(vals):
        top = jnp.max(vals, axis=-1, keepdims=True)
        idx = jnp.min(jnp.where(vals == top, lane, float(LANES)), axis=-1, keepdims=True)
        return top, idx

    group_logits = jnp.where(lane < N_GROUPS, logits, NEG_BIG)
    g_top, g_idx = first_argmax(group_logits)
    p_group = 1.0 / jnp.sum(jnp.exp(group_logits - g_top), axis=-1, keepdims=True)

    first = N_GROUPS + g_idx * EXPERTS_PER_GROUP
    within = jnp.where((lane >= first) & (lane < first + EXPERTS_PER_GROUP), logits, NEG_BIG)
    v1, i1 = first_argmax(within)
    v2, i2 = first_argmax(jnp.where(lane == i1, NEG_BIG, within))
    e2 = jnp.exp(v2 - v1)
    w1 = p_group / (1.0 + e2)
    w2 = p_group * e2 / (1.0 + e2)
    x1, x2 = i1 - N_GROUPS, i2 - N_GROUPS

    hit1 = jnp.where(lane == x1, 1.0, 0.0)
    hit2 = jnp.where(lane == x2, 1.0, 0.0)
    hits = hit1 + hit2
    tm = hits.shape[0]
    earlier = (lax.broadcasted_iota(jnp.int32, (tm, tm), 0) > lax.broadcasted_iota(jnp.int32, (tm, tm), 1))
    before = count_ref[...] + _dot(jnp.where(earlier, 1.0, 0.0).astype(BF16), hits.astype(BF16))
    rank1 = jnp.sum(before * hit1, axis=-1, keepdims=True)
    rank2 = jnp.sum(before * hit2, axis=-1, keepdims=True)
    count_ref[...] += jnp.sum(hits, axis=0, keepdims=True)

    fields = (x1, x2, w1, w2, rank1, rank2)
    row = jnp.zeros_like(logits)
    for pos, value in enumerate(fields):
        row = jnp.where(lane == pos, value, row)
    return row


def _route(ret, diff, u, x, w_ret, w_diff, w_out, gain, bias, w_router, b_router, alpha):
    n, dm = x.shape
    tm = min(ROUTE_TM, n)
    gate_off = (u.shape[1] - 2 * dm) // dm
    tile = pl.BlockSpec((tm, dm), lambda i: (i, 0))
    gate = lambda c: pl.BlockSpec((tm, dm), lambda i: (i, c))
    full = lambda a: pl.BlockSpec(a.shape, lambda i: (0, 0), pipeline_mode=pl.Buffered(1))
    return pl.pallas_call(
        functools.partial(_route_kernel, alpha=alpha), grid=(n // tm,),
        in_specs=[pl.BlockSpec((tm, ret.shape[1]), lambda i: (i, 0)),
                  pl.BlockSpec((tm, diff.shape[1]), lambda i: (i, 0)),
                  gate(gate_off), gate(gate_off + 1), tile, full(w_ret), full(w_diff), full(w_out),
                  full(gain), full(bias), full(w_router), full(b_router)],
        out_specs=[tile, pl.BlockSpec((tm, LANES), lambda i: (i, 0)),
                   pl.BlockSpec((1, LANES), lambda i: (0, 0))],
        out_shape=[jax.ShapeDtypeStruct((n, dm), F32), jax.ShapeDtypeStruct((n, LANES), F32),
                   jax.ShapeDtypeStruct((1, LANES), F32)],
        compiler_params=_cparams(1), name="merge_outproj_ln_router",
    )(ret, diff, u, u, x, w_ret, w_diff, w_out, gain, bias, w_router, b_router)


def _row_copies(table_ref, first, n_rows, make_copy):
    for g in range(n_rows // DMA_GROUP):
        entries = [table_ref[first + g * DMA_GROUP + k] for k in range(DMA_GROUP)]
        for k, entry in enumerate(entries):
            make_copy(entry, g * DMA_GROUP + k).start()


def _for_buffer(dynamic_buf, fn):
    for buf in range(2):
        pl.when(dynamic_buf == buf)(functools.partial(fn, buf))


def _expert_kernel(tile_expert_ref, run_next_ref, src_ref, n_valid_ref, h_hbm, wg_hbm, wu_hbm, wd_hbm,
                   y_ref, xbuf, stage_g, stage_u, stage_d, wg_ref, wu_ref, wd_ref, gather_sem,
                   weight_sem):
    i = pl.program_id(0)
    n_valid = n_valid_ref[0]
    tm = xbuf.shape[1]
    slot = i & 1
    weights = ((wg_hbm, stage_g, wg_ref), (wu_hbm, stage_u, wu_ref), (wd_hbm, stage_d, wd_ref))

    def start_weights(expert):
        for k, (hbm, stage, _) in enumerate(weights):
            pltpu.make_async_copy(hbm.at[expert], stage, weight_sem.at[k]).start(priority=1)

    def take_weights():
        for k, (hbm, stage, dst) in enumerate(weights):
            pltpu.make_async_copy(hbm.at[0], stage, weight_sem.at[k]).wait()

            def round_rows(c, carry, stage=stage, dst=dst):
                rows = pl.ds(pl.multiple_of(c * WEIGHT_CAST_ROWS, WEIGHT_CAST_ROWS), WEIGHT_CAST_ROWS)
                dst[rows, :] = stage[rows, :].astype(BF16)
                return carry
            lax.fori_loop(0, stage.shape[0] // WEIGHT_CAST_ROWS, round_rows, 0)

    def start_gather(tile, dynamic_buf):
        def issue(buf):
            _row_copies(src_ref, tile * tm, tm, lambda tok, r: pltpu.make_async_copy(
                h_hbm.at[pl.ds(tok, 1)], xbuf.at[buf, pl.ds(r, 1)], gather_sem.at[buf]))
        _for_buffer(dynamic_buf, issue)

    @pl.when(i == 0)
    def _():
        start_gather(0, 0)
        start_weights(tile_expert_ref[0])

    @pl.when(i + 1 < n_valid)
    def _():
        start_gather(i + 1, 1 - slot)

    @pl.when(i < n_valid)
    def _():
        run_next = run_next_ref[i]

        @pl.when(run_next != NOT_RUN_START)
        def _():
            take_weights()

            @pl.when(run_next != NO_NEXT_EXPERT)
            def _():
                start_weights(run_next)

        pltpu.make_async_copy(h_hbm.at[pl.ds(0, tm)], xbuf.at[slot], gather_sem.at[slot]).wait()
        x = xbuf[slot].astype(BF16)
        g = _dot(x, wg_ref[...])
        act = (g * _sigmoid(g) * _dot(x, wu_ref[...])).astype(BF16)
        y_ref[...] = _dot(act, wd_ref[...])

    @pl.when(i >= n_valid)
    def _():
        y_ref[...] = jnp.zeros_like(y_ref)


def _experts(h, tile_expert, run_next, src, n_valid, w_gate, w_up, w_down):
    dm = h.shape[1]
    n_tiles = tile_expert.shape[0]
    tm = src.shape[0] // n_tiles
    ff = w_gate.shape[2]
    hbm = pl.BlockSpec(memory_space=pl.ANY)
    grid_spec = pltpu.PrefetchScalarGridSpec(
        num_scalar_prefetch=4, grid=(n_tiles,),
        in_specs=[hbm, hbm, hbm, hbm],
        out_specs=pl.BlockSpec((tm, dm), lambda i, *_: (i, 0)),
        scratch_shapes=[pltpu.VMEM((2, tm, dm), F32),
                        pltpu.VMEM((dm, ff), F32), pltpu.VMEM((dm, ff), F32), pltpu.VMEM((ff, dm), F32),
                        pltpu.VMEM((dm, ff), BF16), pltpu.VMEM((dm, ff), BF16), pltpu.VMEM((ff, dm), BF16),
                        pltpu.SemaphoreType.DMA((2,)), pltpu.SemaphoreType.DMA((3,))])
    return pl.pallas_call(
        _expert_kernel, grid_spec=grid_spec,
        out_shape=jax.ShapeDtypeStruct((src.shape[0], dm), F32),
        compiler_params=_cparams(1), name="experts",
    )(tile_expert, run_next, src, n_valid, h, w_gate, w_up, w_down)


def _dispatch_tables(expert_ids, ranks, counts, tm):
    n_assign = expert_ids.shape[0]
    n_tiles = n_assign // tm + N_EXPERTS
    n_slots = n_tiles * tm
    experts = jnp.arange(N_EXPERTS, dtype=jnp.int32)
    padded = (counts + tm - 1) // tm * tm
    pad_end = jnp.cumsum(padded)
    pad_start = pad_end - padded
    slot_of = jnp.sum(jnp.where(expert_ids[:, None] == experts[None, :], pad_start[None, :], 0), axis=1) + ranks
    tile_start = jnp.arange(n_tiles, dtype=jnp.int32) * tm
    tile_expert = jnp.minimum(
        jnp.sum((pad_end[None, :] <= tile_start[:, None]).astype(jnp.int32), axis=1), N_EXPERTS - 1)
    later = lax.cummin(jnp.where(counts > 0, experts, N_EXPERTS)[::-1])[::-1]
    following = jnp.concatenate([later[1:], jnp.full((1,), N_EXPERTS, jnp.int32)])[tile_expert]
    run_start = jnp.concatenate([jnp.ones((1,), bool), tile_expert[1:] != tile_expert[:-1]])
    run_next = jnp.where(run_start, jnp.where(following < N_EXPERTS, following, NO_NEXT_EXPERT),
                         NOT_RUN_START).astype(jnp.int32)
    token = jnp.arange(n_assign, dtype=jnp.int32) // TOP_K
    by_slot = token[jnp.argsort(slot_of)]
    dense_start = jnp.cumsum(counts) - counts
    tile_shift = (dense_start - pad_start)[tile_expert]
    dense_index = jnp.repeat(tile_shift, tm) + jnp.arange(n_slots, dtype=jnp.int32)
    src = by_slot[jnp.clip(dense_index, 0, n_assign - 1)]
    n_valid = (pad_end[-1] // tm).astype(jnp.int32).reshape(1)
    return slot_of, tile_expert, run_next, src, n_valid


def _final_kernel(slot_ref, h_ref, y_hbm, r_ref, gain_ref, bias_ref, o_ref, ybuf, sem, *, alpha):
    i = pl.program_id(0)
    tm = o_ref.shape[0]
    slot = i & 1

    def start_gather(step, dynamic_buf):
        def issue(buf):
            _row_copies(slot_ref, step * (TOP_K * tm), TOP_K * tm, lambda row, r: pltpu.make_async_copy(
                y_hbm.at[pl.ds(row, 1)], ybuf.at[buf, r % TOP_K, pl.ds(r // TOP_K, 1)], sem.at[buf]))
        _for_buffer(dynamic_buf, issue)

    @pl.when(i == 0)
    def _():
        start_gather(0, 0)

    @pl.when(i + 1 < pl.num_programs(0))
    def _():
        start_gather(i + 1, 1 - slot)

    for k in range(TOP_K):
        pltpu.make_async_copy(y_hbm.at[pl.ds(0, tm)], ybuf.at[slot, k], sem.at[slot]).wait()
    r = r_ref[...]
    ffn = ybuf[slot, 0] * r[:, 2:3] + ybuf[slot, 1] * r[:, 3:4]
    o_ref[...] = _layer_norm(alpha * h_ref[...] + ffn, gain_ref[...], bias_ref[...])


def _final(h, y_slots, slot_of, routing, gain, bias, alpha):
    n, dm = h.shape
    tm = min(FINAL_TM, n)
    tile = pl.BlockSpec((tm, dm), lambda i, *_: (i, 0))
    full = lambda a: pl.BlockSpec(a.shape, lambda i, *_: (0, 0))
    grid_spec = pltpu.PrefetchScalarGridSpec(
        num_scalar_prefetch=1, grid=(n // tm,),
        in_specs=[tile, pl.BlockSpec(memory_space=pl.ANY), pl.BlockSpec((tm, LANES), lambda i, *_: (i, 0)),
                  full(gain), full(bias)],
        out_specs=tile,
        scratch_shapes=[pltpu.VMEM((2, TOP_K, tm, dm), F32), pltpu.SemaphoreType.DMA((2,))])
    return pl.pallas_call(
        functools.partial(_final_kernel, alpha=alpha), grid_spec=grid_spec,
        out_shape=jax.ShapeDtypeStruct((n, dm), F32),
        compiler_params=_cparams(1), name="combine_ln",
    )(slot_of, h, y_slots, routing, gain, bias)


def _rope_frequencies():
    ret_freq = 1.0 / (RET_THETA ** (jnp.arange(0, RET_QK_DIM, 2, dtype=F32) / RET_QK_DIM))
    diff_freq = 1.0 / (ROPE_THETA ** (jnp.arange(0, ROPE_DIM, 2, dtype=F32) / ROPE_DIM))
    pad = jnp.zeros((LANES - ret_freq.shape[0] - diff_freq.shape[0],), F32)
    return jnp.concatenate([ret_freq, diff_freq, pad]).reshape(1, LANES)


def kernel(x, positions, w_in, ret_gn_gain, diff_lambda_q1, diff_lambda_k1, diff_lambda_q2, diff_lambda_k2, diff_subln_gain, w_ret_proj, w_diff_proj, w_out, ln1_gain, ln1_bias, w_group, b_group, w_router, b_router, w_expert_gate, w_expert_up, w_expert_down, ln2_gain, ln2_bias):
    batch, seq, dm = x.shape
    n = batch * seq
    depth = w_in.shape[0]
    assert seq % CHUNK == 0
    alpha = (2 * depth) ** 0.25

    ret_cos, ret_sin, diff_cos, diff_sin_lo, diff_sin_hi = _rope_tables(
        positions.reshape(n, 1), _rope_frequencies())

    log_gamma = jnp.log(1.0 - 2.0 ** (-5.0 - jnp.arange(RET_HEADS, dtype=F32)))
    block_decay = jnp.exp(log_gamma * min(RET_BLOCK, seq))

    xt = x.reshape(n, dm)
    for l in range(depth):
        lambda_init = 0.8 - 0.6 * math.exp(-0.3 * l)
        u = _matmul(xt, w_in[l], BF16, "in_proj")
        ret = _retention(u, ret_cos, ret_sin, log_gamma, block_decay,
                         ret_gn_gain[l].reshape(RET_HEADS, 1, RET_V_DIM), batch, seq)
        lam_params = jnp.stack([diff_lambda_q1[l], diff_lambda_k1[l], diff_lambda_q2[l], diff_lambda_k2[l]])
        diff = _diff_attention(u, diff_cos, diff_sin_lo, diff_sin_hi, lam_params.astype(F32),
                               diff_subln_gain[l].reshape(1, DIFF_V_DIM), batch, seq, lambda_init)
        w_route = jnp.concatenate(
            [w_group[l], w_router[l], jnp.zeros((dm, LANES - N_GROUPS - N_EXPERTS), F32)], axis=1)
        b_route = jnp.concatenate(
            [b_group[l], b_router[l], jnp.zeros((LANES - N_GROUPS - N_EXPERTS,), F32)]).reshape(1, LANES)
        h, routing, counts = _route(
            ret, diff, u, xt, w_ret_proj[l].astype(BF16), w_diff_proj[l].astype(BF16),
            w_out[l].astype(BF16), ln1_gain[l].reshape(1, dm), ln1_bias[l].reshape(1, dm),
            w_route, b_route, alpha)

        expert_ids = routing[:, :TOP_K].astype(jnp.int32).reshape(-1)
        ranks = routing[:, 2 * TOP_K:3 * TOP_K].astype(jnp.int32).reshape(-1)
        slot_of, tile_expert, run_next, src, n_valid = _dispatch_tables(
            expert_ids, ranks, counts[0, :N_EXPERTS].astype(jnp.int32), min(MOE_TM, n))
        y_slots = _experts(h, tile_expert, run_next, src, n_valid, w_expert_gate[l], w_expert_up[l],
                           w_expert_down[l])
        xt = _final(h, y_slots, slot_of, routing,
                    ln2_gain[l].reshape(1, dm), ln2_bias[l].reshape(1, dm), alpha)
    return xt.reshape(batch, seq, dm)
```

```python
import functools
import math

import jax
import jax.numpy as jnp
from jax import lax
from jax.experimental import pallas as pl
from jax.experimental.pallas import tpu as pltpu

F32 = jnp.float32
BF16 = jnp.bfloat16

CHUNK = 64
RET_HEADS = 8
RET_QK_DIM = 128
RET_V_DIM = 256
RET_THETA = 10000.0
DIFF_HEADS = 8
DIFF_QK_DIM = 128
DIFF_V_DIM = 256
ROPE_THETA = 500000.0
ROPE_DIM = DIFF_QK_DIM // 4
N_GROUPS = 4
EXPERTS_PER_GROUP = 8
N_EXPERTS = N_GROUPS * EXPERTS_PER_GROUP
TOP_K = 2
LN_EPS = 1e-5

LANES = 128
BF16_ROWS = 16
NEG_BIG = -1e30

VMEM_LIMIT_BYTES = 56 * 1024 * 1024
TABLE_ROWS = 512
PROJ_TM = 1024
PROJ_TN = 1024
RET_BLOCK = 256
RET_HEAD_GROUP = 8
ATT_BLOCK = 1024
ROUTE_TM = 256
ROUTE_SPLITS = 1
MOE_TM = 256
DMA_GROUP = 8
WEIGHT_CAST_ROWS = 256
NOT_RUN_START = -2
NO_NEXT_EXPERT = -1
FINAL_TM = 256


def _cparams(n_axes):
    return pltpu.CompilerParams(
        dimension_semantics=("arbitrary",) * n_axes, vmem_limit_bytes=VMEM_LIMIT_BYTES)


def _sigmoid(x):
    return 1.0 / (1.0 + jnp.exp(-x))


def _chunk_of(idx):
    return jnp.right_shift(idx, CHUNK.bit_length() - 1)


def _dot(a, b):
    return jnp.dot(a, b, preferred_element_type=F32)


def _dot_nt(a, b):
    return lax.dot_general(a, b, (((1,), (1,)), ((), ())), preferred_element_type=F32)


def _table_kernel(pos_ref, freq_ref, rc_ref, rs_ref, dc_ref, dlo_ref, dhi_ref):
    rh, dh = RET_QK_DIM // 2, ROPE_DIM // 2
    ang = pos_ref[...].astype(F32) * freq_ref[...]
    c, s = jnp.cos(ang), jnp.sin(ang)
    lane = lax.broadcasted_iota(jnp.int32, ang.shape, 1)
    rc_ref[...] = jnp.where(lane < rh, c, pltpu.roll(c, rh, 1))
    rs_ref[...] = jnp.where(lane < rh, -s, pltpu.roll(s, rh, 1))
    c_lo, s_lo = pltpu.roll(c, LANES - rh, 1), pltpu.roll(s, LANES - rh, 1)
    c_hi, s_hi = pltpu.roll(c, LANES - rh + dh, 1), pltpu.roll(s, LANES - rh + dh, 1)
    dc_ref[...] = jnp.where(lane < dh, c_lo, jnp.where(lane < 2 * dh, c_hi, 1.0))
    dlo_ref[...] = jnp.where(lane < dh, -s_lo, 0.0)
    dhi_ref[...] = jnp.where((lane >= dh) & (lane < 2 * dh), s_hi, 0.0)


def _rope_tables(pos_col, freq_row):
    n = pos_col.shape[0]
    tr = min(TABLE_ROWS, n)
    out = pl.BlockSpec((tr, LANES), lambda i: (i, 0))
    shp = jax.ShapeDtypeStruct((n, LANES), F32)
    return pl.pallas_call(
        _table_kernel, grid=(n // tr,),
        in_specs=[pl.BlockSpec((tr, 1), lambda i: (i, 0)), pl.BlockSpec((1, LANES), lambda i: (0, 0))],
        out_specs=[out] * 5, out_shape=[shp] * 5,
        compiler_params=_cparams(1), name="rope_tables",
    )(pos_col, freq_row)


def _matmul_kernel(x_ref, w_ref, o_ref, wb_ref):
    @pl.when(pl.program_id(1) == 0)
    def _():
        wb_ref[...] = w_ref[...].astype(BF16)

    o_ref[...] = _dot(x_ref[...].astype(BF16), wb_ref[...]).astype(o_ref.dtype)


def _matmul(x, w, out_dtype, name):
    m, k = x.shape
    n = w.shape[1]
    tm, tn = min(PROJ_TM, m), min(PROJ_TN, n)
    return pl.pallas_call(
        _matmul_kernel, grid=(n // tn, m // tm),
        in_specs=[pl.BlockSpec((tm, k), lambda j, i: (i, 0)),
                  pl.BlockSpec((k, tn), lambda j, i: (0, j))],
        out_specs=pl.BlockSpec((tm, tn), lambda j, i: (i, j)),
        out_shape=jax.ShapeDtypeStruct((m, n), out_dtype),
        scratch_shapes=[pltpu.VMEM((k, tn), BF16)],
        compiler_params=_cparams(2), name=name,
    )(x, w)


def _retention_kernel(lg_ref, bd_ref, q_ref, k_ref, v_ref, g_ref, cos_ref, sin_ref, gain_ref,
                      o_ref, state_ref, decay_ref, qdecay_ref, kdecay_ref):
    group = state_ref.shape[0]
    first_head = pl.program_id(1) * group
    t = q_ref.shape[0]
    dk, dv = RET_QK_DIM, RET_V_DIM

    @pl.when(pl.program_id(2) == 0)
    def _():
        state_ref[...] = jnp.zeros_like(state_ref)
        ri = lax.broadcasted_iota(jnp.int32, (t, t), 0)
        ci = lax.broadcasted_iota(jnp.int32, (t, t), 1)
        dist = jnp.abs(ri - ci).astype(F32)
        visible = _chunk_of(ri) >= _chunk_of(ci)
        row = lax.broadcasted_iota(jnp.int32, (t, dk), 0).astype(F32)
        for j in range(group):
            lg = lg_ref[first_head + j]
            decay_ref[j] = jnp.where(visible, jnp.exp(lg * dist), 0.0)
            qdecay_ref[j] = jnp.exp(lg * (row + 1.0))
            kdecay_ref[j] = jnp.exp(lg * (t - 1.0 - row))

    cos, sin = cos_ref[...], sin_ref[...]

    def rot(x):
        return x * cos + pltpu.roll(x, dk // 2, 1) * sin

    for j in range(group):
        q = rot(q_ref[:, j * dk:(j + 1) * dk].astype(F32))
        k = rot(k_ref[:, j * dk:(j + 1) * dk].astype(F32)) * (dk ** -0.5)
        v = v_ref[:, j * dv:(j + 1) * dv]

        s = _dot_nt(q.astype(BF16), k.astype(BF16)) * decay_ref[j]
        intra = _dot(s.astype(BF16), v)

        state = state_ref[j]
        cross = _dot((q * qdecay_ref[j]).astype(BF16), state.astype(BF16))
        kd = k * kdecay_ref[j]
        state_ref[j] = state * bd_ref[first_head + j] + _dot(kd.T.astype(BF16), v)

        out = intra + cross
        mu = jnp.mean(out, axis=-1, keepdims=True)
        var = jnp.mean(jnp.square(out - mu), axis=-1, keepdims=True)
        y = (out - mu) * lax.rsqrt(var + LN_EPS) * gain_ref[j]
        gate = g_ref[:, j * dv:(j + 1) * dv].astype(F32)
        o_ref[:, j * dv:(j + 1) * dv] = (y * (gate * _sigmoid(gate))).astype(o_ref.dtype)


def _retention(u, cos, sin, log_gamma, block_decay, gain, batch, seq):
    t = min(RET_BLOCK, seq)
    nb = seq // t
    grp = RET_HEAD_GROUP
    n_groups = RET_HEADS // grp
    rows = lambda b, h, i: b * nb + i
    qk_w, v_w = grp * RET_QK_DIM, grp * RET_V_DIM
    k_off, v_off, g_off = n_groups, n_groups, 2 * n_groups
    smem = pl.BlockSpec(memory_space=pltpu.SMEM)
    return pl.pallas_call(
        _retention_kernel, grid=(batch, n_groups, nb),
        in_specs=[
            smem, smem,
            pl.BlockSpec((t, qk_w), lambda b, h, i: (rows(b, h, i), h)),
            pl.BlockSpec((t, qk_w), lambda b, h, i: (rows(b, h, i), k_off + h)),
            pl.BlockSpec((t, v_w), lambda b, h, i: (rows(b, h, i), v_off + h)),
            pl.BlockSpec((t, v_w), lambda b, h, i: (rows(b, h, i), g_off + h)),
            pl.BlockSpec((t, LANES), lambda b, h, i: (rows(b, h, i), 0)),
            pl.BlockSpec((t, LANES), lambda b, h, i: (rows(b, h, i), 0)),
            pl.BlockSpec((grp, 1, RET_V_DIM), lambda b, h, i: (h, 0, 0)),
        ],
        out_specs=pl.BlockSpec((t, v_w), lambda b, h, i: (rows(b, h, i), h)),
        out_shape=jax.ShapeDtypeStruct((batch * seq, RET_HEADS * RET_V_DIM), BF16),
        scratch_shapes=[pltpu.VMEM((grp, RET_QK_DIM, RET_V_DIM), F32), pltpu.VMEM((grp, t, t), F32),
                        pltpu.VMEM((grp, t, RET_QK_DIM), F32), pltpu.VMEM((grp, t, RET_QK_DIM), F32)],
        compiler_params=_cparams(3), name="retention",
    )(log_gamma, block_decay, u, u, u, u, cos, sin, gain)


def _partial_rope(x, cos, sin_lo, sin_hi):
    half = ROPE_DIM // 2
    return (x * cos + pltpu.roll(x, DIFF_QK_DIM - half, 1) * sin_lo
            + pltpu.roll(x, half, 1) * sin_hi)


def _diff_attn_kernel(lam_ref, q_ref, k_ref, v_ref, qc_ref, qa_ref, qb_ref, kc_ref, ka_ref, kb_ref,
                      gain_ref, o_ref, krot_ref, vt_ref, m_ref, acc_ref, sa_ref, sb_ref, *,
                      lambda_init):
    qblk = pl.program_id(2)
    tq = q_ref.shape[0]
    d = DIFF_QK_DIM

    def key_rows(j):
        return pl.ds(pl.multiple_of(j * tq, tq), tq)

    @pl.when(qblk == 0)
    def _():
        def prep_rows(c, carry):
            rows = key_rows(c)
            for i in range(2):
                x = k_ref[rows, i * d:(i + 1) * d].astype(F32)
                krot_ref[rows, i * d:(i + 1) * d] = _partial_rope(
                    x, kc_ref[rows, :], ka_ref[rows, :], kb_ref[rows, :]).astype(BF16)
            vt_ref[c, :DIFF_V_DIM] = v_ref[rows, :].astype(F32).T.astype(BF16)
            vt_ref[c, DIFF_V_DIM:] = jnp.ones((BF16_ROWS, tq), BF16)
            return carry
        lax.fori_loop(0, k_ref.shape[0] // tq, prep_rows, 0)

    scale = d ** -0.5
    qs = []
    for i in range(2):
        x = q_ref[:, i * d:(i + 1) * d].astype(F32)
        qs.append((_partial_rope(x, qc_ref[...], qa_ref[...], qb_ref[...]) * scale).astype(BF16))

    m_ref[...] = jnp.full_like(m_ref, NEG_BIG)
    acc_ref[...] = jnp.zeros_like(acc_ref)

    s_bufs = (sa_ref, sb_ref)

    def scores(j, buf, diagonal):
        for i in range(2):
            s = _dot_nt(krot_ref[key_rows(j), i * d:(i + 1) * d], qs[i])
            if diagonal:
                key = lax.broadcasted_iota(jnp.int32, s.shape, 0)
                qry = lax.broadcasted_iota(jnp.int32, s.shape, 1)
                s = jnp.where(_chunk_of(qry) >= _chunk_of(key), s, NEG_BIG)
            s_bufs[buf][i] = s

    def absorb(j, buf):
        vt = vt_ref[j]
        for i in range(2):
            s = s_bufs[buf][i]
            m_old = m_ref[i]
            m_new = jnp.maximum(m_old, jnp.max(s, axis=0, keepdims=True))
            alpha = jnp.exp(m_old - m_new)
            p = jnp.exp(s - m_new)
            acc_ref[i] = alpha * acc_ref[i] + _dot(vt, p.astype(BF16))
            m_ref[i] = m_new

    scores(qblk, 0, True)

    def pair(p, carry):
        pending = jnp.where(p == 0, qblk, 2 * p - 1)
        scores(2 * p, 1, False)
        absorb(pending, 0)
        scores(2 * p + 1, 0, False)
        absorb(2 * p, 1)
        return carry
    n_pairs = qblk // 2
    lax.fori_loop(0, n_pairs, pair, 0)
    pending = jnp.where(n_pairs == 0, qblk, 2 * n_pairs - 1)

    @pl.when(qblk % 2 == 1)
    def _():
        scores(qblk - 1, 1, False)
        absorb(pending, 0)
        absorb(qblk - 1, 1)

    @pl.when(qblk % 2 == 0)
    def _():
        absorb(pending, 0)

    lp = lam_ref[...]
    lam = (jnp.exp(jnp.sum(lp[0:1] * lp[1:2], axis=-1, keepdims=True))
           - jnp.exp(jnp.sum(lp[2:3] * lp[3:4], axis=-1, keepdims=True)) + lambda_init)
    dv = DIFF_V_DIM
    o = (acc_ref[0, :dv] / acc_ref[0, dv:dv + 1] - lam * (acc_ref[1, :dv] / acc_ref[1, dv:dv + 1])).T
    o = o * lax.rsqrt(jnp.mean(jnp.square(o), axis=-1, keepdims=True) + LN_EPS) * gain_ref[...]
    o_ref[...] = (o * (1.0 - lambda_init)).astype(o_ref.dtype)


def _diff_attention(u, cos, sin_lo, sin_hi, lam_params, gain, batch, seq, lambda_init):
    tq = min(ATT_BLOCK, seq)
    nb = seq // tq
    width = 2 * DIFF_QK_DIM
    ret_cols = 2 * RET_HEADS * RET_QK_DIM + 2 * RET_HEADS * RET_V_DIM
    q_off = ret_cols // width
    k_off = q_off + DIFF_HEADS
    v_off = k_off + DIFF_HEADS
    qrow = lambda b, h, i: b * nb + i
    qtab = pl.BlockSpec((tq, LANES), lambda b, h, i: (qrow(b, h, i), 0))
    ktab = pl.BlockSpec((seq, LANES), lambda b, h, i: (b, 0), pipeline_mode=pl.Buffered(1))
    return pl.pallas_call(
        functools.partial(_diff_attn_kernel, lambda_init=lambda_init),
        grid=(batch, DIFF_HEADS, nb),
        in_specs=[
            pl.BlockSpec((4, DIFF_QK_DIM), lambda b, h, i: (0, 0)),
            pl.BlockSpec((tq, width), lambda b, h, i: (qrow(b, h, i), q_off + h)),
            pl.BlockSpec((seq, width), lambda b, h, i: (b, k_off + h)),
            pl.BlockSpec((seq, DIFF_V_DIM), lambda b, h, i: (b, v_off + h)),
            qtab, qtab, qtab, ktab, ktab, ktab,
            pl.BlockSpec((1, DIFF_V_DIM), lambda b, h, i: (0, 0)),
        ],
        out_specs=pl.BlockSpec((tq, DIFF_V_DIM), lambda b, h, i: (qrow(b, h, i), h)),
        out_shape=jax.ShapeDtypeStruct((batch * seq, DIFF_HEADS * DIFF_V_DIM), BF16),
        scratch_shapes=[pltpu.VMEM((seq, width), BF16),
                        pltpu.VMEM((nb, DIFF_V_DIM + BF16_ROWS, tq), BF16),
                        pltpu.VMEM((2, 1, tq), F32),
                        pltpu.VMEM((2, DIFF_V_DIM + BF16_ROWS, tq), F32),
                        pltpu.VMEM((2, tq, tq), F32), pltpu.VMEM((2, tq, tq), F32)],
        compiler_params=_cparams(3), name="diff_attention",
    )(lam_params, u, u, u, cos, sin_lo, sin_hi, cos, sin_lo, sin_hi, gain)


def _layer_norm(z, gain, bias):
    mu = jnp.mean(z, axis=-1, keepdims=True)
    var = jnp.mean(jnp.square(z - mu), axis=-1, keepdims=True)
    return (z - mu) * lax.rsqrt(var + LN_EPS) * gain + bias


def _split_bf16(a):
    hi = a.astype(BF16)
    return hi, (a - hi.astype(F32)).astype(BF16)


def _route_kernel(ret_ref, diff_ref, gr_ref, gd_ref, x_ref, wret_ref, wdiff_ref, wo_ref, gain_ref,
                  bias_ref, wr_ref, br_ref, h_ref, r_ref, count_ref, *, alpha):
    @pl.when(pl.program_id(0) == 0)
    def _():
        count_ref[...] = jnp.zeros_like(count_ref)

    w_hi, w_lo = _split_bf16(wr_ref[...])
    w_both = jnp.concatenate([w_hi, w_lo], axis=1)
    part = h_ref.shape[0] // ROUTE_SPLITS
    for g in range(ROUTE_SPLITS):
        rows = slice(g * part, (g + 1) * part)
        merged = (_sigmoid(gr_ref[rows, :].astype(F32)) * _dot(ret_ref[rows, :], wret_ref[...])
                  + _sigmoid(gd_ref[rows, :].astype(F32)) * _dot(diff_ref[rows, :], wdiff_ref[...]))
        h = _layer_norm(alpha * x_ref[rows, :] + _dot(merged.astype(BF16), wo_ref[...]),
                        gain_ref[...], bias_ref[...])
        h_ref[rows, :] = h
        h_hi, h_lo = _split_bf16(h)
        both = _dot(h_hi, w_both)
        logits = both[:, :LANES] + (both[:, LANES:] + _dot(h_lo, w_hi)) + br_ref[...]
        r_ref[rows, :] = _routing_rows(logits, count_ref)


def _routing_rows(logits, count_ref):
    lane = lax.broadcasted_iota(jnp.int32, logits.shape, 1).astype(F32)

    def first_argmax(vals):
        top = jnp.max(vals, axis=-1, keepdims=True)
        idx = jnp.min(jnp.where(vals == top, lane, float(LANES)), axis=-1, keepdims=True)
        return top, idx

    group_logits = jnp.where(lane < N_GROUPS, logits, NEG_BIG)
    g_top, g_idx = first_argmax(group_logits)
    p_group = 1.0 / jnp.sum(jnp.exp(group_logits - g_top), axis=-1, keepdims=True)

    first = N_GROUPS + g_idx * EXPERTS_PER_GROUP
    within = jnp.where((lane >= first) & (lane < first + EXPERTS_PER_GROUP), logits, NEG_BIG)
    v1, i1 = first_argmax(within)
    v2, i2 = first_argmax(jnp.where(lane == i1, NEG_BIG, within))
    e2 = jnp.exp(v2 - v1)
    w1 = p_group / (1.0 + e2)
    w2 = p_group * e2 / (1.0 + e2)
    x1, x2 = i1 - N_GROUPS, i2 - N_GROUPS

    hit1 = jnp.where(lane == x1, 1.0, 0.0)
    hit2 = jnp.where(lane == x2, 1.0, 0.0)
    hits = hit1 + hit2
    tm = hits.shape[0]
    earlier = (lax.broadcasted_iota(jnp.int32, (tm, tm), 0) > lax.broadcasted_iota(jnp.int32, (tm, tm), 1))
    before = count_ref[...] + _dot(jnp.where(earlier, 1.0, 0.0).astype(BF16), hits.astype(BF16))
    rank1 = jnp.sum(before * hit1, axis=-1, keepdims=True)
    rank2 = jnp.sum(before * hit2, axis=-1, keepdims=True)
    count_ref[...] += jnp.sum(hits, axis=0, keepdims=True)

    fields = (x1, x2, w1, w2, rank1, rank2)
    row = jnp.zeros_like(logits)
    for pos, value in enumerate(fields):
        row = jnp.where(lane == pos, value, row)
    return row


def _route(ret, diff, u, x, w_ret, w_diff, w_out, gain, bias, w_router, b_router, alpha):
    n, dm = x.shape
    tm = min(ROUTE_TM, n)
    gate_off = (u.shape[1] - 2 * dm) // dm
    tile = pl.BlockSpec((tm, dm), lambda i: (i, 0))
    gate = lambda c: pl.BlockSpec((tm, dm), lambda i: (i, c))
    full = lambda a: pl.BlockSpec(a.shape, lambda i: (0, 0), pipeline_mode=pl.Buffered(1))
    return pl.pallas_call(
        functools.partial(_route_kernel, alpha=alpha), grid=(n // tm,),
        in_specs=[pl.BlockSpec((tm, ret.shape[1]), lambda i: (i, 0)),
                  pl.BlockSpec((tm, diff.shape[1]), lambda i: (i, 0)),
                  gate(gate_off), gate(gate_off + 1), tile, full(w_ret), full(w_diff), full(w_out),
                  full(gain), full(bias), full(w_router), full(b_router)],
        out_specs=[tile, pl.BlockSpec((tm, LANES), lambda i: (i, 0)),
                   pl.BlockSpec((1, LANES), lambda i: (0, 0))],
        out_shape=[jax.ShapeDtypeStruct((n, dm), F32), jax.ShapeDtypeStruct((n, LANES), F32),
                   jax.ShapeDtypeStruct((1, LANES), F32)],
        compiler_params=_cparams(1), name="merge_outproj_ln_router",
    )(ret, diff, u, u, x, w_ret, w_diff, w_out, gain, bias, w_router, b_router)


def _row_copies(table_ref, first, n_rows, make_copy):
    for g in range(n_rows // DMA_GROUP):
        entries = [table_ref[first + g * DMA_GROUP + k] for k in range(DMA_GROUP)]
        for k, entry in enumerate(entries):
            make_copy(entry, g * DMA_GROUP + k).start()


def _for_buffer(dynamic_buf, fn):
    for buf in range(2):
        pl.when(dynamic_buf == buf)(functools.partial(fn, buf))


def _expert_kernel(tile_expert_ref, run_next_ref, src_ref, n_valid_ref, h_hbm, wg_hbm, wu_hbm, wd_hbm,
                   y_ref, xbuf, stage_g, stage_u, stage_d, wg_ref, wu_ref, wd_ref, gather_sem,
                   weight_sem):
    i = pl.program_id(0)
    n_valid = n_valid_ref[0]
    tm = xbuf.shape[1]
    slot = i & 1
    weights = ((wg_hbm, stage_g, wg_ref), (wu_hbm, stage_u, wu_ref), (wd_hbm, stage_d, wd_ref))

    def start_weights(expert):
        for k, (hbm, stage, _) in enumerate(weights):
            pltpu.make_async_copy(hbm.at[expert], stage, weight_sem.at[k]).start(priority=1)

    def take_weights():
        for k, (hbm, stage, dst) in enumerate(weights):
            pltpu.make_async_copy(hbm.at[0], stage, weight_sem.at[k]).wait()

            def round_rows(c, carry, stage=stage, dst=dst):
                rows = pl.ds(pl.multiple_of(c * WEIGHT_CAST_ROWS, WEIGHT_CAST_ROWS), WEIGHT_CAST_ROWS)
                dst[rows, :] = stage[rows, :].astype(BF16)
                return carry
            lax.fori_loop(0, stage.shape[0] // WEIGHT_CAST_ROWS, round_rows, 0)

    def start_gather(tile, dynamic_buf):
        def issue(buf):
            _row_copies(src_ref, tile * tm, tm, lambda tok, r: pltpu.make_async_copy(
                h_hbm.at[pl.ds(tok, 1)], xbuf.at[buf, pl.ds(r, 1)], gather_sem.at[buf]))
        _for_buffer(dynamic_buf, issue)

    @pl.when(i == 0)
    def _():
        start_gather(0, 0)
        start_weights(tile_expert_ref[0])

    @pl.when(i + 1 < n_valid)
    def _():
        start_gather(i + 1, 1 - slot)

    @pl.when(i < n_valid)
    def _():
        run_next = run_next_ref[i]

        @pl.when(run_next != NOT_RUN_START)
        def _():
            take_weights()

            @pl.when(run_next != NO_NEXT_EXPERT)
            def _():
                start_weights(run_next)

        pltpu.make_async_copy(h_hbm.at[pl.ds(0, tm)], xbuf.at[slot], gather_sem.at[slot]).wait()
        x = xbuf[slot].astype(BF16)
        g = _dot(x, wg_ref[...])
        act = (g * _sigmoid(g) * _dot(x, wu_ref[...])).astype(BF16)
        y_ref[...] = _dot(act, wd_ref[...])

    @pl.when(i >= n_valid)
    def _():
        y_ref[...] = jnp.zeros_like(y_ref)


def _experts(h, tile_expert, run_next, src, n_valid, w_gate, w_up, w_down):
    dm = h.shape[1]
    n_tiles = tile_expert.shape[0]
    tm = src.shape[0] // n_tiles
    ff = w_gate.shape[2]
    hbm = pl.BlockSpec(memory_space=pl.ANY)
    grid_spec = pltpu.PrefetchScalarGridSpec(
        num_scalar_prefetch=4, grid=(n_tiles,),
        in_specs=[hbm, hbm, hbm, hbm],
        out_specs=pl.BlockSpec((tm, dm), lambda i, *_: (i, 0)),
        scratch_shapes=[pltpu.VMEM((2, tm, dm), F32),
                        pltpu.VMEM((dm, ff), F32), pltpu.VMEM((dm, ff), F32), pltpu.VMEM((ff, dm), F32),
                        pltpu.VMEM((dm, ff), BF16), pltpu.VMEM((dm, ff), BF16), pltpu.VMEM((ff, dm), BF16),
                        pltpu.SemaphoreType.DMA((2,)), pltpu.SemaphoreType.DMA((3,))])
    return pl.pallas_call(
        _expert_kernel, grid_spec=grid_spec,
        out_shape=jax.ShapeDtypeStruct((src.shape[0], dm), F32),
        compiler_params=_cparams(1), name="experts",
    )(tile_expert, run_next, src, n_valid, h, w_gate, w_up, w_down)


def _dispatch_tables(expert_ids, ranks, counts, tm):
    n_assign = expert_ids.shape[0]
    n_tiles = n_assign // tm + N_EXPERTS
    n_slots = n_tiles * tm
    experts = jnp.arange(N_EXPERTS, dtype=jnp.int32)
    padded = (counts + tm - 1) // tm * tm
    pad_end = jnp.cumsum(padded)
    pad_start = pad_end - padded
    slot_of = jnp.sum(jnp.where(expert_ids[:, None] == experts[None, :], pad_start[None, :], 0), axis=1) + ranks
    tile_start = jnp.arange(n_tiles, dtype=jnp.int32) * tm
    tile_expert = jnp.minimum(
        jnp.sum((pad_end[None, :] <= tile_start[:, None]).astype(jnp.int32), axis=1), N_EXPERTS - 1)
    later = lax.cummin(jnp.where(counts > 0, experts, N_EXPERTS)[::-1])[::-1]
    following = jnp.concatenate([later[1:], jnp.full((1,), N_EXPERTS, jnp.int32)])[tile_expert]
    run_start = jnp.concatenate([jnp.ones((1,), bool), tile_expert[1:] != tile_expert[:-1]])
    run_next = jnp.where(run_start, jnp.where(following < N_EXPERTS, following, NO_NEXT_EXPERT),
                         NOT_RUN_START).astype(jnp.int32)
    by_slot = jnp.argsort(slot_of).astype(jnp.int32) // TOP_K
    dense_start = jnp.cumsum(counts) - counts
    tile_shift = (dense_start - pad_start)[tile_expert]
    dense_index = jnp.repeat(tile_shift, tm) + jnp.arange(n_slots, dtype=jnp.int32)
    src = by_slot[jnp.clip(dense_index, 0, n_assign - 1)]
    n_valid = (pad_end[-1] // tm).astype(jnp.int32).reshape(1)
    return slot_of, tile_expert, run_next, src, n_valid


def _final_kernel(slot_ref, h_ref, y_hbm, r_ref, gain_ref, bias_ref, o_ref, ybuf, sem, *, alpha):
    i = pl.program_id(0)
    tm = o_ref.shape[0]
    slot = i & 1

    def start_gather(step, dynamic_buf):
        def issue(buf):
            _row_copies(slot_ref, step * (TOP_K * tm), TOP_K * tm, lambda row, r: pltpu.make_async_copy(
                y_hbm.at[pl.ds(row, 1)], ybuf.at[buf, r % TOP_K, pl.ds(r // TOP_K, 1)], sem.at[buf]))
        _for_buffer(dynamic_buf, issue)

    @pl.when(i == 0)
    def _():
        start_gather(0, 0)

    @pl.when(i + 1 < pl.num_programs(0))
    def _():
        start_gather(i + 1, 1 - slot)

    for k in range(TOP_K):
        pltpu.make_async_copy(y_hbm.at[pl.ds(0, tm)], ybuf.at[slot, k], sem.at[slot]).wait()
    r = r_ref[...]
    ffn = ybuf[slot, 0] * r[:, 2:3] + ybuf[slot, 1] * r[:, 3:4]
    o_ref[...] = _layer_norm(alpha * h_ref[...] + ffn, gain_ref[...], bias_ref[...])


def _final(h, y_slots, slot_of, routing, gain, bias, alpha):
    n, dm = h.shape
    tm = min(FINAL_TM, n)
    tile = pl.BlockSpec((tm, dm), lambda i, *_: (i, 0))
    full = lambda a: pl.BlockSpec(a.shape, lambda i, *_: (0, 0))
    grid_spec = pltpu.PrefetchScalarGridSpec(
        num_scalar_prefetch=1, grid=(n // tm,),
        in_specs=[tile, pl.BlockSpec(memory_space=pl.ANY), pl.BlockSpec((tm, LANES), lambda i, *_: (i, 0)),
                  full(gain), full(bias)],
        out_specs=tile,
        scratch_shapes=[pltpu.VMEM((2, TOP_K, tm, dm), F32), pltpu.SemaphoreType.DMA((2,))])
    return pl.pallas_call(
        functools.partial(_final_kernel, alpha=alpha), grid_spec=grid_spec,
        out_shape=jax.ShapeDtypeStruct((n, dm), F32),
        compiler_params=_cparams(1), name="combine_ln",
    )(slot_of, h, y_slots, routing, gain, bias)


def _rope_frequencies():
    ret_freq = 1.0 / (RET_THETA ** (jnp.arange(0, RET_QK_DIM, 2, dtype=F32) / RET_QK_DIM))
    diff_freq = 1.0 / (ROPE_THETA ** (jnp.arange(0, ROPE_DIM, 2, dtype=F32) / ROPE_DIM))
    pad = jnp.zeros((LANES - ret_freq.shape[0] - diff_freq.shape[0],), F32)
    return jnp.concatenate([ret_freq, diff_freq, pad]).reshape(1, LANES)


def kernel(x, positions, w_in, ret_gn_gain, diff_lambda_q1, diff_lambda_k1, diff_lambda_q2, diff_lambda_k2, diff_subln_gain, w_ret_proj, w_diff_proj, w_out, ln1_gain, ln1_bias, w_group, b_group, w_router, b_router, w_expert_gate, w_expert_up, w_expert_down, ln2_gain, ln2_bias):
    batch, seq, dm = x.shape
    n = batch * seq
    depth = w_in.shape[0]
    assert seq % CHUNK == 0
    alpha = (2 * depth) ** 0.25

    ret_cos, ret_sin, diff_cos, diff_sin_lo, diff_sin_hi = _rope_tables(
        positions.reshape(n, 1), _rope_frequencies())

    log_gamma = jnp.log(1.0 - 2.0 ** (-5.0 - jnp.arange(RET_HEADS, dtype=F32)))
    block_decay = jnp.exp(log_gamma * min(RET_BLOCK, seq))

    xt = x.reshape(n, dm)
    for l in range(depth):
        lambda_init = 0.8 - 0.6 * math.exp(-0.3 * l)
        u = _matmul(xt, w_in[l], BF16, "in_proj")
        ret = _retention(u, ret_cos, ret_sin, log_gamma, block_decay,
                         ret_gn_gain[l].reshape(RET_HEADS, 1, RET_V_DIM), batch, seq)
        lam_params = jnp.stack([diff_lambda_q1[l], diff_lambda_k1[l], diff_lambda_q2[l], diff_lambda_k2[l]])
        diff = _diff_attention(u, diff_cos, diff_sin_lo, diff_sin_hi, lam_params.astype(F32),
                               diff_subln_gain[l].reshape(1, DIFF_V_DIM), batch, seq, lambda_init)
        w_route = jnp.concatenate(
            [w_group[l], w_router[l], jnp.zeros((dm, LANES - N_GROUPS - N_EXPERTS), F32)], axis=1)
        b_route = jnp.concatenate(
            [b_group[l], b_router[l], jnp.zeros((LANES - N_GROUPS - N_EXPERTS,), F32)]).reshape(1, LANES)
        h, routing, counts = _route(
            ret, diff, u, xt, w_ret_proj[l].astype(BF16), w_diff_proj[l].astype(BF16),
            w_out[l].astype(BF16), ln1_gain[l].reshape(1, dm), ln1_bias[l].reshape(1, dm),
            w_route, b_route, alpha)

        expert_ids = routing[:, :TOP_K].astype(jnp.int32).reshape(-1)
        ranks = routing[:, 2 * TOP_K:3 * TOP_K].astype(jnp.int32).reshape(-1)
        slot_of, tile_expert, run_next, src, n_valid = _dispatch_tables(
            expert_ids, ranks, counts[0, :N_EXPERTS].astype(jnp.int32), min(MOE_TM, n))
        y_slots = _experts(h, tile_expert, run_next, src, n_valid, w_expert_gate[l], w_expert_up[l],
                           w_expert_down[l])
        xt = _final(h, y_slots, slot_of, routing,
                    ln2_gain[l].reshape(1, dm), ln2_bias[l].reshape(1, dm), alpha)
    return xt.reshape(batch, seq, dm)
```

```python
import functools
import math

import jax
import jax.numpy as jnp
from jax import lax
from jax.experimental import pallas as pl
from jax.experimental.pallas import tpu as pltpu

F32 = jnp.float32
BF16 = jnp.bfloat16

CHUNK = 64
RET_HEADS = 8
RET_QK_DIM = 128
RET_V_DIM = 256
RET_THETA = 10000.0
DIFF_HEADS = 8
DIFF_QK_DIM = 128
DIFF_V_DIM = 256
ROPE_THETA = 500000.0
ROPE_DIM = DIFF_QK_DIM // 4
N_GROUPS = 4
EXPERTS_PER_GROUP = 8
N_EXPERTS = N_GROUPS * EXPERTS_PER_GROUP
TOP_K = 2
LN_EPS = 1e-5

LANES = 128
BF16_ROWS = 16
NEG_BIG = -1e30

VMEM_LIMIT_BYTES = 56 * 1024 * 1024
TABLE_ROWS = 512
PROJ_TM = 1024
PROJ_TN = 1024
RET_BLOCK = 256
RET_HEAD_GROUP = 8
ATT_BLOCK = 1024
ROUTE_TM = 256
ROUTE_SPLITS = 1
MOE_TM = 256
DMA_GROUP = 8
WEIGHT_CAST_ROWS = 256
NOT_RUN_START = -2
NO_NEXT_EXPERT = -1
FINAL_TM = 256


def _cparams(n_axes):
    return pltpu.CompilerParams(
        dimension_semantics=("arbitrary",) * n_axes, vmem_limit_bytes=VMEM_LIMIT_BYTES)


def _sigmoid(x):
    return 1.0 / (1.0 + jnp.exp(-x))


def _chunk_of(idx):
    return jnp.right_shift(idx, CHUNK.bit_length() - 1)


def _dot(a, b):
    return jnp.dot(a, b, preferred_element_type=F32)


def _dot_nt(a, b):
    return lax.dot_general(a, b, (((1,), (1,)), ((), ())), preferred_element_type=F32)


def _table_kernel(pos_ref, freq_ref, rc_ref, rs_ref, dc_ref, dlo_ref, dhi_ref):
    rh, dh = RET_QK_DIM // 2, ROPE_DIM // 2
    ang = pos_ref[...].astype(F32) * freq_ref[...]
    c, s = jnp.cos(ang), jnp.sin(ang)
    lane = lax.broadcasted_iota(jnp.int32, ang.shape, 1)
    rc_ref[...] = jnp.where(lane < rh, c, pltpu.roll(c, rh, 1))
    rs_ref[...] = jnp.where(lane < rh, -s, pltpu.roll(s, rh, 1))
    c_lo, s_lo = pltpu.roll(c, LANES - rh, 1), pltpu.roll(s, LANES - rh, 1)
    c_hi, s_hi = pltpu.roll(c, LANES - rh + dh, 1), pltpu.roll(s, LANES - rh + dh, 1)
    dc_ref[...] = jnp.where(lane < dh, c_lo, jnp.where(lane < 2 * dh, c_hi, 1.0))
    dlo_ref[...] = jnp.where(lane < dh, -s_lo, 0.0)
    dhi_ref[...] = jnp.where((lane >= dh) & (lane < 2 * dh), s_hi, 0.0)


def _rope_tables(pos_col, freq_row):
    n = pos_col.shape[0]
    tr = min(TABLE_ROWS, n)
    out = pl.BlockSpec((tr, LANES), lambda i: (i, 0))
    shp = jax.ShapeDtypeStruct((n, LANES), F32)
    return pl.pallas_call(
        _table_kernel, grid=(n // tr,),
        in_specs=[pl.BlockSpec((tr, 1), lambda i: (i, 0)), pl.BlockSpec((1, LANES), lambda i: (0, 0))],
        out_specs=[out] * 5, out_shape=[shp] * 5,
        compiler_params=_cparams(1), name="rope_tables",
    )(pos_col, freq_row)


def _matmul_kernel(x_ref, w_ref, o_ref, wb_ref):
    @pl.when(pl.program_id(1) == 0)
    def _():
        wb_ref[...] = w_ref[...].astype(BF16)

    o_ref[...] = _dot(x_ref[...].astype(BF16), wb_ref[...]).astype(o_ref.dtype)


def _matmul(x, w, out_dtype, name):
    m, k = x.shape
    n = w.shape[1]
    tm, tn = min(PROJ_TM, m), min(PROJ_TN, n)
    return pl.pallas_call(
        _matmul_kernel, grid=(n // tn, m // tm),
        in_specs=[pl.BlockSpec((tm, k), lambda j, i: (i, 0)),
                  pl.BlockSpec((k, tn), lambda j, i: (0, j))],
        out_specs=pl.BlockSpec((tm, tn), lambda j, i: (i, j)),
        out_shape=jax.ShapeDtypeStruct((m, n), out_dtype),
        scratch_shapes=[pltpu.VMEM((k, tn), BF16)],
        compiler_params=_cparams(2), name=name,
    )(x, w)


def _retention_kernel(lg_ref, bd_ref, q_ref, k_ref, v_ref, g_ref, cos_ref, sin_ref, gain_ref,
                      o_ref, state_ref, decay_ref, qdecay_ref, kdecay_ref):
    group = state_ref.shape[0]
    first_head = pl.program_id(1) * group
    t = q_ref.shape[0]
    dk, dv = RET_QK_DIM, RET_V_DIM

    @pl.when(pl.program_id(2) == 0)
    def _():
        state_ref[...] = jnp.zeros_like(state_ref)
        ri = lax.broadcasted_iota(jnp.int32, (t, t), 0)
        ci = lax.broadcasted_iota(jnp.int32, (t, t), 1)
        dist = jnp.abs(ri - ci).astype(F32)
        visible = _chunk_of(ri) >= _chunk_of(ci)
        row = lax.broadcasted_iota(jnp.int32, (t, dk), 0).astype(F32)
        for j in range(group):
            lg = lg_ref[first_head + j]
            decay_ref[j] = jnp.where(visible, jnp.exp(lg * dist), 0.0)
            qdecay_ref[j] = jnp.exp(lg * (row + 1.0))
            kdecay_ref[j] = jnp.exp(lg * (t - 1.0 - row))

    cos, sin = cos_ref[...], sin_ref[...]

    def rot(x):
        return x * cos + pltpu.roll(x, dk // 2, 1) * sin

    for j in range(group):
        q = rot(q_ref[:, j * dk:(j + 1) * dk].astype(F32))
        k = rot(k_ref[:, j * dk:(j + 1) * dk].astype(F32)) * (dk ** -0.5)
        v = v_ref[:, j * dv:(j + 1) * dv]

        s = _dot_nt(q.astype(BF16), k.astype(BF16)) * decay_ref[j]
        intra = _dot(s.astype(BF16), v)

        state = state_ref[j]
        cross = _dot((q * qdecay_ref[j]).astype(BF16), state.astype(BF16))
        kd = k * kdecay_ref[j]
        state_ref[j] = state * bd_ref[first_head + j] + _dot(kd.T.astype(BF16), v)

        out = intra + cross
        mu = jnp.mean(out, axis=-1, keepdims=True)
        var = jnp.mean(jnp.square(out - mu), axis=-1, keepdims=True)
        y = (out - mu) * lax.rsqrt(var + LN_EPS) * gain_ref[j]
        gate = g_ref[:, j * dv:(j + 1) * dv].astype(F32)
        o_ref[:, j * dv:(j + 1) * dv] = (y * (gate * _sigmoid(gate))).astype(o_ref.dtype)


def _retention(u, cos, sin, log_gamma, block_decay, gain, batch, seq):
    t = min(RET_BLOCK, seq)
    nb = seq // t
    grp = RET_HEAD_GROUP
    n_groups = RET_HEADS // grp
    rows = lambda b, h, i: b * nb + i
    qk_w, v_w = grp * RET_QK_DIM, grp * RET_V_DIM
    k_off, v_off, g_off = n_groups, n_groups, 2 * n_groups
    smem = pl.BlockSpec(memory_space=pltpu.SMEM)
    return pl.pallas_call(
        _retention_kernel, grid=(batch, n_groups, nb),
        in_specs=[
            smem, smem,
            pl.BlockSpec((t, qk_w), lambda b, h, i: (rows(b, h, i), h)),
            pl.BlockSpec((t, qk_w), lambda b, h, i: (rows(b, h, i), k_off + h)),
            pl.BlockSpec((t, v_w), lambda b, h, i: (rows(b, h, i), v_off + h)),
            pl.BlockSpec((t, v_w), lambda b, h, i: (rows(b, h, i), g_off + h)),
            pl.BlockSpec((t, LANES), lambda b, h, i: (rows(b, h, i), 0)),
            pl.BlockSpec((t, LANES), lambda b, h, i: (rows(b, h, i), 0)),
            pl.BlockSpec((grp, 1, RET_V_DIM), lambda b, h, i: (h, 0, 0)),
        ],
        out_specs=pl.BlockSpec((t, v_w), lambda b, h, i: (rows(b, h, i), h)),
        out_shape=jax.ShapeDtypeStruct((batch * seq, RET_HEADS * RET_V_DIM), BF16),
        scratch_shapes=[pltpu.VMEM((grp, RET_QK_DIM, RET_V_DIM), F32), pltpu.VMEM((grp, t, t), F32),
                        pltpu.VMEM((grp, t, RET_QK_DIM), F32), pltpu.VMEM((grp, t, RET_QK_DIM), F32)],
        compiler_params=_cparams(3), name="retention",
    )(log_gamma, block_decay, u, u, u, u, cos, sin, gain)


def _partial_rope(x, cos, sin_lo, sin_hi):
    half = ROPE_DIM // 2
    return (x * cos + pltpu.roll(x, DIFF_QK_DIM - half, 1) * sin_lo
            + pltpu.roll(x, half, 1) * sin_hi)


def _diff_attn_kernel(lam_ref, q_ref, k_ref, v_ref, qc_ref, qa_ref, qb_ref, kc_ref, ka_ref, kb_ref,
                      gain_ref, o_ref, krot_ref, vt_ref, m_ref, acc_ref, sa_ref, sb_ref, *,
                      lambda_init):
    qblk = pl.program_id(2)
    tq = q_ref.shape[0]
    d = DIFF_QK_DIM

    def key_rows(j):
        return pl.ds(pl.multiple_of(j * tq, tq), tq)

    @pl.when(qblk == 0)
    def _():
        def prep_rows(c, carry):
            rows = key_rows(c)
            for i in range(2):
                x = k_ref[rows, i * d:(i + 1) * d].astype(F32)
                krot_ref[rows, i * d:(i + 1) * d] = _partial_rope(
                    x, kc_ref[rows, :], ka_ref[rows, :], kb_ref[rows, :]).astype(BF16)
            vt_ref[c, :DIFF_V_DIM] = v_ref[rows, :].astype(F32).T.astype(BF16)
            vt_ref[c, DIFF_V_DIM:] = jnp.ones((BF16_ROWS, tq), BF16)
            return carry
        lax.fori_loop(0, k_ref.shape[0] // tq, prep_rows, 0)

    scale = d ** -0.5
    qs = []
    for i in range(2):
        x = q_ref[:, i * d:(i + 1) * d].astype(F32)
        qs.append((_partial_rope(x, qc_ref[...], qa_ref[...], qb_ref[...]) * scale).astype(BF16))

    m_ref[...] = jnp.full_like(m_ref, NEG_BIG)
    acc_ref[...] = jnp.zeros_like(acc_ref)

    s_bufs = (sa_ref, sb_ref)

    def scores(j, buf, diagonal):
        for i in range(2):
            s = _dot_nt(krot_ref[key_rows(j), i * d:(i + 1) * d], qs[i])
            if diagonal:
                key = lax.broadcasted_iota(jnp.int32, s.shape, 0)
                qry = lax.broadcasted_iota(jnp.int32, s.shape, 1)
                s = jnp.where(_chunk_of(qry) >= _chunk_of(key), s, NEG_BIG)
            s_bufs[buf][i] = s

    def absorb(j, buf):
        vt = vt_ref[j]
        for i in range(2):
            s = s_bufs[buf][i]
            m_old = m_ref[i]
            m_new = jnp.maximum(m_old, jnp.max(s, axis=0, keepdims=True))
            alpha = jnp.exp(m_old - m_new)
            p = jnp.exp(s - m_new)
            acc_ref[i] = alpha * acc_ref[i] + _dot(vt, p.astype(BF16))
            m_ref[i] = m_new

    scores(qblk, 0, True)

    def pair(p, carry):
        pending = jnp.where(p == 0, qblk, 2 * p - 1)
        scores(2 * p, 1, False)
        absorb(pending, 0)
        scores(2 * p + 1, 0, False)
        absorb(2 * p, 1)
        return carry
    n_pairs = qblk // 2
    lax.fori_loop(0, n_pairs, pair, 0)
    pending = jnp.where(n_pairs == 0, qblk, 2 * n_pairs - 1)

    @pl.when(qblk % 2 == 1)
    def _():
        scores(qblk - 1, 1, False)
        absorb(pending, 0)
        absorb(qblk - 1, 1)

    @pl.when(qblk % 2 == 0)
    def _():
        absorb(pending, 0)

    lp = lam_ref[...]
    lam = (jnp.exp(jnp.sum(lp[0:1] * lp[1:2], axis=-1, keepdims=True))
           - jnp.exp(jnp.sum(lp[2:3] * lp[3:4], axis=-1, keepdims=True)) + lambda_init)
    dv = DIFF_V_DIM
    o = (acc_ref[0, :dv] / acc_ref[0, dv:dv + 1] - lam * (acc_ref[1, :dv] / acc_ref[1, dv:dv + 1])).T
    o = o * lax.rsqrt(jnp.mean(jnp.square(o), axis=-1, keepdims=True) + LN_EPS) * gain_ref[...]
    o_ref[...] = (o * (1.0 - lambda_init)).astype(o_ref.dtype)


def _diff_attention(u, cos, sin_lo, sin_hi, lam_params, gain, batch, seq, lambda_init):
    tq = min(ATT_BLOCK, seq)
    nb = seq // tq
    width = 2 * DIFF_QK_DIM
    ret_cols = 2 * RET_HEADS * RET_QK_DIM + 2 * RET_HEADS * RET_V_DIM
    q_off = ret_cols // width
    k_off = q_off + DIFF_HEADS
    v_off = k_off + DIFF_HEADS
    qrow = lambda b, h, i: b * nb + i
    qtab = pl.BlockSpec((tq, LANES), lambda b, h, i: (qrow(b, h, i), 0))
    ktab = pl.BlockSpec((seq, LANES), lambda b, h, i: (b, 0), pipeline_mode=pl.Buffered(1))
    return pl.pallas_call(
        functools.partial(_diff_attn_kernel, lambda_init=lambda_init),
        grid=(batch, DIFF_HEADS, nb),
        in_specs=[
            pl.BlockSpec((4, DIFF_QK_DIM), lambda b, h, i: (0, 0)),
            pl.BlockSpec((tq, width), lambda b, h, i: (qrow(b, h, i), q_off + h)),
            pl.BlockSpec((seq, width), lambda b, h, i: (b, k_off + h)),
            pl.BlockSpec((seq, DIFF_V_DIM), lambda b, h, i: (b, v_off + h)),
            qtab, qtab, qtab, ktab, ktab, ktab,
            pl.BlockSpec((1, DIFF_V_DIM), lambda b, h, i: (0, 0)),
        ],
        out_specs=pl.BlockSpec((tq, DIFF_V_DIM), lambda b, h, i: (qrow(b, h, i), h)),
        out_shape=jax.ShapeDtypeStruct((batch * seq, DIFF_HEADS * DIFF_V_DIM), BF16),
        scratch_shapes=[pltpu.VMEM((seq, width), BF16),
                        pltpu.VMEM((nb, DIFF_V_DIM + BF16_ROWS, tq), BF16),
                        pltpu.VMEM((2, 1, tq), F32),
                        pltpu.VMEM((2, DIFF_V_DIM + BF16_ROWS, tq), F32),
                        pltpu.VMEM((2, tq, tq), F32), pltpu.VMEM((2, tq, tq), F32)],
        compiler_params=_cparams(3), name="diff_attention",
    )(lam_params, u, u, u, cos, sin_lo, sin_hi, cos, sin_lo, sin_hi, gain)


def _layer_norm(z, gain, bias):
    mu = jnp.mean(z, axis=-1, keepdims=True)
    var = jnp.mean(jnp.square(z - mu), axis=-1, keepdims=True)
    return (z - mu) * lax.rsqrt(var + LN_EPS) * gain + bias


def _split_bf16(a):
    hi = a.astype(BF16)
    return hi, (a - hi.astype(F32)).astype(BF16)


def _route_kernel(ret_ref, diff_ref, gr_ref, gd_ref, x_ref, wret_ref, wdiff_ref, wo_ref, gain_ref,
                  bias_ref, wr_ref, br_ref, h_ref, r_ref, count_ref, *, alpha):
    @pl.when(pl.program_id(0) == 0)
    def _():
        count_ref[...] = jnp.zeros_like(count_ref)

    w_hi, w_lo = _split_bf16(wr_ref[...])
    w_both = jnp.concatenate([w_hi, w_lo], axis=1)
    part = h_ref.shape[0] // ROUTE_SPLITS
    for g in range(ROUTE_SPLITS):
        rows = slice(g * part, (g + 1) * part)
        merged = (_sigmoid(gr_ref[rows, :].astype(F32)) * _dot(ret_ref[rows, :], wret_ref[...])
                  + _sigmoid(gd_ref[rows, :].astype(F32)) * _dot(diff_ref[rows, :], wdiff_ref[...]))
        h = _layer_norm(alpha * x_ref[rows, :] + _dot(merged.astype(BF16), wo_ref[...]),
                        gain_ref[...], bias_ref[...])
        h_ref[rows, :] = h
        h_hi, h_lo = _split_bf16(h)
        both = _dot(h_hi, w_both)
        logits = both[:, :LANES] + (both[:, LANES:] + _dot(h_lo, w_hi)) + br_ref[...]
        r_ref[rows, :] = _routing_rows(logits, count_ref)


def _routing_rows(logits, count_ref):
    lane = lax.broadcasted_iota(jnp.int32, logits.shape, 1).astype(F32)

    def first_argmax(vals):
        top = jnp.max(vals, axis=-1, keepdims=True)
        idx = jnp.min(jnp.where(vals == top, lane, float(LANES)), axis=-1, keepdims=True)
        return top, idx

    group_logits = jnp.where(lane < N_GROUPS, logits, NEG_BIG)
    g_top, g_idx = first_argmax(group_logits)
    p_group = 1.0 / jnp.sum(jnp.exp(group_logits - g_top), axis=-1, keepdims=True)

    first = N_GROUPS + g_idx * EXPERTS_PER_GROUP
    within = jnp.where((lane >= first) & (lane < first + EXPERTS_PER_GROUP), logits, NEG_BIG)
    v1, i1 = first_argmax(within)
    v2, i2 = first_argmax(jnp.where(lane == i1, NEG_BIG, within))
    e2 = jnp.exp(v2 - v1)
    w1 = p_group / (1.0 + e2)
    w2 = p_group * e2 / (1.0 + e2)
    x1, x2 = i1 - N_GROUPS, i2 - N_GROUPS

    hit1 = jnp.where(lane == x1, 1.0, 0.0)
    hit2 = jnp.where(lane == x2, 1.0, 0.0)
    hits = hit1 + hit2
    tm = hits.shape[0]
    earlier = (lax.broadcasted_iota(jnp.int32, (tm, tm), 0) > lax.broadcasted_iota(jnp.int32, (tm, tm), 1))
    before = count_ref[...] + _dot(jnp.where(earlier, 1.0, 0.0).astype(BF16), hits.astype(BF16))
    rank1 = jnp.sum(before * hit1, axis=-1, keepdims=True)
    rank2 = jnp.sum(before * hit2, axis=-1, keepdims=True)
    count_ref[...] += jnp.sum(hits, axis=0, keepdims=True)

    fields = (x1, x2, w1, w2, rank1, rank2)
    row = jnp.zeros_like(logits)
    for pos, value in enumerate(fields):
        row = jnp.where(lane == pos, value, row)
    return row


def _route(ret, diff, u, x, w_ret, w_diff, w_out, gain, bias, w_router, b_router, alpha):
    n, dm = x.shape
    tm = min(ROUTE_TM, n)
    gate_off = (u.shape[1] - 2 * dm) // dm
    tile = pl.BlockSpec((tm, dm), lambda i: (i, 0))
    gate = lambda c: pl.BlockSpec((tm, dm), lambda i: (i, c))
    full = lambda a: pl.BlockSpec(a.shape, lambda i: (0, 0), pipeline_mode=pl.Buffered(1))
    return pl.pallas_call(
        functools.partial(_route_kernel, alpha=alpha), grid=(n // tm,),
        in_specs=[pl.BlockSpec((tm, ret.shape[1]), lambda i: (i, 0)),
                  pl.BlockSpec((tm, diff.shape[1]), lambda i: (i, 0)),
                  gate(gate_off), gate(gate_off + 1), tile, full(w_ret), full(w_diff), full(w_out),
                  full(gain), full(bias), full(w_router), full(b_router)],
        out_specs=[tile, pl.BlockSpec((tm, LANES), lambda i: (i, 0)),
                   pl.BlockSpec((1, LANES), lambda i: (0, 0))],
        out_shape=[jax.ShapeDtypeStruct((n, dm), F32), jax.ShapeDtypeStruct((n, LANES), F32),
                   jax.ShapeDtypeStruct((1, LANES), F32)],
        compiler_params=_cparams(1), name="merge_outproj_ln_router",
    )(ret, diff, u, u, x, w_ret, w_diff, w_out, gain, bias, w_router, b_router)


def _row_copies(table_ref, first, n_rows, make_copy):
    for g in range(n_rows // DMA_GROUP):
        entries = [table_ref[first + g * DMA_GROUP + k] for k in range(DMA_GROUP)]
        for k, entry in enumerate(entries):
            make_copy(entry, g * DMA_GROUP + k).start()


def _for_buffer(dynamic_buf, fn):
    for buf in range(2):
        pl.when(dynamic_buf == buf)(functools.partial(fn, buf))


def _expert_kernel(tile_expert_ref, run_next_ref, src_ref, n_valid_ref, h_hbm, wg_hbm, wu_hbm, wd_hbm,
                   y_ref, xbuf, stage_g, stage_u, stage_d, wg_ref, wu_ref, wd_ref, gather_sem,
                   weight_sem):
    i = pl.program_id(0)
    n_valid = n_valid_ref[0]
    tm = xbuf.shape[1]
    slot = i & 1
    weights = ((wg_hbm, stage_g, wg_ref), (wu_hbm, stage_u, wu_ref), (wd_hbm, stage_d, wd_ref))

    def start_weights(expert):
        for k, (hbm, stage, _) in enumerate(weights):
            pltpu.make_async_copy(hbm.at[expert], stage, weight_sem.at[k]).start(priority=1)

    def take_weights():
        for k, (hbm, stage, dst) in enumerate(weights):
            pltpu.make_async_copy(hbm.at[0], stage, weight_sem.at[k]).wait()

            def round_rows(c, carry, stage=stage, dst=dst):
                rows = pl.ds(pl.multiple_of(c * WEIGHT_CAST_ROWS, WEIGHT_CAST_ROWS), WEIGHT_CAST_ROWS)
                dst[rows, :] = stage[rows, :].astype(BF16)
                return carry
            lax.fori_loop(0, stage.shape[0] // WEIGHT_CAST_ROWS, round_rows, 0)

    def start_gather(tile, dynamic_buf):
        def issue(buf):
            _row_copies(src_ref, tile * tm, tm, lambda tok, r: pltpu.make_async_copy(
                h_hbm.at[pl.ds(tok, 1)], xbuf.at[buf, pl.ds(r, 1)], gather_sem.at[buf]))
        _for_buffer(dynamic_buf, issue)

    @pl.when(i == 0)
    def _():
        start_gather(0, 0)
        start_weights(tile_expert_ref[0])

    @pl.when(i + 1 < n_valid)
    def _():
        start_gather(i + 1, 1 - slot)

    @pl.when(i < n_valid)
    def _():
        run_next = run_next_ref[i]

        @pl.when(run_next != NOT_RUN_START)
        def _():
            take_weights()

            @pl.when(run_next != NO_NEXT_EXPERT)
            def _():
                start_weights(run_next)

        pltpu.make_async_copy(h_hbm.at[pl.ds(0, tm)], xbuf.at[slot], gather_sem.at[slot]).wait()
        x = xbuf[slot].astype(BF16)
        g = _dot(x, wg_ref[...])
        act = (g * _sigmoid(g) * _dot(x, wu_ref[...])).astype(BF16)
        y_ref[...] = _dot(act, wd_ref[...])

    @pl.when(i >= n_valid)
    def _():
        y_ref[...] = jnp.zeros_like(y_ref)


def _experts(h, tile_expert, run_next, src, n_valid, w_gate, w_up, w_down):
    dm = h.shape[1]
    n_tiles = tile_expert.shape[0]
    tm = src.shape[0] // n_tiles
    ff = w_gate.shape[2]
    hbm = pl.BlockSpec(memory_space=pl.ANY)
    grid_spec = pltpu.PrefetchScalarGridSpec(
        num_scalar_prefetch=4, grid=(n_tiles,),
        in_specs=[hbm, hbm, hbm, hbm],
        out_specs=pl.BlockSpec((tm, dm), lambda i, *_: (i, 0)),
        scratch_shapes=[pltpu.VMEM((2, tm, dm), F32),
                        pltpu.VMEM((dm, ff), F32), pltpu.VMEM((dm, ff), F32), pltpu.VMEM((ff, dm), F32),
                        pltpu.VMEM((dm, ff), BF16), pltpu.VMEM((dm, ff), BF16), pltpu.VMEM((ff, dm), BF16),
                        pltpu.SemaphoreType.DMA((2,)), pltpu.SemaphoreType.DMA((3,))])
    return pl.pallas_call(
        _expert_kernel, grid_spec=grid_spec,
        out_shape=jax.ShapeDtypeStruct((src.shape[0], dm), F32),
        compiler_params=_cparams(1), name="experts",
    )(tile_expert, run_next, src, n_valid, h, w_gate, w_up, w_down)


def _dispatch_tables(expert_ids, ranks, counts, tm):
    n_assign = expert_ids.size
    n_tiles = n_assign // tm + N_EXPERTS
    n_slots = n_tiles * tm
    experts = jnp.arange(N_EXPERTS, dtype=jnp.int32)
    padded = (counts + tm - 1) // tm * tm
    pad_end = jnp.cumsum(padded)
    pad_start = pad_end - padded
    slot_of = (jnp.sum(jnp.where(expert_ids[..., None] == experts, pad_start, 0), axis=-1) + ranks).reshape(-1)
    tile_start = jnp.arange(n_tiles, dtype=jnp.int32) * tm
    tile_expert = jnp.minimum(
        jnp.sum((pad_end[None, :] <= tile_start[:, None]).astype(jnp.int32), axis=1), N_EXPERTS - 1)
    later = lax.cummin(jnp.where(counts > 0, experts, N_EXPERTS)[::-1])[::-1]
    following = jnp.concatenate([later[1:], jnp.full((1,), N_EXPERTS, jnp.int32)])[tile_expert]
    run_start = jnp.concatenate([jnp.ones((1,), bool), tile_expert[1:] != tile_expert[:-1]])
    run_next = jnp.where(run_start, jnp.where(following < N_EXPERTS, following, NO_NEXT_EXPERT),
                         NOT_RUN_START).astype(jnp.int32)
    by_slot = jnp.argsort(slot_of).astype(jnp.int32) // TOP_K
    dense_start = jnp.cumsum(counts) - counts
    tile_shift = (dense_start - pad_start)[tile_expert]
    dense_index = jnp.repeat(tile_shift, tm) + jnp.arange(n_slots, dtype=jnp.int32)
    src = by_slot[jnp.clip(dense_index, 0, n_assign - 1)]
    n_valid = (pad_end[-1] // tm).astype(jnp.int32).reshape(1)
    return slot_of, tile_expert, run_next, src, n_valid


def _final_kernel(slot_ref, h_ref, y_hbm, r_ref, gain_ref, bias_ref, o_ref, ybuf, sem, *, alpha):
    i = pl.program_id(0)
    tm = o_ref.shape[0]
    slot = i & 1

    def start_gather(step, dynamic_buf):
        def issue(buf):
            _row_copies(slot_ref, step * (TOP_K * tm), TOP_K * tm, lambda row, r: pltpu.make_async_copy(
                y_hbm.at[pl.ds(row, 1)], ybuf.at[buf, r % TOP_K, pl.ds(r // TOP_K, 1)], sem.at[buf]))
        _for_buffer(dynamic_buf, issue)

    @pl.when(i == 0)
    def _():
        start_gather(0, 0)

    @pl.when(i + 1 < pl.num_programs(0))
    def _():
        start_gather(i + 1, 1 - slot)

    for k in range(TOP_K):
        pltpu.make_async_copy(y_hbm.at[pl.ds(0, tm)], ybuf.at[slot, k], sem.at[slot]).wait()
    r = r_ref[...]
    ffn = ybuf[slot, 0] * r[:, 2:3] + ybuf[slot, 1] * r[:, 3:4]
    o_ref[...] = _layer_norm(alpha * h_ref[...] + ffn, gain_ref[...], bias_ref[...])


def _final(h, y_slots, slot_of, routing, gain, bias, alpha):
    n, dm = h.shape
    tm = min(FINAL_TM, n)
    tile = pl.BlockSpec((tm, dm), lambda i, *_: (i, 0))
    full = lambda a: pl.BlockSpec(a.shape, lambda i, *_: (0, 0))
    grid_spec = pltpu.PrefetchScalarGridSpec(
        num_scalar_prefetch=1, grid=(n // tm,),
        in_specs=[tile, pl.BlockSpec(memory_space=pl.ANY), pl.BlockSpec((tm, LANES), lambda i, *_: (i, 0)),
                  full(gain), full(bias)],
        out_specs=tile,
        scratch_shapes=[pltpu.VMEM((2, TOP_K, tm, dm), F32), pltpu.SemaphoreType.DMA((2,))])
    return pl.pallas_call(
        functools.partial(_final_kernel, alpha=alpha), grid_spec=grid_spec,
        out_shape=jax.ShapeDtypeStruct((n, dm), F32),
        compiler_params=_cparams(1), name="combine_ln",
    )(slot_of, h, y_slots, routing, gain, bias)


def _rope_frequencies():
    ret_freq = 1.0 / (RET_THETA ** (jnp.arange(0, RET_QK_DIM, 2, dtype=F32) / RET_QK_DIM))
    diff_freq = 1.0 / (ROPE_THETA ** (jnp.arange(0, ROPE_DIM, 2, dtype=F32) / ROPE_DIM))
    pad = jnp.zeros((LANES - ret_freq.shape[0] - diff_freq.shape[0],), F32)
    return jnp.concatenate([ret_freq, diff_freq, pad]).reshape(1, LANES)


def kernel(x, positions, w_in, ret_gn_gain, diff_lambda_q1, diff_lambda_k1, diff_lambda_q2, diff_lambda_k2, diff_subln_gain, w_ret_proj, w_diff_proj, w_out, ln1_gain, ln1_bias, w_group, b_group, w_router, b_router, w_expert_gate, w_expert_up, w_expert_down, ln2_gain, ln2_bias):
    batch, seq, dm = x.shape
    n = batch * seq
    depth = w_in.shape[0]
    assert seq % CHUNK == 0
    alpha = (2 * depth) ** 0.25

    ret_cos, ret_sin, diff_cos, diff_sin_lo, diff_sin_hi = _rope_tables(
        positions.reshape(n, 1), _rope_frequencies())

    log_gamma = jnp.log(1.0 - 2.0 ** (-5.0 - jnp.arange(RET_HEADS, dtype=F32)))
    block_decay = jnp.exp(log_gamma * min(RET_BLOCK, seq))

    xt = x.reshape(n, dm)
    for l in range(depth):
        lambda_init = 0.8 - 0.6 * math.exp(-0.3 * l)
        u = _matmul(xt, w_in[l], BF16, "in_proj")
        ret = _retention(u, ret_cos, ret_sin, log_gamma, block_decay,
                         ret_gn_gain[l].reshape(RET_HEADS, 1, RET_V_DIM), batch, seq)
        lam_params = jnp.stack([diff_lambda_q1[l], diff_lambda_k1[l], diff_lambda_q2[l], diff_lambda_k2[l]])
        diff = _diff_attention(u, diff_cos, diff_sin_lo, diff_sin_hi, lam_params.astype(F32),
                               diff_subln_gain[l].reshape(1, DIFF_V_DIM), batch, seq, lambda_init)
        w_route = jnp.concatenate(
            [w_group[l], w_router[l], jnp.zeros((dm, LANES - N_GROUPS - N_EXPERTS), F32)], axis=1)
        b_route = jnp.concatenate(
            [b_group[l], b_router[l], jnp.zeros((LANES - N_GROUPS - N_EXPERTS,), F32)]).reshape(1, LANES)
        h, routing, counts = _route(
            ret, diff, u, xt, w_ret_proj[l].astype(BF16), w_diff_proj[l].astype(BF16),
            w_out[l].astype(BF16), ln1_gain[l].reshape(1, dm), ln1_bias[l].reshape(1, dm),
            w_route, b_route, alpha)

        fields = routing[:, :3 * TOP_K].astype(jnp.int32)
        expert_ids, ranks = fields[:, :TOP_K], fields[:, 2 * TOP_K:]
        slot_of, tile_expert, run_next, src, n_valid = _dispatch_tables(
            expert_ids, ranks, counts[0, :N_EXPERTS].astype(jnp.int32), min(MOE_TM, n))
        y_slots = _experts(h, tile_expert, run_next, src, n_valid, w_expert_gate[l], w_expert_up[l],
                           w_expert_down[l])
        xt = _final(h, y_slots, slot_of, routing,
                    ln2_gain[l].reshape(1, dm), ln2_bias[l].reshape(1, dm), alpha)
    return xt.reshape(batch, seq, dm)
```
